```python
import math
import jax
import jax.numpy as jnp
from jax import lax
import numpy as np

D_MODEL = 1024
BATCH = 4
SEQ = 8192
DEPTH = 4

GRID_W = 64
CTX_LEN = 256
HEAD_DIM = 64
A_HEADS = 4
A_KV_HEADS = 2
A_WIDTH = A_HEADS * HEAD_DIM
A_KV_WIDTH = A_KV_HEADS * HEAD_DIM
HY_WIDTH = 256
HY_ORDER = 2
HY_IN = (HY_ORDER + 1) * HY_WIDTH
HY_BANDS = 16
HY_EMB_DIM = 1 + 2 * HY_BANDS
HY_HIDDEN = 64
HY_FILTER_CH = 2 * HY_ORDER * HY_WIDTH
HY_FAST_DECAY = 0.3
HY_SLOW_DECAY = 1.5
HY_TARGET = 1e-2
D_HEADS = 4
D_VALUE_DIM = 2 * HEAD_DIM
D_WIDTH = D_HEADS * D_VALUE_DIM
D_QK_WIDTH = D_HEADS * 2 * HEAD_DIM
MIX_WIDTH = A_WIDTH + HY_WIDTH + D_WIDTH
IN_SPLITS = (A_WIDTH, A_WIDTH + A_KV_WIDTH, A_WIDTH + 2 * A_KV_WIDTH, A_WIDTH + 2 * A_KV_WIDTH + HY_IN, A_WIDTH + 2 * A_KV_WIDTH + HY_IN + D_QK_WIDTH, A_WIDTH + 2 * A_KV_WIDTH + HY_IN + 2 * D_QK_WIDTH)
IN_WIDTH = A_WIDTH + 2 * A_KV_WIDTH + HY_IN + 2 * D_QK_WIDTH + D_WIDTH
D_FF = 2816
ROPE_THETA = 10000.0
Q_BLOCK = 128
EPS = 1e-6

kernel_name = 'hybrid_parallel_heads_flow_block'


def rms_norm(x, g):
    xf = x.astype(jnp.float32)
    y = xf * lax.rsqrt(jnp.mean(xf * xf, axis=-1, keepdims=True) + EPS)
    return (y * g.astype(jnp.float32)).astype(x.dtype)


def modulate(h, shift, scale):
    return h * (1.0 + scale) + shift


def dwconv3(u, w, b):
    up = jnp.pad(u, ((0, 0), (1, 1), (0, 0)))
    return up[:, :-2] * w[0] + up[:, 1:-1] * w[1] + up[:, 2:] * w[2] + b


def axial_rope_tables(rows, cols):
    n_freq = HEAD_DIM // 4
    inv = ROPE_THETA ** (-jnp.arange(n_freq, dtype=jnp.float32) / n_freq)
    ang = jnp.concatenate([rows[:, None] * inv, cols[:, None] * inv], axis=-1)
    return jnp.cos(ang), jnp.sin(ang)


def apply_rope(x, cos, sin):
    t = x.shape[1]
    bshape = (t,) + (1,) * (x.ndim - 3) + (cos.shape[-1],)
    c = cos.reshape(bshape)
    s = sin.reshape(bshape)
    xf = x.astype(jnp.float32)
    x1 = xf[..., 0::2]
    x2 = xf[..., 1::2]
    out = jnp.stack([x1 * c - x2 * s, x1 * s + x2 * c], axis=-1).reshape(x.shape)
    return out.astype(x.dtype)


def sweep_attention(q, k, v):
    b, hk, g, n, d = q.shape
    nb = n // Q_BLOCK
    qb = jnp.moveaxis(q.reshape(b, hk, g, nb, Q_BLOCK, d), 3, 0)
    scale = d ** -0.5

    def one_block(qi):
        s = jnp.einsum('bkgqd,bksd->bkgqs', qi, k, preferred_element_type=jnp.float32) * scale
        p = jax.nn.softmax(s, axis=-1)
        return jnp.einsum('bkgqs,bksd->bkgqd', p.astype(v.dtype), v)

    o = lax.map(one_block, qb)
    return jnp.moveaxis(o, 0, 3).reshape(b, hk, g, n, v.shape[-1])


def gqa_attention(q, k, v):
    b, n, h, d = q.shape
    hk = k.shape[2]
    qg = q.reshape(b, n, hk, h // hk, d).transpose(0, 2, 3, 1, 4)
    o = sweep_attention(qg, k.transpose(0, 2, 1, 3), v.transpose(0, 2, 1, 3))
    return o.transpose(0, 3, 1, 2, 4).reshape(b, n, h * v.shape[-1])


def diff_attention(q, k, v, lam):
    vv = v.transpose(0, 2, 1, 3)

    def component(j):
        qj = q[..., j, :].transpose(0, 2, 1, 3)[:, :, None]
        kj = k[..., j, :].transpose(0, 2, 1, 3)
        return sweep_attention(qj, kj, vv)[:, :, 0]

    o = component(0) - lam.astype(v.dtype) * component(1)
    return o.transpose(0, 2, 1, 3)


def hyena_filters(n, fw1, fb1, fw2, fb2, fw3, fb3, freq):
    pos = jnp.arange(n, dtype=jnp.float32)
    t = (pos / max(n - 1, 1))[:, None]
    w = 2.0 * math.pi * pos[:, None] / n
    bands = jnp.linspace(1e-4, HY_BANDS - 1, HY_BANDS, dtype=jnp.float32)
    z = jnp.concatenate([t, jnp.cos(bands * w), -jnp.sin(bands * w)], axis=-1)
    h = jnp.sin(freq * (z.astype(fw1.dtype) @ fw1 + fb1))
    h = jnp.sin(freq * (h @ fw2 + fb2))
    h = (h @ fw3 + fb3).astype(jnp.float32)
    max_decay = math.log(HY_TARGET) / HY_FAST_DECAY
    min_decay = math.log(HY_TARGET) / HY_SLOW_DECAY
    deltas = jnp.linspace(min_decay, max_decay, HY_WIDTH, dtype=jnp.float32)
    decay = jnp.exp(-t * jnp.abs(deltas))
    h = h.reshape(n, 2, HY_ORDER, HY_WIDTH) * decay[:, None, None, :]
    return h / (jnp.sum(jnp.abs(h), axis=(0, 1), keepdims=True) + EPS)


def long_conv(u, h_fwd, h_bwd, bias):
    n = u.shape[1]
    k = jnp.concatenate([h_fwd, jnp.zeros_like(h_fwd[:1]), h_bwd[:0:-1]], axis=0)
    k_f = jnp.fft.rfft(k, n=2 * n, axis=0)
    u32 = u.astype(jnp.float32)
    y = jnp.fft.irfft(jnp.fft.rfft(u32, n=2 * n, axis=1) * k_f[None], n=2 * n, axis=1)[:, :n]
    return (y + u32 * bias.astype(jnp.float32)).astype(u.dtype)


def hyena(u, conv_w, conv_b, fw1, fb1, fw2, fb2, fw3, fb3, freq, bias):
    n = u.shape[1]
    u = dwconv3(u, conv_w, conv_b)
    v, x1, x2 = jnp.split(u, 3, axis=-1)
    h = hyena_filters(n, fw1, fb1, fw2, fb2, fw3, fb3, freq)
    z = x1 * long_conv(v, h[:, 0, 0], h[:, 1, 0], bias[0])
    return x2 * long_conv(z, h[:, 0, 1], h[:, 1, 1], bias[1])


def merge_groups(ya, yh, yd, g_out, w_out, lam_init):
    b, n = ya.shape[:2]
    ya = rms_norm(ya, g_out[:A_WIDTH])
    yh = rms_norm(yh, g_out[A_WIDTH:A_WIDTH + HY_WIDTH])
    yd = rms_norm(yd, g_out[A_WIDTH + HY_WIDTH:].reshape(D_HEADS, D_VALUE_DIM)) * (1.0 - lam_init)
    return jnp.concatenate([ya, yh, yd.reshape(b, n, D_WIDTH)], axis=-1) @ w_out


def token_mixers(h, hc, cos, sin, lam, lam_init, need_ctx, w_in, qn_a, kn_a, qn_d, kn_d, hy_conv_w, hy_conv_b, hy_fw1, hy_fb1, hy_fw2, hy_fb2, hy_fw3, hy_fb3, hy_freq, hy_bias, g_out, w_out):
    b, n = h.shape[:2]
    m = hc.shape[1]
    qa, ka, va, hy, qd, kd, vd = jnp.split(h @ w_in, IN_SPLITS, axis=-1)
    qa_c, ka_c, va_c, hy_c, qd_c, kd_c, vd_c = jnp.split(hc @ w_in, IN_SPLITS, axis=-1)
    ka = apply_rope(rms_norm(ka.reshape(b, n, A_KV_HEADS, HEAD_DIM), kn_a), cos, sin)
    va = va.reshape(b, n, A_KV_HEADS, HEAD_DIM)
    ka_c = rms_norm(ka_c.reshape(b, m, A_KV_HEADS, HEAD_DIM), kn_a)
    va_c = va_c.reshape(b, m, A_KV_HEADS, HEAD_DIM)
    qa = apply_rope(rms_norm(qa.reshape(b, n, A_HEADS, HEAD_DIM), qn_a), cos, sin)
    ya = gqa_attention(qa, jnp.concatenate([ka_c, ka], axis=1), jnp.concatenate([va_c, va], axis=1))
    kd = apply_rope(rms_norm(kd.reshape(b, n, D_HEADS, 2, HEAD_DIM), kn_d), cos, sin)
    vd = vd.reshape(b, n, D_HEADS, D_VALUE_DIM)
    kd_c = rms_norm(kd_c.reshape(b, m, D_HEADS, 2, HEAD_DIM), kn_d)
    vd_c = vd_c.reshape(b, m, D_HEADS, D_VALUE_DIM)
    qd = apply_rope(rms_norm(qd.reshape(b, n, D_HEADS, 2, HEAD_DIM), qn_d), cos, sin)
    yd = diff_attention(qd, jnp.concatenate([kd_c, kd], axis=1), jnp.concatenate([vd_c, vd], axis=1), lam)
    yh = hyena(hy, hy_conv_w, hy_conv_b, hy_fw1, hy_fb1, hy_fw2, hy_fb2, hy_fw3, hy_fb3, hy_freq, hy_bias)
    y = merge_groups(ya, yh, yd, g_out, w_out, lam_init)
    if not need_ctx:
        return y, None
    qa_c = rms_norm(qa_c.reshape(b, m, A_HEADS, HEAD_DIM), qn_a)
    ya_c = gqa_attention(qa_c, ka_c, va_c)
    qd_c = rms_norm(qd_c.reshape(b, m, D_HEADS, 2, HEAD_DIM), qn_d)
    yd_c = diff_attention(qd_c, kd_c, vd_c, lam)
    yh_c = hyena(hy_c, hy_conv_w, hy_conv_b, hy_fw1, hy_fb1, hy_fw2, hy_fb2, hy_fw3, hy_fb3, hy_freq, hy_bias)
    return y, merge_groups(ya_c, yh_c, yd_c, g_out, w_out, lam_init)


def conv_glu(h, w_up, conv_w, conv_b, w_down):
    a, v = jnp.split(h @ w_up, 2, axis=-1)
    a = dwconv3(a, conv_w, conv_b)
    return (jax.nn.gelu(a, approximate=True) * v) @ w_down


def setup_inputs(seed: int = 0) -> dict:
    key = jax.random.key(seed)
    keys = iter(jax.random.split(key, 40))

    def nrm(shape, scale):
        return jax.random.normal(next(keys), shape, jnp.float32) * scale

    def gain(shape):
        return 1.0 + nrm(shape, 0.02)

    L = DEPTH
    return {
        'x': nrm((BATCH, SEQ, D_MODEL), 1.0),
        'c': nrm((BATCH, D_MODEL), 1.0),
        'ctx': nrm((BATCH, CTX_LEN, D_MODEL), 1.0),
        'c_ctx': nrm((D_MODEL,), 1.0),
        'norm1_g': gain((L, D_MODEL)),
        'norm2_g': gain((L, D_MODEL)),
        'w_mod': nrm((L, D_MODEL, 6 * D_MODEL), 0.5 * D_MODEL ** -0.5),
        'b_mod': nrm((L, 6 * D_MODEL), 0.02),
        'w_in': nrm((L, D_MODEL, IN_WIDTH), D_MODEL ** -0.5),
        'qn_a': gain((L, HEAD_DIM)),
        'kn_a': gain((L, HEAD_DIM)),
        'qn_d': gain((L, HEAD_DIM)),
        'kn_d': gain((L, HEAD_DIM)),
        'lam_q1': nrm((L, HEAD_DIM), 0.1),
        'lam_k1': nrm((L, HEAD_DIM), 0.1),
        'lam_q2': nrm((L, HEAD_DIM), 0.1),
        'lam_k2': nrm((L, HEAD_DIM), 0.1),
        'hy_conv_w': nrm((L, 3, HY_IN), 3 ** -0.5),
        'hy_conv_b': nrm((L, HY_IN), 0.02),
        'hy_fw1': nrm((L, HY_EMB_DIM, HY_HIDDEN), HY_EMB_DIM ** -0.5),
        'hy_fb1': nrm((L, HY_HIDDEN), 0.5),
        'hy_fw2': nrm((L, HY_HIDDEN, HY_HIDDEN), HY_HIDDEN ** -0.5),
        'hy_fb2': nrm((L, HY_HIDDEN), 0.5),
        'hy_fw3': nrm((L, HY_HIDDEN, HY_FILTER_CH), HY_HIDDEN ** -0.5),
        'hy_fb3': nrm((L, HY_FILTER_CH), 0.02),
        'hy_freq': gain((L, HY_HIDDEN)),
        'hy_bias': nrm((L, HY_ORDER, HY_WIDTH), 0.1),
        'g_out': gain((L, MIX_WIDTH)),
        'w_out': nrm((L, MIX_WIDTH, D_MODEL), MIX_WIDTH ** -0.5),
        'w_up': nrm((L, D_MODEL, 2 * D_FF), D_MODEL ** -0.5),
        'ffn_conv_w': nrm((L, 3, D_FF), 3 ** -0.5),
        'ffn_conv_b': nrm((L, D_FF), 0.02),
        'w_down': nrm((L, D_FF, D_MODEL), D_FF ** -0.5),
    }


def reference(x, c, ctx, c_ctx, norm1_g, norm2_g, w_mod, b_mod, w_in, qn_a, kn_a, qn_d, kn_d, lam_q1, lam_k1, lam_q2, lam_k2, hy_conv_w, hy_conv_b, hy_fw1, hy_fb1, hy_fw2, hy_fb2, hy_fw3, hy_fb3, hy_freq, hy_bias, g_out, w_out, w_up, ffn_conv_w, ffn_conv_b, w_down):
    n_tok = x.shape[1]
    n_rows = n_tok // GRID_W
    rows = jnp.repeat(jnp.arange(n_rows, dtype=jnp.float32), GRID_W)
    cols = jnp.tile(jnp.arange(GRID_W, dtype=jnp.float32), n_rows)
    cos, sin = axial_rope_tables(rows, cols)
    s_lat = jax.nn.silu(c)
    s_ctx = jax.nn.silu(c_ctx)
    for i in range(DEPTH):
        need_ctx = i < DEPTH - 1
        lam_init = 0.8 - 0.6 * math.exp(-0.3 * i)
        lam = (jnp.exp(jnp.sum((lam_q1[i] * lam_k1[i]).astype(jnp.float32)))
               - jnp.exp(jnp.sum((lam_q2[i] * lam_k2[i]).astype(jnp.float32))) + lam_init)
        sh1, sc1, g1, sh2, sc2, g2 = jnp.split((s_lat @ w_mod[i] + b_mod[i])[:, None, :], 6, axis=-1)
        csh1, csc1, cg1, csh2, csc2, cg2 = jnp.split(s_ctx @ w_mod[i] + b_mod[i], 6, axis=-1)
        h = modulate(rms_norm(x, norm1_g[i]), sh1, sc1)
        hc = modulate(rms_norm(ctx, norm1_g[i]), csh1, csc1)
        y, yc = token_mixers(h, hc, cos, sin, lam, lam_init, need_ctx, w_in[i], qn_a[i], kn_a[i], qn_d[i], kn_d[i], hy_conv_w[i], hy_conv_b[i], hy_fw1[i], hy_fb1[i], hy_fw2[i], hy_fb2[i], hy_fw3[i], hy_fb3[i], hy_freq[i], hy_bias[i], g_out[i], w_out[i])
        x = x + g1 * y
        h = modulate(rms_norm(x, norm2_g[i]), sh2, sc2)
        x = x + g2 * conv_glu(h, w_up[i], ffn_conv_w[i], ffn_conv_b[i], w_down[i])
        if need_ctx:
            ctx = ctx + cg1 * yc
            hc = modulate(rms_norm(ctx, norm2_g[i]), csh2, csc2)
            ctx = ctx + cg2 * conv_glu(hc, w_up[i], ffn_conv_w[i], ffn_conv_b[i], w_down[i])
    return x
```

```python
import functools
import math

import numpy as np
import jax
import jax.numpy as jnp
from jax import lax
from jax.experimental import pallas as pl
from jax.experimental.pallas import tpu as pltpu

F32 = jnp.float32
BF16 = jnp.bfloat16

HEAD_DIM = 64
HALF = HEAD_DIM // 2
A_HEADS = 4
A_KV_HEADS = 2
A_WIDTH = A_HEADS * HEAD_DIM
A_KV_WIDTH = A_KV_HEADS * HEAD_DIM
HY_WIDTH = 256
HY_IN = 3 * HY_WIDTH
HY_BANDS = 16
HY_EMB_DIM = 1 + 2 * HY_BANDS
HY_HIDDEN = 64
HY_FILTER_CH = 4 * HY_WIDTH
HY_FAST_DECAY = 0.3
HY_SLOW_DECAY = 1.5
HY_TARGET = 1e-2
D_HEADS = 4
D_VALUE_DIM = 2 * HEAD_DIM
D_WIDTH = D_HEADS * D_VALUE_DIM
D_QK_WIDTH = D_HEADS * 2 * HEAD_DIM
MIX_WIDTH = A_WIDTH + HY_WIDTH + D_WIDTH
OFF_QA = 0
OFF_KA = A_WIDTH
OFF_VA = OFF_KA + A_KV_WIDTH
OFF_HY = OFF_VA + A_KV_WIDTH
OFF_QD = OFF_HY + HY_IN
OFF_KD = OFF_QD + D_QK_WIDTH
OFF_VD = OFF_KD + D_QK_WIDTH
GRID_W = 64
ROPE_THETA = 10000.0
EPS = 1e-6

N_QBLK = A_HEADS + 2 * D_HEADS
N_KBLK = A_KV_HEADS + 2 * D_HEADS
Q_ROWS = N_QBLK * HEAD_DIM
K_ROWS = N_KBLK * HEAD_DIM
V_ROWS = A_KV_WIDTH + D_WIDTH
QK_ROWS = Q_ROWS + K_ROWS
QKV_ROWS = QK_ROWS + V_ROWS

LANE = 128
DFT_N2 = 128
VMEM_LIMIT = 56 * 1024 * 1024
HIGHEST = lax.Precision.HIGHEST


def _cparams(sem):
    return pltpu.CompilerParams(dimension_semantics=sem, vmem_limit_bytes=VMEM_LIMIT)


def _split_bf16(x):
    hi = x.astype(BF16)
    lo = (x - hi.astype(F32)).astype(BF16)
    return hi, lo


def _dot(a, b):
    return jnp.dot(a, b, preferred_element_type=F32)


def _dot3(a_hi, a_lo, b):
    b_hi, b_lo = _split_bf16(b)
    return _dot(a_hi, b_hi) + (_dot(a_hi, b_lo) + _dot(a_lo, b_hi))


def _mod_kernel(c_ref, w_ref, b_ref, o_ref):
    c = c_ref[...]
    s = c * (1.0 / (1.0 + jnp.exp(-c)))
    o_ref[0] = jnp.dot(s, w_ref[0], preferred_element_type=F32, precision=HIGHEST) + b_ref[0]


def _modulation(cvec, w_mod, b_mod):
    depth, d, n6 = w_mod.shape
    tn = 1024
    return pl.pallas_call(
        _mod_kernel,
        out_shape=jax.ShapeDtypeStruct((depth, cvec.shape[0], n6), F32),
        grid=(depth, n6 // tn),
        in_specs=[
            pl.BlockSpec(cvec.shape, lambda l, j: (0, 0)),
            pl.BlockSpec((1, d, tn), lambda l, j: (l, 0, j)),
            pl.BlockSpec((1, 1, tn), lambda l, j: (l, 0, j)),
        ],
        out_specs=pl.BlockSpec((1, cvec.shape[0], tn), lambda l, j: (l, 0, j)),
        compiler_params=_cparams(("arbitrary", "arbitrary")),
        name="modulation",
    )(cvec, w_mod, b_mod.reshape(depth, 1, n6))


def _norm_modulate(x, g, mod_ref, is_ctx, which):
    ms = jnp.mean(x * x, axis=-1, keepdims=True)
    xn = x * lax.rsqrt(ms + EPS) * g
    sh = jnp.where(is_ctx, mod_ref[0, 3 * which:3 * which + 1, :], mod_ref[0, 8 + 3 * which:9 + 3 * which, :])
    sc = jnp.where(is_ctx, mod_ref[0, 3 * which + 1:3 * which + 2, :],
                   mod_ref[0, 9 + 3 * which:10 + 3 * which, :])
    return xn * (1.0 + sc) + sh


def _proj_kernel(x_ref, mod_ref, g_ref, wq_ref, why_ref, gain_ref, cos_ref, sin_ref,
                 qT_ref, k_ref, vT_ref, hy_ref, *, tm, ctx_len):
    i = pl.program_id(1)
    row = i * tm + lax.broadcasted_iota(jnp.int32, (tm, 1), 0)
    h = _norm_modulate(x_ref[0], g_ref[...], mod_ref, row < ctx_len, 0).astype(BF16)
    pT = lax.dot_general(wq_ref[...], h, (((1,), (1,)), ((), ())), preferred_element_type=F32)
    hy_ref[0] = _dot(h, why_ref[...])
    c = cos_ref[...]
    s = sin_ref[...]
    blocks = []
    for b in range(N_QBLK + N_KBLK):
        blk = pT[b * HEAD_DIM:(b + 1) * HEAD_DIM]
        ssq = jnp.sum(blk * blk, axis=0, keepdims=True)
        blk = blk * lax.rsqrt(ssq * (1.0 / HEAD_DIM) + EPS) * gain_ref[b * HEAD_DIM:(b + 1) * HEAD_DIM, :]
        x1 = blk[:HALF]
        x2 = blk[HALF:]
        blocks.append(x1 * c - x2 * s)
        blocks.append(x1 * s + x2 * c)
    qT_ref[0] = jnp.concatenate(blocks[:2 * N_QBLK], axis=0).astype(BF16)
    kT = jnp.concatenate(blocks[2 * N_QBLK:], axis=0)
    k_ref[0] = kT.T.astype(BF16)
    vT_ref[0] = pT[QK_ROWS:].astype(BF16)


def _projection(xs, mod, g, wqT, why, gains, cosT, sinT, *, ctx_len, tm):
    bsz, s_len, d = xs.shape
    kern = functools.partial(_proj_kernel, tm=tm, ctx_len=ctx_len)
    return pl.pallas_call(
        kern,
        out_shape=(
            jax.ShapeDtypeStruct((bsz, Q_ROWS, s_len), BF16),
            jax.ShapeDtypeStruct((bsz, s_len, K_ROWS), BF16),
            jax.ShapeDtypeStruct((bsz, V_ROWS, s_len), BF16),
            jax.ShapeDtypeStruct((bsz, s_len, HY_IN), F32),
        ),
        grid=(bsz, s_len // tm),
        in_specs=[
            pl.BlockSpec((1, tm, d), lambda b, i: (b, i, 0)),
            pl.BlockSpec((1, 16, d), lambda b, i: (b, 0, 0)),
            pl.BlockSpec((1, d), lambda b, i: (0, 0)),
            pl.BlockSpec((QKV_ROWS, d), lambda b, i: (0, 0)),
            pl.BlockSpec((d, HY_IN), lambda b, i: (0, 0)),
            pl.BlockSpec((QK_ROWS, 1), lambda b, i: (0, 0)),
            pl.BlockSpec((HALF, tm), lambda b, i: (0, i)),
            pl.BlockSpec((HALF, tm), lambda b, i: (0, i)),
        ],
        out_specs=(
            pl.BlockSpec((1, Q_ROWS, tm), lambda b, i: (b, 0, i)),
            pl.BlockSpec((1, tm, K_ROWS), lambda b, i: (b, i, 0)),
            pl.BlockSpec((1, V_ROWS, tm), lambda b, i: (b, 0, i)),
            pl.BlockSpec((1, tm, HY_IN), lambda b, i: (b, i, 0)),
        ),
        compiler_params=_cparams(("parallel", "parallel")),
        name="projection",
    )(xs, mod, g, wqT, why, gains, cosT, sinT)


def _attend(qz, k_ref, vT_ref, v0, dv, *, ctx_only, ctx_len, tk):
    tq = qz.shape[1]
    s_len = k_ref.shape[1]

    def step(start, size, carry):
        m, l, acc = carry
        s = _dot(k_ref[0, pl.ds(start, size), :], qz)
        m_new = jnp.maximum(m, jnp.max(s, axis=0, keepdims=True))
        p = jnp.exp2(s - m_new)
        alpha = jnp.exp2(m - m_new)
        l = alpha * l + jnp.sum(p, axis=0, keepdims=True)
        acc = alpha * acc + _dot(vT_ref[0, v0:v0 + dv, pl.ds(start, size)], p.astype(BF16))
        return m_new, l, acc

    init = (jnp.full((1, tq), -1e30, F32), jnp.zeros((1, tq), F32), jnp.zeros((dv, tq), F32))
    if ctx_only:
        _, l, acc = step(0, ctx_len, init)
    else:
        def body(c, carry):
            return step(pl.multiple_of(c * tk, tk), tk, carry)
        _, l, acc = lax.fori_loop(0, s_len // tk, body, init)
    return acc * (1.0 / l)


def _attn_a_kernel(qT_ref, k_ref, vT_ref, g_ref, o_ref, *, ctx_len, tq, tk):
    qi = pl.program_id(1)
    zeros = jnp.zeros((HEAD_DIM, tq), BF16)

    def run(ctx_only):
        outs = []
        for h in range(A_HEADS):
            kv = h // (A_HEADS // A_KV_HEADS)
            q = qT_ref[0, h * HEAD_DIM:(h + 1) * HEAD_DIM, :]
            qz = jnp.concatenate([q, zeros] if kv == 0 else [zeros, q], axis=0)
            outs.append(_attend(qz, k_ref, vT_ref, kv * HEAD_DIM, HEAD_DIM,
                                ctx_only=ctx_only, ctx_len=ctx_len, tk=tk))
        y = jnp.concatenate(outs, axis=0)
        ms = jnp.mean(y * y, axis=0, keepdims=True)
        y = y * lax.rsqrt(ms + EPS) * g_ref[...]
        o_ref[0] = y.T

    n_ctx_tiles = ctx_len // tq

    @pl.when(qi < n_ctx_tiles)
    def _():
        run(True)

    @pl.when(qi >= n_ctx_tiles)
    def _():
        run(False)


def _attention_a(qT, k, vT, g_a, *, ctx_len, tq, tk):
    bsz, _, s_len = qT.shape
    kern = functools.partial(_attn_a_kernel, ctx_len=ctx_len, tq=tq, tk=tk)
    return pl.pallas_call(
        kern,
        out_shape=jax.ShapeDtypeStruct((bsz, s_len, A_WIDTH), F32),
        grid=(bsz, s_len // tq),
        in_specs=[
            pl.BlockSpec((1, A_WIDTH, tq), lambda b, i: (b, 0, i)),
            pl.BlockSpec((1, s_len, LANE), lambda b, i: (b, 0, 0)),
            pl.BlockSpec((1, A_KV_WIDTH, s_len), lambda b, i: (b, 0, 0)),
            pl.BlockSpec((A_WIDTH, 1), lambda b, i: (0, 0)),
        ],
        out_specs=pl.BlockSpec((1, tq, A_WIDTH), lambda b, i: (b, i, 0)),
        compiler_params=_cparams(("parallel", "parallel")),
        name="attention_gqa",
    )(qT, k, vT, g_a)


def _attn_d_kernel(qT_ref, k_ref, vT_ref, g_ref, lamv_ref, lami_ref, o_ref, *, ctx_len, tq, tk):
    qi = pl.program_id(2)
    zeros = jnp.zeros((HEAD_DIM, tq), BF16)
    lv = lamv_ref[...]
    lam_init = lami_ref[0:1, 0:1]
    lam = (jnp.exp(jnp.sum(lv[0:1] * lv[1:2], axis=-1, keepdims=True))
           - jnp.exp(jnp.sum(lv[2:3] * lv[3:4], axis=-1, keepdims=True)) + lam_init)

    def run(ctx_only):
        q1 = jnp.concatenate([qT_ref[0, :HEAD_DIM, :], zeros], axis=0)
        q2 = jnp.concatenate([zeros, qT_ref[0, HEAD_DIM:, :]], axis=0)
        o1 = _attend(q1, k_ref, vT_ref, 0, D_VALUE_DIM, ctx_only=ctx_only, ctx_len=ctx_len, tk=tk)
        o2 = _attend(q2, k_ref, vT_ref, 0, D_VALUE_DIM, ctx_only=ctx_only, ctx_len=ctx_len, tk=tk)
        y = o1 - lam * o2
        ms = jnp.mean(y * y, axis=0, keepdims=True)
        y = y * lax.rsqrt(ms + EPS) * g_ref[0] * (1.0 - lam_init)
        o_ref[0] = y.T

    n_ctx_tiles = ctx_len // tq

    @pl.when(qi < n_ctx_tiles)
    def _():
        run(True)

    @pl.when(qi >= n_ctx_tiles)
    def _():
        run(False)


def _attention_d(qT, k, vT, g_d, lamv, lami, *, ctx_len, tq, tk):
    bsz, _, s_len = qT.shape
    kern = functools.partial(_attn_d_kernel, ctx_len=ctx_len, tq=tq, tk=tk)
    q_blk0 = A_WIDTH // LANE
    k_blk0 = A_KV_WIDTH // LANE
    v_blk0 = A_KV_WIDTH // LANE
    return pl.pallas_call(
        kern,
        out_shape=jax.ShapeDtypeStruct((bsz, s_len, D_WIDTH), F32),
        grid=(bsz, D_HEADS, s_len // tq),
        in_specs=[
            pl.BlockSpec((1, LANE, tq), lambda b, h, i: (b, q_blk0 + h, i)),
            pl.BlockSpec((1, s_len, LANE), lambda b, h, i: (b, 0, k_blk0 + h)),
            pl.BlockSpec((1, D_VALUE_DIM, s_len), lambda b, h, i: (b, v_blk0 + h, 0)),
            pl.BlockSpec((1, D_VALUE_DIM, 1), lambda b, h, i: (h, 0, 0)),
            pl.BlockSpec((4, HEAD_DIM), lambda b, h, i: (0, 0)),
            pl.BlockSpec((8, LANE), lambda b, h, i: (0, 0)),
        ],
        out_specs=pl.BlockSpec((1, tq, D_VALUE_DIM), lambda b, h, i: (b, i, h)),
        compiler_params=_cparams(("parallel", "parallel", "parallel")),
        name="attention_diff",
    )(qT, k, vT, g_d, lamv, lami)


def _dwconv3_seq(u, w_ref, b_ref):
    n = u.shape[0]
    row = lax.broadcasted_iota(jnp.int32, (n, 1), 0)
    prev = jnp.where(row == 0, 0.0, pltpu.roll(u, 1, axis=0))
    nxt = jnp.where(row == n - 1, 0.0, pltpu.roll(u, n - 1, axis=0))
    return prev * w_ref[0:1, :] + u * w_ref[1:2, :] + nxt * w_ref[2:3, :] + b_ref[...]


def _hy_conv_kernel(hy_ref, w_ref, b_ref, lat_ref, ctx_ref, *, ctx_len):
    ctx_ref[0, 0] = _dwconv3_seq(hy_ref[0, :ctx_len, :], w_ref, b_ref)
    lat_ref[0, 0] = _dwconv3_seq(hy_ref[0, ctx_len:, :], w_ref, b_ref)


def _hyena_short_conv(hy, w, b, *, ctx_len):
    bsz, s_len, _ = hy.shape
    n = s_len - ctx_len
    per = HY_WIDTH // LANE
    kern = functools.partial(_hy_conv_kernel, ctx_len=ctx_len)
    return pl.pallas_call(
        kern,
        out_shape=(
            jax.ShapeDtypeStruct((3, bsz, n, HY_WIDTH), F32),
            jax.ShapeDtypeStruct((3, bsz, ctx_len, HY_WIDTH), F32),
        ),
        grid=(bsz, HY_IN // LANE),
        in_specs=[
            pl.BlockSpec((1, s_len, LANE), lambda b, j: (b, 0, j)),
            pl.BlockSpec((3, LANE), lambda b, j: (0, j)),
            pl.BlockSpec((1, LANE), lambda b, j: (0, j)),
        ],
        out_specs=(
            pl.BlockSpec((1, 1, n, LANE), lambda b, j: (j // per, b, 0, j % per)),
            pl.BlockSpec((1, 1, ctx_len, LANE), lambda b, j: (j // per, b, 0, j % per)),
        ),
        compiler_params=_cparams(("parallel", "parallel")),
        name="hyena_short_conv",
    )(hy, w, b.reshape(1, HY_IN))


def _filter_kernel(z_ref, w1_ref, b1_ref, w2_ref, b2_ref, w3_ref, b3_ref, fr_ref, ad_ref, h_ref, sum_ref):
    i = pl.program_id(0)
    z = z_ref[...]
    fr = fr_ref[...]
    h = jnp.sin(fr * (jnp.dot(z, w1_ref[...], preferred_element_type=F32, precision=HIGHEST) + b1_ref[...]))
    h = jnp.sin(fr * (jnp.dot(h, w2_ref[...], preferred_element_type=F32, precision=HIGHEST) + b2_ref[...]))
    h = jnp.dot(h, w3_ref[...], preferred_element_type=F32, precision=HIGHEST) + b3_ref[...]
    h = h * jnp.exp(-z[:, 0:1] * ad_ref[...])
    h_ref[...] = h

    @pl.when(i == 0)
    def _():
        sum_ref[...] = jnp.zeros_like(sum_ref)

    sum_ref[...] += jnp.sum(jnp.abs(h), axis=0, keepdims=True)


def _hyena_filters(zfeat, w1, b1, w2, b2, w3, b3, freq, absd):
    n, kz = zfeat.shape
    tn = min(n, 1024)
    full = lambda a: pl.BlockSpec(a.shape, lambda i: (0, 0))
    return pl.pallas_call(
        _filter_kernel,
        out_shape=(
            jax.ShapeDtypeStruct((n, HY_FILTER_CH), F32),
            jax.ShapeDtypeStruct((1, HY_FILTER_CH), F32),
        ),
        grid=(n // tn,),
        in_specs=[pl.BlockSpec((tn, kz), lambda i: (i, 0)), full(w1), full(b1), full(w2), full(b2),
                  full(w3), full(b3), full(freq), full(absd)],
        out_specs=(
            pl.BlockSpec((tn, HY_FILTER_CH), lambda i: (i, 0)),
            pl.BlockSpec((1, HY_FILTER_CH), lambda i: (0, 0)),
        ),
        compiler_params=_cparams(("arbitrary",)),
        name="hyena_filters",
    )(zfeat, w1, b1, w2, b2, w3, b3, freq, absd)


def _lmm_kernel(mh_ref, ml_ref, x_ref, o_ref):
    o_ref[0] = _dot3(mh_ref[...], ml_ref[...], x_ref[0])


def _lmm_gate_kernel(mh_ref, ml_ref, x_ref, gate_ref, u_ref, bias_ref, o_ref):
    y = _dot3(mh_ref[...], ml_ref[...], x_ref[0])
    o_ref[0] = gate_ref[0] * (y + u_ref[0] * bias_ref[0])


def _left_matmul(m_hi, m_lo, x, gate=None, u=None, bias=None, *, tc):
    p_n, kr, cols = x.shape
    mr = m_hi.shape[0]
    tc = min(tc, cols)
    mspec = pl.BlockSpec((mr, kr), lambda p, j: (0, 0))
    xspec = pl.BlockSpec((1, kr, tc), lambda p, j: (p, 0, j))
    ospec = pl.BlockSpec((1, mr, tc), lambda p, j: (p, 0, j))
    if gate is None:
        kern, ins, specs = _lmm_kernel, (m_hi, m_lo, x), [mspec, mspec, xspec]
    else:
        kern, ins = _lmm_gate_kernel, (m_hi, m_lo, x, gate, u, bias)
        specs = [mspec, mspec, xspec, ospec, ospec, pl.BlockSpec((1, 1, tc), lambda p, j: (0, 0, j))]
    return pl.pallas_call(
        kern,
        out_shape=jax.ShapeDtypeStruct((p_n, mr, cols), F32),
        grid=(p_n, cols // tc),
        in_specs=specs,
        out_specs=ospec,
        compiler_params=_cparams(("parallel", "parallel")),
        name="dft_outer",
    )(*ins)


def _kf_kernel(gh_ref, gl_ref, a_ref, ss_ref, o_ref, *, fb):
    scale = 1.0 / (ss_ref[0, 0:1, :] + ss_ref[0, 1:2, :] + EPS)
    for j in range(fb):
        a2 = jnp.concatenate([a_ref[0, 0, j], a_ref[0, 1, j]], axis=0)
        o_ref[0, j] = _dot3(gh_ref[j], gl_ref[j], a2) * scale


def _filter_spectrum(g_hi, g_lo, a5, ss, *, fb):
    n_ord, _, n1, n2, c = a5.shape
    kern = functools.partial(_kf_kernel, fb=fb)
    return pl.pallas_call(
        kern,
        out_shape=jax.ShapeDtypeStruct((n_ord, n1, 2 * n2, c), F32),
        grid=(n1 // fb, n_ord),
        in_specs=[
            pl.BlockSpec((fb, 2 * n2, 2 * n2), lambda f, o: (f, 0, 0)),
            pl.BlockSpec((fb, 2 * n2, 2 * n2), lambda f, o: (f, 0, 0)),
            pl.BlockSpec((1, 2, fb, n2, c), lambda f, o: (o, 0, f, 0, 0)),
            pl.BlockSpec((1, 2, c), lambda f, o: (o, 0, 0)),
        ],
        out_specs=pl.BlockSpec((1, fb, 2 * n2, c), lambda f, o: (o, f, 0, 0)),
        compiler_params=_cparams(("parallel", "parallel")),
        name="filter_spectrum",
    )(g_hi, g_lo, a5, ss)


def _spec_kernel(gh_ref, gl_ref, hh_ref, hl_ref, kf_ref, a_ref, o_ref, *, fb, n2):
    for j in range(fb):
        a2 = jnp.concatenate([a_ref[0, 0, j], a_ref[0, 1, j]], axis=0)
        x = _dot3(gh_ref[j], gl_ref[j], a2)
        xr, xi = x[:n2], x[n2:]
        kr, ki = kf_ref[0, j, :n2], kf_ref[0, j, n2:]
        y = jnp.concatenate([xr * kr - xi * ki, xr * ki + xi * kr], axis=0)
        c2 = _dot3(hh_ref[j], hl_ref[j], y)
        o_ref[0, 0, j] = c2[:n2]
        o_ref[0, 1, j] = c2[n2:]


def _spectral_multiply(g_hi, g_lo, h_hi, h_lo, kf, a5, order, *, fb):
    p_n, _, n1, n2, c = a5.shape
    kern = functools.partial(_spec_kernel, fb=fb, n2=n2)
    gspec = pl.BlockSpec((fb, 2 * n2, 2 * n2), lambda f, p: (f, 0, 0))
    aspec = pl.BlockSpec((1, 2, fb, n2, c), lambda f, p: (p, 0, f, 0, 0))
    return pl.pallas_call(
        kern,
        out_shape=jax.ShapeDtypeStruct(a5.shape, F32),
        grid=(n1 // fb, p_n),
        in_specs=[gspec, gspec, gspec, gspec,
                  pl.BlockSpec((1, fb, 2 * n2, c), lambda f, p: (order, f, 0, 0)), aspec],
        out_specs=aspec,
        compiler_params=_cparams(("parallel", "parallel")),
        name="spectral_multiply",
    )(g_hi, g_lo, h_hi, h_lo, kf, a5)


def _ctx_kf_kernel(fh_ref, fl_ref, k_ref, ss_ref, o_ref):
    scale = 1.0 / (ss_ref[0, 0:1, :] + ss_ref[0, 1:2, :] + EPS)
    o_ref[0] = _dot3(fh_ref[...], fl_ref[...], k_ref[0]) * scale


def _ctx_filter_spectrum(f_hi, f_lo, k, ss):
    n_ord, nc, c = k.shape
    return pl.pallas_call(
        _ctx_kf_kernel,
        out_shape=jax.ShapeDtypeStruct((n_ord, 2 * nc, c), F32),
        grid=(n_ord,),
        in_specs=[
            pl.BlockSpec(f_hi.shape, lambda o: (0, 0)),
            pl.BlockSpec(f_lo.shape, lambda o: (0, 0)),
            pl.BlockSpec((1, nc, c), lambda o: (o, 0, 0)),
            pl.BlockSpec((1, 2, c), lambda o: (o, 0, 0)),
        ],
        out_specs=pl.BlockSpec((1, 2 * nc, c), lambda o: (o, 0, 0)),
        compiler_params=_cparams(("parallel",)),
        name="ctx_filter_spectrum",
    )(f_hi, f_lo, k, ss)


def _ctx_conv_kernel(fh_ref, fl_ref, eh_ref, el_ref, kf_ref, u_ref, bias_ref, o_ref, *, nc):
    v, x1, x2 = u_ref[0, 0], u_ref[1, 0], u_ref[2, 0]

    def conv(z, order):
        x = _dot3(fh_ref[...], fl_ref[...], z)
        xr, xi = x[:nc], x[nc:]
        kr, ki = kf_ref[order, :nc], kf_ref[order, nc:]
        y = jnp.concatenate([xr * kr - xi * ki, xr * ki + xi * kr], axis=0)
        return _dot3(eh_ref[...], el_ref[...], y) + z * bias_ref[order:order + 1, :]

    o_ref[0] = x2 * conv(x1 * conv(v, 0), 1)


def _ctx_hyena(f_hi, f_lo, e_hi, e_lo, kf, u3, bias):
    _, p_n, n2x, c = u3.shape
    nc = kf.shape[1] // 2
    kern = functools.partial(_ctx_conv_kernel, nc=nc)
    full = lambda a: pl.BlockSpec(a.shape, lambda p: (0,) * a.ndim)
    return pl.pallas_call(
        kern,
        out_shape=jax.ShapeDtypeStruct((p_n, n2x, c), F32),
        grid=(p_n,),
        in_specs=[full(f_hi), full(f_lo), full(e_hi), full(e_lo), full(kf),
                  pl.BlockSpec((3, 1, n2x, c), lambda p: (0, p, 0, 0)), full(bias)],
        out_specs=pl.BlockSpec((1, n2x, c), lambda p: (p, 0, 0)),
        compiler_params=_cparams(("parallel",)),
        name="ctx_hyena",
    )(f_hi, f_lo, e_hi, e_lo, kf, u3, bias)


def _merge_kernel(x_ref, ya_ref, yh_ref, yd_ref, gh_ref, w_ref, mod_ref, o_ref, *, tm, ctx_len):
    i = pl.program_id(1)
    row = i * tm + lax.broadcasted_iota(jnp.int32, (tm, 1), 0)
    yh = yh_ref[0]
    ms = jnp.mean(yh * yh, axis=-1, keepdims=True)
    yh = yh * lax.rsqrt(ms + EPS) * gh_ref[...]
    y = _dot(ya_ref[0].astype(BF16), w_ref[:A_WIDTH, :])
    y += _dot(yh.astype(BF16), w_ref[A_WIDTH:A_WIDTH + HY_WIDTH, :])
    y += _dot(yd_ref[0].astype(BF16), w_ref[A_WIDTH + HY_WIDTH:, :])
    gate = jnp.where(row < ctx_len, mod_ref[0, 2:3, :], mod_ref[0, 10:11, :])
    o_ref[0] = x_ref[0] + gate * y


def _merge(xs, ya, yh, yd, g_h, w_out, mod, *, ctx_len, tm):
    bsz, s_len, d = xs.shape
    kern = functools.partial(_merge_kernel, tm=tm, ctx_len=ctx_len)
    tok = lambda w: pl.BlockSpec((1, tm, w), lambda b, i: (b, i, 0))
    return pl.pallas_call(
        kern,
        out_shape=jax.ShapeDtypeStruct(xs.shape, F32),
        grid=(bsz, s_len // tm),
        in_specs=[tok(d), tok(A_WIDTH), tok(HY_WIDTH), tok(D_WIDTH),
                  pl.BlockSpec((1, HY_WIDTH), lambda b, i: (0, 0)),
                  pl.BlockSpec((MIX_WIDTH, d), lambda b, i: (0, 0)),
                  pl.BlockSpec((1, 16, d), lambda b, i: (b, 0, 0))],
        out_specs=tok(d),
        compiler_params=_cparams(("parallel", "parallel")),
        name="merge",
    )(xs, ya, yh, yd, g_h, w_out, mod)


HALO = 8


def _ffn_kernel(x_ref, xp_ref, xn_ref, mod_ref, g_ref, wa_ref, wv_ref, cw_ref, cb_ref, wd_ref, o_ref,
                *, tm, fc, ctx_len, s_len):
    i = pl.program_id(1)
    x = x_ref[0]
    xa = jnp.concatenate([xp_ref[0], x, xn_ref[0]], axis=0)
    rows = tm + 2 * HALO
    row = i * tm - HALO + lax.broadcasted_iota(jnp.int32, (rows, 1), 0)
    h = _norm_modulate(xa, g_ref[...], mod_ref, row < ctx_len, 1).astype(BF16)
    hm = h[HALO:HALO + tm]
    rowm = row[HALO:HALO + tm]
    has_prev = jnp.logical_and(rowm != 0, rowm != ctx_len)
    has_next = jnp.logical_and(rowm != ctx_len - 1, rowm != s_len - 1)
    acc = jnp.zeros((tm, x.shape[1]), F32)
    for j in range(wa_ref.shape[1] // fc):
        cs = slice(j * fc, (j + 1) * fc)
        a = _dot(h, wa_ref[:, cs])
        v = _dot(hm, wv_ref[:, cs])
        a_prev = jnp.where(has_prev, pltpu.roll(a, 1, axis=0)[HALO:HALO + tm], 0.0)
        a_next = jnp.where(has_next, pltpu.roll(a, rows - 1, axis=0)[HALO:HALO + tm], 0.0)
        ac = a_prev * cw_ref[0:1, cs] + a[HALO:HALO + tm] * cw_ref[1:2, cs] + a_next * cw_ref[2:3, cs] + cb_ref[:, cs]
        gl = 0.5 * ac * (1.0 + jnp.tanh(math.sqrt(2.0 / math.pi) * (ac + 0.044715 * (ac * ac * ac))))
        acc += _dot((gl * v).astype(BF16), wd_ref[cs, :])
    gate = jnp.where(rowm < ctx_len, mod_ref[0, 5:6, :], mod_ref[0, 13:14, :])
    o_ref[0] = x + gate * acc


def _ffn(xs, mod, g, wa, wv, cw, cb, wd, *, ctx_len, tm, fc):
    bsz, s_len, d = xs.shape
    d_ff = wa.shape[1]
    nh = tm // HALO
    last = s_len // HALO - 1
    kern = functools.partial(_ffn_kernel, tm=tm, fc=fc, ctx_len=ctx_len, s_len=s_len)
    const = lambda shape: pl.BlockSpec(shape, lambda b, i: (0, 0), pipeline_mode=pl.Buffered(1))
    return pl.pallas_call(
        kern,
        out_shape=jax.ShapeDtypeStruct(xs.shape, F32),
        grid=(bsz, s_len // tm),
        in_specs=[
            pl.BlockSpec((1, tm, d), lambda b, i: (b, i, 0)),
            pl.BlockSpec((1, HALO, d), lambda b, i: (b, jnp.maximum(i * nh - 1, 0), 0)),
            pl.BlockSpec((1, HALO, d), lambda b, i: (b, jnp.minimum((i + 1) * nh, last), 0)),
            pl.BlockSpec((1, 16, d), lambda b, i: (b, 0, 0)),
            pl.BlockSpec((1, d), lambda b, i: (0, 0)),
            const((d, d_ff)), const((d, d_ff)), const((3, d_ff)), const((1, d_ff)), const((d_ff, d)),
        ],
        out_specs=pl.BlockSpec((1, tm, d), lambda b, i: (b, i, 0)),
        compiler_params=_cparams(("parallel", "parallel")),
        name="conv_glu",
    )(xs, xs, xs, mod, g, wa, wv, cw, cb, wd)


def _cis(num, den):
    ang = num.astype(F32) * (2.0 * math.pi / den)
    return jnp.cos(ang), jnp.sin(ang)


def _blockmat(re, im):
    return jnp.concatenate([jnp.concatenate([re, -im], axis=-1), jnp.concatenate([im, re], axis=-1)], axis=-2)


def _dft_tables(n):
    n2 = DFT_N2
    t1n = n // n2
    n1 = 2 * t1n
    big = n1 * n2
    f1 = jnp.arange(n1, dtype=jnp.int32)
    t1 = jnp.arange(t1n, dtype=jnp.int32)
    j2 = jnp.arange(n2, dtype=jnp.int32)
    c, s = _cis((f1[:, None] * t1[None, :]) % n1, n1)
    fwd_sig = _blockmat(c, -s)
    c, s = _cis((f1[:, None] * f1[None, :]) % n1, n1)
    fwd_fil = jnp.concatenate([c, -s], axis=0)
    ph = (n1 * j2[None, :, None] * j2[None, None, :] + f1[:, None, None] * j2[None, None, :]) % big
    c, s = _cis(ph, big)
    g = _blockmat(c, -s)
    h = jnp.swapaxes(g, 1, 2)
    c, s = _cis((t1[:, None] * f1[None, :]) % n1, n1)
    inv = _blockmat(c, s) * (1.0 / big)
    return tuple(_split_bf16(m) for m in (fwd_sig, fwd_fil, g, h, inv))


def _dense_dft_tables(n):
    big = 2 * n
    f = jnp.arange(big, dtype=jnp.int32)
    t = jnp.arange(n, dtype=jnp.int32)
    c, s = _cis((f[:, None] * t[None, :]) % big, big)
    fwd_sig = _blockmat(c, -s)
    c, s = _cis((f[:, None] * f[None, :]) % big, big)
    fwd_fil = jnp.concatenate([c, -s], axis=0)
    c, s = _cis((t[:, None] * f[None, :]) % big, big)
    inv = _blockmat(c, s) * (1.0 / big)
    return tuple(_split_bf16(m) for m in (fwd_sig, fwd_fil, inv))


def _filter_features(n):
    pos = jnp.arange(n, dtype=F32)
    t = (pos / max(n - 1, 1))[:, None]
    w = 2.0 * math.pi * pos[:, None] / n
    bands = jnp.linspace(1e-4, HY_BANDS - 1, HY_BANDS, dtype=F32)
    z = jnp.concatenate([t, jnp.cos(bands * w), -jnp.sin(bands * w)], axis=-1)
    return jnp.pad(z, ((0, 0), (0, LANE - HY_EMB_DIM)))


def _circular_filters(h, big):
    n = h.shape[0]
    fwd = jnp.moveaxis(h[:, 0], 1, 0)
    bwd = jnp.moveaxis(h[:, 1], 1, 0)
    zeros = jnp.zeros((fwd.shape[0], big - 2 * n + 1, fwd.shape[2]), F32)
    return jnp.concatenate([fwd, zeros, bwd[:, :0:-1]], axis=1)


def _pick(total, candidates):
    for t in candidates:
        if total % t == 0:
            return t
    raise ValueError(f"no tile for {total}")


def kernel(x, c, ctx, c_ctx, norm1_g, norm2_g, w_mod, b_mod, w_in, qn_a, kn_a, qn_d, kn_d, lam_q1, lam_k1, lam_q2, lam_k2, hy_conv_w, hy_conv_b, hy_fw1, hy_fb1, hy_fw2, hy_fb2, hy_fw3, hy_fb3, hy_freq, hy_bias, g_out, w_out, w_up, ffn_conv_w, ffn_conv_b, w_down):
    bsz, n_tok, d = x.shape
    ctx_len = ctx.shape[1]
    depth = w_in.shape[0]
    s_len = ctx_len + n_tok
    d_ff = w_down.shape[1]
    assert bsz % 2 == 0 and n_tok % (DFT_N2 * 8) == 0 and ctx_len % LANE == 0

    tm = _pick(s_len, (768, 512, 256, 128))
    tq = _pick(ctx_len, (256, 128))
    tk = _pick(s_len, (768, 512, 256, 128))
    fc = _pick(d_ff, (512, 256, 128))

    n_rows = n_tok // GRID_W
    rows = jnp.repeat(jnp.arange(n_rows, dtype=F32), GRID_W)
    cols = jnp.tile(jnp.arange(GRID_W, dtype=F32), n_rows)
    n_freq = HEAD_DIM // 4
    inv = ROPE_THETA ** (-jnp.arange(n_freq, dtype=F32) / n_freq)
    ang = jnp.concatenate([rows[:, None] * inv, cols[:, None] * inv], axis=-1)
    ang = jnp.concatenate([jnp.zeros((ctx_len, HALF), F32), ang], axis=0)
    cosT, sinT = jnp.cos(ang).T, jnp.sin(ang).T

    sig_f, fil_f, g_tab, h_tab, inv_f = _dft_tables(n_tok)
    csig_f, cfil_f, cinv_f = _dense_dft_tables(ctx_len)
    z_lat = _filter_features(n_tok)
    z_ctx = _filter_features(ctx_len)
    max_decay = math.log(HY_TARGET) / HY_FAST_DECAY
    min_decay = math.log(HY_TARGET) / HY_SLOW_DECAY
    absd = jnp.tile(jnp.abs(jnp.linspace(min_decay, max_decay, HY_WIDTH, dtype=F32)), 4)[None, :]
    t1n = n_tok // DFT_N2
    n1 = 2 * t1n
    big = n1 * DFT_N2
    fb = 8

    perm = np.concatenate([np.arange(0, HEAD_DIM, 2), np.arange(1, HEAD_DIM, 2)])
    qk_cols = np.concatenate(
        [OFF_QA + b * HEAD_DIM + perm for b in range(A_HEADS)]
        + [OFF_QD + b * HEAD_DIM + perm for b in range(2 * D_HEADS)]
        + [OFF_KA + b * HEAD_DIM + perm for b in range(A_KV_HEADS)]
        + [OFF_KD + b * HEAD_DIM + perm for b in range(2 * D_HEADS)])
    v_cols = np.concatenate([np.arange(OFF_VA, OFF_VA + A_KV_WIDTH), np.arange(OFF_VD, OFF_VD + D_WIDTH)])
    qkv_cols = np.concatenate([qk_cols, v_cols])
    wqT = jnp.swapaxes(w_in[:, :, qkv_cols], 1, 2).astype(BF16)
    why = w_in[:, :, OFF_HY:OFF_HY + HY_IN].astype(BF16)
    q_scale = HEAD_DIM ** -0.5 * math.log2(math.e)
    gains = jnp.concatenate(
        [jnp.tile(qn_a[:, perm], (1, A_HEADS)) * q_scale, jnp.tile(qn_d[:, perm], (1, 2 * D_HEADS)) * q_scale,
         jnp.tile(kn_a[:, perm], (1, A_KV_HEADS)), jnp.tile(kn_d[:, perm], (1, 2 * D_HEADS))],
        axis=1)[:, :, None].astype(F32)
    w_out_b = w_out.astype(BF16)
    w_up_a = w_up[:, :, :d_ff].astype(BF16)
    w_up_v = w_up[:, :, d_ff:].astype(BF16)
    w_down_b = w_down.astype(BF16)
    fw1p = jnp.pad(hy_fw1, ((0, 0), (0, LANE - HY_EMB_DIM), (0, 0)))
    lamv = jnp.stack([lam_q1, lam_k1, lam_q2, lam_k2], axis=1)
    lam_init = jnp.asarray([0.8 - 0.6 * math.exp(-0.3 * i) for i in range(depth)], F32)
    lami = jnp.broadcast_to(lam_init[:, None, None], (depth, 8, LANE))

    n_c = 8 * ((bsz + 1 + 7) // 8)
    cvec = jnp.concatenate([c, c_ctx[None, :], jnp.zeros((n_c - bsz - 1, d), F32)], axis=0)
    mods = _modulation(cvec, w_mod, b_mod).reshape(depth, n_c, 6, d)
    pad2 = jnp.zeros((depth, bsz, 2, d), F32)
    mod_all = jnp.concatenate(
        [jnp.broadcast_to(mods[:, bsz:bsz + 1], (depth, bsz, 6, d)), pad2, mods[:, :bsz], pad2], axis=2)

    xs = jnp.concatenate([ctx, x], axis=1)
    for i in range(depth):
        mod = mod_all[i]
        qT, k, vT, hy = _projection(xs, mod, norm1_g[i][None, :], wqT[i], why[i], gains[i], cosT, sinT,
                                    ctx_len=ctx_len, tm=tm)
        ya = _attention_a(qT, k, vT, g_out[i, :A_WIDTH, None], ctx_len=ctx_len, tq=tq, tk=tk)
        yd = _attention_d(qT, k, vT, g_out[i, A_WIDTH + HY_WIDTH:].reshape(D_HEADS, D_VALUE_DIM, 1),
                          lamv[i], lami[i], ctx_len=ctx_len, tq=tq, tk=tk)

        u_lat, u_ctx = _hyena_short_conv(hy, hy_conv_w[i], hy_conv_b[i], ctx_len=ctx_len)
        fil_args = (fw1p[i], hy_fb1[i][None], hy_fw2[i], hy_fb2[i][None], hy_fw3[i], hy_fb3[i][None],
                    hy_freq[i][None], absd)
        h_lat, sum_lat = _hyena_filters(z_lat, *fil_args)
        h_ctx, sum_ctx = _hyena_filters(z_ctx, *fil_args)
        ss_lat = jnp.swapaxes(sum_lat.reshape(2, 2, HY_WIDTH), 0, 1)
        ss_ctx = jnp.swapaxes(sum_ctx.reshape(2, 2, HY_WIDTH), 0, 1)
        k_lat = _circular_filters(h_lat.reshape(n_tok, 2, 2, HY_WIDTH), big)
        k_ctx = _circular_filters(h_ctx.reshape(ctx_len, 2, 2, HY_WIDTH), 2 * ctx_len)

        cols_n = DFT_N2 * HY_WIDTH
        ak = _left_matmul(fil_f[0], fil_f[1], k_lat.reshape(2, n1, cols_n), tc=2048)
        kf = _filter_spectrum(g_tab[0], g_tab[1], ak.reshape(2, 2, n1, DFT_N2, HY_WIDTH), ss_lat, fb=fb)
        p_n = bsz // 2
        v_in = u_lat[0].reshape(p_n, 2 * t1n, cols_n)
        x1_in = u_lat[1].reshape(p_n, 2 * t1n, cols_n)
        x2_in = u_lat[2].reshape(p_n, 2 * t1n, cols_n)
        bias_t = jnp.tile(hy_bias[i], (1, DFT_N2))[:, None, :]

        def long_conv(z_in, gate, order):
            a = _left_matmul(sig_f[0], sig_f[1], z_in, tc=2048)
            cc = _spectral_multiply(g_tab[0], g_tab[1], h_tab[0], h_tab[1], kf,
                                    a.reshape(p_n, 2, n1, DFT_N2, HY_WIDTH), order, fb=fb)
            return _left_matmul(inv_f[0], inv_f[1], cc.reshape(p_n, 2 * n1, cols_n),
                                gate, z_in, bias_t[order:order + 1], tc=2048)

        yh_lat = long_conv(long_conv(v_in, x1_in, 0), x2_in, 1).reshape(bsz, n_tok, HY_WIDTH)

        kf_c = _ctx_filter_spectrum(cfil_f[0], cfil_f[1], k_ctx, ss_ctx)
        yh_ctx = _ctx_hyena(csig_f[0], csig_f[1], cinv_f[0], cinv_f[1], kf_c,
                            u_ctx.reshape(3, p_n, 2 * ctx_len, HY_WIDTH), hy_bias[i])
        yh = jnp.concatenate([yh_ctx.reshape(bsz, ctx_len, HY_WIDTH), yh_lat], axis=1)

        xs = _merge(xs, ya, yh, yd, g_out[i, A_WIDTH:A_WIDTH + HY_WIDTH][None, :], w_out_b[i], mod,
                    ctx_len=ctx_len, tm=tm)
        xs = _ffn(xs, mod, norm2_g[i][None, :], w_up_a[i], w_up_v[i], ffn_conv_w[i], ffn_conv_b[i][None, :],
                  w_down_b[i], ctx_len=ctx_len, tm=tm, fc=fc)
    return xs[:, ctx_len:]
```

```python
import functools
import math

import numpy as np
import jax
import jax.numpy as jnp
from jax import lax
from jax.experimental import pallas as pl
from jax.experimental.pallas import tpu as pltpu

F32 = jnp.float32
BF16 = jnp.bfloat16

HEAD_DIM = 64
HALF = HEAD_DIM // 2
A_HEADS = 4
A_KV_HEADS = 2
A_WIDTH = A_HEADS * HEAD_DIM
A_KV_WIDTH = A_KV_HEADS * HEAD_DIM
HY_WIDTH = 256
HY_IN = 3 * HY_WIDTH
HY_BANDS = 16
HY_EMB_DIM = 1 + 2 * HY_BANDS
HY_HIDDEN = 64
HY_FILTER_CH = 4 * HY_WIDTH
HY_FAST_DECAY = 0.3
HY_SLOW_DECAY = 1.5
HY_TARGET = 1e-2
D_HEADS = 4
D_VALUE_DIM = 2 * HEAD_DIM
D_WIDTH = D_HEADS * D_VALUE_DIM
D_QK_WIDTH = D_HEADS * 2 * HEAD_DIM
MIX_WIDTH = A_WIDTH + HY_WIDTH + D_WIDTH
OFF_QA = 0
OFF_KA = A_WIDTH
OFF_VA = OFF_KA + A_KV_WIDTH
OFF_HY = OFF_VA + A_KV_WIDTH
OFF_QD = OFF_HY + HY_IN
OFF_KD = OFF_QD + D_QK_WIDTH
OFF_VD = OFF_KD + D_QK_WIDTH
GRID_W = 64
ROPE_THETA = 10000.0
EPS = 1e-6

N_QBLK = A_HEADS + 2 * D_HEADS
N_KBLK = A_KV_HEADS + 2 * D_HEADS
Q_ROWS = N_QBLK * HEAD_DIM
K_ROWS = N_KBLK * HEAD_DIM
V_ROWS = A_KV_WIDTH + D_WIDTH
QK_ROWS = Q_ROWS + K_ROWS
QKV_ROWS = QK_ROWS + V_ROWS

LANE = 128
DFT_N2 = 128
VMEM_LIMIT = 56 * 1024 * 1024
HIGHEST = lax.Precision.HIGHEST


def _cparams(sem):
    return pltpu.CompilerParams(dimension_semantics=sem, vmem_limit_bytes=VMEM_LIMIT)


def _split_bf16(x):
    hi = x.astype(BF16)
    lo = (x - hi.astype(F32)).astype(BF16)
    return hi, lo


def _dot(a, b):
    return jnp.dot(a, b, preferred_element_type=F32)


def _dot3(a_hi, a_lo, b):
    b_hi, b_lo = _split_bf16(b)
    return _dot(a_hi, b_hi) + (_dot(a_hi, b_lo) + _dot(a_lo, b_hi))


def _mod_kernel(c_ref, w_ref, b_ref, o_ref):
    c = c_ref[...]
    s = c * (1.0 / (1.0 + jnp.exp(-c)))
    o_ref[0] = jnp.dot(s, w_ref[0], preferred_element_type=F32, precision=HIGHEST) + b_ref[0]


def _modulation(cvec, w_mod, b_mod):
    depth, d, n6 = w_mod.shape
    tn = 1024
    return pl.pallas_call(
        _mod_kernel,
        out_shape=jax.ShapeDtypeStruct((depth, cvec.shape[0], n6), F32),
        grid=(depth, n6 // tn),
        in_specs=[
            pl.BlockSpec(cvec.shape, lambda l, j: (0, 0)),
            pl.BlockSpec((1, d, tn), lambda l, j: (l, 0, j)),
            pl.BlockSpec((1, 1, tn), lambda l, j: (l, 0, j)),
        ],
        out_specs=pl.BlockSpec((1, cvec.shape[0], tn), lambda l, j: (l, 0, j)),
        compiler_params=_cparams(("arbitrary", "arbitrary")),
        name="modulation",
    )(cvec, w_mod, b_mod.reshape(depth, 1, n6))


def _norm_modulate(x, g, mod_ref, is_ctx, which):
    ms = jnp.mean(x * x, axis=-1, keepdims=True)
    xn = x * lax.rsqrt(ms + EPS) * g
    sh = jnp.where(is_ctx, mod_ref[0, 3 * which:3 * which + 1, :], mod_ref[0, 8 + 3 * which:9 + 3 * which, :])
    sc = jnp.where(is_ctx, mod_ref[0, 3 * which + 1:3 * which + 2, :],
                   mod_ref[0, 9 + 3 * which:10 + 3 * which, :])
    return xn * (1.0 + sc) + sh


def _proj_kernel(x_ref, mod_ref, g_ref, wq_ref, why_ref, gain_ref, cos_ref, sin_ref,
                 qT_ref, k_ref, vT_ref, hy_ref, *, tm, ctx_len):
    i = pl.program_id(1)
    row = i * tm + lax.broadcasted_iota(jnp.int32, (tm, 1), 0)
    h = _norm_modulate(x_ref[0], g_ref[...], mod_ref, row < ctx_len, 0).astype(BF16)
    pT = lax.dot_general(wq_ref[...], h, (((1,), (1,)), ((), ())), preferred_element_type=F32)
    hy_ref[0] = _dot(h, why_ref[...])
    c = cos_ref[...]
    s = sin_ref[...]
    blocks = []
    for b in range(N_QBLK + N_KBLK):
        blk = pT[b * HEAD_DIM:(b + 1) * HEAD_DIM]
        ssq = jnp.sum(blk * blk, axis=0, keepdims=True)
        blk = blk * lax.rsqrt(ssq * (1.0 / HEAD_DIM) + EPS) * gain_ref[b * HEAD_DIM:(b + 1) * HEAD_DIM, :]
        x1 = blk[:HALF]
        x2 = blk[HALF:]
        blocks.append(x1 * c - x2 * s)
        blocks.append(x1 * s + x2 * c)
    qT_ref[0] = jnp.concatenate(blocks[:2 * N_QBLK], axis=0).astype(BF16)
    kT = jnp.concatenate(blocks[2 * N_QBLK:], axis=0)
    k_ref[0] = kT.T.astype(BF16)
    vT_ref[0] = pT[QK_ROWS:].astype(BF16)


def _projection(xs, mod, g, wqT, why, gains, cosT, sinT, *, ctx_len, tm):
    bsz, s_len, d = xs.shape
    kern = functools.partial(_proj_kernel, tm=tm, ctx_len=ctx_len)
    return pl.pallas_call(
        kern,
        out_shape=(
            jax.ShapeDtypeStruct((bsz, Q_ROWS, s_len), BF16),
            jax.ShapeDtypeStruct((bsz, s_len, K_ROWS), BF16),
            jax.ShapeDtypeStruct((bsz, V_ROWS, s_len), BF16),
            jax.ShapeDtypeStruct((bsz, s_len, HY_IN), F32),
        ),
        grid=(bsz, s_len // tm),
        in_specs=[
            pl.BlockSpec((1, tm, d), lambda b, i: (b, i, 0)),
            pl.BlockSpec((1, 16, d), lambda b, i: (b, 0, 0)),
            pl.BlockSpec((1, d), lambda b, i: (0, 0)),
            pl.BlockSpec((QKV_ROWS, d), lambda b, i: (0, 0)),
            pl.BlockSpec((d, HY_IN), lambda b, i: (0, 0)),
            pl.BlockSpec((QK_ROWS, 1), lambda b, i: (0, 0)),
            pl.BlockSpec((HALF, tm), lambda b, i: (0, i)),
            pl.BlockSpec((HALF, tm), lambda b, i: (0, i)),
        ],
        out_specs=(
            pl.BlockSpec((1, Q_ROWS, tm), lambda b, i: (b, 0, i)),
            pl.BlockSpec((1, tm, K_ROWS), lambda b, i: (b, i, 0)),
            pl.BlockSpec((1, V_ROWS, tm), lambda b, i: (b, 0, i)),
            pl.BlockSpec((1, tm, HY_IN), lambda b, i: (b, i, 0)),
        ),
        compiler_params=_cparams(("parallel", "parallel")),
        name="projection",
    )(xs, mod, g, wqT, why, gains, cosT, sinT)


def _attend(k_ref, vT_ref, v_rows, scr, *, ctx_only, ctx_len, tk):
    qz_ref, m_ref, l_ref, acc_ref, al_ref, s_ref, p_ref = scr
    n_u = len(v_rows)
    s_len = k_ref.shape[1]
    m_ref[...] = jnp.full(m_ref.shape, -1e30, F32)
    l_ref[...] = jnp.zeros(l_ref.shape, F32)
    acc_ref[...] = jnp.zeros(acc_ref.shape, F32)

    def scores(u, start, size):
        return _dot(k_ref[0, pl.ds(start, size), :], qz_ref[u])

    def softmax(u, s):
        m = m_ref[u]
        m_new = jnp.maximum(m, jnp.max(s, axis=0, keepdims=True))
        p = jnp.exp2(s - m_new)
        alpha = jnp.exp2(m - m_new)
        l_ref[u] = alpha * l_ref[u] + jnp.sum(p, axis=0, keepdims=True)
        m_ref[u] = m_new
        return alpha, p.astype(BF16)

    def accumulate(u, alpha, p, start, size):
        v0, dv = v_rows[u]
        pv = _dot(vT_ref[0, v0:v0 + dv, pl.ds(start, size)], p)
        acc_ref[u, :dv, :] = alpha * acc_ref[u, :dv, :] + pv

    if ctx_only:
        for u in range(n_u):
            alpha, p = softmax(u, scores(u, 0, ctx_len))
            accumulate(u, alpha, p, 0, ctx_len)
    else:
        n_chunks = s_len // tk
        s_ref[0] = scores(0, 0, tk)
        p_ref[(n_u - 1) % 2] = jnp.zeros(p_ref.shape[1:], BF16)
        al_ref[n_u - 1] = jnp.ones(al_ref.shape[1:], F32)

        def body(c, carry):
            start = pl.multiple_of(c * tk, tk)
            nxt = pl.multiple_of(jnp.minimum(c + 1, n_chunks - 1) * tk, tk)
            prv = pl.multiple_of(jnp.maximum(c - 1, 0) * tk, tk)
            for u in range(n_u):
                if u + 1 < n_u:
                    s_ref[(u + 1) % 2] = scores(u + 1, start, tk)
                else:
                    s_ref[(u + 1) % 2] = scores(0, nxt, tk)
                up = (u - 1) % n_u
                accumulate(up, al_ref[up], p_ref[(u + 1) % 2], start if u > 0 else prv, tk)
                alpha, p = softmax(u, s_ref[u % 2])
                al_ref[u] = alpha
                p_ref[u % 2] = p
            return carry
        lax.fori_loop(0, n_chunks, body, 0)
        accumulate(n_u - 1, al_ref[n_u - 1], p_ref[(n_u - 1) % 2], (n_chunks - 1) * tk, tk)
    return [acc_ref[u, :v_rows[u][1], :] * (1.0 / l_ref[u]) for u in range(n_u)]


def _attn_scratch(n_u, dv, tq, tk):
    return [pltpu.VMEM((n_u, LANE, tq), BF16), pltpu.VMEM((n_u, 1, tq), F32), pltpu.VMEM((n_u, 1, tq), F32),
            pltpu.VMEM((n_u, dv, tq), F32), pltpu.VMEM((n_u, 1, tq), F32),
            pltpu.VMEM((2, tk, tq), F32), pltpu.VMEM((2, tk, tq), BF16)]


def _attn_a_kernel(qT_ref, k_ref, vT_ref, g_ref, o_ref, *scr, ctx_len, tq, tk):
    qi = pl.program_id(1)
    qz_ref = scr[0]
    zeros = jnp.zeros((HEAD_DIM, tq), BF16)
    group = A_HEADS // A_KV_HEADS
    for h in range(A_HEADS):
        q = qT_ref[0, h * HEAD_DIM:(h + 1) * HEAD_DIM, :]
        qz_ref[h] = jnp.concatenate([q, zeros] if h // group == 0 else [zeros, q], axis=0)
    v_rows = [((h // group) * HEAD_DIM, HEAD_DIM) for h in range(A_HEADS)]

    def run(ctx_only):
        outs = _attend(k_ref, vT_ref, v_rows, scr, ctx_only=ctx_only, ctx_len=ctx_len, tk=tk)
        y = jnp.concatenate(outs, axis=0)
        ms = jnp.mean(y * y, axis=0, keepdims=True)
        y = y * lax.rsqrt(ms + EPS) * g_ref[...]
        o_ref[0] = y.T

    n_ctx_tiles = ctx_len // tq

    @pl.when(qi < n_ctx_tiles)
    def _():
        run(True)

    @pl.when(qi >= n_ctx_tiles)
    def _():
        run(False)


def _attention_a(qT, k, vT, g_a, *, ctx_len, tq, tk):
    bsz, _, s_len = qT.shape
    kern = functools.partial(_attn_a_kernel, ctx_len=ctx_len, tq=tq, tk=tk)
    return pl.pallas_call(
        kern,
        out_shape=jax.ShapeDtypeStruct((bsz, s_len, A_WIDTH), F32),
        grid=(bsz, s_len // tq),
        in_specs=[
            pl.BlockSpec((1, A_WIDTH, tq), lambda b, i: (b, 0, i)),
            pl.BlockSpec((1, s_len, LANE), lambda b, i: (b, 0, 0)),
            pl.BlockSpec((1, A_KV_WIDTH, s_len), lambda b, i: (b, 0, 0)),
            pl.BlockSpec((A_WIDTH, 1), lambda b, i: (0, 0)),
        ],
        out_specs=pl.BlockSpec((1, tq, A_WIDTH), lambda b, i: (b, i, 0)),
        scratch_shapes=_attn_scratch(A_HEADS, HEAD_DIM, tq, tk),
        compiler_params=_cparams(("parallel", "parallel")),
        name="attention_gqa",
    )(qT, k, vT, g_a)


def _attn_d_kernel(qT_ref, k_ref, vT_ref, g_ref, lamv_ref, lami_ref, o_ref, *scr, ctx_len, tq, tk):
    qi = pl.program_id(2)
    qz_ref = scr[0]
    zeros = jnp.zeros((HEAD_DIM, tq), BF16)
    lv = lamv_ref[...]
    lam_init = lami_ref[0:1, 0:1]
    lam = (jnp.exp(jnp.sum(lv[0:1] * lv[1:2], axis=-1, keepdims=True))
           - jnp.exp(jnp.sum(lv[2:3] * lv[3:4], axis=-1, keepdims=True)) + lam_init)

    qz_ref[0] = jnp.concatenate([qT_ref[0, :HEAD_DIM, :], zeros], axis=0)
    qz_ref[1] = jnp.concatenate([zeros, qT_ref[0, HEAD_DIM:, :]], axis=0)
    v_rows = [(0, D_VALUE_DIM), (0, D_VALUE_DIM)]

    def run(ctx_only):
        o1, o2 = _attend(k_ref, vT_ref, v_rows, scr, ctx_only=ctx_only, ctx_len=ctx_len, tk=tk)
        y = o1 - lam * o2
        ms = jnp.mean(y * y, axis=0, keepdims=True)
        y = y * lax.rsqrt(ms + EPS) * g_ref[0] * (1.0 - lam_init)
        o_ref[0] = y.T

    n_ctx_tiles = ctx_len // tq

    @pl.when(qi < n_ctx_tiles)
    def _():
        run(True)

    @pl.when(qi >= n_ctx_tiles)
    def _():
        run(False)


def _attention_d(qT, k, vT, g_d, lamv, lami, *, ctx_len, tq, tk):
    bsz, _, s_len = qT.shape
    kern = functools.partial(_attn_d_kernel, ctx_len=ctx_len, tq=tq, tk=tk)
    q_blk0 = A_WIDTH // LANE
    k_blk0 = A_KV_WIDTH // LANE
    v_blk0 = A_KV_WIDTH // LANE
    return pl.pallas_call(
        kern,
        out_shape=jax.ShapeDtypeStruct((bsz, s_len, D_WIDTH), F32),
        grid=(bsz, D_HEADS, s_len // tq),
        in_specs=[
            pl.BlockSpec((1, LANE, tq), lambda b, h, i: (b, q_blk0 + h, i)),
            pl.BlockSpec((1, s_len, LANE), lambda b, h, i: (b, 0, k_blk0 + h)),
            pl.BlockSpec((1, D_VALUE_DIM, s_len), lambda b, h, i: (b, v_blk0 + h, 0)),
            pl.BlockSpec((1, D_VALUE_DIM, 1), lambda b, h, i: (h, 0, 0)),
            pl.BlockSpec((4, HEAD_DIM), lambda b, h, i: (0, 0)),
            pl.BlockSpec((8, LANE), lambda b, h, i: (0, 0)),
        ],
        out_specs=pl.BlockSpec((1, tq, D_VALUE_DIM), lambda b, h, i: (b, i, h)),
        scratch_shapes=_attn_scratch(2, D_VALUE_DIM, tq, tk),
        compiler_params=_cparams(("parallel", "parallel", "parallel")),
        name="attention_diff",
    )(qT, k, vT, g_d, lamv, lami)


def _dwconv3_seq(u, w_ref, b_ref):
    n = u.shape[0]
    row = lax.broadcasted_iota(jnp.int32, (n, 1), 0)
    prev = jnp.where(row == 0, 0.0, pltpu.roll(u, 1, axis=0))
    nxt = jnp.where(row == n - 1, 0.0, pltpu.roll(u, n - 1, axis=0))
    return prev * w_ref[0:1, :] + u * w_ref[1:2, :] + nxt * w_ref[2:3, :] + b_ref[...]


def _hy_conv_kernel(hy_ref, w_ref, b_ref, lat_ref, ctx_ref, *, ctx_len):
    ctx_ref[0, 0] = _dwconv3_seq(hy_ref[0, :ctx_len, :], w_ref, b_ref)
    lat_ref[0, 0] = _dwconv3_seq(hy_ref[0, ctx_len:, :], w_ref, b_ref)


def _hyena_short_conv(hy, w, b, *, ctx_len):
    bsz, s_len, _ = hy.shape
    n = s_len - ctx_len
    per = HY_WIDTH // LANE
    kern = functools.partial(_hy_conv_kernel, ctx_len=ctx_len)
    return pl.pallas_call(
        kern,
        out_shape=(
            jax.ShapeDtypeStruct((3, bsz, n, HY_WIDTH), F32),
            jax.ShapeDtypeStruct((3, bsz, ctx_len, HY_WIDTH), F32),
        ),
        grid=(bsz, HY_IN // LANE),
        in_specs=[
            pl.BlockSpec((1, s_len, LANE), lambda b, j: (b, 0, j)),
            pl.BlockSpec((3, LANE), lambda b, j: (0, j)),
            pl.BlockSpec((1, LANE), lambda b, j: (0, j)),
        ],
        out_specs=(
            pl.BlockSpec((1, 1, n, LANE), lambda b, j: (j // per, b, 0, j % per)),
            pl.BlockSpec((1, 1, ctx_len, LANE), lambda b, j: (j // per, b, 0, j % per)),
        ),
        compiler_params=_cparams(("parallel", "parallel")),
        name="hyena_short_conv",
    )(hy, w, b.reshape(1, HY_IN))


def _filter_kernel(z_ref, w1_ref, b1_ref, w2_ref, b2_ref, w3_ref, b3_ref, fr_ref, ad_ref, h_ref, sum_ref):
    i = pl.program_id(0)
    z = z_ref[...]
    fr = fr_ref[...]
    h = jnp.sin(fr * (jnp.dot(z, w1_ref[...], preferred_element_type=F32, precision=HIGHEST) + b1_ref[...]))
    h = jnp.sin(fr * (jnp.dot(h, w2_ref[...], preferred_element_type=F32, precision=HIGHEST) + b2_ref[...]))
    h = jnp.dot(h, w3_ref[...], preferred_element_type=F32, precision=HIGHEST) + b3_ref[...]
    h = h * jnp.exp(-z[:, 0:1] * ad_ref[...])
    h_ref[...] = h

    @pl.when(i == 0)
    def _():
        sum_ref[...] = jnp.zeros_like(sum_ref)

    sum_ref[...] += jnp.sum(jnp.abs(h), axis=0, keepdims=True)


def _hyena_filters(zfeat, w1, b1, w2, b2, w3, b3, freq, absd):
    n, kz = zfeat.shape
    tn = min(n, 1024)
    full = lambda a: pl.BlockSpec(a.shape, lambda i: (0, 0))
    return pl.pallas_call(
        _filter_kernel,
        out_shape=(
            jax.ShapeDtypeStruct((n, HY_FILTER_CH), F32),
            jax.ShapeDtypeStruct((1, HY_FILTER_CH), F32),
        ),
        grid=(n // tn,),
        in_specs=[pl.BlockSpec((tn, kz), lambda i: (i, 0)), full(w1), full(b1), full(w2), full(b2),
                  full(w3), full(b3), full(freq), full(absd)],
        out_specs=(
            pl.BlockSpec((tn, HY_FILTER_CH), lambda i: (i, 0)),
            pl.BlockSpec((1, HY_FILTER_CH), lambda i: (0, 0)),
        ),
        compiler_params=_cparams(("arbitrary",)),
        name="hyena_filters",
    )(zfeat, w1, b1, w2, b2, w3, b3, freq, absd)


def _lmm_kernel(mh_ref, ml_ref, x_ref, o_ref):
    o_ref[0] = _dot3(mh_ref[...], ml_ref[...], x_ref[0])


def _lmm_gate_kernel(mh_ref, ml_ref, x_ref, gate_ref, u_ref, bias_ref, o_ref):
    y = _dot3(mh_ref[...], ml_ref[...], x_ref[0])
    o_ref[0] = gate_ref[0] * (y + u_ref[0] * bias_ref[0])


def _left_matmul(m_hi, m_lo, x, gate=None, u=None, bias=None, *, tc):
    p_n, kr, cols = x.shape
    mr = m_hi.shape[0]
    tc = min(tc, cols)
    mspec = pl.BlockSpec((mr, kr), lambda p, j: (0, 0))
    xspec = pl.BlockSpec((1, kr, tc), lambda p, j: (p, 0, j))
    ospec = pl.BlockSpec((1, mr, tc), lambda p, j: (p, 0, j))
    if gate is None:
        kern, ins, specs = _lmm_kernel, (m_hi, m_lo, x), [mspec, mspec, xspec]
    else:
        kern, ins = _lmm_gate_kernel, (m_hi, m_lo, x, gate, u, bias)
        specs = [mspec, mspec, xspec, ospec, ospec, pl.BlockSpec((1, 1, tc), lambda p, j: (0, 0, j))]
    return pl.pallas_call(
        kern,
        out_shape=jax.ShapeDtypeStruct((p_n, mr, cols), F32),
        grid=(p_n, cols // tc),
        in_specs=specs,
        out_specs=ospec,
        compiler_params=_cparams(("parallel", "parallel")),
        name="dft_outer",
    )(*ins)


def _kf_kernel(gh_ref, gl_ref, a_ref, ss_ref, o_ref, *, fb):
    scale = 1.0 / (ss_ref[0, 0:1, :] + ss_ref[0, 1:2, :] + EPS)
    for j in range(fb):
        a2 = jnp.concatenate([a_ref[0, 0, j], a_ref[0, 1, j]], axis=0)
        o_ref[0, j] = _dot3(gh_ref[j], gl_ref[j], a2) * scale


def _filter_spectrum(g_hi, g_lo, a5, ss, *, fb):
    n_ord, _, n1, n2, c = a5.shape
    kern = functools.partial(_kf_kernel, fb=fb)
    return pl.pallas_call(
        kern,
        out_shape=jax.ShapeDtypeStruct((n_ord, n1, 2 * n2, c), F32),
        grid=(n1 // fb, n_ord),
        in_specs=[
            pl.BlockSpec((fb, 2 * n2, 2 * n2), lambda f, o: (f, 0, 0)),
            pl.BlockSpec((fb, 2 * n2, 2 * n2), lambda f, o: (f, 0, 0)),
            pl.BlockSpec((1, 2, fb, n2, c), lambda f, o: (o, 0, f, 0, 0)),
            pl.BlockSpec((1, 2, c), lambda f, o: (o, 0, 0)),
        ],
        out_specs=pl.BlockSpec((1, fb, 2 * n2, c), lambda f, o: (o, f, 0, 0)),
        compiler_params=_cparams(("parallel", "parallel")),
        name="filter_spectrum",
    )(g_hi, g_lo, a5, ss)


def _spec_kernel(gh_ref, gl_ref, hh_ref, hl_ref, kf_ref, a_ref, o_ref, *, fb, n2):
    for j in range(fb):
        a2 = jnp.concatenate([a_ref[0, 0, j], a_ref[0, 1, j]], axis=0)
        x = _dot3(gh_ref[j], gl_ref[j], a2)
        xr, xi = x[:n2], x[n2:]
        kr, ki = kf_ref[0, j, :n2], kf_ref[0, j, n2:]
        y = jnp.concatenate([xr * kr - xi * ki, xr * ki + xi * kr], axis=0)
        c2 = _dot3(hh_ref[j], hl_ref[j], y)
        o_ref[0, 0, j] = c2[:n2]
        o_ref[0, 1, j] = c2[n2:]


def _spectral_multiply(g_hi, g_lo, h_hi, h_lo, kf, a5, order, *, fb):
    p_n, _, n1, n2, c = a5.shape
    kern = functools.partial(_spec_kernel, fb=fb, n2=n2)
    gspec = pl.BlockSpec((fb, 2 * n2, 2 * n2), lambda f, p: (f, 0, 0))
    aspec = pl.BlockSpec((1, 2, fb, n2, c), lambda f, p: (p, 0, f, 0, 0))
    return pl.pallas_call(
        kern,
        out_shape=jax.ShapeDtypeStruct(a5.shape, F32),
        grid=(n1 // fb, p_n),
        in_specs=[gspec, gspec, gspec, gspec,
                  pl.BlockSpec((1, fb, 2 * n2, c), lambda f, p: (order, f, 0, 0)), aspec],
        out_specs=aspec,
        compiler_params=_cparams(("parallel", "parallel")),
        name="spectral_multiply",
    )(g_hi, g_lo, h_hi, h_lo, kf, a5)


def _ctx_kf_kernel(fh_ref, fl_ref, k_ref, ss_ref, o_ref):
    scale = 1.0 / (ss_ref[0, 0:1, :] + ss_ref[0, 1:2, :] + EPS)
    o_ref[0] = _dot3(fh_ref[...], fl_ref[...], k_ref[0]) * scale


def _ctx_filter_spectrum(f_hi, f_lo, k, ss):
    n_ord, nc, c = k.shape
    return pl.pallas_call(
        _ctx_kf_kernel,
        out_shape=jax.ShapeDtypeStruct((n_ord, 2 * nc, c), F32),
        grid=(n_ord,),
        in_specs=[
            pl.BlockSpec(f_hi.shape, lambda o: (0, 0)),
            pl.BlockSpec(f_lo.shape, lambda o: (0, 0)),
            pl.BlockSpec((1, nc, c), lambda o: (o, 0, 0)),
            pl.BlockSpec((1, 2, c), lambda o: (o, 0, 0)),
        ],
        out_specs=pl.BlockSpec((1, 2 * nc, c), lambda o: (o, 0, 0)),
        compiler_params=_cparams(("parallel",)),
        name="ctx_filter_spectrum",
    )(f_hi, f_lo, k, ss)


def _ctx_conv_kernel(fh_ref, fl_ref, eh_ref, el_ref, kf_ref, u_ref, bias_ref, o_ref, *, nc):
    v, x1, x2 = u_ref[0, 0], u_ref[1, 0], u_ref[2, 0]

    def conv(z, order):
        x = _dot3(fh_ref[...], fl_ref[...], z)
        xr, xi = x[:nc], x[nc:]
        kr, ki = kf_ref[order, :nc], kf_ref[order, nc:]
        y = jnp.concatenate([xr * kr - xi * ki, xr * ki + xi * kr], axis=0)
        return _dot3(eh_ref[...], el_ref[...], y) + z * bias_ref[order:order + 1, :]

    o_ref[0] = x2 * conv(x1 * conv(v, 0), 1)


def _ctx_hyena(f_hi, f_lo, e_hi, e_lo, kf, u3, bias):
    _, p_n, n2x, c = u3.shape
    nc = kf.shape[1] // 2
    kern = functools.partial(_ctx_conv_kernel, nc=nc)
    full = lambda a: pl.BlockSpec(a.shape, lambda p: (0,) * a.ndim)
    return pl.pallas_call(
        kern,
        out_shape=jax.ShapeDtypeStruct((p_n, n2x, c), F32),
        grid=(p_n,),
        in_specs=[full(f_hi), full(f_lo), full(e_hi), full(e_lo), full(kf),
                  pl.BlockSpec((3, 1, n2x, c), lambda p: (0, p, 0, 0)), full(bias)],
        out_specs=pl.BlockSpec((1, n2x, c), lambda p: (p, 0, 0)),
        compiler_params=_cparams(("parallel",)),
        name="ctx_hyena",
    )(f_hi, f_lo, e_hi, e_lo, kf, u3, bias)


def _merge_kernel(x_ref, ya_ref, yh_ref, yd_ref, gh_ref, w_ref, mod_ref, o_ref, *, tm, ctx_len):
    i = pl.program_id(1)
    row = i * tm + lax.broadcasted_iota(jnp.int32, (tm, 1), 0)
    yh = yh_ref[0]
    ms = jnp.mean(yh * yh, axis=-1, keepdims=True)
    yh = yh * lax.rsqrt(ms + EPS) * gh_ref[...]
    y = _dot(ya_ref[0].astype(BF16), w_ref[:A_WIDTH, :])
    y += _dot(yh.astype(BF16), w_ref[A_WIDTH:A_WIDTH + HY_WIDTH, :])
    y += _dot(yd_ref[0].astype(BF16), w_ref[A_WIDTH + HY_WIDTH:, :])
    gate = jnp.where(row < ctx_len, mod_ref[0, 2:3, :], mod_ref[0, 10:11, :])
    o_ref[0] = x_ref[0] + gate * y


def _merge(xs, ya, yh, yd, g_h, w_out, mod, *, ctx_len, tm):
    bsz, s_len, d = xs.shape
    kern = functools.partial(_merge_kernel, tm=tm, ctx_len=ctx_len)
    tok = lambda w: pl.BlockSpec((1, tm, w), lambda b, i: (b, i, 0))
    return pl.pallas_call(
        kern,
        out_shape=jax.ShapeDtypeStruct(xs.shape, F32),
        grid=(bsz, s_len // tm),
        in_specs=[tok(d), tok(A_WIDTH), tok(HY_WIDTH), tok(D_WIDTH),
                  pl.BlockSpec((1, HY_WIDTH), lambda b, i: (0, 0)),
                  pl.BlockSpec((MIX_WIDTH, d), lambda b, i: (0, 0)),
                  pl.BlockSpec((1, 16, d), lambda b, i: (b, 0, 0))],
        out_specs=tok(d),
        compiler_params=_cparams(("parallel", "parallel")),
        name="merge",
    )(xs, ya, yh, yd, g_h, w_out, mod)


HALO = 8


def _ffn_kernel(x_ref, xp_ref, xn_ref, mod_ref, g_ref, wa_ref, wv_ref, cw_ref, cb_ref, wd_ref, o_ref,
                *, tm, fc, ctx_len, s_len):
    i = pl.program_id(1)
    x = x_ref[0]
    xa = jnp.concatenate([xp_ref[0], x, xn_ref[0]], axis=0)
    rows = tm + 2 * HALO
    row = i * tm - HALO + lax.broadcasted_iota(jnp.int32, (rows, 1), 0)
    h = _norm_modulate(xa, g_ref[...], mod_ref, row < ctx_len, 1).astype(BF16)
    hm = h[HALO:HALO + tm]
    rowm = row[HALO:HALO + tm]
    has_prev = jnp.logical_and(rowm != 0, rowm != ctx_len)
    has_next = jnp.logical_and(rowm != ctx_len - 1, rowm != s_len - 1)
    acc = jnp.zeros((tm, x.shape[1]), F32)
    for j in range(wa_ref.shape[1] // fc):
        cs = slice(j * fc, (j + 1) * fc)
        a = _dot(h, wa_ref[:, cs])
        v = _dot(hm, wv_ref[:, cs])
        a_prev = jnp.where(has_prev, pltpu.roll(a, 1, axis=0)[HALO:HALO + tm], 0.0)
        a_next = jnp.where(has_next, pltpu.roll(a, rows - 1, axis=0)[HALO:HALO + tm], 0.0)
        ac = a_prev * cw_ref[0:1, cs] + a[HALO:HALO + tm] * cw_ref[1:2, cs] + a_next * cw_ref[2:3, cs] + cb_ref[:, cs]
        gl = 0.5 * ac * (1.0 + jnp.tanh(math.sqrt(2.0 / math.pi) * (ac + 0.044715 * (ac * ac * ac))))
        acc += _dot((gl * v).astype(BF16), wd_ref[cs, :])
    gate = jnp.where(rowm < ctx_len, mod_ref[0, 5:6, :], mod_ref[0, 13:14, :])
    o_ref[0] = x + gate * acc


def _ffn(xs, mod, g, wa, wv, cw, cb, wd, *, ctx_len, tm, fc):
    bsz, s_len, d = xs.shape
    d_ff = wa.shape[1]
    nh = tm // HALO
    last = s_len // HALO - 1
    kern = functools.partial(_ffn_kernel, tm=tm, fc=fc, ctx_len=ctx_len, s_len=s_len)
    const = lambda shape: pl.BlockSpec(shape, lambda b, i: (0, 0), pipeline_mode=pl.Buffered(1))
    return pl.pallas_call(
        kern,
        out_shape=jax.ShapeDtypeStruct(xs.shape, F32),
        grid=(bsz, s_len // tm),
        in_specs=[
            pl.BlockSpec((1, tm, d), lambda b, i: (b, i, 0)),
            pl.BlockSpec((1, HALO, d), lambda b, i: (b, jnp.maximum(i * nh - 1, 0), 0)),
            pl.BlockSpec((1, HALO, d), lambda b, i: (b, jnp.minimum((i + 1) * nh, last), 0)),
            pl.BlockSpec((1, 16, d), lambda b, i: (b, 0, 0)),
            pl.BlockSpec((1, d), lambda b, i: (0, 0)),
            const((d, d_ff)), const((d, d_ff)), const((3, d_ff)), const((1, d_ff)), const((d_ff, d)),
        ],
        out_specs=pl.BlockSpec((1, tm, d), lambda b, i: (b, i, 0)),
        compiler_params=_cparams(("parallel", "parallel")),
        name="conv_glu",
    )(xs, xs, xs, mod, g, wa, wv, cw, cb, wd)


def _cis(num, den):
    ang = num.astype(F32) * (2.0 * math.pi / den)
    return jnp.cos(ang), jnp.sin(ang)


def _blockmat(re, im):
    return jnp.concatenate([jnp.concatenate([re, -im], axis=-1), jnp.concatenate([im, re], axis=-1)], axis=-2)


def _dft_tables(n):
    n2 = DFT_N2
    t1n = n // n2
    n1 = 2 * t1n
    big = n1 * n2
    f1 = jnp.arange(n1, dtype=jnp.int32)
    t1 = jnp.arange(t1n, dtype=jnp.int32)
    j2 = jnp.arange(n2, dtype=jnp.int32)
    c, s = _cis((f1[:, None] * t1[None, :]) % n1, n1)
    fwd_sig = _blockmat(c, -s)
    c, s = _cis((f1[:, None] * f1[None, :]) % n1, n1)
    fwd_fil = jnp.concatenate([c, -s], axis=0)
    ph = (n1 * j2[None, :, None] * j2[None, None, :] + f1[:, None, None] * j2[None, None, :]) % big
    c, s = _cis(ph, big)
    g = _blockmat(c, -s)
    h = jnp.swapaxes(g, 1, 2)
    c, s = _cis((t1[:, None] * f1[None, :]) % n1, n1)
    inv = _blockmat(c, s) * (1.0 / big)
    return tuple(_split_bf16(m) for m in (fwd_sig, fwd_fil, g, h, inv))


def _dense_dft_tables(n):
    big = 2 * n
    f = jnp.arange(big, dtype=jnp.int32)
    t = jnp.arange(n, dtype=jnp.int32)
    c, s = _cis((f[:, None] * t[None, :]) % big, big)
    fwd_sig = _blockmat(c, -s)
    c, s = _cis((f[:, None] * f[None, :]) % big, big)
    fwd_fil = jnp.concatenate([c, -s], axis=0)
    c, s = _cis((t[:, None] * f[None, :]) % big, big)
    inv = _blockmat(c, s) * (1.0 / big)
    return tuple(_split_bf16(m) for m in (fwd_sig, fwd_fil, inv))


def _filter_features(n):
    pos = jnp.arange(n, dtype=F32)
    t = (pos / max(n - 1, 1))[:, None]
    w = 2.0 * math.pi * pos[:, None] / n
    bands = jnp.linspace(1e-4, HY_BANDS - 1, HY_BANDS, dtype=F32)
    z = jnp.concatenate([t, jnp.cos(bands * w), -jnp.sin(bands * w)], axis=-1)
    return jnp.pad(z, ((0, 0), (0, LANE - HY_EMB_DIM)))


def _circular_filters(h, big):
    n = h.shape[0]
    fwd = jnp.moveaxis(h[:, 0], 1, 0)
    bwd = jnp.moveaxis(h[:, 1], 1, 0)
    zeros = jnp.zeros((fwd.shape[0], big - 2 * n + 1, fwd.shape[2]), F32)
    return jnp.concatenate([fwd, zeros, bwd[:, :0:-1]], axis=1)


def _pick(total, candidates):
    for t in candidates:
        if total % t == 0:
            return t
    raise ValueError(f"no tile for {total}")


def kernel(x, c, ctx, c_ctx, norm1_g, norm2_g, w_mod, b_mod, w_in, qn_a, kn_a, qn_d, kn_d, lam_q1, lam_k1, lam_q2, lam_k2, hy_conv_w, hy_conv_b, hy_fw1, hy_fb1, hy_fw2, hy_fb2, hy_fw3, hy_fb3, hy_freq, hy_bias, g_out, w_out, w_up, ffn_conv_w, ffn_conv_b, w_down):
    bsz, n_tok, d = x.shape
    ctx_len = ctx.shape[1]
    depth = w_in.shape[0]
    s_len = ctx_len + n_tok
    d_ff = w_down.shape[1]
    assert bsz % 2 == 0 and n_tok % (DFT_N2 * 8) == 0 and ctx_len % LANE == 0

    tm = _pick(s_len, (768, 512, 256, 128))
    tq = _pick(ctx_len, (256, 128))
    tk = _pick(s_len, (1408, 768, 512, 256, 128))
    fc = _pick(d_ff, (512, 256, 128))

    n_rows = n_tok // GRID_W
    rows = jnp.repeat(jnp.arange(n_rows, dtype=F32), GRID_W)
    cols = jnp.tile(jnp.arange(GRID_W, dtype=F32), n_rows)
    n_freq = HEAD_DIM // 4
    inv = ROPE_THETA ** (-jnp.arange(n_freq, dtype=F32) / n_freq)
    ang = jnp.concatenate([rows[:, None] * inv, cols[:, None] * inv], axis=-1)
    ang = jnp.concatenate([jnp.zeros((ctx_len, HALF), F32), ang], axis=0)
    cosT, sinT = jnp.cos(ang).T, jnp.sin(ang).T

    sig_f, fil_f, g_tab, h_tab, inv_f = _dft_tables(n_tok)
    csig_f, cfil_f, cinv_f = _dense_dft_tables(ctx_len)
    z_lat = _filter_features(n_tok)
    z_ctx = _filter_features(ctx_len)
    max_decay = math.log(HY_TARGET) / HY_FAST_DECAY
    min_decay = math.log(HY_TARGET) / HY_SLOW_DECAY
    absd = jnp.tile(jnp.abs(jnp.linspace(min_decay, max_decay, HY_WIDTH, dtype=F32)), 4)[None, :]
    t1n = n_tok // DFT_N2
    n1 = 2 * t1n
    big = n1 * DFT_N2
    fb = 8

    perm = np.concatenate([np.arange(0, HEAD_DIM, 2), np.arange(1, HEAD_DIM, 2)])
    qk_cols = np.concatenate(
        [OFF_QA + b * HEAD_DIM + perm for b in range(A_HEADS)]
        + [OFF_QD + b * HEAD_DIM + perm for b in range(2 * D_HEADS)]
        + [OFF_KA + b * HEAD_DIM + perm for b in range(A_KV_HEADS)]
        + [OFF_KD + b * HEAD_DIM + perm for b in range(2 * D_HEADS)])
    v_cols = np.concatenate([np.arange(OFF_VA, OFF_VA + A_KV_WIDTH), np.arange(OFF_VD, OFF_VD + D_WIDTH)])
    qkv_cols = np.concatenate([qk_cols, v_cols])
    wqT = jnp.swapaxes(w_in[:, :, qkv_cols], 1, 2).astype(BF16)
    why = w_in[:, :, OFF_HY:OFF_HY + HY_IN].astype(BF16)
    q_scale = HEAD_DIM ** -0.5 * math.log2(math.e)
    gains = jnp.concatenate(
        [jnp.tile(qn_a[:, perm], (1, A_HEADS)) * q_scale, jnp.tile(qn_d[:, perm], (1, 2 * D_HEADS)) * q_scale,
         jnp.tile(kn_a[:, perm], (1, A_KV_HEADS)), jnp.tile(kn_d[:, perm], (1, 2 * D_HEADS))],
        axis=1)[:, :, None].astype(F32)
    w_out_b = w_out.astype(BF16)
    w_up_a = w_up[:, :, :d_ff].astype(BF16)
    w_up_v = w_up[:, :, d_ff:].astype(BF16)
    w_down_b = w_down.astype(BF16)
    fw1p = jnp.pad(hy_fw1, ((0, 0), (0, LANE - HY_EMB_DIM), (0, 0)))
    lamv = jnp.stack([lam_q1, lam_k1, lam_q2, lam_k2], axis=1)
    lam_init = jnp.asarray([0.8 - 0.6 * math.exp(-0.3 * i) for i in range(depth)], F32)
    lami = jnp.broadcast_to(lam_init[:, None, None], (depth, 8, LANE))

    n_c = 8 * ((bsz + 1 + 7) // 8)
    cvec = jnp.concatenate([c, c_ctx[None, :], jnp.zeros((n_c - bsz - 1, d), F32)], axis=0)
    mods = _modulation(cvec, w_mod, b_mod).reshape(depth, n_c, 6, d)
    pad2 = jnp.zeros((depth, bsz, 2, d), F32)
    mod_all = jnp.concatenate(
        [jnp.broadcast_to(mods[:, bsz:bsz + 1], (depth, bsz, 6, d)), pad2, mods[:, :bsz], pad2], axis=2)

    xs = jnp.concatenate([ctx, x], axis=1)
    for i in range(depth):
        mod = mod_all[i]
        qT, k, vT, hy = _projection(xs, mod, norm1_g[i][None, :], wqT[i], why[i], gains[i], cosT, sinT,
                                    ctx_len=ctx_len, tm=tm)
        ya = _attention_a(qT, k, vT, g_out[i, :A_WIDTH, None], ctx_len=ctx_len, tq=tq, tk=tk)
        yd = _attention_d(qT, k, vT, g_out[i, A_WIDTH + HY_WIDTH:].reshape(D_HEADS, D_VALUE_DIM, 1),
                          lamv[i], lami[i], ctx_len=ctx_len, tq=tq, tk=tk)

        u_lat, u_ctx = _hyena_short_conv(hy, hy_conv_w[i], hy_conv_b[i], ctx_len=ctx_len)
        fil_args = (fw1p[i], hy_fb1[i][None], hy_fw2[i], hy_fb2[i][None], hy_fw3[i], hy_fb3[i][None],
                    hy_freq[i][None], absd)
        h_lat, sum_lat = _hyena_filters(z_lat, *fil_args)
        h_ctx, sum_ctx = _hyena_filters(z_ctx, *fil_args)
        ss_lat = jnp.swapaxes(sum_lat.reshape(2, 2, HY_WIDTH), 0, 1)
        ss_ctx = jnp.swapaxes(sum_ctx.reshape(2, 2, HY_WIDTH), 0, 1)
        k_lat = _circular_filters(h_lat.reshape(n_tok, 2, 2, HY_WIDTH), big)
        k_ctx = _circular_filters(h_ctx.reshape(ctx_len, 2, 2, HY_WIDTH), 2 * ctx_len)

        cols_n = DFT_N2 * HY_WIDTH
        ak = _left_matmul(fil_f[0], fil_f[1], k_lat.reshape(2, n1, cols_n), tc=2048)
        kf = _filter_spectrum(g_tab[0], g_tab[1], ak.reshape(2, 2, n1, DFT_N2, HY_WIDTH), ss_lat, fb=fb)
        p_n = bsz // 2
        v_in = u_lat[0].reshape(p_n, 2 * t1n, cols_n)
        x1_in = u_lat[1].reshape(p_n, 2 * t1n, cols_n)
        x2_in = u_lat[2].reshape(p_n, 2 * t1n, cols_n)
        bias_t = jnp.tile(hy_bias[i], (1, DFT_N2))[:, None, :]

        def long_conv(z_in, gate, order):
            a = _left_matmul(sig_f[0], sig_f[1], z_in, tc=2048)
            cc = _spectral_multiply(g_tab[0], g_tab[1], h_tab[0], h_tab[1], kf,
                                    a.reshape(p_n, 2, n1, DFT_N2, HY_WIDTH), order, fb=fb)
            return _left_matmul(inv_f[0], inv_f[1], cc.reshape(p_n, 2 * n1, cols_n),
                                gate, z_in, bias_t[order:order + 1], tc=2048)

        yh_lat = long_conv(long_conv(v_in, x1_in, 0), x2_in, 1).reshape(bsz, n_tok, HY_WIDTH)

        kf_c = _ctx_filter_spectrum(cfil_f[0], cfil_f[1], k_ctx, ss_ctx)
        yh_ctx = _ctx_hyena(csig_f[0], csig_f[1], cinv_f[0], cinv_f[1], kf_c,
                            u_ctx.reshape(3, p_n, 2 * ctx_len, HY_WIDTH), hy_bias[i])
        yh = jnp.concatenate([yh_ctx.reshape(bsz, ctx_len, HY_WIDTH), yh_lat], axis=1)

        xs = _merge(xs, ya, yh, yd, g_out[i, A_WIDTH:A_WIDTH + HY_WIDTH][None, :], w_out_b[i], mod,
                    ctx_len=ctx_len, tm=tm)
        xs = _ffn(xs, mod, norm2_g[i][None, :], w_up_a[i], w_up_v[i], ffn_conv_w[i], ffn_conv_b[i][None, :],
                  w_down_b[i], ctx_len=ctx_len, tm=tm, fc=fc)
    return xs[:, ctx_len:]
```

```python
import functools
import math

import numpy as np
import jax
import jax.numpy as jnp
from jax import lax
from jax.experimental import pallas as pl
from jax.experimental.pallas import tpu as pltpu

F32 = jnp.float32
BF16 = jnp.bfloat16

HEAD_DIM = 64
HALF = HEAD_DIM // 2
A_HEADS = 4
A_KV_HEADS = 2
A_WIDTH = A_HEADS * HEAD_DIM
A_KV_WIDTH = A_KV_HEADS * HEAD_DIM
HY_WIDTH = 256
HY_IN = 3 * HY_WIDTH
HY_BANDS = 16
HY_EMB_DIM = 1 + 2 * HY_BANDS
HY_HIDDEN = 64
HY_FILTER_CH = 4 * HY_WIDTH
HY_FAST_DECAY = 0.3
HY_SLOW_DECAY = 1.5
HY_TARGET = 1e-2
D_HEADS = 4
D_VALUE_DIM = 2 * HEAD_DIM
D_WIDTH = D_HEADS * D_VALUE_DIM
D_QK_WIDTH = D_HEADS * 2 * HEAD_DIM
MIX_WIDTH = A_WIDTH + HY_WIDTH + D_WIDTH
OFF_QA = 0
OFF_KA = A_WIDTH
OFF_VA = OFF_KA + A_KV_WIDTH
OFF_HY = OFF_VA + A_KV_WIDTH
OFF_QD = OFF_HY + HY_IN
OFF_KD = OFF_QD + D_QK_WIDTH
OFF_VD = OFF_KD + D_QK_WIDTH
GRID_W = 64
ROPE_THETA = 10000.0
EPS = 1e-6

N_QBLK = A_HEADS + 2 * D_HEADS
N_KBLK = A_KV_HEADS + 2 * D_HEADS
Q_ROWS = N_QBLK * HEAD_DIM
K_ROWS = N_KBLK * HEAD_DIM
V_ROWS = A_KV_WIDTH + D_WIDTH
QK_ROWS = Q_ROWS + K_ROWS
QKV_ROWS = QK_ROWS + V_ROWS
SUM_ROWS = 16
VA_BLK = HEAD_DIM + SUM_ROWS
VD_BLK = D_VALUE_DIM + SUM_ROWS

LANE = 128
DFT_N2 = 128
KEY_BLOCK = 256
VMEM_LIMIT = 56 * 1024 * 1024
HIGHEST = lax.Precision.HIGHEST


def _cparams(sem):
    return pltpu.CompilerParams(dimension_semantics=sem, vmem_limit_bytes=VMEM_LIMIT)


def _split_bf16(x):
    hi = x.astype(BF16)
    lo = (x - hi.astype(F32)).astype(BF16)
    return hi, lo


def _dot(a, b):
    return jnp.dot(a, b, preferred_element_type=F32)


def _dot3(a_hi, a_lo, b):
    b_hi, b_lo = _split_bf16(b)
    return _dot(a_hi, b_hi) + (_dot(a_hi, b_lo) + _dot(a_lo, b_hi))


def _mod_kernel(c_ref, w_ref, b_ref, o_ref):
    c = c_ref[...]
    s = c * (1.0 / (1.0 + jnp.exp(-c)))
    o_ref[0] = jnp.dot(s, w_ref[0], preferred_element_type=F32, precision=HIGHEST) + b_ref[0]


def _modulation(cvec, w_mod, b_mod):
    depth, d, n6 = w_mod.shape
    tn = 1024
    return pl.pallas_call(
        _mod_kernel,
        out_shape=jax.ShapeDtypeStruct((depth, cvec.shape[0], n6), F32),
        grid=(depth, n6 // tn),
        in_specs=[
            pl.BlockSpec(cvec.shape, lambda l, j: (0, 0)),
            pl.BlockSpec((1, d, tn), lambda l, j: (l, 0, j)),
            pl.BlockSpec((1, 1, tn), lambda l, j: (l, 0, j)),
        ],
        out_specs=pl.BlockSpec((1, cvec.shape[0], tn), lambda l, j: (l, 0, j)),
        compiler_params=_cparams(("arbitrary", "arbitrary")),
        name="modulation",
    )(cvec, w_mod, b_mod.reshape(depth, 1, n6))


def _norm_modulate(x, g, mod_ref, is_ctx, which):
    ms = jnp.mean(x * x, axis=-1, keepdims=True)
    xn = x * lax.rsqrt(ms + EPS) * g
    sh = jnp.where(is_ctx, mod_ref[0, 3 * which:3 * which + 1, :], mod_ref[0, 8 + 3 * which:9 + 3 * which, :])
    sc = jnp.where(is_ctx, mod_ref[0, 3 * which + 1:3 * which + 2, :],
                   mod_ref[0, 9 + 3 * which:10 + 3 * which, :])
    return xn * (1.0 + sc) + sh


def _proj_kernel(x_ref, mod_ref, g_ref, wq_ref, why_ref, gain_ref, cos_ref, sin_ref,
                 qT_ref, k_ref, vTa_ref, vTd_ref, hy_ref, *, tm, ctx_len):
    i = pl.program_id(1)
    row = i * tm + lax.broadcasted_iota(jnp.int32, (tm, 1), 0)
    h = _norm_modulate(x_ref[0], g_ref[...], mod_ref, row < ctx_len, 0).astype(BF16)
    pT = lax.dot_general(wq_ref[...], h, (((1,), (1,)), ((), ())), preferred_element_type=F32)
    hy_ref[0] = _dot(h, why_ref[...])
    c = cos_ref[...]
    s = sin_ref[...]
    blocks = []
    for b in range(N_QBLK + N_KBLK):
        blk = pT[b * HEAD_DIM:(b + 1) * HEAD_DIM]
        ssq = jnp.sum(blk * blk, axis=0, keepdims=True)
        blk = blk * lax.rsqrt(ssq * (1.0 / HEAD_DIM) + EPS) * gain_ref[b * HEAD_DIM:(b + 1) * HEAD_DIM, :]
        x1 = blk[:HALF]
        x2 = blk[HALF:]
        blocks.append(x1 * c - x2 * s)
        blocks.append(x1 * s + x2 * c)
    qT_ref[0] = jnp.concatenate(blocks[:2 * N_QBLK], axis=0).astype(BF16)
    kT = jnp.concatenate(blocks[2 * N_QBLK:], axis=0)
    k_ref[0] = kT.T.astype(BF16)
    ones = jnp.ones((SUM_ROWS, tm), BF16)
    va = [pT[QK_ROWS + g * HEAD_DIM:QK_ROWS + (g + 1) * HEAD_DIM].astype(BF16) for g in range(A_KV_HEADS)]
    vTa_ref[0] = jnp.concatenate([t for v in va for t in (v, ones)], axis=0)
    vd0 = QK_ROWS + A_KV_WIDTH
    vd = [pT[vd0 + h * D_VALUE_DIM:vd0 + (h + 1) * D_VALUE_DIM].astype(BF16) for h in range(D_HEADS)]
    vTd_ref[0] = jnp.concatenate([t for v in vd for t in (v, ones)], axis=0)


def _projection(xs, mod, g, wqT, why, gains, cosT, sinT, *, ctx_len, tm):
    bsz, s_len, d = xs.shape
    kern = functools.partial(_proj_kernel, tm=tm, ctx_len=ctx_len)
    return pl.pallas_call(
        kern,
        out_shape=(
            jax.ShapeDtypeStruct((bsz, Q_ROWS, s_len), BF16),
            jax.ShapeDtypeStruct((bsz, s_len, K_ROWS), BF16),
            jax.ShapeDtypeStruct((bsz, A_KV_HEADS * VA_BLK, s_len), BF16),
            jax.ShapeDtypeStruct((bsz, D_HEADS * VD_BLK, s_len), BF16),
            jax.ShapeDtypeStruct((bsz, s_len, HY_IN), F32),
        ),
        grid=(bsz, s_len // tm),
        in_specs=[
            pl.BlockSpec((1, tm, d), lambda b, i: (b, i, 0)),
            pl.BlockSpec((1, 16, d), lambda b, i: (b, 0, 0)),
            pl.BlockSpec((1, d), lambda b, i: (0, 0)),
            pl.BlockSpec((QKV_ROWS, d), lambda b, i: (0, 0)),
            pl.BlockSpec((d, HY_IN), lambda b, i: (0, 0)),
            pl.BlockSpec((QK_ROWS, 1), lambda b, i: (0, 0)),
            pl.BlockSpec((HALF, tm), lambda b, i: (0, i)),
            pl.BlockSpec((HALF, tm), lambda b, i: (0, i)),
        ],
        out_specs=(
            pl.BlockSpec((1, Q_ROWS, tm), lambda b, i: (b, 0, i)),
            pl.BlockSpec((1, tm, K_ROWS), lambda b, i: (b, i, 0)),
            pl.BlockSpec((1, A_KV_HEADS * VA_BLK, tm), lambda b, i: (b, 0, i)),
            pl.BlockSpec((1, D_HEADS * VD_BLK, tm), lambda b, i: (b, 0, i)),
            pl.BlockSpec((1, tm, HY_IN), lambda b, i: (b, i, 0)),
        ),
        compiler_params=_cparams(("parallel", "parallel")),
        name="projection",
    )(xs, mod, g, wqT, why, gains, cosT, sinT)


def _attend(k_ref, vT_ref, v_rows, scr, *, ctx_only, ctx_len, tk):
    n_u = len(v_rows)
    qz, m_run, acc, s_buf, cm_buf = _split_scratch(scr, n_u)
    s_len = k_ref.shape[1]
    for u in range(n_u):
        m_run[u][...] = jnp.full(m_run[u].shape, -1e30, F32)
        acc[u][...] = jnp.zeros(acc[u].shape, F32)

    def colmax8(s):
        return jnp.max(s.reshape(s.shape[0] // 8, 8, s.shape[1]), axis=0)

    def key_rows(start, j, rb):
        if isinstance(start, int):
            return pl.ds(start + j * rb, rb)
        return pl.ds(pl.multiple_of(start + j * rb, rb), rb)

    def advance(nxt, cur, size):
        rb = min(size, KEY_BLOCK)
        if cur is not None:
            u_c, start_c, slot_c = cur
            v0, dv = v_rows[u_c]
            m = m_run[u_c][...]
            m_new = jnp.maximum(m, jnp.max(cm_buf[slot_c][...], axis=0, keepdims=True))
            m_run[u_c][...] = m_new
            alpha = jnp.exp2(m - m_new)
        cmax, pv = None, None
        for j in range(size // rb):
            rows = slice(j * rb, (j + 1) * rb)
            if nxt is not None:
                u_n, start_n, slot_n = nxt
                s = _dot(k_ref[0, key_rows(start_n, j, rb), :], qz[u_n][...])
                s_buf[slot_n][rows, :] = s
                cmax = colmax8(s) if cmax is None else jnp.maximum(cmax, colmax8(s))
            if cur is not None:
                p = jnp.exp2(s_buf[slot_c][rows, :] - m_new).astype(BF16)
                d = _dot(vT_ref[0, v0:v0 + dv, key_rows(start_c, j, rb)], p)
                pv = d if pv is None else pv + d
        if nxt is not None:
            cm_buf[slot_n][...] = cmax
        if cur is not None:
            acc[u_c][...] = alpha * acc[u_c][...] + pv

    if ctx_only:
        for u in range(n_u):
            advance((u, 0, u % 2), None, ctx_len)
            advance(None, (u, 0, u % 2), ctx_len)
    else:
        n_chunks = s_len // tk
        advance((0, 0, 0), None, tk)

        def body(c, carry):
            start = pl.multiple_of(c * tk, tk)
            nxt = pl.multiple_of(jnp.minimum(c + 1, n_chunks - 1) * tk, tk)
            for u in range(n_u):
                task_n = (u + 1, start, (u + 1) % 2) if u + 1 < n_u else (0, nxt, (u + 1) % 2)
                advance(task_n, (u, start, u % 2), tk)
            return carry
        lax.fori_loop(0, n_chunks, body, 0)
    outs = []
    for u in range(n_u):
        dv = v_rows[u][1] - SUM_ROWS
        outs.append(acc[u][:dv, :] * (1.0 / acc[u][dv:dv + 1, :]))
    return outs


def _attn_scratch(n_u, rows, tq, tk):
    per_unit = [((LANE, tq), BF16), ((1, tq), F32), ((rows, tq), F32)]
    per_slot = [((tk, tq), F32), ((8, tq), F32)]
    return ([pltpu.VMEM(s, d) for s, d in per_unit for _ in range(n_u)]
            + [pltpu.VMEM(s, d) for s, d in per_slot for _ in range(2)])


def _split_scratch(scr, n_u):
    groups = [scr[i * n_u:(i + 1) * n_u] for i in range(3)]
    rest = scr[3 * n_u:]
    return groups + [rest[0:2], rest[2:4]]


def _attn_a_kernel(qT_ref, k_ref, vT_ref, g_ref, o_ref, *scr, ctx_len, tq, tk):
    qi = pl.program_id(1)
    zeros = jnp.zeros((HEAD_DIM, tq), BF16)
    group = A_HEADS // A_KV_HEADS
    for h in range(A_HEADS):
        q = qT_ref[0, h * HEAD_DIM:(h + 1) * HEAD_DIM, :]
        scr[h][...] = jnp.concatenate([q, zeros] if h // group == 0 else [zeros, q], axis=0)
    v_rows = [((h // group) * VA_BLK, VA_BLK) for h in range(A_HEADS)]

    def run(ctx_only):
        outs = _attend(k_ref, vT_ref, v_rows, scr, ctx_only=ctx_only, ctx_len=ctx_len, tk=tk)
        y = jnp.concatenate(outs, axis=0)
        ms = jnp.mean(y * y, axis=0, keepdims=True)
        y = y * lax.rsqrt(ms + EPS) * g_ref[...]
        o_ref[0] = y.T

    n_ctx_tiles = ctx_len // tq

    @pl.when(qi < n_ctx_tiles)
    def _():
        run(True)

    @pl.when(qi >= n_ctx_tiles)
    def _():
        run(False)


def _attention_a(qT, k, vT, g_a, *, ctx_len, tq, tk):
    bsz, _, s_len = qT.shape
    kern = functools.partial(_attn_a_kernel, ctx_len=ctx_len, tq=tq, tk=tk)
    return pl.pallas_call(
        kern,
        out_shape=jax.ShapeDtypeStruct((bsz, s_len, A_WIDTH), F32),
        grid=(bsz, s_len // tq),
        in_specs=[
            pl.BlockSpec((1, A_WIDTH, tq), lambda b, i: (b, 0, i)),
            pl.BlockSpec((1, s_len, LANE), lambda b, i: (b, 0, 0)),
            pl.BlockSpec((1, A_KV_HEADS * VA_BLK, s_len), lambda b, i: (b, 0, 0)),
            pl.BlockSpec((A_WIDTH, 1), lambda b, i: (0, 0)),
        ],
        out_specs=pl.BlockSpec((1, tq, A_WIDTH), lambda b, i: (b, i, 0)),
        scratch_shapes=_attn_scratch(A_HEADS, VA_BLK, tq, tk),
        compiler_params=_cparams(("parallel", "parallel")),
        name="attention_gqa",
    )(qT, k, vT, g_a)


def _attn_d_kernel(qT_ref, k_ref, vT_ref, g_ref, lamv_ref, lami_ref, o_ref, *scr, ctx_len, tq, tk):
    qi = pl.program_id(2)
    zeros = jnp.zeros((HEAD_DIM, tq), BF16)
    lv = lamv_ref[...]
    lam_init = lami_ref[0:1, 0:1]
    lam = (jnp.exp(jnp.sum(lv[0:1] * lv[1:2], axis=-1, keepdims=True))
           - jnp.exp(jnp.sum(lv[2:3] * lv[3:4], axis=-1, keepdims=True)) + lam_init)

    scr[0][...] = jnp.concatenate([qT_ref[0, :HEAD_DIM, :], zeros], axis=0)
    scr[1][...] = jnp.concatenate([zeros, qT_ref[0, HEAD_DIM:, :]], axis=0)
    v_rows = [(0, VD_BLK), (0, VD_BLK)]

    def run(ctx_only):
        o1, o2 = _attend(k_ref, vT_ref, v_rows, scr, ctx_only=ctx_only, ctx_len=ctx_len, tk=tk)
        y = o1 - lam * o2
        ms = jnp.mean(y * y, axis=0, keepdims=True)
        y = y * lax.rsqrt(ms + EPS) * g_ref[0] * (1.0 - lam_init)
        o_ref[0] = y.T

    n_ctx_tiles = ctx_len // tq

    @pl.when(qi < n_ctx_tiles)
    def _():
        run(True)

    @pl.when(qi >= n_ctx_tiles)
    def _():
        run(False)


def _attention_d(qT, k, vT, g_d, lamv, lami, *, ctx_len, tq, tk):
    bsz, _, s_len = qT.shape
    kern = functools.partial(_attn_d_kernel, ctx_len=ctx_len, tq=tq, tk=tk)
    q_blk0 = A_WIDTH // LANE
    k_blk0 = A_KV_WIDTH // LANE
    return pl.pallas_call(
        kern,
        out_shape=jax.ShapeDtypeStruct((bsz, s_len, D_WIDTH), F32),
        grid=(bsz, D_HEADS, s_len // tq),
        in_specs=[
            pl.BlockSpec((1, LANE, tq), lambda b, h, i: (b, q_blk0 + h, i)),
            pl.BlockSpec((1, s_len, LANE), lambda b, h, i: (b, 0, k_blk0 + h)),
            pl.BlockSpec((1, VD_BLK, s_len), lambda b, h, i: (b, h, 0)),
            pl.BlockSpec((1, D_VALUE_DIM, 1), lambda b, h, i: (h, 0, 0)),
            pl.BlockSpec((4, HEAD_DIM), lambda b, h, i: (0, 0)),
            pl.BlockSpec((8, LANE), lambda b, h, i: (0, 0)),
        ],
        out_specs=pl.BlockSpec((1, tq, D_VALUE_DIM), lambda b, h, i: (b, i, h)),
        scratch_shapes=_attn_scratch(2, VD_BLK, tq, tk),
        compiler_params=_cparams(("parallel", "parallel", "parallel")),
        name="attention_diff",
    )(qT, k, vT, g_d, lamv, lami)


def _dwconv3_seq(u, w_ref, b_ref):
    n = u.shape[0]
    row = lax.broadcasted_iota(jnp.int32, (n, 1), 0)
    prev = jnp.where(row == 0, 0.0, pltpu.roll(u, 1, axis=0))
    nxt = jnp.where(row == n - 1, 0.0, pltpu.roll(u, n - 1, axis=0))
    return prev * w_ref[0:1, :] + u * w_ref[1:2, :] + nxt * w_ref[2:3, :] + b_ref[...]


def _hy_conv_kernel(hy_ref, w_ref, b_ref, lat_ref, ctx_ref, *, ctx_len):
    ctx_ref[0, 0] = _dwconv3_seq(hy_ref[0, :ctx_len, :], w_ref, b_ref)
    lat_ref[0, 0, 0] = _dwconv3_seq(hy_ref[0, ctx_len:, :], w_ref, b_ref)


def _hyena_short_conv(hy, w, b, *, ctx_len):
    bsz, s_len, _ = hy.shape
    n = s_len - ctx_len
    per = HY_WIDTH // LANE
    kern = functools.partial(_hy_conv_kernel, ctx_len=ctx_len)
    return pl.pallas_call(
        kern,
        out_shape=(
            jax.ShapeDtypeStruct((3, per, bsz, n, LANE), F32),
            jax.ShapeDtypeStruct((3, bsz, ctx_len, HY_WIDTH), F32),
        ),
        grid=(bsz, HY_IN // LANE),
        in_specs=[
            pl.BlockSpec((1, s_len, LANE), lambda b, j: (b, 0, j)),
            pl.BlockSpec((3, LANE), lambda b, j: (0, j)),
            pl.BlockSpec((1, LANE), lambda b, j: (0, j)),
        ],
        out_specs=(
            pl.BlockSpec((1, 1, 1, n, LANE), lambda b, j: (j // per, j % per, b, 0, 0)),
            pl.BlockSpec((1, 1, ctx_len, LANE), lambda b, j: (j // per, b, 0, j % per)),
        ),
        compiler_params=_cparams(("parallel", "parallel")),
        name="hyena_short_conv",
    )(hy, w, b.reshape(1, HY_IN))


def _filter_kernel(z_ref, w1_ref, b1_ref, w2_ref, b2_ref, w3_ref, b3_ref, fr_ref, ad_ref, h_ref, sum_ref):
    i = pl.program_id(0)
    z = z_ref[...]
    fr = fr_ref[...]
    h = jnp.sin(fr * (jnp.dot(z, w1_ref[...], preferred_element_type=F32, precision=HIGHEST) + b1_ref[...]))
    h = jnp.sin(fr * (jnp.dot(h, w2_ref[...], preferred_element_type=F32, precision=HIGHEST) + b2_ref[...]))
    h = jnp.dot(h, w3_ref[...], preferred_element_type=F32, precision=HIGHEST) + b3_ref[...]
    h = h * jnp.exp(-z[:, 0:1] * ad_ref[...])
    h_ref[...] = h

    @pl.when(i == 0)
    def _():
        sum_ref[...] = jnp.zeros_like(sum_ref)

    sum_ref[...] += jnp.sum(jnp.abs(h), axis=0, keepdims=True)


def _hyena_filters(zfeat, w1, b1, w2, b2, w3, b3, freq, absd):
    n, kz = zfeat.shape
    tn = min(n, 1024)
    full = lambda a: pl.BlockSpec(a.shape, lambda i: (0, 0))
    return pl.pallas_call(
        _filter_kernel,
        out_shape=(
            jax.ShapeDtypeStruct((n, HY_FILTER_CH), F32),
            jax.ShapeDtypeStruct((1, HY_FILTER_CH), F32),
        ),
        grid=(n // tn,),
        in_specs=[pl.BlockSpec((tn, kz), lambda i: (i, 0)), full(w1), full(b1), full(w2), full(b2),
                  full(w3), full(b3), full(freq), full(absd)],
        out_specs=(
            pl.BlockSpec((tn, HY_FILTER_CH), lambda i: (i, 0)),
            pl.BlockSpec((1, HY_FILTER_CH), lambda i: (0, 0)),
        ),
        compiler_params=_cparams(("arbitrary",)),
        name="hyena_filters",
    )(zfeat, w1, b1, w2, b2, w3, b3, freq, absd)


T2_BLOCK = 8


def _outer_fwd_kernel(mh_ref, ml_ref, x_ref, o_ref, *, n1):
    rows = x_ref.shape[3]
    xs = [x_ref.at[0, h, 0].reshape(rows * T2_BLOCK, LANE) for h in range(2)]
    outs = [[o_ref.at[0, ri, h].reshape(n1 * T2_BLOCK, LANE) for h in range(2)] for ri in range(2)]
    for t in range(T2_BLOCK):
        sel = pl.ds(t, rows, stride=T2_BLOCK)
        y = _dot3(mh_ref[...], ml_ref[...], jnp.concatenate([xs[0][sel, :], xs[1][sel, :]], axis=1))
        for ri in range(2):
            for h in range(2):
                outs[ri][h][pl.ds(t, n1, stride=T2_BLOCK), :] = y[ri * n1:(ri + 1) * n1, h * LANE:(h + 1) * LANE]


def _outer_forward(m_hi, m_lo, x6, g):
    _, _, p_n, rows, n2, _ = x6.shape
    n1 = m_hi.shape[0] // 2
    kern = functools.partial(_outer_fwd_kernel, n1=n1)
    return pl.pallas_call(
        kern,
        out_shape=jax.ShapeDtypeStruct((p_n, 2, 2, n1, n2, LANE), F32),
        grid=(p_n, n2 // T2_BLOCK),
        in_specs=[pl.BlockSpec(m_hi.shape, lambda p, j: (0, 0)), pl.BlockSpec(m_lo.shape, lambda p, j: (0, 0)),
                  pl.BlockSpec((1, 2, 1, rows, T2_BLOCK, LANE), lambda p, j: (g, 0, p, 0, j, 0))],
        out_specs=pl.BlockSpec((1, 2, 2, n1, T2_BLOCK, LANE), lambda p, j: (p, 0, 0, 0, j, 0)),
        compiler_params=_cparams(("parallel", "parallel")),
        name="dft_outer_fwd",
    )(m_hi, m_lo, x6)


def _outer_inv_kernel(mh_ref, ml_ref, c_ref, gate_ref, u_ref, bias_ref, o_ref):
    n1 = c_ref.shape[3]
    rows = o_ref.shape[2]
    cs = [[c_ref.at[0, ri, h].reshape(n1 * T2_BLOCK, LANE) for h in range(2)] for ri in range(2)]
    gates = [gate_ref.at[0, h, 0].reshape(rows * T2_BLOCK, LANE) for h in range(2)]
    us = [u_ref.at[0, h, 0].reshape(rows * T2_BLOCK, LANE) for h in range(2)]
    outs = [o_ref.at[h, 0].reshape(rows * T2_BLOCK, LANE) for h in range(2)]
    for t in range(T2_BLOCK):
        sel_f = pl.ds(t, n1, stride=T2_BLOCK)
        sel_t = pl.ds(t, rows, stride=T2_BLOCK)
        c2 = jnp.concatenate([jnp.concatenate([cs[ri][0][sel_f, :], cs[ri][1][sel_f, :]], axis=1)
                              for ri in range(2)], axis=0)
        y = _dot3(mh_ref[...], ml_ref[...], c2)
        for h in range(2):
            lanes = slice(h * LANE, (h + 1) * LANE)
            outs[h][sel_t, :] = gates[h][sel_t, :] * (y[:, lanes] + us[h][sel_t, :] * bias_ref[:, lanes])


def _outer_inverse(m_hi, m_lo, c6, gate6, g_gate, u6, g_u, bias):
    p_n, _, _, n1, n2, _ = c6.shape
    rows = m_hi.shape[0]
    gspec = lambda g: pl.BlockSpec((1, 2, 1, rows, T2_BLOCK, LANE), lambda p, j: (g, 0, p, 0, j, 0))
    return pl.pallas_call(
        _outer_inv_kernel,
        out_shape=jax.ShapeDtypeStruct((2, p_n, rows, n2, LANE), F32),
        grid=(p_n, n2 // T2_BLOCK),
        in_specs=[pl.BlockSpec(m_hi.shape, lambda p, j: (0, 0)), pl.BlockSpec(m_lo.shape, lambda p, j: (0, 0)),
                  pl.BlockSpec((1, 2, 2, n1, T2_BLOCK, LANE), lambda p, j: (p, 0, 0, 0, j, 0)),
                  gspec(g_gate), gspec(g_u), pl.BlockSpec((1, 2 * LANE), lambda p, j: (0, 0))],
        out_specs=pl.BlockSpec((2, 1, rows, T2_BLOCK, LANE), lambda p, j: (0, p, 0, j, 0)),
        compiler_params=_cparams(("parallel", "parallel")),
        name="dft_outer_inv",
    )(m_hi, m_lo, c6, gate6, u6, bias)


def _halves_to_rows(a_ref, j):
    return jnp.concatenate([jnp.concatenate([a_ref[0, ri, 0, j], a_ref[0, ri, 1, j]], axis=1) for ri in range(2)],
                           axis=0)


def _kf_kernel(gh_ref, gl_ref, a_ref, ss_ref, o_ref, *, fb):
    scale = 1.0 / (ss_ref[0, 0:1, :] + ss_ref[0, 1:2, :] + EPS)
    for j in range(fb):
        o_ref[0, j] = _dot3(gh_ref[j], gl_ref[j], _halves_to_rows(a_ref, j)) * scale


def _filter_spectrum(g_hi, g_lo, a6, ss, *, fb):
    n_ord, _, _, n1, n2, _ = a6.shape
    c = 2 * LANE
    kern = functools.partial(_kf_kernel, fb=fb)
    return pl.pallas_call(
        kern,
        out_shape=jax.ShapeDtypeStruct((n_ord, n1, 2 * n2, c), F32),
        grid=(n1 // fb, n_ord),
        in_specs=[
            pl.BlockSpec((fb, 2 * n2, 2 * n2), lambda f, o: (f, 0, 0)),
            pl.BlockSpec((fb, 2 * n2, 2 * n2), lambda f, o: (f, 0, 0)),
            pl.BlockSpec((1, 2, 2, fb, n2, LANE), lambda f, o: (o, 0, 0, f, 0, 0)),
            pl.BlockSpec((1, 2, c), lambda f, o: (o, 0, 0)),
        ],
        out_specs=pl.BlockSpec((1, fb, 2 * n2, c), lambda f, o: (o, f, 0, 0)),
        compiler_params=_cparams(("parallel", "parallel")),
        name="filter_spectrum",
    )(g_hi, g_lo, a6, ss)


def _spec_kernel(gh_ref, gl_ref, hh_ref, hl_ref, kf_ref, a_ref, o_ref, *, fb, n2):
    for j in range(fb):
        x = _dot3(gh_ref[j], gl_ref[j], _halves_to_rows(a_ref, j))
        xr, xi = x[:n2], x[n2:]
        kr, ki = kf_ref[0, j, :n2], kf_ref[0, j, n2:]
        y = jnp.concatenate([xr * kr - xi * ki, xr * ki + xi * kr], axis=0)
        c2 = _dot3(hh_ref[j], hl_ref[j], y)
        for ri in range(2):
            for h in range(2):
                o_ref[0, ri, h, j] = c2[ri * n2:(ri + 1) * n2, h * LANE:(h + 1) * LANE]


def _spectral_multiply(g_hi, g_lo, h_hi, h_lo, kf, a6, order, *, fb):
    p_n, _, _, n1, n2, _ = a6.shape
    c = 2 * LANE
    kern = functools.partial(_spec_kernel, fb=fb, n2=n2)
    gspec = pl.BlockSpec((fb, 2 * n2, 2 * n2), lambda f, p: (f, 0, 0))
    aspec = pl.BlockSpec((1, 2, 2, fb, n2, LANE), lambda f, p: (p, 0, 0, f, 0, 0))
    return pl.pallas_call(
        kern,
        out_shape=jax.ShapeDtypeStruct(a6.shape, F32),
        grid=(n1 // fb, p_n),
        in_specs=[gspec, gspec, gspec, gspec,
                  pl.BlockSpec((1, fb, 2 * n2, c), lambda f, p: (order, f, 0, 0)), aspec],
        out_specs=aspec,
        compiler_params=_cparams(("parallel", "parallel")),
        name="spectral_multiply",
    )(g_hi, g_lo, h_hi, h_lo, kf, a6)


def _ctx_kf_kernel(fh_ref, fl_ref, k_ref, ss_ref, o_ref):
    scale = 1.0 / (ss_ref[0, 0:1, :] + ss_ref[0, 1:2, :] + EPS)
    o_ref[0] = _dot3(fh_ref[...], fl_ref[...], k_ref[0]) * scale


def _ctx_filter_spectrum(f_hi, f_lo, k, ss):
    n_ord, nc, c = k.shape
    return pl.pallas_call(
        _ctx_kf_kernel,
        out_shape=jax.ShapeDtypeStruct((n_ord, 2 * nc, c), F32),
        grid=(n_ord,),
        in_specs=[
            pl.BlockSpec(f_hi.shape, lambda o: (0, 0)),
            pl.BlockSpec(f_lo.shape, lambda o: (0, 0)),
            pl.BlockSpec((1, nc, c), lambda o: (o, 0, 0)),
            pl.BlockSpec((1, 2, c), lambda o: (o, 0, 0)),
        ],
        out_specs=pl.BlockSpec((1, 2 * nc, c), lambda o: (o, 0, 0)),
        compiler_params=_cparams(("parallel",)),
        name="ctx_filter_spectrum",
    )(f_hi, f_lo, k, ss)


def _ctx_conv_kernel(fh_ref, fl_ref, eh_ref, el_ref, kf_ref, u_ref, bias_ref, o_ref, *, nc):
    v, x1, x2 = u_ref[0, 0], u_ref[1, 0], u_ref[2, 0]

    def conv(z, order):
        x = _dot3(fh_ref[...], fl_ref[...], z)
        xr, xi = x[:nc], x[nc:]
        kr, ki = kf_ref[order, :nc], kf_ref[order, nc:]
        y = jnp.concatenate([xr * kr - xi * ki, xr * ki + xi * kr], axis=0)
        return _dot3(eh_ref[...], el_ref[...], y) + z * bias_ref[order:order + 1, :]

    o_ref[0] = x2 * conv(x1 * conv(v, 0), 1)


def _ctx_hyena(f_hi, f_lo, e_hi, e_lo, kf, u3, bias):
    _, p_n, n2x, c = u3.shape
    nc = kf.shape[1] // 2
    kern = functools.partial(_ctx_conv_kernel, nc=nc)
    full = lambda a: pl.BlockSpec(a.shape, lambda p: (0,) * a.ndim)
    return pl.pallas_call(
        kern,
        out_shape=jax.ShapeDtypeStruct((p_n, n2x, c), F32),
        grid=(p_n,),
        in_specs=[full(f_hi), full(f_lo), full(e_hi), full(e_lo), full(kf),
                  pl.BlockSpec((3, 1, n2x, c), lambda p: (0, p, 0, 0)), full(bias)],
        out_specs=pl.BlockSpec((1, n2x, c), lambda p: (p, 0, 0)),
        compiler_params=_cparams(("parallel",)),
        name="ctx_hyena",
    )(f_hi, f_lo, e_hi, e_lo, kf, u3, bias)


def _merge_kernel(x_ref, ya_ref, yh_ref, yd_ref, gh_ref, w_ref, mod_ref, o_ref, *, tm, ctx_len):
    i = pl.program_id(1)
    row = i * tm + lax.broadcasted_iota(jnp.int32, (tm, 1), 0)
    yh = yh_ref[0]
    ms = jnp.mean(yh * yh, axis=-1, keepdims=True)
    yh = yh * lax.rsqrt(ms + EPS) * gh_ref[...]
    y = _dot(ya_ref[0].astype(BF16), w_ref[:A_WIDTH, :])
    y += _dot(yh.astype(BF16), w_ref[A_WIDTH:A_WIDTH + HY_WIDTH, :])
    y += _dot(yd_ref[0].astype(BF16), w_ref[A_WIDTH + HY_WIDTH:, :])
    gate = jnp.where(row < ctx_len, mod_ref[0, 2:3, :], mod_ref[0, 10:11, :])
    o_ref[0] = x_ref[0] + gate * y


def _merge(xs, ya, yh, yd, g_h, w_out, mod, *, ctx_len, tm):
    bsz, s_len, d = xs.shape
    kern = functools.partial(_merge_kernel, tm=tm, ctx_len=ctx_len)
    tok = lambda w: pl.BlockSpec((1, tm, w), lambda b, i: (b, i, 0))
    return pl.pallas_call(
        kern,
        out_shape=jax.ShapeDtypeStruct(xs.shape, F32),
        grid=(bsz, s_len // tm),
        in_specs=[tok(d), tok(A_WIDTH), tok(HY_WIDTH), tok(D_WIDTH),
                  pl.BlockSpec((1, HY_WIDTH), lambda b, i: (0, 0)),
                  pl.BlockSpec((MIX_WIDTH, d), lambda b, i: (0, 0)),
                  pl.BlockSpec((1, 16, d), lambda b, i: (b, 0, 0))],
        out_specs=tok(d),
        compiler_params=_cparams(("parallel", "parallel")),
        name="merge",
    )(xs, ya, yh, yd, g_h, w_out, mod)


HALO = 8


def _ffn_kernel(x_ref, xp_ref, xn_ref, mod_ref, g_ref, wa_ref, wv_ref, cw_ref, cb_ref, wd_ref, o_ref,
                *, tm, fc, ctx_len, s_len):
    i = pl.program_id(1)
    x = x_ref[0]
    xa = jnp.concatenate([xp_ref[0], x, xn_ref[0]], axis=0)
    rows = tm + 2 * HALO
    row = i * tm - HALO + lax.broadcasted_iota(jnp.int32, (rows, 1), 0)
    h = _norm_modulate(xa, g_ref[...], mod_ref, row < ctx_len, 1).astype(BF16)
    hm = h[HALO:HALO + tm]
    rowm = row[HALO:HALO + tm]
    has_prev = jnp.logical_and(rowm != 0, rowm != ctx_len)
    has_next = jnp.logical_and(rowm != ctx_len - 1, rowm != s_len - 1)
    acc = jnp.zeros((tm, x.shape[1]), F32)
    for j in range(wa_ref.shape[1] // fc):
        cs = slice(j * fc, (j + 1) * fc)
        a = _dot(h, wa_ref[:, cs])
        v = _dot(hm, wv_ref[:, cs])
        a_prev = jnp.where(has_prev, pltpu.roll(a, 1, axis=0)[HALO:HALO + tm], 0.0)
        a_next = jnp.where(has_next, pltpu.roll(a, rows - 1, axis=0)[HALO:HALO + tm], 0.0)
        ac = a_prev * cw_ref[0:1, cs] + a[HALO:HALO + tm] * cw_ref[1:2, cs] + a_next * cw_ref[2:3, cs] + cb_ref[:, cs]
        gl = 0.5 * ac * (1.0 + jnp.tanh(math.sqrt(2.0 / math.pi) * (ac + 0.044715 * (ac * ac * ac))))
        acc += _dot((gl * v).astype(BF16), wd_ref[cs, :])
    gate = jnp.where(rowm < ctx_len, mod_ref[0, 5:6, :], mod_ref[0, 13:14, :])
    o_ref[0] = x + gate * acc


def _ffn(xs, mod, g, wa, wv, cw, cb, wd, *, ctx_len, tm, fc):
    bsz, s_len, d = xs.shape
    d_ff = wa.shape[1]
    nh = tm // HALO
    last = s_len // HALO - 1
    kern = functools.partial(_ffn_kernel, tm=tm, fc=fc, ctx_len=ctx_len, s_len=s_len)
    const = lambda shape: pl.BlockSpec(shape, lambda b, i: (0, 0), pipeline_mode=pl.Buffered(1))
    return pl.pallas_call(
        kern,
        out_shape=jax.ShapeDtypeStruct(xs.shape, F32),
        grid=(bsz, s_len // tm),
        in_specs=[
            pl.BlockSpec((1, tm, d), lambda b, i: (b, i, 0)),
            pl.BlockSpec((1, HALO, d), lambda b, i: (b, jnp.maximum(i * nh - 1, 0), 0)),
            pl.BlockSpec((1, HALO, d), lambda b, i: (b, jnp.minimum((i + 1) * nh, last), 0)),
            pl.BlockSpec((1, 16, d), lambda b, i: (b, 0, 0)),
            pl.BlockSpec((1, d), lambda b, i: (0, 0)),
            const((d, d_ff)), const((d, d_ff)), const((3, d_ff)), const((1, d_ff)), const((d_ff, d)),
        ],
        out_specs=pl.BlockSpec((1, tm, d), lambda b, i: (b, i, 0)),
        compiler_params=_cparams(("parallel", "parallel")),
        name="conv_glu",
    )(xs, xs, xs, mod, g, wa, wv, cw, cb, wd)


def _cis(num, den):
    ang = num.astype(F32) * (2.0 * math.pi / den)
    return jnp.cos(ang), jnp.sin(ang)


def _blockmat(re, im):
    return jnp.concatenate([jnp.concatenate([re, -im], axis=-1), jnp.concatenate([im, re], axis=-1)], axis=-2)


def _dft_tables(n):
    n2 = DFT_N2
    t1n = n // n2
    n1 = 2 * t1n
    big = n1 * n2
    f1 = jnp.arange(n1, dtype=jnp.int32)
    t1 = jnp.arange(t1n, dtype=jnp.int32)
    j2 = jnp.arange(n2, dtype=jnp.int32)
    c, s = _cis((f1[:, None] * t1[None, :]) % n1, n1)
    fwd_sig = _blockmat(c, -s)
    c, s = _cis((f1[:, None] * f1[None, :]) % n1, n1)
    fwd_fil = jnp.concatenate([c, -s], axis=0)
    ph = (n1 * j2[None, :, None] * j2[None, None, :] + f1[:, None, None] * j2[None, None, :]) % big
    c, s = _cis(ph, big)
    g = _blockmat(c, -s)
    h = jnp.swapaxes(g, 1, 2)
    c, s = _cis((t1[:, None] * f1[None, :]) % n1, n1)
    inv = _blockmat(c, s) * (1.0 / big)
    return tuple(_split_bf16(m) for m in (fwd_sig, fwd_fil, g, h, inv))


def _dense_dft_tables(n):
    big = 2 * n
    f = jnp.arange(big, dtype=jnp.int32)
    t = jnp.arange(n, dtype=jnp.int32)
    c, s = _cis((f[:, None] * t[None, :]) % big, big)
    fwd_sig = _blockmat(c, -s)
    c, s = _cis((f[:, None] * f[None, :]) % big, big)
    fwd_fil = jnp.concatenate([c, -s], axis=0)
    c, s = _cis((t[:, None] * f[None, :]) % big, big)
    inv = _blockmat(c, s) * (1.0 / big)
    return tuple(_split_bf16(m) for m in (fwd_sig, fwd_fil, inv))


def _filter_features(n):
    pos = jnp.arange(n, dtype=F32)
    t = (pos / max(n - 1, 1))[:, None]
    w = 2.0 * math.pi * pos[:, None] / n
    bands = jnp.linspace(1e-4, HY_BANDS - 1, HY_BANDS, dtype=F32)
    z = jnp.concatenate([t, jnp.cos(bands * w), -jnp.sin(bands * w)], axis=-1)
    return jnp.pad(z, ((0, 0), (0, LANE - HY_EMB_DIM)))


def _circular_filters(h, big):
    n = h.shape[0]
    fwd = jnp.moveaxis(h[:, 0], 1, 0)
    bwd = jnp.moveaxis(h[:, 1], 1, 0)
    zeros = jnp.zeros((fwd.shape[0], big - 2 * n + 1, fwd.shape[2]), F32)
    return jnp.concatenate([fwd, zeros, bwd[:, :0:-1]], axis=1)


def _pick(total, candidates):
    for t in candidates:
        if total % t == 0:
            return t
    raise ValueError(f"no tile for {total}")


def kernel(x, c, ctx, c_ctx, norm1_g, norm2_g, w_mod, b_mod, w_in, qn_a, kn_a, qn_d, kn_d, lam_q1, lam_k1, lam_q2, lam_k2, hy_conv_w, hy_conv_b, hy_fw1, hy_fb1, hy_fw2, hy_fb2, hy_fw3, hy_fb3, hy_freq, hy_bias, g_out, w_out, w_up, ffn_conv_w, ffn_conv_b, w_down):
    bsz, n_tok, d = x.shape
    ctx_len = ctx.shape[1]
    depth = w_in.shape[0]
    s_len = ctx_len + n_tok
    d_ff = w_down.shape[1]
    assert bsz % 2 == 0 and n_tok % (DFT_N2 * 8) == 0 and ctx_len % LANE == 0

    tm = _pick(s_len, (768, 512, 256, 128))
    tq = _pick(ctx_len, (256, 128))
    tk = _pick(s_len, (2816, 768, 256))
    fc = _pick(d_ff, (512, 256, 128))

    n_rows = n_tok // GRID_W
    rows = jnp.repeat(jnp.arange(n_rows, dtype=F32), GRID_W)
    cols = jnp.tile(jnp.arange(GRID_W, dtype=F32), n_rows)
    n_freq = HEAD_DIM // 4
    inv = ROPE_THETA ** (-jnp.arange(n_freq, dtype=F32) / n_freq)
    ang = jnp.concatenate([rows[:, None] * inv, cols[:, None] * inv], axis=-1)
    ang = jnp.concatenate([jnp.zeros((ctx_len, HALF), F32), ang], axis=0)
    cosT, sinT = jnp.cos(ang).T, jnp.sin(ang).T

    sig_f, fil_f, g_tab, h_tab, inv_f = _dft_tables(n_tok)
    csig_f, cfil_f, cinv_f = _dense_dft_tables(ctx_len)
    z_lat = _filter_features(n_tok)
    z_ctx = _filter_features(ctx_len)
    max_decay = math.log(HY_TARGET) / HY_FAST_DECAY
    min_decay = math.log(HY_TARGET) / HY_SLOW_DECAY
    absd = jnp.tile(jnp.abs(jnp.linspace(min_decay, max_decay, HY_WIDTH, dtype=F32)), 4)[None, :]
    t1n = n_tok // DFT_N2
    n1 = 2 * t1n
    big = n1 * DFT_N2
    fb = 8

    perm = np.concatenate([np.arange(0, HEAD_DIM, 2), np.arange(1, HEAD_DIM, 2)])
    qk_cols = np.concatenate(
        [OFF_QA + b * HEAD_DIM + perm for b in range(A_HEADS)]
        + [OFF_QD + b * HEAD_DIM + perm for b in range(2 * D_HEADS)]
        + [OFF_KA + b * HEAD_DIM + perm for b in range(A_KV_HEADS)]
        + [OFF_KD + b * HEAD_DIM + perm for b in range(2 * D_HEADS)])
    v_cols = np.concatenate([np.arange(OFF_VA, OFF_VA + A_KV_WIDTH), np.arange(OFF_VD, OFF_VD + D_WIDTH)])
    qkv_cols = np.concatenate([qk_cols, v_cols])
    wqT = jnp.swapaxes(w_in[:, :, qkv_cols], 1, 2).astype(BF16)
    why = w_in[:, :, OFF_HY:OFF_HY + HY_IN].astype(BF16)
    q_scale = HEAD_DIM ** -0.5 * math.log2(math.e)
    gains = jnp.concatenate(
        [jnp.tile(qn_a[:, perm], (1, A_HEADS)) * q_scale, jnp.tile(qn_d[:, perm], (1, 2 * D_HEADS)) * q_scale,
         jnp.tile(kn_a[:, perm], (1, A_KV_HEADS)), jnp.tile(kn_d[:, perm], (1, 2 * D_HEADS))],
        axis=1)[:, :, None].astype(F32)
    w_out_b = w_out.astype(BF16)
    w_up_a = w_up[:, :, :d_ff].astype(BF16)
    w_up_v = w_up[:, :, d_ff:].astype(BF16)
    w_down_b = w_down.astype(BF16)
    fw1p = jnp.pad(hy_fw1, ((0, 0), (0, LANE - HY_EMB_DIM), (0, 0)))
    lamv = jnp.stack([lam_q1, lam_k1, lam_q2, lam_k2], axis=1)
    lam_init = jnp.asarray([0.8 - 0.6 * math.exp(-0.3 * i) for i in range(depth)], F32)
    lami = jnp.broadcast_to(lam_init[:, None, None], (depth, 8, LANE))

    n_c = 8 * ((bsz + 1 + 7) // 8)
    cvec = jnp.concatenate([c, c_ctx[None, :], jnp.zeros((n_c - bsz - 1, d), F32)], axis=0)
    mods = _modulation(cvec, w_mod, b_mod).reshape(depth, n_c, 6, d)
    pad2 = jnp.zeros((depth, bsz, 2, d), F32)
    mod_all = jnp.concatenate(
        [jnp.broadcast_to(mods[:, bsz:bsz + 1], (depth, bsz, 6, d)), pad2, mods[:, :bsz], pad2], axis=2)

    xs = jnp.concatenate([ctx, x], axis=1)
    for i in range(depth):
        mod = mod_all[i]
        qT, k, vTa, vTd, hy = _projection(xs, mod, norm1_g[i][None, :], wqT[i], why[i], gains[i], cosT, sinT,
                                    ctx_len=ctx_len, tm=tm)
        ya = _attention_a(qT, k, vTa, g_out[i, :A_WIDTH, None], ctx_len=ctx_len, tq=tq, tk=tk)
        yd = _attention_d(qT, k, vTd, g_out[i, A_WIDTH + HY_WIDTH:].reshape(D_HEADS, D_VALUE_DIM, 1),
                          lamv[i], lami[i], ctx_len=ctx_len, tq=tq, tk=tk)

        u_lat, u_ctx = _hyena_short_conv(hy, hy_conv_w[i], hy_conv_b[i], ctx_len=ctx_len)
        fil_args = (fw1p[i], hy_fb1[i][None], hy_fw2[i], hy_fb2[i][None], hy_fw3[i], hy_fb3[i][None],
                    hy_freq[i][None], absd)
        h_lat, sum_lat = _hyena_filters(z_lat, *fil_args)
        h_ctx, sum_ctx = _hyena_filters(z_ctx, *fil_args)
        ss_lat = jnp.swapaxes(sum_lat.reshape(2, 2, HY_WIDTH), 0, 1)
        ss_ctx = jnp.swapaxes(sum_ctx.reshape(2, 2, HY_WIDTH), 0, 1)
        k_lat = _circular_filters(h_lat.reshape(n_tok, 2, 2, HY_WIDTH), big)
        k_ctx = _circular_filters(h_ctx.reshape(ctx_len, 2, 2, HY_WIDTH), 2 * ctx_len)

        k6 = jnp.moveaxis(k_lat.reshape(1, 2, n1, DFT_N2, 2, LANE), 4, 1)
        ak = _outer_forward(fil_f[0], fil_f[1], k6, 0)
        kf = _filter_spectrum(g_tab[0], g_tab[1], ak, ss_lat, fb=fb)
        p_n = bsz // 2
        u6 = u_lat.reshape(3, 2, p_n, 2 * t1n, DFT_N2, LANE)

        def long_conv(z6, g_z, g_gate, order):
            a = _outer_forward(sig_f[0], sig_f[1], z6, g_z)
            cc = _spectral_multiply(g_tab[0], g_tab[1], h_tab[0], h_tab[1], kf, a, order, fb=fb)
            return _outer_inverse(inv_f[0], inv_f[1], cc, u6, g_gate, z6, g_z, hy_bias[i, order][None, :])

        z1 = long_conv(u6, 0, 1, 0)
        yh2 = long_conv(z1[None], 0, 2, 1).reshape(2, bsz, n_tok, LANE)
        yh_lat = jnp.concatenate([yh2[0], yh2[1]], axis=-1)

        kf_c = _ctx_filter_spectrum(cfil_f[0], cfil_f[1], k_ctx, ss_ctx)
        yh_ctx = _ctx_hyena(csig_f[0], csig_f[1], cinv_f[0], cinv_f[1], kf_c,
                            u_ctx.reshape(3, p_n, 2 * ctx_len, HY_WIDTH), hy_bias[i])
        yh = jnp.concatenate([yh_ctx.reshape(bsz, ctx_len, HY_WIDTH), yh_lat], axis=1)

        xs = _merge(xs, ya, yh, yd, g_out[i, A_WIDTH:A_WIDTH + HY_WIDTH][None, :], w_out_b[i], mod,
                    ctx_len=ctx_len, tm=tm)
        xs = _ffn(xs, mod, norm2_g[i][None, :], w_up_a[i], w_up_v[i], ffn_conv_w[i], ffn_conv_b[i][None, :],
                  w_down_b[i], ctx_len=ctx_len, tm=tm, fc=fc)
    return xs[:, ctx_len:]
```

```python
import functools
import math

import numpy as np
import jax
import jax.numpy as jnp
from jax import lax
from jax.experimental import pallas as pl
from jax.experimental.pallas import tpu as pltpu

F32 = jnp.float32
BF16 = jnp.bfloat16

HEAD_DIM = 64
HALF = HEAD_DIM // 2
A_HEADS = 4
A_KV_HEADS = 2
A_WIDTH = A_HEADS * HEAD_DIM
A_KV_WIDTH = A_KV_HEADS * HEAD_DIM
HY_WIDTH = 256
HY_IN = 3 * HY_WIDTH
HY_BANDS = 16
HY_EMB_DIM = 1 + 2 * HY_BANDS
HY_HIDDEN = 64
HY_FILTER_CH = 4 * HY_WIDTH
HY_FAST_DECAY = 0.3
HY_SLOW_DECAY = 1.5
HY_TARGET = 1e-2
D_HEADS = 4
D_VALUE_DIM = 2 * HEAD_DIM
D_WIDTH = D_HEADS * D_VALUE_DIM
D_QK_WIDTH = D_HEADS * 2 * HEAD_DIM
MIX_WIDTH = A_WIDTH + HY_WIDTH + D_WIDTH
OFF_QA = 0
OFF_KA = A_WIDTH
OFF_VA = OFF_KA + A_KV_WIDTH
OFF_HY = OFF_VA + A_KV_WIDTH
OFF_QD = OFF_HY + HY_IN
OFF_KD = OFF_QD + D_QK_WIDTH
OFF_VD = OFF_KD + D_QK_WIDTH
GRID_W = 64
ROPE_THETA = 10000.0
EPS = 1e-6

N_QBLK = A_HEADS + 2 * D_HEADS
N_KBLK = A_KV_HEADS + 2 * D_HEADS
Q_ROWS = N_QBLK * HEAD_DIM
K_ROWS = N_KBLK * HEAD_DIM
V_ROWS = A_KV_WIDTH + D_WIDTH
QK_ROWS = Q_ROWS + K_ROWS
QKV_ROWS = QK_ROWS + V_ROWS
SUM_ROWS = 16
VA_BLK = HEAD_DIM + SUM_ROWS
VD_BLK = D_VALUE_DIM + SUM_ROWS

LANE = 128
DFT_N2 = 128
KEY_BLOCK = 256
VMEM_LIMIT = 56 * 1024 * 1024
HIGHEST = lax.Precision.HIGHEST


def _cparams(sem):
    return pltpu.CompilerParams(dimension_semantics=sem, vmem_limit_bytes=VMEM_LIMIT)


def _split_bf16(x):
    hi = x.astype(BF16)
    lo = (x - hi.astype(F32)).astype(BF16)
    return hi, lo


def _dot(a, b):
    return jnp.dot(a, b, preferred_element_type=F32)


def _dot3(a_hi, a_lo, b):
    b_hi, b_lo = _split_bf16(b)
    return _dot(a_hi, b_hi) + (_dot(a_hi, b_lo) + _dot(a_lo, b_hi))


def _dot_p(a_hi, a_lo, b, passes):
    return _dot3(a_hi, a_lo, b) if passes == 3 else _dot(a_hi, b.astype(BF16))


def _mod_kernel(c_ref, w_ref, b_ref, o_ref):
    c = c_ref[...]
    s = c * (1.0 / (1.0 + jnp.exp(-c)))
    o_ref[0] = jnp.dot(s, w_ref[0], preferred_element_type=F32, precision=HIGHEST) + b_ref[0]


def _modulation(cvec, w_mod, b_mod):
    depth, d, n6 = w_mod.shape
    tn = 1024
    return pl.pallas_call(
        _mod_kernel,
        out_shape=jax.ShapeDtypeStruct((depth, cvec.shape[0], n6), F32),
        grid=(depth, n6 // tn),
        in_specs=[
            pl.BlockSpec(cvec.shape, lambda l, j: (0, 0)),
            pl.BlockSpec((1, d, tn), lambda l, j: (l, 0, j)),
            pl.BlockSpec((1, 1, tn), lambda l, j: (l, 0, j)),
        ],
        out_specs=pl.BlockSpec((1, cvec.shape[0], tn), lambda l, j: (l, 0, j)),
        compiler_params=_cparams(("arbitrary", "arbitrary")),
        name="modulation",
    )(cvec, w_mod, b_mod.reshape(depth, 1, n6))


def _norm_modulate(x, g, mod_ref, is_ctx, which):
    ms = jnp.mean(x * x, axis=-1, keepdims=True)
    xn = x * lax.rsqrt(ms + EPS) * g
    sh = jnp.where(is_ctx, mod_ref[0, 3 * which:3 * which + 1, :], mod_ref[0, 8 + 3 * which:9 + 3 * which, :])
    sc = jnp.where(is_ctx, mod_ref[0, 3 * which + 1:3 * which + 2, :],
                   mod_ref[0, 9 + 3 * which:10 + 3 * which, :])
    return xn * (1.0 + sc) + sh


def _proj_kernel(x_ref, mod_ref, g_ref, wq_ref, why_ref, gain_ref, cos_ref, sin_ref,
                 qT_ref, k_ref, vTa_ref, vTd_ref, hy_ref, *, tm, ctx_len):
    i = pl.program_id(1)
    row = i * tm + lax.broadcasted_iota(jnp.int32, (tm, 1), 0)
    h = _norm_modulate(x_ref[0], g_ref[...], mod_ref, row < ctx_len, 0).astype(BF16)
    pT = lax.dot_general(wq_ref[...], h, (((1,), (1,)), ((), ())), preferred_element_type=F32)
    hy_ref[0] = _dot(h, why_ref[...])
    c = cos_ref[...]
    s = sin_ref[...]
    blocks = []
    for b in range(N_QBLK + N_KBLK):
        blk = pT[b * HEAD_DIM:(b + 1) * HEAD_DIM]
        ssq = jnp.sum(blk * blk, axis=0, keepdims=True)
        blk = blk * lax.rsqrt(ssq * (1.0 / HEAD_DIM) + EPS) * gain_ref[b * HEAD_DIM:(b + 1) * HEAD_DIM, :]
        x1 = blk[:HALF]
        x2 = blk[HALF:]
        blocks.append(x1 * c - x2 * s)
        blocks.append(x1 * s + x2 * c)
    qT_ref[0] = jnp.concatenate(blocks[:2 * N_QBLK], axis=0).astype(BF16)
    kT = jnp.concatenate(blocks[2 * N_QBLK:], axis=0)
    k_ref[0] = kT.T.astype(BF16)
    ones = jnp.ones((SUM_ROWS, tm), BF16)
    va = [pT[QK_ROWS + g * HEAD_DIM:QK_ROWS + (g + 1) * HEAD_DIM].astype(BF16) for g in range(A_KV_HEADS)]
    vTa_ref[0] = jnp.concatenate([t for v in va for t in (v, ones)], axis=0)
    vd0 = QK_ROWS + A_KV_WIDTH
    vd = [pT[vd0 + h * D_VALUE_DIM:vd0 + (h + 1) * D_VALUE_DIM].astype(BF16) for h in range(D_HEADS)]
    vTd_ref[0] = jnp.concatenate([t for v in vd for t in (v, ones)], axis=0)


def _projection(xs, mod, g, wqT, why, gains, cosT, sinT, *, ctx_len, tm):
    bsz, s_len, d = xs.shape
    kern = functools.partial(_proj_kernel, tm=tm, ctx_len=ctx_len)
    return pl.pallas_call(
        kern,
        out_shape=(
            jax.ShapeDtypeStruct((bsz, Q_ROWS, s_len), BF16),
            jax.ShapeDtypeStruct((bsz, s_len, K_ROWS), BF16),
            jax.ShapeDtypeStruct((bsz, A_KV_HEADS * VA_BLK, s_len), BF16),
            jax.ShapeDtypeStruct((bsz, D_HEADS * VD_BLK, s_len), BF16),
            jax.ShapeDtypeStruct((bsz, s_len, HY_IN), F32),
        ),
        grid=(bsz, s_len // tm),
        in_specs=[
            pl.BlockSpec((1, tm, d), lambda b, i: (b, i, 0)),
            pl.BlockSpec((1, 16, d), lambda b, i: (b, 0, 0)),
            pl.BlockSpec((1, d), lambda b, i: (0, 0)),
            pl.BlockSpec((QKV_ROWS, d), lambda b, i: (0, 0)),
            pl.BlockSpec((d, HY_IN), lambda b, i: (0, 0)),
            pl.BlockSpec((QK_ROWS, 1), lambda b, i: (0, 0)),
            pl.BlockSpec((HALF, tm), lambda b, i: (0, i)),
            pl.BlockSpec((HALF, tm), lambda b, i: (0, i)),
        ],
        out_specs=(
            pl.BlockSpec((1, Q_ROWS, tm), lambda b, i: (b, 0, i)),
            pl.BlockSpec((1, tm, K_ROWS), lambda b, i: (b, i, 0)),
            pl.BlockSpec((1, A_KV_HEADS * VA_BLK, tm), lambda b, i: (b, 0, i)),
            pl.BlockSpec((1, D_HEADS * VD_BLK, tm), lambda b, i: (b, 0, i)),
            pl.BlockSpec((1, tm, HY_IN), lambda b, i: (b, i, 0)),
        ),
        compiler_params=_cparams(("parallel", "parallel")),
        name="projection",
    )(xs, mod, g, wqT, why, gains, cosT, sinT)


def _attend(k_ref, vT_ref, v_rows, scr, *, ctx_only, ctx_len, tk):
    n_u = len(v_rows)
    qz, m_run, acc, s_buf, cm_buf = _split_scratch(scr, n_u)
    s_len = k_ref.shape[1]
    for u in range(n_u):
        m_run[u][...] = jnp.full(m_run[u].shape, -1e30, F32)
        acc[u][...] = jnp.zeros(acc[u].shape, F32)

    def colmax8(s):
        return jnp.max(s.reshape(s.shape[0] // 8, 8, s.shape[1]), axis=0)

    def key_rows(start, j, rb):
        if isinstance(start, int):
            return pl.ds(start + j * rb, rb)
        return pl.ds(pl.multiple_of(start + j * rb, rb), rb)

    def advance(nxt, cur, size):
        rb = min(size, KEY_BLOCK)
        if cur is not None:
            u_c, start_c, slot_c = cur
            v0, dv = v_rows[u_c]
            m = m_run[u_c][...]
            m_new = jnp.maximum(m, jnp.max(cm_buf[slot_c][...], axis=0, keepdims=True))
            m_run[u_c][...] = m_new
            alpha = jnp.exp2(m - m_new)
        cmax, pv = None, None
        for j in range(size // rb):
            rows = slice(j * rb, (j + 1) * rb)
            if nxt is not None:
                u_n, start_n, slot_n = nxt
                s = _dot(k_ref[0, key_rows(start_n, j, rb), :], qz[u_n][...])
                s_buf[slot_n][rows, :] = s
                cmax = colmax8(s) if cmax is None else jnp.maximum(cmax, colmax8(s))
            if cur is not None:
                p = jnp.exp2(s_buf[slot_c][rows, :] - m_new).astype(BF16)
                d = _dot(vT_ref[0, v0:v0 + dv, key_rows(start_c, j, rb)], p)
                pv = d if pv is None else pv + d
        if nxt is not None:
            cm_buf[slot_n][...] = cmax
        if cur is not None:
            acc[u_c][...] = alpha * acc[u_c][...] + pv

    if ctx_only:
        for u in range(n_u):
            advance((u, 0, u % 2), None, ctx_len)
            advance(None, (u, 0, u % 2), ctx_len)
    else:
        n_chunks = s_len // tk
        advance((0, 0, 0), None, tk)

        def body(c, carry):
            start = pl.multiple_of(c * tk, tk)
            nxt = pl.multiple_of(jnp.minimum(c + 1, n_chunks - 1) * tk, tk)
            for u in range(n_u):
                task_n = (u + 1, start, (u + 1) % 2) if u + 1 < n_u else (0, nxt, (u + 1) % 2)
                advance(task_n, (u, start, u % 2), tk)
            return carry
        lax.fori_loop(0, n_chunks, body, 0)
    outs = []
    for u in range(n_u):
        dv = v_rows[u][1] - SUM_ROWS
        outs.append(acc[u][:dv, :] * (1.0 / acc[u][dv:dv + 1, :]))
    return outs


def _attn_scratch(n_u, rows, tq, tk):
    per_unit = [((LANE, tq), BF16), ((1, tq), F32), ((rows, tq), F32)]
    per_slot = [((tk, tq), F32), ((8, tq), F32)]
    return ([pltpu.VMEM(s, d) for s, d in per_unit for _ in range(n_u)]
            + [pltpu.VMEM(s, d) for s, d in per_slot for _ in range(2)])


def _split_scratch(scr, n_u):
    groups = [scr[i * n_u:(i + 1) * n_u] for i in range(3)]
    rest = scr[3 * n_u:]
    return groups + [rest[0:2], rest[2:4]]


def _attn_a_kernel(qT_ref, k_ref, vT_ref, g_ref, o_ref, *scr, ctx_len, tq, tk):
    qi = pl.program_id(1)
    zeros = jnp.zeros((HEAD_DIM, tq), BF16)
    group = A_HEADS // A_KV_HEADS
    for h in range(A_HEADS):
        q = qT_ref[0, h * HEAD_DIM:(h + 1) * HEAD_DIM, :]
        scr[h][...] = jnp.concatenate([q, zeros] if h // group == 0 else [zeros, q], axis=0)
    v_rows = [((h // group) * VA_BLK, VA_BLK) for h in range(A_HEADS)]

    def run(ctx_only):
        outs = _attend(k_ref, vT_ref, v_rows, scr, ctx_only=ctx_only, ctx_len=ctx_len, tk=tk)
        y = jnp.concatenate(outs, axis=0)
        ms = jnp.mean(y * y, axis=0, keepdims=True)
        y = y * lax.rsqrt(ms + EPS) * g_ref[...]
        o_ref[0] = y.T

    n_ctx_tiles = ctx_len // tq

    @pl.when(qi < n_ctx_tiles)
    def _():
        run(True)

    @pl.when(qi >= n_ctx_tiles)
    def _():
        run(False)


def _attention_a(qT, k, vT, g_a, *, ctx_len, tq, tk):
    bsz, _, s_len = qT.shape
    kern = functools.partial(_attn_a_kernel, ctx_len=ctx_len, tq=tq, tk=tk)
    return pl.pallas_call(
        kern,
        out_shape=jax.ShapeDtypeStruct((bsz, s_len, A_WIDTH), F32),
        grid=(bsz, s_len // tq),
        in_specs=[
            pl.BlockSpec((1, A_WIDTH, tq), lambda b, i: (b, 0, i)),
            pl.BlockSpec((1, s_len, LANE), lambda b, i: (b, 0, 0)),
            pl.BlockSpec((1, A_KV_HEADS * VA_BLK, s_len), lambda b, i: (b, 0, 0)),
            pl.BlockSpec((A_WIDTH, 1), lambda b, i: (0, 0)),
        ],
        out_specs=pl.BlockSpec((1, tq, A_WIDTH), lambda b, i: (b, i, 0)),
        scratch_shapes=_attn_scratch(A_HEADS, VA_BLK, tq, tk),
        compiler_params=_cparams(("parallel", "parallel")),
        name="attention_gqa",
    )(qT, k, vT, g_a)


def _attn_d_kernel(qT_ref, k_ref, vT_ref, g_ref, lamv_ref, lami_ref, o_ref, *scr, ctx_len, tq, tk):
    qi = pl.program_id(2)
    zeros = jnp.zeros((HEAD_DIM, tq), BF16)
    lv = lamv_ref[...]
    lam_init = lami_ref[0:1, 0:1]
    lam = (jnp.exp(jnp.sum(lv[0:1] * lv[1:2], axis=-1, keepdims=True))
           - jnp.exp(jnp.sum(lv[2:3] * lv[3:4], axis=-1, keepdims=True)) + lam_init)

    scr[0][...] = jnp.concatenate([qT_ref[0, :HEAD_DIM, :], zeros], axis=0)
    scr[1][...] = jnp.concatenate([zeros, qT_ref[0, HEAD_DIM:, :]], axis=0)
    v_rows = [(0, VD_BLK), (0, VD_BLK)]

    def run(ctx_only):
        o1, o2 = _attend(k_ref, vT_ref, v_rows, scr, ctx_only=ctx_only, ctx_len=ctx_len, tk=tk)
        y = o1 - lam * o2
        ms = jnp.mean(y * y, axis=0, keepdims=True)
        y = y * lax.rsqrt(ms + EPS) * g_ref[0] * (1.0 - lam_init)
        o_ref[0] = y.T

    n_ctx_tiles = ctx_len // tq

    @pl.when(qi < n_ctx_tiles)
    def _():
        run(True)

    @pl.when(qi >= n_ctx_tiles)
    def _():
        run(False)


def _attention_d(qT, k, vT, g_d, lamv, lami, *, ctx_len, tq, tk):
    bsz, _, s_len = qT.shape
    kern = functools.partial(_attn_d_kernel, ctx_len=ctx_len, tq=tq, tk=tk)
    q_blk0 = A_WIDTH // LANE
    k_blk0 = A_KV_WIDTH // LANE
    return pl.pallas_call(
        kern,
        out_shape=jax.ShapeDtypeStruct((bsz, s_len, D_WIDTH), F32),
        grid=(bsz, D_HEADS, s_len // tq),
        in_specs=[
            pl.BlockSpec((1, LANE, tq), lambda b, h, i: (b, q_blk0 + h, i)),
            pl.BlockSpec((1, s_len, LANE), lambda b, h, i: (b, 0, k_blk0 + h)),
            pl.BlockSpec((1, VD_BLK, s_len), lambda b, h, i: (b, h, 0)),
            pl.BlockSpec((1, D_VALUE_DIM, 1), lambda b, h, i: (h, 0, 0)),
            pl.BlockSpec((4, HEAD_DIM), lambda b, h, i: (0, 0)),
            pl.BlockSpec((8, LANE), lambda b, h, i: (0, 0)),
        ],
        out_specs=pl.BlockSpec((1, tq, D_VALUE_DIM), lambda b, h, i: (b, i, h)),
        scratch_shapes=_attn_scratch(2, VD_BLK, tq, tk),
        compiler_params=_cparams(("parallel", "parallel", "parallel")),
        name="attention_diff",
    )(qT, k, vT, g_d, lamv, lami)


def _dwconv3_seq(u, w_ref, b_ref):
    n = u.shape[0]
    row = lax.broadcasted_iota(jnp.int32, (n, 1), 0)
    prev = jnp.where(row == 0, 0.0, pltpu.roll(u, 1, axis=0))
    nxt = jnp.where(row == n - 1, 0.0, pltpu.roll(u, n - 1, axis=0))
    return prev * w_ref[0:1, :] + u * w_ref[1:2, :] + nxt * w_ref[2:3, :] + b_ref[...]


def _hy_conv_kernel(hy_ref, w_ref, b_ref, lat_ref, ctx_ref, *, ctx_len):
    ctx_ref[0, 0] = _dwconv3_seq(hy_ref[0, :ctx_len, :], w_ref, b_ref)
    lat_ref[0, 0, 0] = _dwconv3_seq(hy_ref[0, ctx_len:, :], w_ref, b_ref)


def _hyena_short_conv(hy, w, b, *, ctx_len):
    bsz, s_len, _ = hy.shape
    n = s_len - ctx_len
    per = HY_WIDTH // LANE
    kern = functools.partial(_hy_conv_kernel, ctx_len=ctx_len)
    return pl.pallas_call(
        kern,
        out_shape=(
            jax.ShapeDtypeStruct((3, per, bsz, n, LANE), F32),
            jax.ShapeDtypeStruct((3, bsz, ctx_len, HY_WIDTH), F32),
        ),
        grid=(bsz, HY_IN // LANE),
        in_specs=[
            pl.BlockSpec((1, s_len, LANE), lambda b, j: (b, 0, j)),
            pl.BlockSpec((3, LANE), lambda b, j: (0, j)),
            pl.BlockSpec((1, LANE), lambda b, j: (0, j)),
        ],
        out_specs=(
            pl.BlockSpec((1, 1, 1, n, LANE), lambda b, j: (j // per, j % per, b, 0, 0)),
            pl.BlockSpec((1, 1, ctx_len, LANE), lambda b, j: (j // per, b, 0, j % per)),
        ),
        compiler_params=_cparams(("parallel", "parallel")),
        name="hyena_short_conv",
    )(hy, w, b.reshape(1, HY_IN))


def _filter_kernel(z_ref, w1_ref, b1_ref, w2_ref, b2_ref, w3_ref, b3_ref, fr_ref, ad_ref, h_ref, sum_ref):
    i = pl.program_id(0)
    z = z_ref[...]
    fr = fr_ref[...]
    h = jnp.sin(fr * (jnp.dot(z, w1_ref[...], preferred_element_type=F32, precision=HIGHEST) + b1_ref[...]))
    h = jnp.sin(fr * (jnp.dot(h, w2_ref[...], preferred_element_type=F32, precision=HIGHEST) + b2_ref[...]))
    h = jnp.dot(h, w3_ref[...], preferred_element_type=F32, precision=HIGHEST) + b3_ref[...]
    h = h * jnp.exp(-z[:, 0:1] * ad_ref[...])
    for q in range(HY_FILTER_CH // HY_WIDTH):
        for hh in range(HY_WIDTH // LANE):
            c0 = q * HY_WIDTH + hh * LANE
            h_ref[hh, q] = h[:, c0:c0 + LANE]

    @pl.when(i == 0)
    def _():
        sum_ref[...] = jnp.zeros_like(sum_ref)

    sum_ref[...] += jnp.sum(jnp.abs(h), axis=0, keepdims=True)


def _hyena_filters(zfeat, w1, b1, w2, b2, w3, b3, freq, absd):
    n, kz = zfeat.shape
    tn = min(n, 1024)
    full = lambda a: pl.BlockSpec(a.shape, lambda i: (0, 0))
    return pl.pallas_call(
        _filter_kernel,
        out_shape=(
            jax.ShapeDtypeStruct((HY_WIDTH // LANE, HY_FILTER_CH // HY_WIDTH, n, LANE), F32),
            jax.ShapeDtypeStruct((1, HY_FILTER_CH), F32),
        ),
        grid=(n // tn,),
        in_specs=[pl.BlockSpec((tn, kz), lambda i: (i, 0)), full(w1), full(b1), full(w2), full(b2),
                  full(w3), full(b3), full(freq), full(absd)],
        out_specs=(
            pl.BlockSpec((HY_WIDTH // LANE, HY_FILTER_CH // HY_WIDTH, tn, LANE), lambda i: (0, 0, i, 0)),
            pl.BlockSpec((1, HY_FILTER_CH), lambda i: (0, 0)),
        ),
        compiler_params=_cparams(("arbitrary",)),
        name="hyena_filters",
    )(zfeat, w1, b1, w2, b2, w3, b3, freq, absd)


T2_BLOCK = 8


def _outer_fwd_kernel(mh_ref, ml_ref, x_ref, o_ref, *, n1, passes):
    rows = x_ref.shape[3]
    xs = [x_ref.at[0, h, 0].reshape(rows * T2_BLOCK, LANE) for h in range(2)]
    outs = [[o_ref.at[0, ri, h].reshape(n1 * T2_BLOCK, LANE) for h in range(2)] for ri in range(2)]
    for t in range(T2_BLOCK):
        sel = pl.ds(t, rows, stride=T2_BLOCK)
        y = _dot_p(mh_ref[...], ml_ref[...], jnp.concatenate([xs[0][sel, :], xs[1][sel, :]], axis=1), passes)
        for ri in range(2):
            for h in range(2):
                outs[ri][h][pl.ds(t, n1, stride=T2_BLOCK), :] = y[ri * n1:(ri + 1) * n1, h * LANE:(h + 1) * LANE]


def _outer_forward(m_hi, m_lo, x6, g, passes):
    _, _, p_n, rows, n2, _ = x6.shape
    n1 = m_hi.shape[0] // 2
    kern = functools.partial(_outer_fwd_kernel, n1=n1, passes=passes)
    return pl.pallas_call(
        kern,
        out_shape=jax.ShapeDtypeStruct((p_n, 2, 2, n1, n2, LANE), F32),
        grid=(p_n, n2 // T2_BLOCK),
        in_specs=[pl.BlockSpec(m_hi.shape, lambda p, j: (0, 0)), pl.BlockSpec(m_lo.shape, lambda p, j: (0, 0)),
                  pl.BlockSpec((1, 2, 1, rows, T2_BLOCK, LANE), lambda p, j: (g, 0, p, 0, j, 0))],
        out_specs=pl.BlockSpec((1, 2, 2, n1, T2_BLOCK, LANE), lambda p, j: (p, 0, 0, 0, j, 0)),
        compiler_params=_cparams(("parallel", "parallel")),
        name="dft_outer_fwd",
    )(m_hi, m_lo, x6)


def _outer_inv_kernel(mh_ref, c_ref, gate_ref, u_ref, bias_ref, o_ref):
    n1 = c_ref.shape[3]
    rows = o_ref.shape[2]
    cs = [[c_ref.at[0, ri, h].reshape(n1 * T2_BLOCK, LANE) for h in range(2)] for ri in range(2)]
    gates = [gate_ref.at[0, h, 0].reshape(rows * T2_BLOCK, LANE) for h in range(2)]
    us = [u_ref.at[0, h, 0].reshape(rows * T2_BLOCK, LANE) for h in range(2)]
    outs = [o_ref.at[h, 0].reshape(rows * T2_BLOCK, LANE) for h in range(2)]
    for t in range(T2_BLOCK):
        sel_f = pl.ds(t, n1, stride=T2_BLOCK)
        sel_t = pl.ds(t, rows, stride=T2_BLOCK)
        c2 = jnp.concatenate([jnp.concatenate([cs[ri][0][sel_f, :], cs[ri][1][sel_f, :]], axis=1)
                              for ri in range(2)], axis=0)
        y = _dot(mh_ref[...], c2.astype(BF16))
        for h in range(2):
            lanes = slice(h * LANE, (h + 1) * LANE)
            outs[h][sel_t, :] = gates[h][sel_t, :] * (y[:, lanes] + us[h][sel_t, :] * bias_ref[:, lanes])


def _outer_inverse(m_hi, c6, gate6, g_gate, u6, g_u, bias):
    p_n, _, _, n1, n2, _ = c6.shape
    rows = m_hi.shape[0]
    gspec = lambda g: pl.BlockSpec((1, 2, 1, rows, T2_BLOCK, LANE), lambda p, j: (g, 0, p, 0, j, 0))
    return pl.pallas_call(
        _outer_inv_kernel,
        out_shape=jax.ShapeDtypeStruct((2, p_n, rows, n2, LANE), F32),
        grid=(p_n, n2 // T2_BLOCK),
        in_specs=[pl.BlockSpec(m_hi.shape, lambda p, j: (0, 0)),
                  pl.BlockSpec((1, 2, 2, n1, T2_BLOCK, LANE), lambda p, j: (p, 0, 0, 0, j, 0)),
                  gspec(g_gate), gspec(g_u), pl.BlockSpec((1, 2 * LANE), lambda p, j: (0, 0))],
        out_specs=pl.BlockSpec((2, 1, rows, T2_BLOCK, LANE), lambda p, j: (0, p, 0, j, 0)),
        compiler_params=_cparams(("parallel", "parallel")),
        name="dft_outer_inv",
    )(m_hi, c6, gate6, u6, bias)


def _halves_to_rows(a_ref, j):
    return jnp.concatenate([jnp.concatenate([a_ref[0, ri, 0, j], a_ref[0, ri, 1, j]], axis=1) for ri in range(2)],
                           axis=0)


def _kf_kernel(gh_ref, gl_ref, af_ref, ab_ref, h0_ref, ss_ref, o_ref, *, fb, n2):
    scale = 1.0 / (ss_ref[0, 0:1, :] + ss_ref[0, 1:2, :] + EPS)
    b0 = jnp.concatenate([h0_ref[0, 0, 0:1, :], h0_ref[1, 0, 0:1, :]], axis=1)
    for j in range(fb):
        xf = _dot3(gh_ref[j], gl_ref[j], _halves_to_rows(af_ref, j))
        xb = _dot3(gh_ref[j], gl_ref[j], _halves_to_rows(ab_ref, j))
        o_ref[0, j, :n2, :] = (xf[:n2] + xb[:n2] - b0) * scale
        o_ref[0, j, n2:, :] = (xf[n2:] - xb[n2:]) * scale


def _filter_spectrum(g_hi, g_lo, a6, h6, ss, *, fb):
    n_q, _, _, n1, n2, _ = a6.shape
    n_ord = n_q // 2
    c = 2 * LANE
    kern = functools.partial(_kf_kernel, fb=fb, n2=n2)
    return pl.pallas_call(
        kern,
        out_shape=jax.ShapeDtypeStruct((n_ord, n1, 2 * n2, c), F32),
        grid=(n1 // fb, n_ord),
        in_specs=[
            pl.BlockSpec((fb, 2 * n2, 2 * n2), lambda f, o: (f, 0, 0)),
            pl.BlockSpec((fb, 2 * n2, 2 * n2), lambda f, o: (f, 0, 0)),
            pl.BlockSpec((1, 2, 2, fb, n2, LANE), lambda f, o: (o, 0, 0, f, 0, 0)),
            pl.BlockSpec((1, 2, 2, fb, n2, LANE), lambda f, o: (n_ord + o, 0, 0, f, 0, 0)),
            pl.BlockSpec((2, 1, 8, LANE), lambda f, o: (0, n_ord + o, 0, 0)),
            pl.BlockSpec((1, 2, c), lambda f, o: (o, 0, 0)),
        ],
        out_specs=pl.BlockSpec((1, fb, 2 * n2, c), lambda f, o: (o, f, 0, 0)),
        compiler_params=_cparams(("parallel", "parallel")),
        name="filter_spectrum",
    )(g_hi, g_lo, a6, a6, h6, ss)


def _spec_kernel(gh_ref, hh_ref, kf_ref, a_ref, o_ref, *, fb, n2):
    for j in range(fb):
        x = _dot(gh_ref[j], _halves_to_rows(a_ref, j).astype(BF16))
        xr, xi = x[:n2], x[n2:]
        kr, ki = kf_ref[0, j, :n2], kf_ref[0, j, n2:]
        y = jnp.concatenate([xr * kr - xi * ki, xr * ki + xi * kr], axis=0)
        c2 = _dot(hh_ref[j], y.astype(BF16))
        for ri in range(2):
            for h in range(2):
                o_ref[0, ri, h, j] = c2[ri * n2:(ri + 1) * n2, h * LANE:(h + 1) * LANE]


def _spectral_multiply(g_hi, h_hi, kf, a6, order, *, fb):
    p_n, _, _, n1, n2, _ = a6.shape
    c = 2 * LANE
    kern = functools.partial(_spec_kernel, fb=fb, n2=n2)
    gspec = pl.BlockSpec((fb, 2 * n2, 2 * n2), lambda f, p: (f, 0, 0))
    aspec = pl.BlockSpec((1, 2, 2, fb, n2, LANE), lambda f, p: (p, 0, 0, f, 0, 0))
    return pl.pallas_call(
        kern,
        out_shape=jax.ShapeDtypeStruct(a6.shape, F32),
        grid=(n1 // fb, p_n),
        in_specs=[gspec, gspec,
                  pl.BlockSpec((1, fb, 2 * n2, c), lambda f, p: (order, f, 0, 0)), aspec],
        out_specs=aspec,
        compiler_params=_cparams(("parallel", "parallel")),
        name="spectral_multiply",
    )(g_hi, h_hi, kf, a6)


def _ctx_kf_kernel(fh_ref, fl_ref, k_ref, ss_ref, o_ref):
    scale = 1.0 / (ss_ref[0, 0:1, :] + ss_ref[0, 1:2, :] + EPS)
    o_ref[0] = _dot3(fh_ref[...], fl_ref[...], k_ref[0]) * scale


def _ctx_filter_spectrum(f_hi, f_lo, k, ss):
    n_ord, nc, c = k.shape
    return pl.pallas_call(
        _ctx_kf_kernel,
        out_shape=jax.ShapeDtypeStruct((n_ord, 2 * nc, c), F32),
        grid=(n_ord,),
        in_specs=[
            pl.BlockSpec(f_hi.shape, lambda o: (0, 0)),
            pl.BlockSpec(f_lo.shape, lambda o: (0, 0)),
            pl.BlockSpec((1, nc, c), lambda o: (o, 0, 0)),
            pl.BlockSpec((1, 2, c), lambda o: (o, 0, 0)),
        ],
        out_specs=pl.BlockSpec((1, 2 * nc, c), lambda o: (o, 0, 0)),
        compiler_params=_cparams(("parallel",)),
        name="ctx_filter_spectrum",
    )(f_hi, f_lo, k, ss)


def _ctx_conv_kernel(fh_ref, fl_ref, eh_ref, el_ref, kf_ref, u_ref, bias_ref, o_ref, *, nc):
    v, x1, x2 = u_ref[0, 0], u_ref[1, 0], u_ref[2, 0]

    def conv(z, order):
        x = _dot3(fh_ref[...], fl_ref[...], z)
        xr, xi = x[:nc], x[nc:]
        kr, ki = kf_ref[order, :nc], kf_ref[order, nc:]
        y = jnp.concatenate([xr * kr - xi * ki, xr * ki + xi * kr], axis=0)
        return _dot3(eh_ref[...], el_ref[...], y) + z * bias_ref[order:order + 1, :]

    o_ref[0] = x2 * conv(x1 * conv(v, 0), 1)


def _ctx_hyena(f_hi, f_lo, e_hi, e_lo, kf, u3, bias):
    _, p_n, n2x, c = u3.shape
    nc = kf.shape[1] // 2
    kern = functools.partial(_ctx_conv_kernel, nc=nc)
    full = lambda a: pl.BlockSpec(a.shape, lambda p: (0,) * a.ndim)
    return pl.pallas_call(
        kern,
        out_shape=jax.ShapeDtypeStruct((p_n, n2x, c), F32),
        grid=(p_n,),
        in_specs=[full(f_hi), full(f_lo), full(e_hi), full(e_lo), full(kf),
                  pl.BlockSpec((3, 1, n2x, c), lambda p: (0, p, 0, 0)), full(bias)],
        out_specs=pl.BlockSpec((1, n2x, c), lambda p: (p, 0, 0)),
        compiler_params=_cparams(("parallel",)),
        name="ctx_hyena",
    )(f_hi, f_lo, e_hi, e_lo, kf, u3, bias)


def _merge_kernel(x_ref, ya_ref, yh_ref, yd_ref, gh_ref, w_ref, mod_ref, o_ref, *, tm, ctx_len):
    i = pl.program_id(1)
    row = i * tm + lax.broadcasted_iota(jnp.int32, (tm, 1), 0)
    yh = yh_ref[0]
    ms = jnp.mean(yh * yh, axis=-1, keepdims=True)
    yh = yh * lax.rsqrt(ms + EPS) * gh_ref[...]
    y = _dot(ya_ref[0].astype(BF16), w_ref[:A_WIDTH, :])
    y += _dot(yh.astype(BF16), w_ref[A_WIDTH:A_WIDTH + HY_WIDTH, :])
    y += _dot(yd_ref[0].astype(BF16), w_ref[A_WIDTH + HY_WIDTH:, :])
    gate = jnp.where(row < ctx_len, mod_ref[0, 2:3, :], mod_ref[0, 10:11, :])
    o_ref[0] = x_ref[0] + gate * y


def _merge(xs, ya, yh, yd, g_h, w_out, mod, *, ctx_len, tm):
    bsz, s_len, d = xs.shape
    kern = functools.partial(_merge_kernel, tm=tm, ctx_len=ctx_len)
    tok = lambda w: pl.BlockSpec((1, tm, w), lambda b, i: (b, i, 0))
    return pl.pallas_call(
        kern,
        out_shape=jax.ShapeDtypeStruct(xs.shape, F32),
        grid=(bsz, s_len // tm),
        in_specs=[tok(d), tok(A_WIDTH), tok(HY_WIDTH), tok(D_WIDTH),
                  pl.BlockSpec((1, HY_WIDTH), lambda b, i: (0, 0)),
                  pl.BlockSpec((MIX_WIDTH, d), lambda b, i: (0, 0)),
                  pl.BlockSpec((1, 16, d), lambda b, i: (b, 0, 0))],
        out_specs=tok(d),
        compiler_params=_cparams(("parallel", "parallel")),
        name="merge",
    )(xs, ya, yh, yd, g_h, w_out, mod)


HALO = 8


def _ffn_kernel(x_ref, xp_ref, xn_ref, mod_ref, g_ref, wa_ref, wv_ref, cw_ref, cb_ref, wd_ref, o_ref,
                *, tm, fc, ctx_len, s_len):
    i = pl.program_id(1)
    x = x_ref[0]
    xa = jnp.concatenate([xp_ref[0], x, xn_ref[0]], axis=0)
    rows = tm + 2 * HALO
    row = i * tm - HALO + lax.broadcasted_iota(jnp.int32, (rows, 1), 0)
    h = _norm_modulate(xa, g_ref[...], mod_ref, row < ctx_len, 1).astype(BF16)
    hm = h[HALO:HALO + tm]
    rowm = row[HALO:HALO + tm]
    has_prev = jnp.logical_and(rowm != 0, rowm != ctx_len)
    has_next = jnp.logical_and(rowm != ctx_len - 1, rowm != s_len - 1)
    def up(j):
        cs = slice(j * fc, (j + 1) * fc)
        return _dot(h, wa_ref[:, cs]), _dot(hm, wv_ref[:, cs])

    n_chunks = wa_ref.shape[1] // fc
    acc = jnp.zeros((tm, x.shape[1]), F32)
    nxt = up(0)
    for j in range(n_chunks):
        cs = slice(j * fc, (j + 1) * fc)
        a, v = nxt
        if j + 1 < n_chunks:
            nxt = up(j + 1)
        a_prev = jnp.where(has_prev, pltpu.roll(a, 1, axis=0)[HALO:HALO + tm], 0.0)
        a_next = jnp.where(has_next, pltpu.roll(a, rows - 1, axis=0)[HALO:HALO + tm], 0.0)
        ac = a_prev * cw_ref[0:1, cs] + a[HALO:HALO + tm] * cw_ref[1:2, cs] + a_next * cw_ref[2:3, cs] + cb_ref[:, cs]
        gl = 0.5 * ac * (1.0 + jnp.tanh(math.sqrt(2.0 / math.pi) * (ac + 0.044715 * (ac * ac * ac))))
        acc += _dot((gl * v).astype(BF16), wd_ref[cs, :])
    gate = jnp.where(rowm < ctx_len, mod_ref[0, 5:6, :], mod_ref[0, 13:14, :])
    o_ref[0] = x + gate * acc


def _ffn(xs, mod, g, wa, wv, cw, cb, wd, *, ctx_len, tm, fc):
    bsz, s_len, d = xs.shape
    d_ff = wa.shape[1]
    nh = tm // HALO
    last = s_len // HALO - 1
    kern = functools.partial(_ffn_kernel, tm=tm, fc=fc, ctx_len=ctx_len, s_len=s_len)
    const = lambda shape: pl.BlockSpec(shape, lambda b, i: (0, 0), pipeline_mode=pl.Buffered(1))
    return pl.pallas_call(
        kern,
        out_shape=jax.ShapeDtypeStruct(xs.shape, F32),
        grid=(bsz, s_len // tm),
        in_specs=[
            pl.BlockSpec((1, tm, d), lambda b, i: (b, i, 0)),
            pl.BlockSpec((1, HALO, d), lambda b, i: (b, jnp.maximum(i * nh - 1, 0), 0)),
            pl.BlockSpec((1, HALO, d), lambda b, i: (b, jnp.minimum((i + 1) * nh, last), 0)),
            pl.BlockSpec((1, 16, d), lambda b, i: (b, 0, 0)),
            pl.BlockSpec((1, d), lambda b, i: (0, 0)),
            const((d, d_ff)), const((d, d_ff)), const((3, d_ff)), const((1, d_ff)), const((d_ff, d)),
        ],
        out_specs=pl.BlockSpec((1, tm, d), lambda b, i: (b, i, 0)),
        compiler_params=_cparams(("parallel", "parallel")),
        name="conv_glu",
    )(xs, xs, xs, mod, g, wa, wv, cw, cb, wd)


def _cis(num, den):
    ang = num.astype(F32) * (2.0 * math.pi / den)
    return jnp.cos(ang), jnp.sin(ang)


def _blockmat(re, im):
    return jnp.concatenate([jnp.concatenate([re, -im], axis=-1), jnp.concatenate([im, re], axis=-1)], axis=-2)


def _dft_tables(n):
    n2 = DFT_N2
    t1n = n // n2
    n1 = 2 * t1n
    big = n1 * n2
    f1 = jnp.arange(n1, dtype=jnp.int32)
    t1 = jnp.arange(t1n, dtype=jnp.int32)
    j2 = jnp.arange(n2, dtype=jnp.int32)
    c, s = _cis((f1[:, None] * t1[None, :]) % n1, n1)
    fwd_sig = _blockmat(c, -s)
    c, s = _cis((f1[:, None] * f1[None, :]) % n1, n1)
    fwd_fil = jnp.concatenate([c, -s], axis=0)[:, :t1n]
    ph = (n1 * j2[None, :, None] * j2[None, None, :] + f1[:, None, None] * j2[None, None, :]) % big
    c, s = _cis(ph, big)
    g = _blockmat(c, -s)
    h = jnp.swapaxes(g, 1, 2)
    c, s = _cis((t1[:, None] * f1[None, :]) % n1, n1)
    inv = _blockmat(c, s) * (1.0 / big)
    return tuple(_split_bf16(m) for m in (fwd_sig, fwd_fil, g, h, inv))


def _dense_dft_tables(n):
    big = 2 * n
    f = jnp.arange(big, dtype=jnp.int32)
    t = jnp.arange(n, dtype=jnp.int32)
    c, s = _cis((f[:, None] * t[None, :]) % big, big)
    fwd_sig = _blockmat(c, -s)
    c, s = _cis((f[:, None] * f[None, :]) % big, big)
    fwd_fil = jnp.concatenate([c, -s], axis=0)
    c, s = _cis((t[:, None] * f[None, :]) % big, big)
    inv = _blockmat(c, s) * (1.0 / big)
    return tuple(_split_bf16(m) for m in (fwd_sig, fwd_fil, inv))


def _filter_features(n):
    pos = jnp.arange(n, dtype=F32)
    t = (pos / max(n - 1, 1))[:, None]
    w = 2.0 * math.pi * pos[:, None] / n
    bands = jnp.linspace(1e-4, HY_BANDS - 1, HY_BANDS, dtype=F32)
    z = jnp.concatenate([t, jnp.cos(bands * w), -jnp.sin(bands * w)], axis=-1)
    return jnp.pad(z, ((0, 0), (0, LANE - HY_EMB_DIM)))


def _circular_filters(h, big):
    n = h.shape[0]
    fwd = jnp.moveaxis(h[:, 0], 1, 0)
    bwd = jnp.moveaxis(h[:, 1], 1, 0)
    zeros = jnp.zeros((fwd.shape[0], big - 2 * n + 1, fwd.shape[2]), F32)
    return jnp.concatenate([fwd, zeros, bwd[:, :0:-1]], axis=1)


def _pick(total, candidates):
    for t in candidates:
        if total % t == 0:
            return t
    raise ValueError(f"no tile for {total}")


def kernel(x, c, ctx, c_ctx, norm1_g, norm2_g, w_mod, b_mod, w_in, qn_a, kn_a, qn_d, kn_d, lam_q1, lam_k1, lam_q2, lam_k2, hy_conv_w, hy_conv_b, hy_fw1, hy_fb1, hy_fw2, hy_fb2, hy_fw3, hy_fb3, hy_freq, hy_bias, g_out, w_out, w_up, ffn_conv_w, ffn_conv_b, w_down):
    bsz, n_tok, d = x.shape
    ctx_len = ctx.shape[1]
    depth = w_in.shape[0]
    s_len = ctx_len + n_tok
    d_ff = w_down.shape[1]
    assert bsz % 2 == 0 and n_tok % (DFT_N2 * 8) == 0 and ctx_len % LANE == 0

    tm = _pick(s_len, (768, 512, 256, 128))
    tq = _pick(ctx_len, (256, 128))
    tk = _pick(s_len, (2816, 768, 256))
    fc = _pick(d_ff, (512, 256, 128))

    n_rows = n_tok // GRID_W
    rows = jnp.repeat(jnp.arange(n_rows, dtype=F32), GRID_W)
    cols = jnp.tile(jnp.arange(GRID_W, dtype=F32), n_rows)
    n_freq = HEAD_DIM // 4
    inv = ROPE_THETA ** (-jnp.arange(n_freq, dtype=F32) / n_freq)
    ang = jnp.concatenate([rows[:, None] * inv, cols[:, None] * inv], axis=-1)
    ang = jnp.concatenate([jnp.zeros((ctx_len, HALF), F32), ang], axis=0)
    cosT, sinT = jnp.cos(ang).T, jnp.sin(ang).T

    sig_f, fil_f, g_tab, h_tab, inv_f = _dft_tables(n_tok)
    csig_f, cfil_f, cinv_f = _dense_dft_tables(ctx_len)
    z_lat = _filter_features(n_tok)
    z_ctx = _filter_features(ctx_len)
    max_decay = math.log(HY_TARGET) / HY_FAST_DECAY
    min_decay = math.log(HY_TARGET) / HY_SLOW_DECAY
    absd = jnp.tile(jnp.abs(jnp.linspace(min_decay, max_decay, HY_WIDTH, dtype=F32)), 4)[None, :]
    t1n = n_tok // DFT_N2
    n1 = 2 * t1n
    big = n1 * DFT_N2
    fb = 8

    perm = np.concatenate([np.arange(0, HEAD_DIM, 2), np.arange(1, HEAD_DIM, 2)])
    qk_cols = np.concatenate(
        [OFF_QA + b * HEAD_DIM + perm for b in range(A_HEADS)]
        + [OFF_QD + b * HEAD_DIM + perm for b in range(2 * D_HEADS)]
        + [OFF_KA + b * HEAD_DIM + perm for b in range(A_KV_HEADS)]
        + [OFF_KD + b * HEAD_DIM + perm for b in range(2 * D_HEADS)])
    v_cols = np.concatenate([np.arange(OFF_VA, OFF_VA + A_KV_WIDTH), np.arange(OFF_VD, OFF_VD + D_WIDTH)])
    qkv_cols = np.concatenate([qk_cols, v_cols])
    wqT = jnp.swapaxes(w_in[:, :, qkv_cols], 1, 2).astype(BF16)
    why = w_in[:, :, OFF_HY:OFF_HY + HY_IN].astype(BF16)
    q_scale = HEAD_DIM ** -0.5 * math.log2(math.e)
    gains = jnp.concatenate(
        [jnp.tile(qn_a[:, perm], (1, A_HEADS)) * q_scale, jnp.tile(qn_d[:, perm], (1, 2 * D_HEADS)) * q_scale,
         jnp.tile(kn_a[:, perm], (1, A_KV_HEADS)), jnp.tile(kn_d[:, perm], (1, 2 * D_HEADS))],
        axis=1)[:, :, None].astype(F32)
    w_out_b = w_out.astype(BF16)
    w_up_a = w_up[:, :, :d_ff].astype(BF16)
    w_up_v = w_up[:, :, d_ff:].astype(BF16)
    w_down_b = w_down.astype(BF16)
    fw1p = jnp.pad(hy_fw1, ((0, 0), (0, LANE - HY_EMB_DIM), (0, 0)))
    lamv = jnp.stack([lam_q1, lam_k1, lam_q2, lam_k2], axis=1)
    lam_init = jnp.asarray([0.8 - 0.6 * math.exp(-0.3 * i) for i in range(depth)], F32)
    lami = jnp.broadcast_to(lam_init[:, None, None], (depth, 8, LANE))

    n_c = 8 * ((bsz + 1 + 7) // 8)
    cvec = jnp.concatenate([c, c_ctx[None, :], jnp.zeros((n_c - bsz - 1, d), F32)], axis=0)
    mods = _modulation(cvec, w_mod, b_mod).reshape(depth, n_c, 6, d)
    pad2 = jnp.zeros((depth, bsz, 2, d), F32)
    mod_all = jnp.concatenate(
        [jnp.broadcast_to(mods[:, bsz:bsz + 1], (depth, bsz, 6, d)), pad2, mods[:, :bsz], pad2], axis=2)

    xs = jnp.concatenate([ctx, x], axis=1)
    for i in range(depth):
        mod = mod_all[i]
        qT, k, vTa, vTd, hy = _projection(xs, mod, norm1_g[i][None, :], wqT[i], why[i], gains[i], cosT, sinT,
                                    ctx_len=ctx_len, tm=tm)
        ya = _attention_a(qT, k, vTa, g_out[i, :A_WIDTH, None], ctx_len=ctx_len, tq=tq, tk=tk)
        yd = _attention_d(qT, k, vTd, g_out[i, A_WIDTH + HY_WIDTH:].reshape(D_HEADS, D_VALUE_DIM, 1),
                          lamv[i], lami[i], ctx_len=ctx_len, tq=tq, tk=tk)

        u_lat, u_ctx = _hyena_short_conv(hy, hy_conv_w[i], hy_conv_b[i], ctx_len=ctx_len)
        fil_args = (fw1p[i], hy_fb1[i][None], hy_fw2[i], hy_fb2[i][None], hy_fw3[i], hy_fb3[i][None],
                    hy_freq[i][None], absd)
        h_lat, sum_lat = _hyena_filters(z_lat, *fil_args)
        h_ctx, sum_ctx = _hyena_filters(z_ctx, *fil_args)
        ss_lat = jnp.swapaxes(sum_lat.reshape(2, 2, HY_WIDTH), 0, 1)
        ss_ctx = jnp.swapaxes(sum_ctx.reshape(2, 2, HY_WIDTH), 0, 1)
        h_ctx = jnp.transpose(h_ctx, (2, 1, 0, 3)).reshape(ctx_len, 2, 2, HY_WIDTH)
        k_ctx = _circular_filters(h_ctx, 2 * ctx_len)

        ak = _outer_forward(fil_f[0], fil_f[1], h_lat.reshape(1, 2, 4, t1n, DFT_N2, LANE), 0, 3)
        kf = _filter_spectrum(g_tab[0], g_tab[1], ak, h_lat, ss_lat, fb=fb)
        p_n = bsz // 2
        u6 = u_lat.reshape(3, 2, p_n, 2 * t1n, DFT_N2, LANE)

        def long_conv(z6, g_z, g_gate, order):
            a = _outer_forward(sig_f[0], sig_f[1], z6, g_z, 1)
            cc = _spectral_multiply(g_tab[0], h_tab[0], kf, a, order, fb=fb)
            return _outer_inverse(inv_f[0], cc, u6, g_gate, z6, g_z, hy_bias[i, order][None, :])

        z1 = long_conv(u6, 0, 1, 0)
        yh2 = long_conv(z1[None], 0, 2, 1).reshape(2, bsz, n_tok, LANE)
        yh_lat = jnp.concatenate([yh2[0], yh2[1]], axis=-1)

        kf_c = _ctx_filter_spectrum(cfil_f[0], cfil_f[1], k_ctx, ss_ctx)
        yh_ctx = _ctx_hyena(csig_f[0], csig_f[1], cinv_f[0], cinv_f[1], kf_c,
                            u_ctx.reshape(3, p_n, 2 * ctx_len, HY_WIDTH), hy_bias[i])
        yh = jnp.concatenate([yh_ctx.reshape(bsz, ctx_len, HY_WIDTH), yh_lat], axis=1)

        xs = _merge(xs, ya, yh, yd, g_out[i, A_WIDTH:A_WIDTH + HY_WIDTH][None, :], w_out_b[i], mod,
                    ctx_len=ctx_len, tm=tm)
        xs = _ffn(xs, mod, norm2_g[i][None, :], w_up_a[i], w_up_v[i], ffn_conv_w[i], ffn_conv_b[i][None, :],
                  w_down_b[i], ctx_len=ctx_len, tm=tm, fc=fc)
    return xs[:, ctx_len:]
```

```python
import functools
import math

import numpy as np
import jax
import jax.numpy as jnp
from jax import lax
from jax.experimental import pallas as pl
from jax.experimental.pallas import tpu as pltpu

F32 = jnp.float32
BF16 = jnp.bfloat16

HEAD_DIM = 64
HALF = HEAD_DIM // 2
A_HEADS = 4
A_KV_HEADS = 2
A_WIDTH = A_HEADS * HEAD_DIM
A_KV_WIDTH = A_KV_HEADS * HEAD_DIM
HY_WIDTH = 256
HY_IN = 3 * HY_WIDTH
HY_BANDS = 16
HY_EMB_DIM = 1 + 2 * HY_BANDS
HY_HIDDEN = 64
HY_FILTER_CH = 4 * HY_WIDTH
HY_FAST_DECAY = 0.3
HY_SLOW_DECAY = 1.5
HY_TARGET = 1e-2
D_HEADS = 4
D_VALUE_DIM = 2 * HEAD_DIM
D_WIDTH = D_HEADS * D_VALUE_DIM
D_QK_WIDTH = D_HEADS * 2 * HEAD_DIM
MIX_WIDTH = A_WIDTH + HY_WIDTH + D_WIDTH
OFF_QA = 0
OFF_KA = A_WIDTH
OFF_VA = OFF_KA + A_KV_WIDTH
OFF_HY = OFF_VA + A_KV_WIDTH
OFF_QD = OFF_HY + HY_IN
OFF_KD = OFF_QD + D_QK_WIDTH
OFF_VD = OFF_KD + D_QK_WIDTH
GRID_W = 64
ROPE_THETA = 10000.0
EPS = 1e-6

N_QBLK = A_HEADS + 2 * D_HEADS
N_KBLK = A_KV_HEADS + 2 * D_HEADS
Q_ROWS = N_QBLK * HEAD_DIM
K_ROWS = N_KBLK * HEAD_DIM
V_ROWS = A_KV_WIDTH + D_WIDTH
QK_ROWS = Q_ROWS + K_ROWS
QKV_ROWS = QK_ROWS + V_ROWS
SUM_ROWS = 16
VA_BLK = HEAD_DIM + SUM_ROWS

LANE = 128
DFT_N2 = 128
KEY_BLOCK = 256
VMEM_LIMIT = 56 * 1024 * 1024
HIGHEST = lax.Precision.HIGHEST


def _cparams(sem):
    return pltpu.CompilerParams(dimension_semantics=sem, vmem_limit_bytes=VMEM_LIMIT)


def _split_bf16(x):
    hi = x.astype(BF16)
    lo = (x - hi.astype(F32)).astype(BF16)
    return hi, lo


def _dot(a, b):
    return jnp.dot(a, b, preferred_element_type=F32)


def _dot3(a_hi, a_lo, b):
    b_hi, b_lo = _split_bf16(b)
    return _dot(a_hi, b_hi) + (_dot(a_hi, b_lo) + _dot(a_lo, b_hi))


def _dot_p(a_hi, a_lo, b, passes):
    return _dot3(a_hi, a_lo, b) if passes == 3 else _dot(a_hi, b.astype(BF16))


def _mod_kernel(c_ref, w_ref, b_ref, o_ref):
    c = c_ref[...]
    s = c * (1.0 / (1.0 + jnp.exp(-c)))
    o_ref[0] = jnp.dot(s, w_ref[0], preferred_element_type=F32, precision=HIGHEST) + b_ref[0]


def _modulation(cvec, w_mod, b_mod):
    depth, d, n6 = w_mod.shape
    tn = 1024
    return pl.pallas_call(
        _mod_kernel,
        out_shape=jax.ShapeDtypeStruct((depth, cvec.shape[0], n6), F32),
        grid=(depth, n6 // tn),
        in_specs=[
            pl.BlockSpec(cvec.shape, lambda l, j: (0, 0)),
            pl.BlockSpec((1, d, tn), lambda l, j: (l, 0, j)),
            pl.BlockSpec((1, 1, tn), lambda l, j: (l, 0, j)),
        ],
        out_specs=pl.BlockSpec((1, cvec.shape[0], tn), lambda l, j: (l, 0, j)),
        compiler_params=_cparams(("arbitrary", "arbitrary")),
        name="modulation",
    )(cvec, w_mod, b_mod.reshape(depth, 1, n6))


def _norm_modulate(x, g, mod_ref, is_ctx, which):
    ms = jnp.mean(x * x, axis=-1, keepdims=True)
    xn = x * lax.rsqrt(ms + EPS) * g
    sh = jnp.where(is_ctx, mod_ref[0, 3 * which:3 * which + 1, :], mod_ref[0, 8 + 3 * which:9 + 3 * which, :])
    sc = jnp.where(is_ctx, mod_ref[0, 3 * which + 1:3 * which + 2, :],
                   mod_ref[0, 9 + 3 * which:10 + 3 * which, :])
    return xn * (1.0 + sc) + sh


def _proj_kernel(x_ref, mod_ref, g_ref, wq_ref, why_ref, gain_ref, cos_ref, sin_ref,
                 qT_ref, k_ref, vTa_ref, vTd_ref, hy_ref, *, tm, ctx_len):
    i = pl.program_id(1)
    row = i * tm + lax.broadcasted_iota(jnp.int32, (tm, 1), 0)
    h = _norm_modulate(x_ref[0], g_ref[...], mod_ref, row < ctx_len, 0).astype(BF16)
    pT = lax.dot_general(wq_ref[...], h, (((1,), (1,)), ((), ())), preferred_element_type=F32)
    hy_ref[0] = _dot(h, why_ref[...])
    c = cos_ref[...]
    s = sin_ref[...]
    blocks = []
    for b in range(N_QBLK + N_KBLK):
        blk = pT[b * HEAD_DIM:(b + 1) * HEAD_DIM]
        ssq = jnp.sum(blk * blk, axis=0, keepdims=True)
        blk = blk * lax.rsqrt(ssq * (1.0 / HEAD_DIM) + EPS) * gain_ref[b * HEAD_DIM:(b + 1) * HEAD_DIM, :]
        x1 = blk[:HALF]
        x2 = blk[HALF:]
        blocks.append(x1 * c - x2 * s)
        blocks.append(x1 * s + x2 * c)
    qT_ref[0] = jnp.concatenate(blocks[:2 * N_QBLK], axis=0).astype(BF16)
    kT = jnp.concatenate(blocks[2 * N_QBLK:], axis=0)
    k_ref[0] = kT.T.astype(BF16)
    ones = jnp.ones((SUM_ROWS, tm), BF16)
    va = [pT[QK_ROWS + g * HEAD_DIM:QK_ROWS + (g + 1) * HEAD_DIM].astype(BF16) for g in range(A_KV_HEADS)]
    vTa_ref[0] = jnp.concatenate([t for v in va for t in (v, ones)], axis=0)
    vTd_ref[0] = pT[QK_ROWS + A_KV_WIDTH:].astype(BF16)


def _projection(xs, mod, g, wqT, why, gains, cosT, sinT, *, ctx_len, tm):
    bsz, s_len, d = xs.shape
    kern = functools.partial(_proj_kernel, tm=tm, ctx_len=ctx_len)
    return pl.pallas_call(
        kern,
        out_shape=(
            jax.ShapeDtypeStruct((bsz, Q_ROWS, s_len), BF16),
            jax.ShapeDtypeStruct((bsz, s_len, K_ROWS), BF16),
            jax.ShapeDtypeStruct((bsz, A_KV_HEADS * VA_BLK, s_len), BF16),
            jax.ShapeDtypeStruct((bsz, D_WIDTH, s_len), BF16),
            jax.ShapeDtypeStruct((bsz, s_len, HY_IN), F32),
        ),
        grid=(bsz, s_len // tm),
        in_specs=[
            pl.BlockSpec((1, tm, d), lambda b, i: (b, i, 0)),
            pl.BlockSpec((1, 16, d), lambda b, i: (b, 0, 0)),
            pl.BlockSpec((1, d), lambda b, i: (0, 0)),
            pl.BlockSpec((QKV_ROWS, d), lambda b, i: (0, 0)),
            pl.BlockSpec((d, HY_IN), lambda b, i: (0, 0)),
            pl.BlockSpec((QK_ROWS, 1), lambda b, i: (0, 0)),
            pl.BlockSpec((HALF, tm), lambda b, i: (0, i)),
            pl.BlockSpec((HALF, tm), lambda b, i: (0, i)),
        ],
        out_specs=(
            pl.BlockSpec((1, Q_ROWS, tm), lambda b, i: (b, 0, i)),
            pl.BlockSpec((1, tm, K_ROWS), lambda b, i: (b, i, 0)),
            pl.BlockSpec((1, A_KV_HEADS * VA_BLK, tm), lambda b, i: (b, 0, i)),
            pl.BlockSpec((1, D_WIDTH, tm), lambda b, i: (b, 0, i)),
            pl.BlockSpec((1, tm, HY_IN), lambda b, i: (b, i, 0)),
        ),
        compiler_params=_cparams(("parallel", "parallel")),
        name="projection",
    )(xs, mod, g, wqT, why, gains, cosT, sinT)


def _attend(units, scr, *, ctx_only, ctx_len, tk, sum_rows):
    n_u = len(units)
    qz, m_run, l_run, acc, s_buf, cm_buf = _split_scratch(scr, n_u)
    s_len = units[0][0].shape[1]
    for u in range(n_u):
        m_run[u][...] = jnp.full(m_run[u].shape, -1e30, F32)
        l_run[u][...] = jnp.zeros(l_run[u].shape, F32)
        acc[u][...] = jnp.zeros(acc[u].shape, F32)

    def fold8(s, op):
        return op(s.reshape(s.shape[0] // 8, 8, s.shape[1]), axis=0)

    def key_rows(start, j, rb):
        if isinstance(start, int):
            return pl.ds(start + j * rb, rb)
        return pl.ds(pl.multiple_of(start + j * rb, rb), rb)

    def advance(nxt, cur, size):
        rb = min(size, KEY_BLOCK)
        if cur is not None:
            u_c, start_c, slot_c = cur
            _, vT_ref, v0, dv = units[u_c]
            m = m_run[u_c][...]
            m_new = jnp.maximum(m, jnp.max(cm_buf[slot_c][...], axis=0, keepdims=True))
            m_run[u_c][...] = m_new
            alpha = jnp.exp2(m - m_new)
        cmax, psum, pv = None, None, None
        for j in range(size // rb):
            rows = slice(j * rb, (j + 1) * rb)
            if nxt is not None:
                u_n, start_n, slot_n = nxt
                s = _dot(units[u_n][0][0, key_rows(start_n, j, rb), :], qz[u_n][...])
                s_buf[slot_n][rows, :] = s
                cmax = fold8(s, jnp.max) if cmax is None else jnp.maximum(cmax, fold8(s, jnp.max))
            if cur is not None:
                p = jnp.exp2(s_buf[slot_c][rows, :] - m_new)
                if not sum_rows:
                    psum = fold8(p, jnp.sum) if psum is None else psum + fold8(p, jnp.sum)
                d = _dot(vT_ref[0, v0:v0 + dv, key_rows(start_c, j, rb)], p.astype(BF16))
                pv = d if pv is None else pv + d
        if nxt is not None:
            cm_buf[slot_n][...] = cmax
        if cur is not None:
            if not sum_rows:
                l_run[u_c][...] = alpha * l_run[u_c][...] + jnp.sum(psum, axis=0, keepdims=True)
            acc[u_c][...] = alpha * acc[u_c][...] + pv

    if ctx_only:
        for u in range(n_u):
            advance((u, 0, u % 2), None, ctx_len)
            advance(None, (u, 0, u % 2), ctx_len)
    else:
        n_chunks = s_len // tk
        advance((0, 0, 0), None, tk)

        def body(c, carry):
            start = pl.multiple_of(c * tk, tk)
            nxt = pl.multiple_of(jnp.minimum(c + 1, n_chunks - 1) * tk, tk)
            for u in range(n_u):
                task_n = (u + 1, start, (u + 1) % 2) if u + 1 < n_u else (0, nxt, (u + 1) % 2)
                advance(task_n, (u, start, u % 2), tk)
            return carry
        lax.fori_loop(0, n_chunks, body, 0)
    outs = []
    for u in range(n_u):
        dv = units[u][3] - sum_rows
        l = acc[u][dv:dv + 1, :] if sum_rows else l_run[u][...]
        outs.append(acc[u][:dv, :] * (1.0 / l))
    return outs


def _attn_scratch(n_u, rows, tq, tk):
    per_unit = [((LANE, tq), BF16), ((1, tq), F32), ((1, tq), F32), ((rows, tq), F32)]
    per_slot = [((tk, tq), F32), ((8, tq), F32)]
    return ([pltpu.VMEM(s, d) for s, d in per_unit for _ in range(n_u)]
            + [pltpu.VMEM(s, d) for s, d in per_slot for _ in range(2)])


def _split_scratch(scr, n_u):
    groups = [scr[i * n_u:(i + 1) * n_u] for i in range(4)]
    rest = scr[4 * n_u:]
    return groups + [rest[0:2], rest[2:4]]


def _run_ctx_or_latent(qi, n_ctx_tiles, run):
    @pl.when(qi < n_ctx_tiles)
    def _():
        run(True)

    @pl.when(qi >= n_ctx_tiles)
    def _():
        run(False)


def _attn_a_kernel(qT_ref, k_ref, vT_ref, g_ref, o_ref, *scr, ctx_len, tq, tk):
    zeros = jnp.zeros((HEAD_DIM, tq), BF16)
    group = A_HEADS // A_KV_HEADS
    for h in range(A_HEADS):
        q = qT_ref[0, h * HEAD_DIM:(h + 1) * HEAD_DIM, :]
        scr[h][...] = jnp.concatenate([q, zeros] if h // group == 0 else [zeros, q], axis=0)
    units = [(k_ref, vT_ref, (h // group) * VA_BLK, VA_BLK) for h in range(A_HEADS)]

    def run(ctx_only):
        outs = _attend(units, scr, ctx_only=ctx_only, ctx_len=ctx_len, tk=tk, sum_rows=SUM_ROWS)
        y = jnp.concatenate(outs, axis=0)
        ms = jnp.mean(y * y, axis=0, keepdims=True)
        y = y * lax.rsqrt(ms + EPS) * g_ref[...]
        o_ref[0] = y.T

    _run_ctx_or_latent(pl.program_id(1), ctx_len // tq, run)


def _attention_a(qT, k, vT, g_a, *, ctx_len, tq, tk):
    bsz, _, s_len = qT.shape
    kern = functools.partial(_attn_a_kernel, ctx_len=ctx_len, tq=tq, tk=tk)
    return pl.pallas_call(
        kern,
        out_shape=jax.ShapeDtypeStruct((bsz, s_len, A_WIDTH), F32),
        grid=(bsz, s_len // tq),
        in_specs=[
            pl.BlockSpec((1, A_WIDTH, tq), lambda b, i: (b, 0, i)),
            pl.BlockSpec((1, s_len, LANE), lambda b, i: (b, 0, 0)),
            pl.BlockSpec((1, A_KV_HEADS * VA_BLK, s_len), lambda b, i: (b, 0, 0)),
            pl.BlockSpec((A_WIDTH, 1), lambda b, i: (0, 0)),
        ],
        out_specs=pl.BlockSpec((1, tq, A_WIDTH), lambda b, i: (b, i, 0)),
        scratch_shapes=_attn_scratch(A_HEADS, VA_BLK, tq, tk),
        compiler_params=_cparams(("parallel", "parallel")),
        name="attention_gqa",
    )(qT, k, vT, g_a)


D_HEADS_PER_STEP = 2


def _attn_d_kernel(*refs, ctx_len, tq, tk):
    n_h = D_HEADS_PER_STEP
    q_refs, k_refs, v_refs = refs[:n_h], refs[n_h:2 * n_h], refs[2 * n_h:3 * n_h]
    g_ref, lamv_ref, lami_ref, o_ref = refs[3 * n_h:3 * n_h + 4]
    scr = refs[3 * n_h + 4:]
    zeros = jnp.zeros((HEAD_DIM, tq), BF16)
    lv = lamv_ref[...]
    lam_init = lami_ref[0:1, 0:1]
    lam = (jnp.exp(jnp.sum(lv[0:1] * lv[1:2], axis=-1, keepdims=True))
           - jnp.exp(jnp.sum(lv[2:3] * lv[3:4], axis=-1, keepdims=True)) + lam_init)
    units = []
    for h in range(n_h):
        scr[2 * h][...] = jnp.concatenate([q_refs[h][0, :HEAD_DIM, :], zeros], axis=0)
        scr[2 * h + 1][...] = jnp.concatenate([zeros, q_refs[h][0, HEAD_DIM:, :]], axis=0)
        units += [(k_refs[h], v_refs[h], 0, D_VALUE_DIM)] * 2

    def run(ctx_only):
        outs = _attend(units, scr, ctx_only=ctx_only, ctx_len=ctx_len, tk=tk, sum_rows=0)
        ys = []
        for h in range(n_h):
            y = outs[2 * h] - lam * outs[2 * h + 1]
            ms = jnp.mean(y * y, axis=0, keepdims=True)
            ys.append(y * lax.rsqrt(ms + EPS) * g_ref[h] * (1.0 - lam_init))
        o_ref[0] = jnp.concatenate(ys, axis=0).T

    _run_ctx_or_latent(pl.program_id(2), ctx_len // tq, run)


def _attention_d(qT, k, vT, g_d, lamv, lami, *, ctx_len, tq, tk):
    bsz, _, s_len = qT.shape
    n_h = D_HEADS_PER_STEP
    kern = functools.partial(_attn_d_kernel, ctx_len=ctx_len, tq=tq, tk=tk)
    q_blk0 = A_WIDTH // LANE
    k_blk0 = A_KV_WIDTH // LANE
    q_specs = [pl.BlockSpec((1, LANE, tq), lambda b, hp, i, j=j: (b, q_blk0 + n_h * hp + j, i)) for j in range(n_h)]
    k_specs = [pl.BlockSpec((1, s_len, LANE), lambda b, hp, i, j=j: (b, 0, k_blk0 + n_h * hp + j)) for j in range(n_h)]
    v_specs = [pl.BlockSpec((1, D_VALUE_DIM, s_len), lambda b, hp, i, j=j: (b, n_h * hp + j, 0)) for j in range(n_h)]
    return pl.pallas_call(
        kern,
        out_shape=jax.ShapeDtypeStruct((bsz, s_len, D_WIDTH), F32),
        grid=(bsz, D_HEADS // n_h, s_len // tq),
        in_specs=q_specs + k_specs + v_specs + [
            pl.BlockSpec((n_h, D_VALUE_DIM, 1), lambda b, hp, i: (hp, 0, 0)),
            pl.BlockSpec((4, HEAD_DIM), lambda b, hp, i: (0, 0)),
            pl.BlockSpec((8, LANE), lambda b, hp, i: (0, 0)),
        ],
        out_specs=pl.BlockSpec((1, tq, n_h * D_VALUE_DIM), lambda b, hp, i: (b, i, hp)),
        scratch_shapes=_attn_scratch(2 * n_h, D_VALUE_DIM, tq, tk),
        compiler_params=_cparams(("parallel", "parallel", "parallel")),
        name="attention_diff",
    )(*([qT] * n_h + [k] * n_h + [vT] * n_h + [g_d, lamv, lami]))


def _dwconv3_seq(u, w_ref, b_ref):
    n = u.shape[0]
    row = lax.broadcasted_iota(jnp.int32, (n, 1), 0)
    prev = jnp.where(row == 0, 0.0, pltpu.roll(u, 1, axis=0))
    nxt = jnp.where(row == n - 1, 0.0, pltpu.roll(u, n - 1, axis=0))
    return prev * w_ref[0:1, :] + u * w_ref[1:2, :] + nxt * w_ref[2:3, :] + b_ref[...]


def _hy_conv_kernel(hy_ref, w_ref, b_ref, lat_ref, ctx_ref, *, ctx_len):
    ctx_ref[0, 0] = _dwconv3_seq(hy_ref[0, :ctx_len, :], w_ref, b_ref)
    lat_ref[0, 0, 0] = _dwconv3_seq(hy_ref[0, ctx_len:, :], w_ref, b_ref)


def _hyena_short_conv(hy, w, b, *, ctx_len):
    bsz, s_len, _ = hy.shape
    n = s_len - ctx_len
    per = HY_WIDTH // LANE
    kern = functools.partial(_hy_conv_kernel, ctx_len=ctx_len)
    return pl.pallas_call(
        kern,
        out_shape=(
            jax.ShapeDtypeStruct((3, per, bsz, n, LANE), F32),
            jax.ShapeDtypeStruct((3, bsz, ctx_len, HY_WIDTH), F32),
        ),
        grid=(bsz, HY_IN // LANE),
        in_specs=[
            pl.BlockSpec((1, s_len, LANE), lambda b, j: (b, 0, j)),
            pl.BlockSpec((3, LANE), lambda b, j: (0, j)),
            pl.BlockSpec((1, LANE), lambda b, j: (0, j)),
        ],
        out_specs=(
            pl.BlockSpec((1, 1, 1, n, LANE), lambda b, j: (j // per, j % per, b, 0, 0)),
            pl.BlockSpec((1, 1, ctx_len, LANE), lambda b, j: (j // per, b, 0, j % per)),
        ),
        compiler_params=_cparams(("parallel", "parallel")),
        name="hyena_short_conv",
    )(hy, w, b.reshape(1, HY_IN))


def _filter_kernel(z_ref, w1_ref, b1_ref, w2_ref, b2_ref, w3_ref, b3_ref, fr_ref, ad_ref, h_ref, sum_ref):
    i = pl.program_id(0)
    z = z_ref[...]
    fr = fr_ref[...]
    h = jnp.sin(fr * (jnp.dot(z, w1_ref[...], preferred_element_type=F32, precision=HIGHEST) + b1_ref[...]))
    h = jnp.sin(fr * (jnp.dot(h, w2_ref[...], preferred_element_type=F32, precision=HIGHEST) + b2_ref[...]))
    h = jnp.dot(h, w3_ref[...], preferred_element_type=F32, precision=HIGHEST) + b3_ref[...]
    h = h * jnp.exp(-z[:, 0:1] * ad_ref[...])
    for q in range(HY_FILTER_CH // HY_WIDTH):
        for hh in range(HY_WIDTH // LANE):
            c0 = q * HY_WIDTH + hh * LANE
            h_ref[hh, q] = h[:, c0:c0 + LANE]

    @pl.when(i == 0)
    def _():
        sum_ref[...] = jnp.zeros_like(sum_ref)

    sum_ref[...] += jnp.sum(jnp.abs(h), axis=0, keepdims=True)


def _hyena_filters(zfeat, w1, b1, w2, b2, w3, b3, freq, absd):
    n, kz = zfeat.shape
    tn = min(n, 1024)
    full = lambda a: pl.BlockSpec(a.shape, lambda i: (0, 0))
    return pl.pallas_call(
        _filter_kernel,
        out_shape=(
            jax.ShapeDtypeStruct((HY_WIDTH // LANE, HY_FILTER_CH // HY_WIDTH, n, LANE), F32),
            jax.ShapeDtypeStruct((1, HY_FILTER_CH), F32),
        ),
        grid=(n // tn,),
        in_specs=[pl.BlockSpec((tn, kz), lambda i: (i, 0)), full(w1), full(b1), full(w2), full(b2),
                  full(w3), full(b3), full(freq), full(absd)],
        out_specs=(
            pl.BlockSpec((HY_WIDTH // LANE, HY_FILTER_CH // HY_WIDTH, tn, LANE), lambda i: (0, 0, i, 0)),
            pl.BlockSpec((1, HY_FILTER_CH), lambda i: (0, 0)),
        ),
        compiler_params=_cparams(("arbitrary",)),
        name="hyena_filters",
    )(zfeat, w1, b1, w2, b2, w3, b3, freq, absd)


T2_BLOCK = 8


def _outer_fwd_kernel(mh_ref, ml_ref, x_ref, o_ref, *, n1, passes):
    rows = x_ref.shape[3]
    xs = [x_ref.at[0, h, 0].reshape(rows * T2_BLOCK, LANE) for h in range(2)]
    outs = [[o_ref.at[0, ri, h].reshape(n1 * T2_BLOCK, LANE) for h in range(2)] for ri in range(2)]
    for t in range(T2_BLOCK):
        sel = pl.ds(t, rows, stride=T2_BLOCK)
        y = _dot_p(mh_ref[...], ml_ref[...], jnp.concatenate([xs[0][sel, :], xs[1][sel, :]], axis=1), passes)
        for ri in range(2):
            for h in range(2):
                outs[ri][h][pl.ds(t, n1, stride=T2_BLOCK), :] = y[ri * n1:(ri + 1) * n1, h * LANE:(h + 1) * LANE]


def _outer_forward(m_hi, m_lo, x6, g, passes):
    _, _, p_n, rows, n2, _ = x6.shape
    n1 = m_hi.shape[0] // 2
    kern = functools.partial(_outer_fwd_kernel, n1=n1, passes=passes)
    return pl.pallas_call(
        kern,
        out_shape=jax.ShapeDtypeStruct((p_n, 2, 2, n1, n2, LANE), F32),
        grid=(p_n, n2 // T2_BLOCK),
        in_specs=[pl.BlockSpec(m_hi.shape, lambda p, j: (0, 0)), pl.BlockSpec(m_lo.shape, lambda p, j: (0, 0)),
                  pl.BlockSpec((1, 2, 1, rows, T2_BLOCK, LANE), lambda p, j: (g, 0, p, 0, j, 0))],
        out_specs=pl.BlockSpec((1, 2, 2, n1, T2_BLOCK, LANE), lambda p, j: (p, 0, 0, 0, j, 0)),
        compiler_params=_cparams(("parallel", "parallel")),
        name="dft_outer_fwd",
    )(m_hi, m_lo, x6)


def _outer_inv_kernel(mh_ref, c_ref, gate_ref, u_ref, bias_ref, o_ref):
    n1 = c_ref.shape[3]
    rows = o_ref.shape[2]
    cs = [[c_ref.at[0, ri, h].reshape(n1 * T2_BLOCK, LANE) for h in range(2)] for ri in range(2)]
    gates = [gate_ref.at[0, h, 0].reshape(rows * T2_BLOCK, LANE) for h in range(2)]
    us = [u_ref.at[0, h, 0].reshape(rows * T2_BLOCK, LANE) for h in range(2)]
    outs = [o_ref.at[h, 0].reshape(rows * T2_BLOCK, LANE) for h in range(2)]
    for t in range(T2_BLOCK):
        sel_f = pl.ds(t, n1, stride=T2_BLOCK)
        sel_t = pl.ds(t, rows, stride=T2_BLOCK)
        c2 = jnp.concatenate([jnp.concatenate([cs[ri][0][sel_f, :], cs[ri][1][sel_f, :]], axis=1)
                              for ri in range(2)], axis=0)
        y = _dot(mh_ref[...], c2.astype(BF16))
        for h in range(2):
            lanes = slice(h * LANE, (h + 1) * LANE)
            outs[h][sel_t, :] = gates[h][sel_t, :] * (y[:, lanes] + us[h][sel_t, :] * bias_ref[:, lanes])


def _outer_inverse(m_hi, c6, gate6, g_gate, u6, g_u, bias):
    p_n, _, _, n1, n2, _ = c6.shape
    rows = m_hi.shape[0]
    gspec = lambda g: pl.BlockSpec((1, 2, 1, rows, T2_BLOCK, LANE), lambda p, j: (g, 0, p, 0, j, 0))
    return pl.pallas_call(
        _outer_inv_kernel,
        out_shape=jax.ShapeDtypeStruct((2, p_n, rows, n2, LANE), F32),
        grid=(p_n, n2 // T2_BLOCK),
        in_specs=[pl.BlockSpec(m_hi.shape, lambda p, j: (0, 0)),
                  pl.BlockSpec((1, 2, 2, n1, T2_BLOCK, LANE), lambda p, j: (p, 0, 0, 0, j, 0)),
                  gspec(g_gate), gspec(g_u), pl.BlockSpec((1, 2 * LANE), lambda p, j: (0, 0))],
        out_specs=pl.BlockSpec((2, 1, rows, T2_BLOCK, LANE), lambda p, j: (0, p, 0, j, 0)),
        compiler_params=_cparams(("parallel", "parallel")),
        name="dft_outer_inv",
    )(m_hi, c6, gate6, u6, bias)


def _halves_to_rows(a_ref, j):
    return jnp.concatenate([jnp.concatenate([a_ref[0, ri, 0, j], a_ref[0, ri, 1, j]], axis=1) for ri in range(2)],
                           axis=0)


def _kf_kernel(gh_ref, gl_ref, af_ref, ab_ref, h0_ref, ss_ref, o_ref, *, fb, n2):
    scale = 1.0 / (ss_ref[0, 0:1, :] + ss_ref[0, 1:2, :] + EPS)
    b0 = jnp.concatenate([h0_ref[0, 0, 0:1, :], h0_ref[1, 0, 0:1, :]], axis=1)
    for j in range(fb):
        xf = _dot3(gh_ref[j], gl_ref[j], _halves_to_rows(af_ref, j))
        xb = _dot3(gh_ref[j], gl_ref[j], _halves_to_rows(ab_ref, j))
        o_ref[0, j, :n2, :] = (xf[:n2] + xb[:n2] - b0) * scale
        o_ref[0, j, n2:, :] = (xf[n2:] - xb[n2:]) * scale


def _filter_spectrum(g_hi, g_lo, a6, h6, ss, *, fb):
    n_q, _, _, n1, n2, _ = a6.shape
    n_ord = n_q // 2
    c = 2 * LANE
    kern = functools.partial(_kf_kernel, fb=fb, n2=n2)
    return pl.pallas_call(
        kern,
        out_shape=jax.ShapeDtypeStruct((n_ord, n1, 2 * n2, c), F32),
        grid=(n1 // fb, n_ord),
        in_specs=[
            pl.BlockSpec((fb, 2 * n2, 2 * n2), lambda f, o: (f, 0, 0)),
            pl.BlockSpec((fb, 2 * n2, 2 * n2), lambda f, o: (f, 0, 0)),
            pl.BlockSpec((1, 2, 2, fb, n2, LANE), lambda f, o: (o, 0, 0, f, 0, 0)),
            pl.BlockSpec((1, 2, 2, fb, n2, LANE), lambda f, o: (n_ord + o, 0, 0, f, 0, 0)),
            pl.BlockSpec((2, 1, 8, LANE), lambda f, o: (0, n_ord + o, 0, 0)),
            pl.BlockSpec((1, 2, c), lambda f, o: (o, 0, 0)),
        ],
        out_specs=pl.BlockSpec((1, fb, 2 * n2, c), lambda f, o: (o, f, 0, 0)),
        compiler_params=_cparams(("parallel", "parallel")),
        name="filter_spectrum",
    )(g_hi, g_lo, a6, a6, h6, ss)


def _spec_kernel(gh_ref, hh_ref, kf_ref, a_ref, o_ref, *, fb, n2):
    for j in range(fb):
        x = _dot(gh_ref[j], _halves_to_rows(a_ref, j).astype(BF16))
        xr, xi = x[:n2], x[n2:]
        kr, ki = kf_ref[0, j, :n2], kf_ref[0, j, n2:]
        y = jnp.concatenate([xr * kr - xi * ki, xr * ki + xi * kr], axis=0)
        c2 = _dot(hh_ref[j], y.astype(BF16))
        for ri in range(2):
            for h in range(2):
                o_ref[0, ri, h, j] = c2[ri * n2:(ri + 1) * n2, h * LANE:(h + 1) * LANE]


def _spectral_multiply(g_hi, h_hi, kf, a6, order, *, fb):
    p_n, _, _, n1, n2, _ = a6.shape
    c = 2 * LANE
    kern = functools.partial(_spec_kernel, fb=fb, n2=n2)
    gspec = pl.BlockSpec((fb, 2 * n2, 2 * n2), lambda f, p: (f, 0, 0))
    aspec = pl.BlockSpec((1, 2, 2, fb, n2, LANE), lambda f, p: (p, 0, 0, f, 0, 0))
    return pl.pallas_call(
        kern,
        out_shape=jax.ShapeDtypeStruct(a6.shape, F32),
        grid=(n1 // fb, p_n),
        in_specs=[gspec, gspec,
                  pl.BlockSpec((1, fb, 2 * n2, c), lambda f, p: (order, f, 0, 0)), aspec],
        out_specs=aspec,
        compiler_params=_cparams(("parallel", "parallel")),
        name="spectral_multiply",
    )(g_hi, h_hi, kf, a6)


def _ctx_kf_kernel(fh_ref, fl_ref, k_ref, ss_ref, o_ref):
    scale = 1.0 / (ss_ref[0, 0:1, :] + ss_ref[0, 1:2, :] + EPS)
    o_ref[0] = _dot3(fh_ref[...], fl_ref[...], k_ref[0]) * scale


def _ctx_filter_spectrum(f_hi, f_lo, k, ss):
    n_ord, nc, c = k.shape
    return pl.pallas_call(
        _ctx_kf_kernel,
        out_shape=jax.ShapeDtypeStruct((n_ord, 2 * nc, c), F32),
        grid=(n_ord,),
        in_specs=[
            pl.BlockSpec(f_hi.shape, lambda o: (0, 0)),
            pl.BlockSpec(f_lo.shape, lambda o: (0, 0)),
            pl.BlockSpec((1, nc, c), lambda o: (o, 0, 0)),
            pl.BlockSpec((1, 2, c), lambda o: (o, 0, 0)),
        ],
        out_specs=pl.BlockSpec((1, 2 * nc, c), lambda o: (o, 0, 0)),
        compiler_params=_cparams(("parallel",)),
        name="ctx_filter_spectrum",
    )(f_hi, f_lo, k, ss)


def _ctx_conv_kernel(fh_ref, fl_ref, eh_ref, el_ref, kf_ref, u_ref, bias_ref, o_ref, *, nc):
    v, x1, x2 = u_ref[0, 0], u_ref[1, 0], u_ref[2, 0]

    def conv(z, order):
        x = _dot3(fh_ref[...], fl_ref[...], z)
        xr, xi = x[:nc], x[nc:]
        kr, ki = kf_ref[order, :nc], kf_ref[order, nc:]
        y = jnp.concatenate([xr * kr - xi * ki, xr * ki + xi * kr], axis=0)
        return _dot3(eh_ref[...], el_ref[...], y) + z * bias_ref[order:order + 1, :]

    o_ref[0] = x2 * conv(x1 * conv(v, 0), 1)


def _ctx_hyena(f_hi, f_lo, e_hi, e_lo, kf, u3, bias):
    _, p_n, n2x, c = u3.shape
    nc = kf.shape[1] // 2
    kern = functools.partial(_ctx_conv_kernel, nc=nc)
    full = lambda a: pl.BlockSpec(a.shape, lambda p: (0,) * a.ndim)
    return pl.pallas_call(
        kern,
        out_shape=jax.ShapeDtypeStruct((p_n, n2x, c), F32),
        grid=(p_n,),
        in_specs=[full(f_hi), full(f_lo), full(e_hi), full(e_lo), full(kf),
                  pl.BlockSpec((3, 1, n2x, c), lambda p: (0, p, 0, 0)), full(bias)],
        out_specs=pl.BlockSpec((1, n2x, c), lambda p: (p, 0, 0)),
        compiler_params=_cparams(("parallel",)),
        name="ctx_hyena",
    )(f_hi, f_lo, e_hi, e_lo, kf, u3, bias)


def _merge_kernel(x_ref, ya_ref, yh_ref, yd_ref, gh_ref, w_ref, mod_ref, o_ref, *, tm, ctx_len):
    i = pl.program_id(1)
    row = i * tm + lax.broadcasted_iota(jnp.int32, (tm, 1), 0)
    yh = yh_ref[0]
    ms = jnp.mean(yh * yh, axis=-1, keepdims=True)
    yh = yh * lax.rsqrt(ms + EPS) * gh_ref[...]
    y = _dot(ya_ref[0].astype(BF16), w_ref[:A_WIDTH, :])
    y += _dot(yh.astype(BF16), w_ref[A_WIDTH:A_WIDTH + HY_WIDTH, :])
    y += _dot(yd_ref[0].astype(BF16), w_ref[A_WIDTH + HY_WIDTH:, :])
    gate = jnp.where(row < ctx_len, mod_ref[0, 2:3, :], mod_ref[0, 10:11, :])
    o_ref[0] = x_ref[0] + gate * y


def _merge(xs, ya, yh, yd, g_h, w_out, mod, *, ctx_len, tm):
    bsz, s_len, d = xs.shape
    kern = functools.partial(_merge_kernel, tm=tm, ctx_len=ctx_len)
    tok = lambda w: pl.BlockSpec((1, tm, w), lambda b, i: (b, i, 0))
    return pl.pallas_call(
        kern,
        out_shape=jax.ShapeDtypeStruct(xs.shape, F32),
        grid=(bsz, s_len // tm),
        in_specs=[tok(d), tok(A_WIDTH), tok(HY_WIDTH), tok(D_WIDTH),
                  pl.BlockSpec((1, HY_WIDTH), lambda b, i: (0, 0)),
                  pl.BlockSpec((MIX_WIDTH, d), lambda b, i: (0, 0)),
                  pl.BlockSpec((1, 16, d), lambda b, i: (b, 0, 0))],
        out_specs=tok(d),
        compiler_params=_cparams(("parallel", "parallel")),
        name="merge",
    )(xs, ya, yh, yd, g_h, w_out, mod)


HALO = 8


def _ffn_kernel(x_ref, xp_ref, xn_ref, mod_ref, g_ref, wa_ref, wv_ref, cw_ref, cb_ref, wd_ref, o_ref,
                *, tm, fc, ctx_len, s_len):
    i = pl.program_id(1)
    x = x_ref[0]
    xa = jnp.concatenate([xp_ref[0], x, xn_ref[0]], axis=0)
    rows = tm + 2 * HALO
    row = i * tm - HALO + lax.broadcasted_iota(jnp.int32, (rows, 1), 0)
    h = _norm_modulate(xa, g_ref[...], mod_ref, row < ctx_len, 1).astype(BF16)
    hm = h[HALO:HALO + tm]
    rowm = row[HALO:HALO + tm]
    has_prev = jnp.logical_and(rowm != 0, rowm != ctx_len)
    has_next = jnp.logical_and(rowm != ctx_len - 1, rowm != s_len - 1)
    def up(j):
        cs = slice(j * fc, (j + 1) * fc)
        return _dot(h, wa_ref[:, cs]), _dot(hm, wv_ref[:, cs])

    n_chunks = wa_ref.shape[1] // fc
    acc = jnp.zeros((tm, x.shape[1]), F32)
    nxt = up(0)
    for j in range(n_chunks):
        cs = slice(j * fc, (j + 1) * fc)
        a, v = nxt
        if j + 1 < n_chunks:
            nxt = up(j + 1)
        a_prev = jnp.where(has_prev, pltpu.roll(a, 1, axis=0)[HALO:HALO + tm], 0.0)
        a_next = jnp.where(has_next, pltpu.roll(a, rows - 1, axis=0)[HALO:HALO + tm], 0.0)
        ac = a_prev * cw_ref[0:1, cs] + a[HALO:HALO + tm] * cw_ref[1:2, cs] + a_next * cw_ref[2:3, cs] + cb_ref[:, cs]
        gl = 0.5 * ac * (1.0 + jnp.tanh(math.sqrt(2.0 / math.pi) * (ac + 0.044715 * (ac * ac * ac))))
        acc += _dot((gl * v).astype(BF16), wd_ref[cs, :])
    gate = jnp.where(rowm < ctx_len, mod_ref[0, 5:6, :], mod_ref[0, 13:14, :])
    o_ref[0] = x + gate * acc


def _ffn(xs, mod, g, wa, wv, cw, cb, wd, *, ctx_len, tm, fc):
    bsz, s_len, d = xs.shape
    d_ff = wa.shape[1]
    nh = tm // HALO
    last = s_len // HALO - 1
    kern = functools.partial(_ffn_kernel, tm=tm, fc=fc, ctx_len=ctx_len, s_len=s_len)
    const = lambda shape: pl.BlockSpec(shape, lambda b, i: (0, 0), pipeline_mode=pl.Buffered(1))
    return pl.pallas_call(
        kern,
        out_shape=jax.ShapeDtypeStruct(xs.shape, F32),
        grid=(bsz, s_len // tm),
        in_specs=[
            pl.BlockSpec((1, tm, d), lambda b, i: (b, i, 0)),
            pl.BlockSpec((1, HALO, d), lambda b, i: (b, jnp.maximum(i * nh - 1, 0), 0)),
            pl.BlockSpec((1, HALO, d), lambda b, i: (b, jnp.minimum((i + 1) * nh, last), 0)),
            pl.BlockSpec((1, 16, d), lambda b, i: (b, 0, 0)),
            pl.BlockSpec((1, d), lambda b, i: (0, 0)),
            const((d, d_ff)), const((d, d_ff)), const((3, d_ff)), const((1, d_ff)), const((d_ff, d)),
        ],
        out_specs=pl.BlockSpec((1, tm, d), lambda b, i: (b, i, 0)),
        compiler_params=_cparams(("parallel", "parallel")),
        name="conv_glu",
    )(xs, xs, xs, mod, g, wa, wv, cw, cb, wd)


def _cis(num, den):
    ang = num.astype(F32) * (2.0 * math.pi / den)
    return jnp.cos(ang), jnp.sin(ang)


def _blockmat(re, im):
    return jnp.concatenate([jnp.concatenate([re, -im], axis=-1), jnp.concatenate([im, re], axis=-1)], axis=-2)


def _dft_tables(n):
    n2 = DFT_N2
    t1n = n // n2
    n1 = 2 * t1n
    big = n1 * n2
    f1 = jnp.arange(n1, dtype=jnp.int32)
    t1 = jnp.arange(t1n, dtype=jnp.int32)
    j2 = jnp.arange(n2, dtype=jnp.int32)
    c, s = _cis((f1[:, None] * t1[None, :]) % n1, n1)
    fwd_sig = _blockmat(c, -s)
    c, s = _cis((f1[:, None] * f1[None, :]) % n1, n1)
    fwd_fil = jnp.concatenate([c, -s], axis=0)[:, :t1n]
    ph = (n1 * j2[None, :, None] * j2[None, None, :] + f1[:, None, None] * j2[None, None, :]) % big
    c, s = _cis(ph, big)
    g = _blockmat(c, -s)
    h = jnp.swapaxes(g, 1, 2)
    c, s = _cis((t1[:, None] * f1[None, :]) % n1, n1)
    inv = _blockmat(c, s) * (1.0 / big)
    return tuple(_split_bf16(m) for m in (fwd_sig, fwd_fil, g, h, inv))


def _dense_dft_tables(n):
    big = 2 * n
    f = jnp.arange(big, dtype=jnp.int32)
    t = jnp.arange(n, dtype=jnp.int32)
    c, s = _cis((f[:, None] * t[None, :]) % big, big)
    fwd_sig = _blockmat(c, -s)
    c, s = _cis((f[:, None] * f[None, :]) % big, big)
    fwd_fil = jnp.concatenate([c, -s], axis=0)
    c, s = _cis((t[:, None] * f[None, :]) % big, big)
    inv = _blockmat(c, s) * (1.0 / big)
    return tuple(_split_bf16(m) for m in (fwd_sig, fwd_fil, inv))


def _filter_features(n):
    pos = jnp.arange(n, dtype=F32)
    t = (pos / max(n - 1, 1))[:, None]
    w = 2.0 * math.pi * pos[:, None] / n
    bands = jnp.linspace(1e-4, HY_BANDS - 1, HY_BANDS, dtype=F32)
    z = jnp.concatenate([t, jnp.cos(bands * w), -jnp.sin(bands * w)], axis=-1)
    return jnp.pad(z, ((0, 0), (0, LANE - HY_EMB_DIM)))


def _circular_filters(h, big):
    n = h.shape[0]
    fwd = jnp.moveaxis(h[:, 0], 1, 0)
    bwd = jnp.moveaxis(h[:, 1], 1, 0)
    zeros = jnp.zeros((fwd.shape[0], big - 2 * n + 1, fwd.shape[2]), F32)
    return jnp.concatenate([fwd, zeros, bwd[:, :0:-1]], axis=1)


def _pick(total, candidates):
    for t in candidates:
        if total % t == 0:
            return t
    raise ValueError(f"no tile for {total}")


def kernel(x, c, ctx, c_ctx, norm1_g, norm2_g, w_mod, b_mod, w_in, qn_a, kn_a, qn_d, kn_d, lam_q1, lam_k1, lam_q2, lam_k2, hy_conv_w, hy_conv_b, hy_fw1, hy_fb1, hy_fw2, hy_fb2, hy_fw3, hy_fb3, hy_freq, hy_bias, g_out, w_out, w_up, ffn_conv_w, ffn_conv_b, w_down):
    bsz, n_tok, d = x.shape
    ctx_len = ctx.shape[1]
    depth = w_in.shape[0]
    s_len = ctx_len + n_tok
    d_ff = w_down.shape[1]
    assert bsz % 2 == 0 and n_tok % (DFT_N2 * 8) == 0 and ctx_len % LANE == 0

    tm = _pick(s_len, (768, 512, 256, 128))
    tq = _pick(ctx_len, (256, 128))
    tk = _pick(s_len, (2816, 768, 256))
    fc = _pick(d_ff, (512, 256, 128))

    n_rows = n_tok // GRID_W
    rows = jnp.repeat(jnp.arange(n_rows, dtype=F32), GRID_W)
    cols = jnp.tile(jnp.arange(GRID_W, dtype=F32), n_rows)
    n_freq = HEAD_DIM // 4
    inv = ROPE_THETA ** (-jnp.arange(n_freq, dtype=F32) / n_freq)
    ang = jnp.concatenate([rows[:, None] * inv, cols[:, None] * inv], axis=-1)
    ang = jnp.concatenate([jnp.zeros((ctx_len, HALF), F32), ang], axis=0)
    cosT, sinT = jnp.cos(ang).T, jnp.sin(ang).T

    sig_f, fil_f, g_tab, h_tab, inv_f = _dft_tables(n_tok)
    csig_f, cfil_f, cinv_f = _dense_dft_tables(ctx_len)
    z_lat = _filter_features(n_tok)
    z_ctx = _filter_features(ctx_len)
    max_decay = math.log(HY_TARGET) / HY_FAST_DECAY
    min_decay = math.log(HY_TARGET) / HY_SLOW_DECAY
    absd = jnp.tile(jnp.abs(jnp.linspace(min_decay, max_decay, HY_WIDTH, dtype=F32)), 4)[None, :]
    t1n = n_tok // DFT_N2
    n1 = 2 * t1n
    big = n1 * DFT_N2
    fb = 8

    perm = np.concatenate([np.arange(0, HEAD_DIM, 2), np.arange(1, HEAD_DIM, 2)])
    qk_cols = np.concatenate(
        [OFF_QA + b * HEAD_DIM + perm for b in range(A_HEADS)]
        + [OFF_QD + b * HEAD_DIM + perm for b in range(2 * D_HEADS)]
        + [OFF_KA + b * HEAD_DIM + perm for b in range(A_KV_HEADS)]
        + [OFF_KD + b * HEAD_DIM + perm for b in range(2 * D_HEADS)])
    v_cols = np.concatenate([np.arange(OFF_VA, OFF_VA + A_KV_WIDTH), np.arange(OFF_VD, OFF_VD + D_WIDTH)])
    qkv_cols = np.concatenate([qk_cols, v_cols])
    wqT = jnp.swapaxes(w_in[:, :, qkv_cols], 1, 2).astype(BF16)
    why = w_in[:, :, OFF_HY:OFF_HY + HY_IN].astype(BF16)
    q_scale = HEAD_DIM ** -0.5 * math.log2(math.e)
    gains = jnp.concatenate(
        [jnp.tile(qn_a[:, perm], (1, A_HEADS)) * q_scale, jnp.tile(qn_d[:, perm], (1, 2 * D_HEADS)) * q_scale,
         jnp.tile(kn_a[:, perm], (1, A_KV_HEADS)), jnp.tile(kn_d[:, perm], (1, 2 * D_HEADS))],
        axis=1)[:, :, None].astype(F32)
    w_out_b = w_out.astype(BF16)
    w_up_a = w_up[:, :, :d_ff].astype(BF16)
    w_up_v = w_up[:, :, d_ff:].astype(BF16)
    w_down_b = w_down.astype(BF16)
    fw1p = jnp.pad(hy_fw1, ((0, 0), (0, LANE - HY_EMB_DIM), (0, 0)))
    lamv = jnp.stack([lam_q1, lam_k1, lam_q2, lam_k2], axis=1)
    lam_init = jnp.asarray([0.8 - 0.6 * math.exp(-0.3 * i) for i in range(depth)], F32)
    lami = jnp.broadcast_to(lam_init[:, None, None], (depth, 8, LANE))

    n_c = 8 * ((bsz + 1 + 7) // 8)
    cvec = jnp.concatenate([c, c_ctx[None, :], jnp.zeros((n_c - bsz - 1, d), F32)], axis=0)
    mods = _modulation(cvec, w_mod, b_mod).reshape(depth, n_c, 6, d)
    pad2 = jnp.zeros((depth, bsz, 2, d), F32)
    mod_all = jnp.concatenate(
        [jnp.broadcast_to(mods[:, bsz:bsz + 1], (depth, bsz, 6, d)), pad2, mods[:, :bsz], pad2], axis=2)

    xs = jnp.concatenate([ctx, x], axis=1)
    for i in range(depth):
        mod = mod_all[i]
        qT, k, vTa, vTd, hy = _projection(xs, mod, norm1_g[i][None, :], wqT[i], why[i], gains[i], cosT, sinT,
                                    ctx_len=ctx_len, tm=tm)
        ya = _attention_a(qT, k, vTa, g_out[i, :A_WIDTH, None], ctx_len=ctx_len, tq=tq, tk=tk)
        yd = _attention_d(qT, k, vTd, g_out[i, A_WIDTH + HY_WIDTH:].reshape(D_HEADS, D_VALUE_DIM, 1),
                          lamv[i], lami[i], ctx_len=ctx_len, tq=tq, tk=tk)

        u_lat, u_ctx = _hyena_short_conv(hy, hy_conv_w[i], hy_conv_b[i], ctx_len=ctx_len)
        fil_args = (fw1p[i], hy_fb1[i][None], hy_fw2[i], hy_fb2[i][None], hy_fw3[i], hy_fb3[i][None],
                    hy_freq[i][None], absd)
        h_lat, sum_lat = _hyena_filters(z_lat, *fil_args)
        h_ctx, sum_ctx = _hyena_filters(z_ctx, *fil_args)
        ss_lat = jnp.swapaxes(sum_lat.reshape(2, 2, HY_WIDTH), 0, 1)
        ss_ctx = jnp.swapaxes(sum_ctx.reshape(2, 2, HY_WIDTH), 0, 1)
        h_ctx = jnp.transpose(h_ctx, (2, 1, 0, 3)).reshape(ctx_len, 2, 2, HY_WIDTH)
        k_ctx = _circular_filters(h_ctx, 2 * ctx_len)

        ak = _outer_forward(fil_f[0], fil_f[1], h_lat.reshape(1, 2, 4, t1n, DFT_N2, LANE), 0, 3)
        kf = _filter_spectrum(g_tab[0], g_tab[1], ak, h_lat, ss_lat, fb=fb)
        p_n = bsz // 2
        u6 = u_lat.reshape(3, 2, p_n, 2 * t1n, DFT_N2, LANE)

        def long_conv(z6, g_z, g_gate, order):
            a = _outer_forward(sig_f[0], sig_f[1], z6, g_z, 1)
            cc = _spectral_multiply(g_tab[0], h_tab[0], kf, a, order, fb=fb)
            return _outer_inverse(inv_f[0], cc, u6, g_gate, z6, g_z, hy_bias[i, order][None, :])

        z1 = long_conv(u6, 0, 1, 0)
        yh2 = long_conv(z1[None], 0, 2, 1).reshape(2, bsz, n_tok, LANE)
        yh_lat = jnp.concatenate([yh2[0], yh2[1]], axis=-1)

        kf_c = _ctx_filter_spectrum(cfil_f[0], cfil_f[1], k_ctx, ss_ctx)
        yh_ctx = _ctx_hyena(csig_f[0], csig_f[1], cinv_f[0], cinv_f[1], kf_c,
                            u_ctx.reshape(3, p_n, 2 * ctx_len, HY_WIDTH), hy_bias[i])
        yh = jnp.concatenate([yh_ctx.reshape(bsz, ctx_len, HY_WIDTH), yh_lat], axis=1)

        xs = _merge(xs, ya, yh, yd, g_out[i, A_WIDTH:A_WIDTH + HY_WIDTH][None, :], w_out_b[i], mod,
                    ctx_len=ctx_len, tm=tm)
        xs = _ffn(xs, mod, norm2_g[i][None, :], w_up_a[i], w_up_v[i], ffn_conv_w[i], ffn_conv_b[i][None, :],
                  w_down_b[i], ctx_len=ctx_len, tm=tm, fc=fc)
    return xs[:, ctx_len:]
```

```python
import functools
import math

import numpy as np
import jax
import jax.numpy as jnp
from jax import lax
from jax.experimental import pallas as pl
from jax.experimental.pallas import tpu as pltpu

F32 = jnp.float32
BF16 = jnp.bfloat16

HEAD_DIM = 64
HALF = HEAD_DIM // 2
A_HEADS = 4
A_KV_HEADS = 2
A_WIDTH = A_HEADS * HEAD_DIM
A_KV_WIDTH = A_KV_HEADS * HEAD_DIM
HY_WIDTH = 256
HY_IN = 3 * HY_WIDTH
HY_BANDS = 16
HY_EMB_DIM = 1 + 2 * HY_BANDS
HY_HIDDEN = 64
HY_FILTER_CH = 4 * HY_WIDTH
HY_FAST_DECAY = 0.3
HY_SLOW_DECAY = 1.5
HY_TARGET = 1e-2
D_HEADS = 4
D_VALUE_DIM = 2 * HEAD_DIM
D_WIDTH = D_HEADS * D_VALUE_DIM
D_QK_WIDTH = D_HEADS * 2 * HEAD_DIM
MIX_WIDTH = A_WIDTH + HY_WIDTH + D_WIDTH
OFF_QA = 0
OFF_KA = A_WIDTH
OFF_VA = OFF_KA + A_KV_WIDTH
OFF_HY = OFF_VA + A_KV_WIDTH
OFF_QD = OFF_HY + HY_IN
OFF_KD = OFF_QD + D_QK_WIDTH
OFF_VD = OFF_KD + D_QK_WIDTH
GRID_W = 64
ROPE_THETA = 10000.0
EPS = 1e-6

N_QBLK = A_HEADS + 2 * D_HEADS
N_KBLK = A_KV_HEADS + 2 * D_HEADS
Q_ROWS = N_QBLK * HEAD_DIM
K_ROWS = N_KBLK * HEAD_DIM
V_ROWS = A_KV_WIDTH + D_WIDTH
QK_ROWS = Q_ROWS + K_ROWS
QKV_ROWS = QK_ROWS + V_ROWS
SUM_ROWS = 16
VA_BLK = HEAD_DIM + SUM_ROWS

LANE = 128
DFT_N2 = 128
KEY_BLOCK = 256
VMEM_LIMIT = 56 * 1024 * 1024
HIGHEST = lax.Precision.HIGHEST


def _cparams(sem):
    return pltpu.CompilerParams(dimension_semantics=sem, vmem_limit_bytes=VMEM_LIMIT)


def _split_bf16(x):
    hi = x.astype(BF16)
    lo = (x - hi.astype(F32)).astype(BF16)
    return hi, lo


def _dot(a, b):
    return jnp.dot(a, b, preferred_element_type=F32)


def _dot3(a_hi, a_lo, b):
    b_hi, b_lo = _split_bf16(b)
    return _dot(a_hi, b_hi) + (_dot(a_hi, b_lo) + _dot(a_lo, b_hi))


def _dot_p(a_hi, a_lo, b, passes):
    return _dot3(a_hi, a_lo, b) if passes == 3 else _dot(a_hi, b.astype(BF16))


def _mod_kernel(c_ref, w_ref, b_ref, o_ref):
    c = c_ref[...]
    s = c * (1.0 / (1.0 + jnp.exp(-c)))
    o_ref[0] = jnp.dot(s, w_ref[0], preferred_element_type=F32, precision=HIGHEST) + b_ref[0]


def _modulation(cvec, w_mod, b_mod):
    depth, d, n6 = w_mod.shape
    tn = 1024
    return pl.pallas_call(
        _mod_kernel,
        out_shape=jax.ShapeDtypeStruct((depth, cvec.shape[0], n6), F32),
        grid=(depth, n6 // tn),
        in_specs=[
            pl.BlockSpec(cvec.shape, lambda l, j: (0, 0)),
            pl.BlockSpec((1, d, tn), lambda l, j: (l, 0, j)),
            pl.BlockSpec((1, 1, tn), lambda l, j: (l, 0, j)),
        ],
        out_specs=pl.BlockSpec((1, cvec.shape[0], tn), lambda l, j: (l, 0, j)),
        compiler_params=_cparams(("arbitrary", "arbitrary")),
        name="modulation",
    )(cvec, w_mod, b_mod.reshape(depth, 1, n6))


def _norm_modulate(x, g, mod_ref, is_ctx, which):
    ms = jnp.mean(x * x, axis=-1, keepdims=True)
    xn = x * lax.rsqrt(ms + EPS) * g
    sh = jnp.where(is_ctx, mod_ref[0, 3 * which:3 * which + 1, :], mod_ref[0, 8 + 3 * which:9 + 3 * which, :])
    sc = jnp.where(is_ctx, mod_ref[0, 3 * which + 1:3 * which + 2, :],
                   mod_ref[0, 9 + 3 * which:10 + 3 * which, :])
    return xn * (1.0 + sc) + sh


def _proj_kernel(x_ref, mod_ref, g_ref, wq_ref, why_ref, gain_ref, cos_ref, sin_ref,
                 qT_ref, k_ref, vTa_ref, vTd_ref, hy_ref, *, tm, ctx_len):
    i = pl.program_id(1)
    row = i * tm + lax.broadcasted_iota(jnp.int32, (tm, 1), 0)
    h = _norm_modulate(x_ref[0], g_ref[...], mod_ref, row < ctx_len, 0).astype(BF16)
    pT = lax.dot_general(wq_ref[...], h, (((1,), (1,)), ((), ())), preferred_element_type=F32)
    hy_ref[0] = _dot(h, why_ref[...])
    c = cos_ref[...]
    s = sin_ref[...]
    blocks = []
    for b in range(N_QBLK + N_KBLK):
        blk = pT[b * HEAD_DIM:(b + 1) * HEAD_DIM]
        ssq = jnp.sum(blk * blk, axis=0, keepdims=True)
        blk = blk * lax.rsqrt(ssq * (1.0 / HEAD_DIM) + EPS) * gain_ref[b * HEAD_DIM:(b + 1) * HEAD_DIM, :]
        x1 = blk[:HALF]
        x2 = blk[HALF:]
        blocks.append(x1 * c - x2 * s)
        blocks.append(x1 * s + x2 * c)
    qT_ref[0] = jnp.concatenate(blocks[:2 * N_QBLK], axis=0).astype(BF16)
    kT = jnp.concatenate(blocks[2 * N_QBLK:], axis=0)
    k_ref[0] = kT.T.astype(BF16)
    ones = jnp.ones((SUM_ROWS, tm), BF16)
    va = [pT[QK_ROWS + g * HEAD_DIM:QK_ROWS + (g + 1) * HEAD_DIM].astype(BF16) for g in range(A_KV_HEADS)]
    vTa_ref[0] = jnp.concatenate([t for v in va for t in (v, ones)], axis=0)
    vTd_ref[0] = pT[QK_ROWS + A_KV_WIDTH:].astype(BF16)


def _projection(xs, mod, g, wqT, why, gains, cosT, sinT, *, ctx_len, tm):
    bsz, s_len, d = xs.shape
    kern = functools.partial(_proj_kernel, tm=tm, ctx_len=ctx_len)
    return pl.pallas_call(
        kern,
        out_shape=(
            jax.ShapeDtypeStruct((bsz, Q_ROWS, s_len), BF16),
            jax.ShapeDtypeStruct((bsz, s_len, K_ROWS), BF16),
            jax.ShapeDtypeStruct((bsz, A_KV_HEADS * VA_BLK, s_len), BF16),
            jax.ShapeDtypeStruct((bsz, D_WIDTH, s_len), BF16),
            jax.ShapeDtypeStruct((bsz, s_len, HY_IN), F32),
        ),
        grid=(bsz, s_len // tm),
        in_specs=[
            pl.BlockSpec((1, tm, d), lambda b, i: (b, i, 0)),
            pl.BlockSpec((1, 16, d), lambda b, i: (b, 0, 0)),
            pl.BlockSpec((1, d), lambda b, i: (0, 0)),
            pl.BlockSpec((QKV_ROWS, d), lambda b, i: (0, 0)),
            pl.BlockSpec((d, HY_IN), lambda b, i: (0, 0)),
            pl.BlockSpec((QK_ROWS, 1), lambda b, i: (0, 0)),
            pl.BlockSpec((HALF, tm), lambda b, i: (0, i)),
            pl.BlockSpec((HALF, tm), lambda b, i: (0, i)),
        ],
        out_specs=(
            pl.BlockSpec((1, Q_ROWS, tm), lambda b, i: (b, 0, i)),
            pl.BlockSpec((1, tm, K_ROWS), lambda b, i: (b, i, 0)),
            pl.BlockSpec((1, A_KV_HEADS * VA_BLK, tm), lambda b, i: (b, 0, i)),
            pl.BlockSpec((1, D_WIDTH, tm), lambda b, i: (b, 0, i)),
            pl.BlockSpec((1, tm, HY_IN), lambda b, i: (b, i, 0)),
        ),
        compiler_params=_cparams(("parallel", "parallel")),
        name="projection",
    )(xs, mod, g, wqT, why, gains, cosT, sinT)


def _attend(units, scr, *, ctx_only, ctx_len, tk, sum_rows):
    n_u = len(units)
    qz, m_run, l_run, acc, s_buf, cm_buf, qz0 = _split_scratch(scr, n_u)
    s_len = units[0][0].shape[1]
    for u in range(n_u):
        m_run[u][...] = jnp.full(m_run[u].shape, -1e30, F32)
        l_run[u][...] = jnp.zeros(l_run[u].shape, F32)
        acc[u][...] = jnp.zeros(acc[u].shape, F32)

    def fold8(s, op):
        return op(s.reshape(s.shape[0] // 8, 8, s.shape[1]), axis=0)

    def key_rows(start, j, rb):
        if isinstance(start, int):
            return pl.ds(start + j * rb, rb)
        return pl.ds(pl.multiple_of(start + j * rb, rb), rb)

    def advance(nxt, cur, size):
        rb = min(size, KEY_BLOCK)
        if cur is not None:
            u_c, start_c, slot_c = cur
            _, vT_ref, v0, dv = units[u_c]
            m = m_run[u_c][...]
            m_new = jnp.maximum(m, jnp.max(cm_buf[slot_c][...], axis=0, keepdims=True))
            m_run[u_c][...] = m_new
            alpha = jnp.exp2(m - m_new)
        cmax, psum, pv = None, None, None
        for j in range(size // rb):
            rows = slice(j * rb, (j + 1) * rb)
            if nxt is not None:
                u_n, start_n, slot_n = nxt[:3]
                q_n = qz[u_n][...] if len(nxt) == 3 else qz0[nxt[3]]
                s = _dot(units[u_n][0][0, key_rows(start_n, j, rb), :], q_n)
                s_buf[slot_n][rows, :] = s
                cmax = fold8(s, jnp.max) if cmax is None else jnp.maximum(cmax, fold8(s, jnp.max))
            if cur is not None:
                p = jnp.exp2(s_buf[slot_c][rows, :] - m_new)
                if not sum_rows:
                    psum = fold8(p, jnp.sum) if psum is None else psum + fold8(p, jnp.sum)
                d = _dot(vT_ref[0, v0:v0 + dv, key_rows(start_c, j, rb)], p.astype(BF16))
                pv = d if pv is None else pv + d
        if nxt is not None:
            cm_buf[slot_n][...] = cmax
        if cur is not None:
            if not sum_rows:
                l_run[u_c][...] = alpha * l_run[u_c][...] + jnp.sum(psum, axis=0, keepdims=True)
            acc[u_c][...] = alpha * acc[u_c][...] + pv

    if ctx_only:
        for u in range(n_u):
            advance((u, 0, u % 2), None, ctx_len)
            advance(None, (u, 0, u % 2), ctx_len)
        advance((0, 0, 0, 1), None, tk)
    else:
        n_chunks = s_len // tk

        def body(c, carry):
            start = pl.multiple_of(c * tk, tk)
            wrap = (c == n_chunks - 1).astype(jnp.int32)
            nxt = pl.multiple_of((1 - wrap) * (c + 1) * tk, tk)
            for u in range(n_u):
                task_n = (u + 1, start, (u + 1) % 2) if u + 1 < n_u else (0, nxt, (u + 1) % 2, wrap)
                advance(task_n, (u, start, u % 2), tk)
            return carry
        lax.fori_loop(0, n_chunks, body, 0)
    outs = []
    for u in range(n_u):
        dv = units[u][3] - sum_rows
        l = acc[u][dv:dv + 1, :] if sum_rows else l_run[u][...]
        outs.append(acc[u][:dv, :] * (1.0 / l))
    return outs


def _attn_scratch(n_u, rows, tq, tk):
    per_unit = [((LANE, tq), BF16), ((1, tq), F32), ((1, tq), F32), ((rows, tq), F32)]
    per_slot = [((tk, tq), F32), ((8, tq), F32)]
    return ([pltpu.VMEM(s, d) for s, d in per_unit for _ in range(n_u)]
            + [pltpu.VMEM(s, d) for s, d in per_slot for _ in range(2)]
            + [pltpu.VMEM((2, LANE, tq), BF16)])


def _split_scratch(scr, n_u):
    groups = [scr[i * n_u:(i + 1) * n_u] for i in range(4)]
    rest = scr[4 * n_u:]
    return groups + [rest[0:2], rest[2:4], rest[4]]


def _run_ctx_or_latent(qi, n_ctx_tiles, run):
    @pl.when(qi < n_ctx_tiles)
    def _():
        run(True)

    @pl.when(qi >= n_ctx_tiles)
    def _():
        run(False)


def _attn_a_kernel(qT_ref, qn_ref, k_ref, vT_ref, g_ref, o_ref, *scr, ctx_len, tq, tk):
    zeros = jnp.zeros((HEAD_DIM, tq), BF16)
    group = A_HEADS // A_KV_HEADS
    for h in range(A_HEADS):
        q = qT_ref[0, h * HEAD_DIM:(h + 1) * HEAD_DIM, :]
        scr[h][...] = jnp.concatenate([q, zeros] if h // group == 0 else [zeros, q], axis=0)
    scr[-1][0] = jnp.concatenate([qT_ref[0, :HEAD_DIM, :], zeros], axis=0)
    scr[-1][1] = jnp.concatenate([qn_ref[0, :HEAD_DIM, :], zeros], axis=0)
    units = [(k_ref, vT_ref, (h // group) * VA_BLK, VA_BLK) for h in range(A_HEADS)]

    def run(ctx_only):
        outs = _attend(units, scr, ctx_only=ctx_only, ctx_len=ctx_len, tk=tk, sum_rows=SUM_ROWS)
        y = jnp.concatenate(outs, axis=0)
        ms = jnp.mean(y * y, axis=0, keepdims=True)
        y = y * lax.rsqrt(ms + EPS) * g_ref[...]
        o_ref[0] = y.T.astype(BF16)

    _run_ctx_or_latent(pl.program_id(1), ctx_len // tq, run)


def _attention_a(qT, k, vT, g_a, *, ctx_len, tq, tk):
    bsz, _, s_len = qT.shape
    kern = functools.partial(_attn_a_kernel, ctx_len=ctx_len, tq=tq, tk=tk)
    return pl.pallas_call(
        kern,
        out_shape=jax.ShapeDtypeStruct((bsz, s_len, A_WIDTH), BF16),
        grid=(bsz, s_len // tq),
        in_specs=[
            pl.BlockSpec((1, A_WIDTH, tq), lambda b, i: (b, 0, i)),
            pl.BlockSpec((1, A_WIDTH, tq), lambda b, i: (b, 0, jnp.minimum(i + 1, s_len // tq - 1))),
            pl.BlockSpec((1, s_len, LANE), lambda b, i: (b, 0, 0)),
            pl.BlockSpec((1, A_KV_HEADS * VA_BLK, s_len), lambda b, i: (b, 0, 0)),
            pl.BlockSpec((A_WIDTH, 1), lambda b, i: (0, 0)),
        ],
        out_specs=pl.BlockSpec((1, tq, A_WIDTH), lambda b, i: (b, i, 0)),
        scratch_shapes=_attn_scratch(A_HEADS, VA_BLK, tq, tk),
        compiler_params=_cparams(("parallel", "arbitrary")),
        name="attention_gqa",
    )(qT, qT, k, vT, g_a)


D_HEADS_PER_STEP = 2


def _attn_d_kernel(*refs, ctx_len, tq, tk):
    n_h = D_HEADS_PER_STEP
    q_refs, k_refs, v_refs = refs[:n_h], refs[n_h:2 * n_h], refs[2 * n_h:3 * n_h]
    qn_ref, g_ref, lamv_ref, lami_ref, o_ref = refs[3 * n_h:3 * n_h + 5]
    scr = refs[3 * n_h + 5:]
    zeros = jnp.zeros((HEAD_DIM, tq), BF16)
    lv = lamv_ref[...]
    lam_init = lami_ref[0:1, 0:1]
    lam = (jnp.exp(jnp.sum(lv[0:1] * lv[1:2], axis=-1, keepdims=True))
           - jnp.exp(jnp.sum(lv[2:3] * lv[3:4], axis=-1, keepdims=True)) + lam_init)
    units = []
    for h in range(n_h):
        scr[2 * h][...] = jnp.concatenate([q_refs[h][0, :HEAD_DIM, :], zeros], axis=0)
        scr[2 * h + 1][...] = jnp.concatenate([zeros, q_refs[h][0, HEAD_DIM:, :]], axis=0)
        units += [(k_refs[h], v_refs[h], 0, D_VALUE_DIM)] * 2
    scr[-1][0] = jnp.concatenate([q_refs[0][0, :HEAD_DIM, :], zeros], axis=0)
    scr[-1][1] = jnp.concatenate([qn_ref[0, :HEAD_DIM, :], zeros], axis=0)

    def run(ctx_only):
        outs = _attend(units, scr, ctx_only=ctx_only, ctx_len=ctx_len, tk=tk, sum_rows=0)
        ys = []
        for h in range(n_h):
            y = outs[2 * h] - lam * outs[2 * h + 1]
            ms = jnp.mean(y * y, axis=0, keepdims=True)
            ys.append(y * lax.rsqrt(ms + EPS) * g_ref[h] * (1.0 - lam_init))
        o_ref[0] = jnp.concatenate(ys, axis=0).T.astype(BF16)

    _run_ctx_or_latent(pl.program_id(2), ctx_len // tq, run)


def _attention_d(qT, k, vT, g_d, lamv, lami, *, ctx_len, tq, tk):
    bsz, _, s_len = qT.shape
    n_h = D_HEADS_PER_STEP
    kern = functools.partial(_attn_d_kernel, ctx_len=ctx_len, tq=tq, tk=tk)
    q_blk0 = A_WIDTH // LANE
    k_blk0 = A_KV_WIDTH // LANE
    q_specs = [pl.BlockSpec((1, LANE, tq), lambda b, hp, i, j=j: (b, q_blk0 + n_h * hp + j, i)) for j in range(n_h)]
    k_specs = [pl.BlockSpec((1, s_len, LANE), lambda b, hp, i, j=j: (b, 0, k_blk0 + n_h * hp + j)) for j in range(n_h)]
    v_specs = [pl.BlockSpec((1, D_VALUE_DIM, s_len), lambda b, hp, i, j=j: (b, n_h * hp + j, 0)) for j in range(n_h)]
    return pl.pallas_call(
        kern,
        out_shape=jax.ShapeDtypeStruct((bsz, s_len, D_WIDTH), BF16),
        grid=(bsz, D_HEADS // n_h, s_len // tq),
        in_specs=q_specs + k_specs + v_specs + [
            pl.BlockSpec((1, LANE, tq), lambda b, hp, i: (b, q_blk0 + n_h * hp, jnp.minimum(i + 1, s_len // tq - 1))),
            pl.BlockSpec((n_h, D_VALUE_DIM, 1), lambda b, hp, i: (hp, 0, 0)),
            pl.BlockSpec((4, HEAD_DIM), lambda b, hp, i: (0, 0)),
            pl.BlockSpec((8, LANE), lambda b, hp, i: (0, 0)),
        ],
        out_specs=pl.BlockSpec((1, tq, n_h * D_VALUE_DIM), lambda b, hp, i: (b, i, hp)),
        scratch_shapes=_attn_scratch(2 * n_h, D_VALUE_DIM, tq, tk),
        compiler_params=_cparams(("parallel", "parallel", "arbitrary")),
        name="attention_diff",
    )(*([qT] * n_h + [k] * n_h + [vT] * n_h + [qT, g_d, lamv, lami]))


def _dwconv3_seq(u, w_ref, b_ref):
    n = u.shape[0]
    row = lax.broadcasted_iota(jnp.int32, (n, 1), 0)
    prev = jnp.where(row == 0, 0.0, pltpu.roll(u, 1, axis=0))
    nxt = jnp.where(row == n - 1, 0.0, pltpu.roll(u, n - 1, axis=0))
    return prev * w_ref[0:1, :] + u * w_ref[1:2, :] + nxt * w_ref[2:3, :] + b_ref[...]


def _hy_conv_kernel(hy_ref, w_ref, b_ref, lat_ref, ctx_ref, *, ctx_len):
    ctx_ref[0, 0] = _dwconv3_seq(hy_ref[0, :ctx_len, :], w_ref, b_ref)
    lat_ref[0, 0, 0] = _dwconv3_seq(hy_ref[0, ctx_len:, :], w_ref, b_ref)


def _hyena_short_conv(hy, w, b, *, ctx_len):
    bsz, s_len, _ = hy.shape
    n = s_len - ctx_len
    per = HY_WIDTH // LANE
    kern = functools.partial(_hy_conv_kernel, ctx_len=ctx_len)
    return pl.pallas_call(
        kern,
        out_shape=(
            jax.ShapeDtypeStruct((3, per, bsz, n, LANE), F32),
            jax.ShapeDtypeStruct((3, bsz, ctx_len, HY_WIDTH), F32),
        ),
        grid=(bsz, HY_IN // LANE),
        in_specs=[
            pl.BlockSpec((1, s_len, LANE), lambda b, j: (b, 0, j)),
            pl.BlockSpec((3, LANE), lambda b, j: (0, j)),
            pl.BlockSpec((1, LANE), lambda b, j: (0, j)),
        ],
        out_specs=(
            pl.BlockSpec((1, 1, 1, n, LANE), lambda b, j: (j // per, j % per, b, 0, 0)),
            pl.BlockSpec((1, 1, ctx_len, LANE), lambda b, j: (j // per, b, 0, j % per)),
        ),
        compiler_params=_cparams(("parallel", "parallel")),
        name="hyena_short_conv",
    )(hy, w, b.reshape(1, HY_IN))


def _filter_kernel(z_ref, w1_ref, b1_ref, w2_ref, b2_ref, w3_ref, b3_ref, fr_ref, ad_ref, h_ref, sum_ref):
    i = pl.program_id(0)
    z = z_ref[...]
    fr = fr_ref[...]
    h = jnp.sin(fr * (jnp.dot(z, w1_ref[...], preferred_element_type=F32, precision=HIGHEST) + b1_ref[...]))
    h = jnp.sin(fr * (jnp.dot(h, w2_ref[...], preferred_element_type=F32, precision=HIGHEST) + b2_ref[...]))
    h = jnp.dot(h, w3_ref[...], preferred_element_type=F32, precision=HIGHEST) + b3_ref[...]
    h = h * jnp.exp(-z[:, 0:1] * ad_ref[...])
    for q in range(HY_FILTER_CH // HY_WIDTH):
        for hh in range(HY_WIDTH // LANE):
            c0 = q * HY_WIDTH + hh * LANE
            h_ref[hh, q] = h[:, c0:c0 + LANE]

    @pl.when(i == 0)
    def _():
        sum_ref[...] = jnp.zeros_like(sum_ref)

    sum_ref[...] += jnp.sum(jnp.abs(h), axis=0, keepdims=True)


def _hyena_filters(zfeat, w1, b1, w2, b2, w3, b3, freq, absd):
    n, kz = zfeat.shape
    tn = min(n, 1024)
    full = lambda a: pl.BlockSpec(a.shape, lambda i: (0, 0))
    return pl.pallas_call(
        _filter_kernel,
        out_shape=(
            jax.ShapeDtypeStruct((HY_WIDTH // LANE, HY_FILTER_CH // HY_WIDTH, n, LANE), F32),
            jax.ShapeDtypeStruct((1, HY_FILTER_CH), F32),
        ),
        grid=(n // tn,),
        in_specs=[pl.BlockSpec((tn, kz), lambda i: (i, 0)), full(w1), full(b1), full(w2), full(b2),
                  full(w3), full(b3), full(freq), full(absd)],
        out_specs=(
            pl.BlockSpec((HY_WIDTH // LANE, HY_FILTER_CH // HY_WIDTH, tn, LANE), lambda i: (0, 0, i, 0)),
            pl.BlockSpec((1, HY_FILTER_CH), lambda i: (0, 0)),
        ),
        compiler_params=_cparams(("arbitrary",)),
        name="hyena_filters",
    )(zfeat, w1, b1, w2, b2, w3, b3, freq, absd)


T2_BLOCK = 8


def _outer_fwd_kernel(mh_ref, ml_ref, x_ref, o_ref, *, n1, passes):
    rows = x_ref.shape[3]
    xs = [x_ref.at[0, h, 0].reshape(rows * T2_BLOCK, LANE) for h in range(2)]
    outs = [[o_ref.at[0, ri, h].reshape(n1 * T2_BLOCK, LANE) for h in range(2)] for ri in range(2)]
    for t in range(T2_BLOCK):
        sel = pl.ds(t, rows, stride=T2_BLOCK)
        y = _dot_p(mh_ref[...], ml_ref[...], jnp.concatenate([xs[0][sel, :], xs[1][sel, :]], axis=1), passes)
        for ri in range(2):
            for h in range(2):
                outs[ri][h][pl.ds(t, n1, stride=T2_BLOCK), :] = y[ri * n1:(ri + 1) * n1, h * LANE:(h + 1) * LANE]


def _outer_forward(m_hi, m_lo, x6, g, passes):
    _, _, p_n, rows, n2, _ = x6.shape
    n1 = m_hi.shape[0] // 2
    kern = functools.partial(_outer_fwd_kernel, n1=n1, passes=passes)
    return pl.pallas_call(
        kern,
        out_shape=jax.ShapeDtypeStruct((p_n, 2, 2, n1, n2, LANE), F32),
        grid=(p_n, n2 // T2_BLOCK),
        in_specs=[pl.BlockSpec(m_hi.shape, lambda p, j: (0, 0)), pl.BlockSpec(m_lo.shape, lambda p, j: (0, 0)),
                  pl.BlockSpec((1, 2, 1, rows, T2_BLOCK, LANE), lambda p, j: (g, 0, p, 0, j, 0))],
        out_specs=pl.BlockSpec((1, 2, 2, n1, T2_BLOCK, LANE), lambda p, j: (p, 0, 0, 0, j, 0)),
        compiler_params=_cparams(("parallel", "parallel")),
        name="dft_outer_fwd",
    )(m_hi, m_lo, x6)


def _outer_inv_kernel(mh_ref, c_ref, gate_ref, u_ref, bias_ref, o_ref):
    n1 = c_ref.shape[3]
    rows = o_ref.shape[2]
    cs = [[c_ref.at[0, ri, h].reshape(n1 * T2_BLOCK, LANE) for h in range(2)] for ri in range(2)]
    gates = [gate_ref.at[0, h, 0].reshape(rows * T2_BLOCK, LANE) for h in range(2)]
    us = [u_ref.at[0, h, 0].reshape(rows * T2_BLOCK, LANE) for h in range(2)]
    outs = [o_ref.at[h, 0].reshape(rows * T2_BLOCK, LANE) for h in range(2)]
    for t in range(T2_BLOCK):
        sel_f = pl.ds(t, n1, stride=T2_BLOCK)
        sel_t = pl.ds(t, rows, stride=T2_BLOCK)
        c2 = jnp.concatenate([jnp.concatenate([cs[ri][0][sel_f, :], cs[ri][1][sel_f, :]], axis=1)
                              for ri in range(2)], axis=0)
        y = _dot(mh_ref[...], c2.astype(BF16))
        for h in range(2):
            lanes = slice(h * LANE, (h + 1) * LANE)
            outs[h][sel_t, :] = gates[h][sel_t, :] * (y[:, lanes] + us[h][sel_t, :] * bias_ref[:, lanes])


def _outer_inverse(m_hi, c6, gate6, g_gate, u6, g_u, bias):
    p_n, _, _, n1, n2, _ = c6.shape
    rows = m_hi.shape[0]
    gspec = lambda g: pl.BlockSpec((1, 2, 1, rows, T2_BLOCK, LANE), lambda p, j: (g, 0, p, 0, j, 0))
    return pl.pallas_call(
        _outer_inv_kernel,
        out_shape=jax.ShapeDtypeStruct((2, p_n, rows, n2, LANE), F32),
        grid=(p_n, n2 // T2_BLOCK),
        in_specs=[pl.BlockSpec(m_hi.shape, lambda p, j: (0, 0)),
                  pl.BlockSpec((1, 2, 2, n1, T2_BLOCK, LANE), lambda p, j: (p, 0, 0, 0, j, 0)),
                  gspec(g_gate), gspec(g_u), pl.BlockSpec((1, 2 * LANE), lambda p, j: (0, 0))],
        out_specs=pl.BlockSpec((2, 1, rows, T2_BLOCK, LANE), lambda p, j: (0, p, 0, j, 0)),
        compiler_params=_cparams(("parallel", "parallel")),
        name="dft_outer_inv",
    )(m_hi, c6, gate6, u6, bias)


def _halves_to_rows(a_ref, j):
    return jnp.concatenate([jnp.concatenate([a_ref[0, ri, 0, j], a_ref[0, ri, 1, j]], axis=1) for ri in range(2)],
                           axis=0)


def _kf_kernel(gh_ref, gl_ref, af_ref, ab_ref, h0_ref, ss_ref, o_ref, *, fb, n2):
    scale = 1.0 / (ss_ref[0, 0:1, :] + ss_ref[0, 1:2, :] + EPS)
    b0 = jnp.concatenate([h0_ref[0, 0, 0:1, :], h0_ref[1, 0, 0:1, :]], axis=1)
    for j in range(fb):
        xf = _dot3(gh_ref[j], gl_ref[j], _halves_to_rows(af_ref, j))
        xb = _dot3(gh_ref[j], gl_ref[j], _halves_to_rows(ab_ref, j))
        o_ref[0, j, :n2, :] = (xf[:n2] + xb[:n2] - b0) * scale
        o_ref[0, j, n2:, :] = (xf[n2:] - xb[n2:]) * scale


def _filter_spectrum(g_hi, g_lo, a6, h6, ss, *, fb):
    n_q, _, _, n1, n2, _ = a6.shape
    n_ord = n_q // 2
    c = 2 * LANE
    kern = functools.partial(_kf_kernel, fb=fb, n2=n2)
    return pl.pallas_call(
        kern,
        out_shape=jax.ShapeDtypeStruct((n_ord, n1, 2 * n2, c), F32),
        grid=(n1 // fb, n_ord),
        in_specs=[
            pl.BlockSpec((fb, 2 * n2, 2 * n2), lambda f, o: (f, 0, 0)),
            pl.BlockSpec((fb, 2 * n2, 2 * n2), lambda f, o: (f, 0, 0)),
            pl.BlockSpec((1, 2, 2, fb, n2, LANE), lambda f, o: (o, 0, 0, f, 0, 0)),
            pl.BlockSpec((1, 2, 2, fb, n2, LANE), lambda f, o: (n_ord + o, 0, 0, f, 0, 0)),
            pl.BlockSpec((2, 1, 8, LANE), lambda f, o: (0, n_ord + o, 0, 0)),
            pl.BlockSpec((1, 2, c), lambda f, o: (o, 0, 0)),
        ],
        out_specs=pl.BlockSpec((1, fb, 2 * n2, c), lambda f, o: (o, f, 0, 0)),
        compiler_params=_cparams(("parallel", "parallel")),
        name="filter_spectrum",
    )(g_hi, g_lo, a6, a6, h6, ss)


def _spec_kernel(gh_ref, hh_ref, kf_ref, a_ref, o_ref, *, fb, n2):
    for j in range(fb):
        x = _dot(gh_ref[j], _halves_to_rows(a_ref, j).astype(BF16))
        xr, xi = x[:n2], x[n2:]
        kr, ki = kf_ref[0, j, :n2], kf_ref[0, j, n2:]
        y = jnp.concatenate([xr * kr - xi * ki, xr * ki + xi * kr], axis=0)
        c2 = _dot(hh_ref[j], y.astype(BF16))
        for ri in range(2):
            for h in range(2):
                o_ref[0, ri, h, j] = c2[ri * n2:(ri + 1) * n2, h * LANE:(h + 1) * LANE]


def _spectral_multiply(g_hi, h_hi, kf, a6, order, *, fb):
    p_n, _, _, n1, n2, _ = a6.shape
    c = 2 * LANE
    kern = functools.partial(_spec_kernel, fb=fb, n2=n2)
    gspec = pl.BlockSpec((fb, 2 * n2, 2 * n2), lambda f, p: (f, 0, 0))
    aspec = pl.BlockSpec((1, 2, 2, fb, n2, LANE), lambda f, p: (p, 0, 0, f, 0, 0))
    return pl.pallas_call(
        kern,
        out_shape=jax.ShapeDtypeStruct(a6.shape, F32),
        grid=(n1 // fb, p_n),
        in_specs=[gspec, gspec,
                  pl.BlockSpec((1, fb, 2 * n2, c), lambda f, p: (order, f, 0, 0)), aspec],
        out_specs=aspec,
        compiler_params=_cparams(("parallel", "parallel")),
        name="spectral_multiply",
    )(g_hi, h_hi, kf, a6)


def _ctx_kf_kernel(fh_ref, fl_ref, k_ref, ss_ref, o_ref):
    scale = 1.0 / (ss_ref[0, 0:1, :] + ss_ref[0, 1:2, :] + EPS)
    o_ref[0] = _dot3(fh_ref[...], fl_ref[...], k_ref[0]) * scale


def _ctx_filter_spectrum(f_hi, f_lo, k, ss):
    n_ord, nc, c = k.shape
    return pl.pallas_call(
        _ctx_kf_kernel,
        out_shape=jax.ShapeDtypeStruct((n_ord, 2 * nc, c), F32),
        grid=(n_ord,),
        in_specs=[
            pl.BlockSpec(f_hi.shape, lambda o: (0, 0)),
            pl.BlockSpec(f_lo.shape, lambda o: (0, 0)),
            pl.BlockSpec((1, nc, c), lambda o: (o, 0, 0)),
            pl.BlockSpec((1, 2, c), lambda o: (o, 0, 0)),
        ],
        out_specs=pl.BlockSpec((1, 2 * nc, c), lambda o: (o, 0, 0)),
        compiler_params=_cparams(("parallel",)),
        name="ctx_filter_spectrum",
    )(f_hi, f_lo, k, ss)


def _ctx_conv_kernel(fh_ref, fl_ref, eh_ref, el_ref, kf_ref, u_ref, bias_ref, o_ref, *, nc):
    v, x1, x2 = u_ref[0, 0], u_ref[1, 0], u_ref[2, 0]

    def conv(z, order):
        x = _dot3(fh_ref[...], fl_ref[...], z)
        xr, xi = x[:nc], x[nc:]
        kr, ki = kf_ref[order, :nc], kf_ref[order, nc:]
        y = jnp.concatenate([xr * kr - xi * ki, xr * ki + xi * kr], axis=0)
        return _dot3(eh_ref[...], el_ref[...], y) + z * bias_ref[order:order + 1, :]

    o_ref[0] = x2 * conv(x1 * conv(v, 0), 1)


def _ctx_hyena(f_hi, f_lo, e_hi, e_lo, kf, u3, bias):
    _, p_n, n2x, c = u3.shape
    nc = kf.shape[1] // 2
    kern = functools.partial(_ctx_conv_kernel, nc=nc)
    full = lambda a: pl.BlockSpec(a.shape, lambda p: (0,) * a.ndim)
    return pl.pallas_call(
        kern,
        out_shape=jax.ShapeDtypeStruct((p_n, n2x, c), F32),
        grid=(p_n,),
        in_specs=[full(f_hi), full(f_lo), full(e_hi), full(e_lo), full(kf),
                  pl.BlockSpec((3, 1, n2x, c), lambda p: (0, p, 0, 0)), full(bias)],
        out_specs=pl.BlockSpec((1, n2x, c), lambda p: (p, 0, 0)),
        compiler_params=_cparams(("parallel",)),
        name="ctx_hyena",
    )(f_hi, f_lo, e_hi, e_lo, kf, u3, bias)


def _merge_kernel(x_ref, ya_ref, yh_ref, yd_ref, gh_ref, w_ref, mod_ref, o_ref, *, tm, ctx_len):
    i = pl.program_id(1)
    row = i * tm + lax.broadcasted_iota(jnp.int32, (tm, 1), 0)
    yh = yh_ref[0]
    ms = jnp.mean(yh * yh, axis=-1, keepdims=True)
    yh = yh * lax.rsqrt(ms + EPS) * gh_ref[...]
    y = _dot(ya_ref[0].astype(BF16), w_ref[:A_WIDTH, :])
    y += _dot(yh.astype(BF16), w_ref[A_WIDTH:A_WIDTH + HY_WIDTH, :])
    y += _dot(yd_ref[0].astype(BF16), w_ref[A_WIDTH + HY_WIDTH:, :])
    gate = jnp.where(row < ctx_len, mod_ref[0, 2:3, :], mod_ref[0, 10:11, :])
    o_ref[0] = x_ref[0] + gate * y


def _merge(xs, ya, yh, yd, g_h, w_out, mod, *, ctx_len, tm):
    bsz, s_len, d = xs.shape
    kern = functools.partial(_merge_kernel, tm=tm, ctx_len=ctx_len)
    tok = lambda w: pl.BlockSpec((1, tm, w), lambda b, i: (b, i, 0))
    return pl.pallas_call(
        kern,
        out_shape=jax.ShapeDtypeStruct(xs.shape, F32),
        grid=(bsz, s_len // tm),
        in_specs=[tok(d), tok(A_WIDTH), tok(HY_WIDTH), tok(D_WIDTH),
                  pl.BlockSpec((1, HY_WIDTH), lambda b, i: (0, 0)),
                  pl.BlockSpec((MIX_WIDTH, d), lambda b, i: (0, 0)),
                  pl.BlockSpec((1, 16, d), lambda b, i: (b, 0, 0))],
        out_specs=tok(d),
        compiler_params=_cparams(("parallel", "parallel")),
        name="merge",
    )(xs, ya, yh, yd, g_h, w_out, mod)


HALO = 8


def _ffn_kernel(x_ref, xp_ref, xn_ref, mod_ref, g_ref, wa_ref, wv_ref, cw_ref, cb_ref, wd_ref, o_ref,
                *, tm, fc, ctx_len, s_len):
    i = pl.program_id(1)
    x = x_ref[0]
    xa = jnp.concatenate([xp_ref[0], x, xn_ref[0]], axis=0)
    rows = tm + 2 * HALO
    row = i * tm - HALO + lax.broadcasted_iota(jnp.int32, (rows, 1), 0)
    h = _norm_modulate(xa, g_ref[...], mod_ref, row < ctx_len, 1).astype(BF16)
    hm = h[HALO:HALO + tm]
    rowm = row[HALO:HALO + tm]
    has_prev = jnp.logical_and(rowm != 0, rowm != ctx_len)
    has_next = jnp.logical_and(rowm != ctx_len - 1, rowm != s_len - 1)
    def up(j):
        cs = slice(j * fc, (j + 1) * fc)
        return _dot(h, wa_ref[:, cs]), _dot(hm, wv_ref[:, cs])

    n_chunks = wa_ref.shape[1] // fc
    acc = jnp.zeros((tm, x.shape[1]), F32)
    nxt = up(0)
    for j in range(n_chunks):
        cs = slice(j * fc, (j + 1) * fc)
        a, v = nxt
        if j + 1 < n_chunks:
            nxt = up(j + 1)
        a_prev = jnp.where(has_prev, pltpu.roll(a, 1, axis=0)[HALO:HALO + tm], 0.0)
        a_next = jnp.where(has_next, pltpu.roll(a, rows - 1, axis=0)[HALO:HALO + tm], 0.0)
        ac = a_prev * cw_ref[0:1, cs] + a[HALO:HALO + tm] * cw_ref[1:2, cs] + a_next * cw_ref[2:3, cs] + cb_ref[:, cs]
        gl = 0.5 * ac * (1.0 + jnp.tanh(math.sqrt(2.0 / math.pi) * (ac + 0.044715 * (ac * ac * ac))))
        acc += _dot((gl * v).astype(BF16), wd_ref[cs, :])
    gate = jnp.where(rowm < ctx_len, mod_ref[0, 5:6, :], mod_ref[0, 13:14, :])
    o_ref[0] = x + gate * acc


def _ffn(xs, mod, g, wa, wv, cw, cb, wd, *, ctx_len, tm, fc):
    bsz, s_len, d = xs.shape
    d_ff = wa.shape[1]
    nh = tm // HALO
    last = s_len // HALO - 1
    kern = functools.partial(_ffn_kernel, tm=tm, fc=fc, ctx_len=ctx_len, s_len=s_len)
    const = lambda shape: pl.BlockSpec(shape, lambda b, i: (0, 0), pipeline_mode=pl.Buffered(1))
    return pl.pallas_call(
        kern,
        out_shape=jax.ShapeDtypeStruct(xs.shape, F32),
        grid=(bsz, s_len // tm),
        in_specs=[
            pl.BlockSpec((1, tm, d), lambda b, i: (b, i, 0)),
            pl.BlockSpec((1, HALO, d), lambda b, i: (b, jnp.maximum(i * nh - 1, 0), 0)),
            pl.BlockSpec((1, HALO, d), lambda b, i: (b, jnp.minimum((i + 1) * nh, last), 0)),
            pl.BlockSpec((1, 16, d), lambda b, i: (b, 0, 0)),
            pl.BlockSpec((1, d), lambda b, i: (0, 0)),
            const((d, d_ff)), const((d, d_ff)), const((3, d_ff)), const((1, d_ff)), const((d_ff, d)),
        ],
        out_specs=pl.BlockSpec((1, tm, d), lambda b, i: (b, i, 0)),
        compiler_params=_cparams(("parallel", "parallel")),
        name="conv_glu",
    )(xs, xs, xs, mod, g, wa, wv, cw, cb, wd)


def _cis(num, den):
    ang = num.astype(F32) * (2.0 * math.pi / den)
    return jnp.cos(ang), jnp.sin(ang)


def _blockmat(re, im):
    return jnp.concatenate([jnp.concatenate([re, -im], axis=-1), jnp.concatenate([im, re], axis=-1)], axis=-2)


def _dft_tables(n):
    n2 = DFT_N2
    t1n = n // n2
    n1 = 2 * t1n
    big = n1 * n2
    f1 = jnp.arange(n1, dtype=jnp.int32)
    t1 = jnp.arange(t1n, dtype=jnp.int32)
    j2 = jnp.arange(n2, dtype=jnp.int32)
    c, s = _cis((f1[:, None] * t1[None, :]) % n1, n1)
    fwd_sig = _blockmat(c, -s)
    c, s = _cis((f1[:, None] * f1[None, :]) % n1, n1)
    fwd_fil = jnp.concatenate([c, -s], axis=0)[:, :t1n]
    ph = (n1 * j2[None, :, None] * j2[None, None, :] + f1[:, None, None] * j2[None, None, :]) % big
    c, s = _cis(ph, big)
    g = _blockmat(c, -s)
    h = jnp.swapaxes(g, 1, 2)
    c, s = _cis((t1[:, None] * f1[None, :]) % n1, n1)
    inv = _blockmat(c, s) * (1.0 / big)
    return tuple(_split_bf16(m) for m in (fwd_sig, fwd_fil, g, h, inv))


def _dense_dft_tables(n):
    big = 2 * n
    f = jnp.arange(big, dtype=jnp.int32)
    t = jnp.arange(n, dtype=jnp.int32)
    c, s = _cis((f[:, None] * t[None, :]) % big, big)
    fwd_sig = _blockmat(c, -s)
    c, s = _cis((f[:, None] * f[None, :]) % big, big)
    fwd_fil = jnp.concatenate([c, -s], axis=0)
    c, s = _cis((t[:, None] * f[None, :]) % big, big)
    inv = _blockmat(c, s) * (1.0 / big)
    return tuple(_split_bf16(m) for m in (fwd_sig, fwd_fil, inv))


def _filter_features(n):
    pos = jnp.arange(n, dtype=F32)
    t = (pos / max(n - 1, 1))[:, None]
    w = 2.0 * math.pi * pos[:, None] / n
    bands = jnp.linspace(1e-4, HY_BANDS - 1, HY_BANDS, dtype=F32)
    z = jnp.concatenate([t, jnp.cos(bands * w), -jnp.sin(bands * w)], axis=-1)
    return jnp.pad(z, ((0, 0), (0, LANE - HY_EMB_DIM)))


def _circular_filters(h, big):
    n = h.shape[0]
    fwd = jnp.moveaxis(h[:, 0], 1, 0)
    bwd = jnp.moveaxis(h[:, 1], 1, 0)
    zeros = jnp.zeros((fwd.shape[0], big - 2 * n + 1, fwd.shape[2]), F32)
    return jnp.concatenate([fwd, zeros, bwd[:, :0:-1]], axis=1)


def _pick(total, candidates):
    for t in candidates:
        if total % t == 0:
            return t
    raise ValueError(f"no tile for {total}")


def kernel(x, c, ctx, c_ctx, norm1_g, norm2_g, w_mod, b_mod, w_in, qn_a, kn_a, qn_d, kn_d, lam_q1, lam_k1, lam_q2, lam_k2, hy_conv_w, hy_conv_b, hy_fw1, hy_fb1, hy_fw2, hy_fb2, hy_fw3, hy_fb3, hy_freq, hy_bias, g_out, w_out, w_up, ffn_conv_w, ffn_conv_b, w_down):
    bsz, n_tok, d = x.shape
    ctx_len = ctx.shape[1]
    depth = w_in.shape[0]
    s_len = ctx_len + n_tok
    d_ff = w_down.shape[1]
    assert bsz % 2 == 0 and n_tok % (DFT_N2 * 8) == 0 and ctx_len % LANE == 0

    tm = _pick(s_len, (768, 512, 256, 128))
    tq = _pick(ctx_len, (256, 128))
    tk = _pick(s_len, (2816, 768, 256))
    fc = _pick(d_ff, (512, 256, 128))

    n_rows = n_tok // GRID_W
    rows = jnp.repeat(jnp.arange(n_rows, dtype=F32), GRID_W)
    cols = jnp.tile(jnp.arange(GRID_W, dtype=F32), n_rows)
    n_freq = HEAD_DIM // 4
    inv = ROPE_THETA ** (-jnp.arange(n_freq, dtype=F32) / n_freq)
    ang = jnp.concatenate([rows[:, None] * inv, cols[:, None] * inv], axis=-1)
    ang = jnp.concatenate([jnp.zeros((ctx_len, HALF), F32), ang], axis=0)
    cosT, sinT = jnp.cos(ang).T, jnp.sin(ang).T

    sig_f, fil_f, g_tab, h_tab, inv_f = _dft_tables(n_tok)
    csig_f, cfil_f, cinv_f = _dense_dft_tables(ctx_len)
    z_lat = _filter_features(n_tok)
    z_ctx = _filter_features(ctx_len)
    max_decay = math.log(HY_TARGET) / HY_FAST_DECAY
    min_decay = math.log(HY_TARGET) / HY_SLOW_DECAY
    absd = jnp.tile(jnp.abs(jnp.linspace(min_decay, max_decay, HY_WIDTH, dtype=F32)), 4)[None, :]
    t1n = n_tok // DFT_N2
    n1 = 2 * t1n
    big = n1 * DFT_N2
    fb = 8

    perm = np.concatenate([np.arange(0, HEAD_DIM, 2), np.arange(1, HEAD_DIM, 2)])
    qk_cols = np.concatenate(
        [OFF_QA + b * HEAD_DIM + perm for b in range(A_HEADS)]
        + [OFF_QD + b * HEAD_DIM + perm for b in range(2 * D_HEADS)]
        + [OFF_KA + b * HEAD_DIM + perm for b in range(A_KV_HEADS)]
        + [OFF_KD + b * HEAD_DIM + perm for b in range(2 * D_HEADS)])
    v_cols = np.concatenate([np.arange(OFF_VA, OFF_VA + A_KV_WIDTH), np.arange(OFF_VD, OFF_VD + D_WIDTH)])
    qkv_cols = np.concatenate([qk_cols, v_cols])
    wqT = jnp.swapaxes(w_in[:, :, qkv_cols], 1, 2).astype(BF16)
    why = w_in[:, :, OFF_HY:OFF_HY + HY_IN].astype(BF16)
    q_scale = HEAD_DIM ** -0.5 * math.log2(math.e)
    gains = jnp.concatenate(
        [jnp.tile(qn_a[:, perm], (1, A_HEADS)) * q_scale, jnp.tile(qn_d[:, perm], (1, 2 * D_HEADS)) * q_scale,
         jnp.tile(kn_a[:, perm], (1, A_KV_HEADS)), jnp.tile(kn_d[:, perm], (1, 2 * D_HEADS))],
        axis=1)[:, :, None].astype(F32)
    w_out_b = w_out.astype(BF16)
    w_up_a = w_up[:, :, :d_ff].astype(BF16)
    w_up_v = w_up[:, :, d_ff:].astype(BF16)
    w_down_b = w_down.astype(BF16)
    fw1p = jnp.pad(hy_fw1, ((0, 0), (0, LANE - HY_EMB_DIM), (0, 0)))
    lamv = jnp.stack([lam_q1, lam_k1, lam_q2, lam_k2], axis=1)
    lam_init = jnp.asarray([0.8 - 0.6 * math.exp(-0.3 * i) for i in range(depth)], F32)
    lami = jnp.broadcast_to(lam_init[:, None, None], (depth, 8, LANE))

    n_c = 8 * ((bsz + 1 + 7) // 8)
    cvec = jnp.concatenate([c, c_ctx[None, :], jnp.zeros((n_c - bsz - 1, d), F32)], axis=0)
    mods = _modulation(cvec, w_mod, b_mod).reshape(depth, n_c, 6, d)
    pad2 = jnp.zeros((depth, bsz, 2, d), F32)
    mod_all = jnp.concatenate(
        [jnp.broadcast_to(mods[:, bsz:bsz + 1], (depth, bsz, 6, d)), pad2, mods[:, :bsz], pad2], axis=2)

    xs = jnp.concatenate([ctx, x], axis=1)
    for i in range(depth):
        mod = mod_all[i]
        qT, k, vTa, vTd, hy = _projection(xs, mod, norm1_g[i][None, :], wqT[i], why[i], gains[i], cosT, sinT,
                                    ctx_len=ctx_len, tm=tm)
        ya = _attention_a(qT, k, vTa, g_out[i, :A_WIDTH, None], ctx_len=ctx_len, tq=tq, tk=tk)
        yd = _attention_d(qT, k, vTd, g_out[i, A_WIDTH + HY_WIDTH:].reshape(D_HEADS, D_VALUE_DIM, 1),
                          lamv[i], lami[i], ctx_len=ctx_len, tq=tq, tk=tk)

        u_lat, u_ctx = _hyena_short_conv(hy, hy_conv_w[i], hy_conv_b[i], ctx_len=ctx_len)
        fil_args = (fw1p[i], hy_fb1[i][None], hy_fw2[i], hy_fb2[i][None], hy_fw3[i], hy_fb3[i][None],
                    hy_freq[i][None], absd)
        h_lat, sum_lat = _hyena_filters(z_lat, *fil_args)
        h_ctx, sum_ctx = _hyena_filters(z_ctx, *fil_args)
        ss_lat = jnp.swapaxes(sum_lat.reshape(2, 2, HY_WIDTH), 0, 1)
        ss_ctx = jnp.swapaxes(sum_ctx.reshape(2, 2, HY_WIDTH), 0, 1)
        h_ctx = jnp.transpose(h_ctx, (2, 1, 0, 3)).reshape(ctx_len, 2, 2, HY_WIDTH)
        k_ctx = _circular_filters(h_ctx, 2 * ctx_len)

        ak = _outer_forward(fil_f[0], fil_f[1], h_lat.reshape(1, 2, 4, t1n, DFT_N2, LANE), 0, 3)
        kf = _filter_spectrum(g_tab[0], g_tab[1], ak, h_lat, ss_lat, fb=fb)
        p_n = bsz // 2
        u6 = u_lat.reshape(3, 2, p_n, 2 * t1n, DFT_N2, LANE)

        def long_conv(z6, g_z, g_gate, order):
            a = _outer_forward(sig_f[0], sig_f[1], z6, g_z, 1)
            cc = _spectral_multiply(g_tab[0], h_tab[0], kf, a, order, fb=fb)
            return _outer_inverse(inv_f[0], cc, u6, g_gate, z6, g_z, hy_bias[i, order][None, :])

        z1 = long_conv(u6, 0, 1, 0)
        yh2 = long_conv(z1[None], 0, 2, 1).reshape(2, bsz, n_tok, LANE)
        yh_lat = jnp.concatenate([yh2[0], yh2[1]], axis=-1)

        kf_c = _ctx_filter_spectrum(cfil_f[0], cfil_f[1], k_ctx, ss_ctx)
        yh_ctx = _ctx_hyena(csig_f[0], csig_f[1], cinv_f[0], cinv_f[1], kf_c,
                            u_ctx.reshape(3, p_n, 2 * ctx_len, HY_WIDTH), hy_bias[i])
        yh = jnp.concatenate([yh_ctx.reshape(bsz, ctx_len, HY_WIDTH), yh_lat], axis=1)

        xs = _merge(xs, ya, yh, yd, g_out[i, A_WIDTH:A_WIDTH + HY_WIDTH][None, :], w_out_b[i], mod,
                    ctx_len=ctx_len, tm=tm)
        xs = _ffn(xs, mod, norm2_g[i][None, :], w_up_a[i], w_up_v[i], ffn_conv_w[i], ffn_conv_b[i][None, :],
                  w_down_b[i], ctx_len=ctx_len, tm=tm, fc=fc)
    return xs[:, ctx_len:]
```

```python
import functools
import math

import numpy as np
import jax
import jax.numpy as jnp
from jax import lax
from jax.experimental import pallas as pl
from jax.experimental.pallas import tpu as pltpu

F32 = jnp.float32
BF16 = jnp.bfloat16

HEAD_DIM = 64
HALF = HEAD_DIM // 2
A_HEADS = 4
A_KV_HEADS = 2
A_WIDTH = A_HEADS * HEAD_DIM
A_KV_WIDTH = A_KV_HEADS * HEAD_DIM
HY_WIDTH = 256
HY_IN = 3 * HY_WIDTH
HY_BANDS = 16
HY_EMB_DIM = 1 + 2 * HY_BANDS
HY_HIDDEN = 64
HY_FILTER_CH = 4 * HY_WIDTH
HY_FAST_DECAY = 0.3
HY_SLOW_DECAY = 1.5
HY_TARGET = 1e-2
D_HEADS = 4
D_VALUE_DIM = 2 * HEAD_DIM
D_WIDTH = D_HEADS * D_VALUE_DIM
D_QK_WIDTH = D_HEADS * 2 * HEAD_DIM
MIX_WIDTH = A_WIDTH + HY_WIDTH + D_WIDTH
OFF_QA = 0
OFF_KA = A_WIDTH
OFF_VA = OFF_KA + A_KV_WIDTH
OFF_HY = OFF_VA + A_KV_WIDTH
OFF_QD = OFF_HY + HY_IN
OFF_KD = OFF_QD + D_QK_WIDTH
OFF_VD = OFF_KD + D_QK_WIDTH
GRID_W = 64
ROPE_THETA = 10000.0
EPS = 1e-6

N_QBLK = A_HEADS + 2 * D_HEADS
N_KBLK = A_KV_HEADS + 2 * D_HEADS
Q_ROWS = N_QBLK * HEAD_DIM
K_ROWS = N_KBLK * HEAD_DIM
V_ROWS = A_KV_WIDTH + D_WIDTH
QK_ROWS = Q_ROWS + K_ROWS
QKV_ROWS = QK_ROWS + V_ROWS
SUM_ROWS = 16
VA_BLK = HEAD_DIM + SUM_ROWS

LANE = 128
DFT_N2 = 128
KEY_BLOCK = 256
VMEM_LIMIT = 56 * 1024 * 1024
HIGHEST = lax.Precision.HIGHEST


def _cparams(sem):
    return pltpu.CompilerParams(dimension_semantics=sem, vmem_limit_bytes=VMEM_LIMIT)


def _split_bf16(x):
    hi = x.astype(BF16)
    lo = (x - hi.astype(F32)).astype(BF16)
    return hi, lo


def _dot(a, b):
    return jnp.dot(a, b, preferred_element_type=F32)


def _dot3(a_hi, a_lo, b):
    b_hi, b_lo = _split_bf16(b)
    return _dot(a_hi, b_hi) + (_dot(a_hi, b_lo) + _dot(a_lo, b_hi))


def _dot_p(a_hi, a_lo, b, passes):
    return _dot3(a_hi, a_lo, b) if passes == 3 else _dot(a_hi, b.astype(BF16))


def _mod_kernel(c_ref, w_ref, b_ref, o_ref):
    c = c_ref[...]
    s = c * (1.0 / (1.0 + jnp.exp(-c)))
    o_ref[0] = jnp.dot(s, w_ref[0], preferred_element_type=F32, precision=HIGHEST) + b_ref[0]


def _modulation(cvec, w_mod, b_mod):
    depth, d, n6 = w_mod.shape
    tn = 1024
    return pl.pallas_call(
        _mod_kernel,
        out_shape=jax.ShapeDtypeStruct((depth, cvec.shape[0], n6), F32),
        grid=(depth, n6 // tn),
        in_specs=[
            pl.BlockSpec(cvec.shape, lambda l, j: (0, 0)),
            pl.BlockSpec((1, d, tn), lambda l, j: (l, 0, j)),
            pl.BlockSpec((1, 1, tn), lambda l, j: (l, 0, j)),
        ],
        out_specs=pl.BlockSpec((1, cvec.shape[0], tn), lambda l, j: (l, 0, j)),
        compiler_params=_cparams(("arbitrary", "arbitrary")),
        name="modulation",
    )(cvec, w_mod, b_mod.reshape(depth, 1, n6))


def _norm_modulate(x, g, mod_ref, is_ctx, which):
    ms = jnp.mean(x * x, axis=-1, keepdims=True)
    xn = x * lax.rsqrt(ms + EPS) * g
    sh = jnp.where(is_ctx, mod_ref[0, 3 * which:3 * which + 1, :], mod_ref[0, 8 + 3 * which:9 + 3 * which, :])
    sc = jnp.where(is_ctx, mod_ref[0, 3 * which + 1:3 * which + 2, :],
                   mod_ref[0, 9 + 3 * which:10 + 3 * which, :])
    return xn * (1.0 + sc) + sh


def _proj_kernel(x_ref, mod_ref, g_ref, wq_ref, why_ref, gain_ref, cos_ref, sin_ref,
                 qT_ref, k_ref, vTa_ref, vTd_ref, hy_ref, *, tm, ctx_len):
    i = pl.program_id(1)
    row = i * tm + lax.broadcasted_iota(jnp.int32, (tm, 1), 0)
    h = _norm_modulate(x_ref[0], g_ref[...], mod_ref, row < ctx_len, 0).astype(BF16)
    pT = lax.dot_general(wq_ref[...], h, (((1,), (1,)), ((), ())), preferred_element_type=F32)
    hy_ref[0] = _dot(h, why_ref[...])
    c = cos_ref[...]
    s = sin_ref[...]
    blocks = []
    for b in range(N_QBLK + N_KBLK):
        blk = pT[b * HEAD_DIM:(b + 1) * HEAD_DIM]
        ssq = jnp.sum(blk * blk, axis=0, keepdims=True)
        blk = blk * lax.rsqrt(ssq * (1.0 / HEAD_DIM) + EPS) * gain_ref[b * HEAD_DIM:(b + 1) * HEAD_DIM, :]
        x1 = blk[:HALF]
        x2 = blk[HALF:]
        blocks.append(x1 * c - x2 * s)
        blocks.append(x1 * s + x2 * c)
    qT_ref[0] = jnp.concatenate(blocks[:2 * N_QBLK], axis=0).astype(BF16)
    kT = jnp.concatenate(blocks[2 * N_QBLK:], axis=0)
    k_ref[0] = kT.T.astype(BF16)
    ones = jnp.ones((SUM_ROWS, tm), BF16)
    va = [pT[QK_ROWS + g * HEAD_DIM:QK_ROWS + (g + 1) * HEAD_DIM].astype(BF16) for g in range(A_KV_HEADS)]
    vTa_ref[0] = jnp.concatenate([t for v in va for t in (v, ones)], axis=0)
    vTd_ref[0] = pT[QK_ROWS + A_KV_WIDTH:].astype(BF16)


def _projection(xs, mod, g, wqT, why, gains, cosT, sinT, *, ctx_len, tm):
    bsz, s_len, d = xs.shape
    kern = functools.partial(_proj_kernel, tm=tm, ctx_len=ctx_len)
    return pl.pallas_call(
        kern,
        out_shape=(
            jax.ShapeDtypeStruct((bsz, Q_ROWS, s_len), BF16),
            jax.ShapeDtypeStruct((bsz, s_len, K_ROWS), BF16),
            jax.ShapeDtypeStruct((bsz, A_KV_HEADS * VA_BLK, s_len), BF16),
            jax.ShapeDtypeStruct((bsz, D_WIDTH, s_len), BF16),
            jax.ShapeDtypeStruct((bsz, s_len, HY_IN), F32),
        ),
        grid=(bsz, s_len // tm),
        in_specs=[
            pl.BlockSpec((1, tm, d), lambda b, i: (b, i, 0)),
            pl.BlockSpec((1, 16, d), lambda b, i: (b, 0, 0)),
            pl.BlockSpec((1, d), lambda b, i: (0, 0)),
            pl.BlockSpec((QKV_ROWS, d), lambda b, i: (0, 0)),
            pl.BlockSpec((d, HY_IN), lambda b, i: (0, 0)),
            pl.BlockSpec((QK_ROWS, 1), lambda b, i: (0, 0)),
            pl.BlockSpec((HALF, tm), lambda b, i: (0, i)),
            pl.BlockSpec((HALF, tm), lambda b, i: (0, i)),
        ],
        out_specs=(
            pl.BlockSpec((1, Q_ROWS, tm), lambda b, i: (b, 0, i)),
            pl.BlockSpec((1, tm, K_ROWS), lambda b, i: (b, i, 0)),
            pl.BlockSpec((1, A_KV_HEADS * VA_BLK, tm), lambda b, i: (b, 0, i)),
            pl.BlockSpec((1, D_WIDTH, tm), lambda b, i: (b, 0, i)),
            pl.BlockSpec((1, tm, HY_IN), lambda b, i: (b, i, 0)),
        ),
        compiler_params=_cparams(("parallel", "parallel")),
        name="projection",
    )(xs, mod, g, wqT, why, gains, cosT, sinT)


def _attend(units, scr, *, ctx_only, ctx_len, tk, sum_rows):
    n_u = len(units)
    qz, m_run, l_run, acc, s_buf, cm_buf, qz0 = _split_scratch(scr, n_u)
    s_len = units[0][0].shape[1]
    for u in range(n_u):
        m_run[u][...] = jnp.full(m_run[u].shape, -1e30, F32)
        l_run[u][...] = jnp.zeros(l_run[u].shape, F32)
        acc[u][...] = jnp.zeros(acc[u].shape, F32)

    def fold8(s, op):
        return op(s.reshape(s.shape[0] // 8, 8, s.shape[1]), axis=0)

    def key_rows(start, j, rb):
        if isinstance(start, int):
            return pl.ds(start + j * rb, rb)
        return pl.ds(pl.multiple_of(start + j * rb, rb), rb)

    def advance(nxt, cur, size):
        rb = min(size, KEY_BLOCK)
        if cur is not None:
            u_c, start_c, slot_c = cur
            _, vT_ref, v0, dv = units[u_c]
            m = m_run[u_c][...]
            m_new = jnp.maximum(m, jnp.max(cm_buf[slot_c][...], axis=0, keepdims=True))
            m_run[u_c][...] = m_new
            alpha = jnp.exp2(m - m_new)
        cmax, psum, pv = None, None, None
        for j in range(size // rb):
            rows = slice(j * rb, (j + 1) * rb)
            if nxt is not None:
                u_n, start_n, slot_n = nxt[:3]
                q_n = qz[u_n][...] if len(nxt) == 3 else qz0[nxt[3]]
                s = _dot(units[u_n][0][0, key_rows(start_n, j, rb), :], q_n)
                s_buf[slot_n][rows, :] = s
                cmax = fold8(s, jnp.max) if cmax is None else jnp.maximum(cmax, fold8(s, jnp.max))
            if cur is not None:
                p = jnp.exp2(s_buf[slot_c][rows, :] - m_new)
                if not sum_rows:
                    psum = fold8(p, jnp.sum) if psum is None else psum + fold8(p, jnp.sum)
                d = _dot(vT_ref[0, v0:v0 + dv, key_rows(start_c, j, rb)], p.astype(BF16))
                pv = d if pv is None else pv + d
        if nxt is not None:
            cm_buf[slot_n][...] = cmax
        if cur is not None:
            if not sum_rows:
                l_run[u_c][...] = alpha * l_run[u_c][...] + jnp.sum(psum, axis=0, keepdims=True)
            acc[u_c][...] = alpha * acc[u_c][...] + pv

    if ctx_only:
        for u in range(n_u):
            advance((u, 0, u % 2), None, ctx_len)
            advance(None, (u, 0, u % 2), ctx_len)
        advance((0, 0, 0, 1), None, tk)
    else:
        n_chunks = s_len // tk

        def body(c, carry):
            start = pl.multiple_of(c * tk, tk)
            wrap = (c == n_chunks - 1).astype(jnp.int32)
            nxt = pl.multiple_of((1 - wrap) * (c + 1) * tk, tk)
            for u in range(n_u):
                task_n = (u + 1, start, (u + 1) % 2) if u + 1 < n_u else (0, nxt, (u + 1) % 2, wrap)
                advance(task_n, (u, start, u % 2), tk)
            return carry
        lax.fori_loop(0, n_chunks, body, 0)
    outs = []
    for u in range(n_u):
        dv = units[u][3] - sum_rows
        l = acc[u][dv:dv + 1, :] if sum_rows else l_run[u][...]
        outs.append(acc[u][:dv, :] * (1.0 / l))
    return outs


def _attn_scratch(n_u, rows, tq, tk):
    per_unit = [((LANE, tq), BF16), ((1, tq), F32), ((1, tq), F32), ((rows, tq), F32)]
    per_slot = [((tk, tq), F32), ((8, tq), F32)]
    return ([pltpu.VMEM(s, d) for s, d in per_unit for _ in range(n_u)]
            + [pltpu.VMEM(s, d) for s, d in per_slot for _ in range(2)]
            + [pltpu.VMEM((2, LANE, tq), BF16)])


def _split_scratch(scr, n_u):
    groups = [scr[i * n_u:(i + 1) * n_u] for i in range(4)]
    rest = scr[4 * n_u:]
    return groups + [rest[0:2], rest[2:4], rest[4]]


def _run_ctx_or_latent(qi, n_ctx_tiles, run):
    @pl.when(qi < n_ctx_tiles)
    def _():
        run(True)

    @pl.when(qi >= n_ctx_tiles)
    def _():
        run(False)


def _attn_a_kernel(qT_ref, qn_ref, k_ref, vT_ref, g_ref, o_ref, *scr, ctx_len, tq, tk):
    zeros = jnp.zeros((HEAD_DIM, tq), BF16)
    group = A_HEADS // A_KV_HEADS
    for h in range(A_HEADS):
        q = qT_ref[0, h * HEAD_DIM:(h + 1) * HEAD_DIM, :]
        scr[h][...] = jnp.concatenate([q, zeros] if h // group == 0 else [zeros, q], axis=0)
    scr[-1][0] = jnp.concatenate([qT_ref[0, :HEAD_DIM, :], zeros], axis=0)
    scr[-1][1] = jnp.concatenate([qn_ref[0, :HEAD_DIM, :], zeros], axis=0)
    units = [(k_ref, vT_ref, (h // group) * VA_BLK, VA_BLK) for h in range(A_HEADS)]

    def run(ctx_only):
        outs = _attend(units, scr, ctx_only=ctx_only, ctx_len=ctx_len, tk=tk, sum_rows=SUM_ROWS)
        y = jnp.concatenate(outs, axis=0)
        ms = jnp.mean(y * y, axis=0, keepdims=True)
        y = y * lax.rsqrt(ms + EPS) * g_ref[...]
        o_ref[0] = y.T.astype(BF16)

    _run_ctx_or_latent(pl.program_id(1), ctx_len // tq, run)


def _attention_a(qT, k, vT, g_a, *, ctx_len, tq, tk):
    bsz, _, s_len = qT.shape
    kern = functools.partial(_attn_a_kernel, ctx_len=ctx_len, tq=tq, tk=tk)
    return pl.pallas_call(
        kern,
        out_shape=jax.ShapeDtypeStruct((bsz, s_len, A_WIDTH), BF16),
        grid=(bsz, s_len // tq),
        in_specs=[
            pl.BlockSpec((1, A_WIDTH, tq), lambda b, i: (b, 0, i)),
            pl.BlockSpec((1, A_WIDTH, tq), lambda b, i: (b, 0, jnp.minimum(i + 1, s_len // tq - 1))),
            pl.BlockSpec((1, s_len, LANE), lambda b, i: (b, 0, 0)),
            pl.BlockSpec((1, A_KV_HEADS * VA_BLK, s_len), lambda b, i: (b, 0, 0)),
            pl.BlockSpec((A_WIDTH, 1), lambda b, i: (0, 0)),
        ],
        out_specs=pl.BlockSpec((1, tq, A_WIDTH), lambda b, i: (b, i, 0)),
        scratch_shapes=_attn_scratch(A_HEADS, VA_BLK, tq, tk),
        compiler_params=_cparams(("parallel", "arbitrary")),
        name="attention_gqa",
    )(qT, qT, k, vT, g_a)


D_HEADS_PER_STEP = 2


def _attn_d_kernel(*refs, ctx_len, tq, tk):
    n_h = D_HEADS_PER_STEP
    q_refs, k_refs, v_refs = refs[:n_h], refs[n_h:2 * n_h], refs[2 * n_h:3 * n_h]
    qn_ref, g_ref, lamv_ref, lami_ref, o_ref = refs[3 * n_h:3 * n_h + 5]
    scr = refs[3 * n_h + 5:]
    zeros = jnp.zeros((HEAD_DIM, tq), BF16)
    lv = lamv_ref[...]
    lam_init = lami_ref[0:1, 0:1]
    lam = (jnp.exp(jnp.sum(lv[0:1] * lv[1:2], axis=-1, keepdims=True))
           - jnp.exp(jnp.sum(lv[2:3] * lv[3:4], axis=-1, keepdims=True)) + lam_init)
    units = []
    for h in range(n_h):
        scr[2 * h][...] = jnp.concatenate([q_refs[h][0, :HEAD_DIM, :], zeros], axis=0)
        scr[2 * h + 1][...] = jnp.concatenate([zeros, q_refs[h][0, HEAD_DIM:, :]], axis=0)
        units += [(k_refs[h], v_refs[h], 0, D_VALUE_DIM)] * 2
    scr[-1][0] = jnp.concatenate([q_refs[0][0, :HEAD_DIM, :], zeros], axis=0)
    scr[-1][1] = jnp.concatenate([qn_ref[0, :HEAD_DIM, :], zeros], axis=0)

    def run(ctx_only):
        outs = _attend(units, scr, ctx_only=ctx_only, ctx_len=ctx_len, tk=tk, sum_rows=0)
        ys = []
        for h in range(n_h):
            y = outs[2 * h] - lam * outs[2 * h + 1]
            ms = jnp.mean(y * y, axis=0, keepdims=True)
            ys.append(y * lax.rsqrt(ms + EPS) * g_ref[h] * (1.0 - lam_init))
        o_ref[0] = jnp.concatenate(ys, axis=0).T.astype(BF16)

    _run_ctx_or_latent(pl.program_id(2), ctx_len // tq, run)


def _attention_d(qT, k, vT, g_d, lamv, lami, *, ctx_len, tq, tk):
    bsz, _, s_len = qT.shape
    n_h = D_HEADS_PER_STEP
    kern = functools.partial(_attn_d_kernel, ctx_len=ctx_len, tq=tq, tk=tk)
    q_blk0 = A_WIDTH // LANE
    k_blk0 = A_KV_WIDTH // LANE
    q_specs = [pl.BlockSpec((1, LANE, tq), lambda b, hp, i, j=j: (b, q_blk0 + n_h * hp + j, i)) for j in range(n_h)]
    k_specs = [pl.BlockSpec((1, s_len, LANE), lambda b, hp, i, j=j: (b, 0, k_blk0 + n_h * hp + j)) for j in range(n_h)]
    v_specs = [pl.BlockSpec((1, D_VALUE_DIM, s_len), lambda b, hp, i, j=j: (b, n_h * hp + j, 0)) for j in range(n_h)]
    return pl.pallas_call(
        kern,
        out_shape=jax.ShapeDtypeStruct((bsz, s_len, D_WIDTH), BF16),
        grid=(bsz, D_HEADS // n_h, s_len // tq),
        in_specs=q_specs + k_specs + v_specs + [
            pl.BlockSpec((1, LANE, tq), lambda b, hp, i: (b, q_blk0 + n_h * hp, jnp.minimum(i + 1, s_len // tq - 1))),
            pl.BlockSpec((n_h, D_VALUE_DIM, 1), lambda b, hp, i: (hp, 0, 0)),
            pl.BlockSpec((4, HEAD_DIM), lambda b, hp, i: (0, 0)),
            pl.BlockSpec((8, LANE), lambda b, hp, i: (0, 0)),
        ],
        out_specs=pl.BlockSpec((1, tq, n_h * D_VALUE_DIM), lambda b, hp, i: (b, i, hp)),
        scratch_shapes=_attn_scratch(2 * n_h, D_VALUE_DIM, tq, tk),
        compiler_params=_cparams(("parallel", "parallel", "arbitrary")),
        name="attention_diff",
    )(*([qT] * n_h + [k] * n_h + [vT] * n_h + [qT, g_d, lamv, lami]))


def _dwconv3_seq(u, w_ref, b_ref):
    n = u.shape[0]
    row = lax.broadcasted_iota(jnp.int32, (n, 1), 0)
    prev = jnp.where(row == 0, 0.0, pltpu.roll(u, 1, axis=0))
    nxt = jnp.where(row == n - 1, 0.0, pltpu.roll(u, n - 1, axis=0))
    return prev * w_ref[0:1, :] + u * w_ref[1:2, :] + nxt * w_ref[2:3, :] + b_ref[...]


def _hy_conv_kernel(hy_ref, w_ref, b_ref, lat_ref, ctx_ref, *, ctx_len):
    ctx_ref[0, 0] = _dwconv3_seq(hy_ref[0, :ctx_len, :], w_ref, b_ref)
    lat_ref[0, 0, 0] = _dwconv3_seq(hy_ref[0, ctx_len:, :], w_ref, b_ref)


def _hyena_short_conv(hy, w, b, *, ctx_len):
    bsz, s_len, _ = hy.shape
    n = s_len - ctx_len
    per = HY_WIDTH // LANE
    kern = functools.partial(_hy_conv_kernel, ctx_len=ctx_len)
    return pl.pallas_call(
        kern,
        out_shape=(
            jax.ShapeDtypeStruct((3, per, bsz, n, LANE), F32),
            jax.ShapeDtypeStruct((3, bsz, ctx_len, HY_WIDTH), F32),
        ),
        grid=(bsz, HY_IN // LANE),
        in_specs=[
            pl.BlockSpec((1, s_len, LANE), lambda b, j: (b, 0, j)),
            pl.BlockSpec((3, LANE), lambda b, j: (0, j)),
            pl.BlockSpec((1, LANE), lambda b, j: (0, j)),
        ],
        out_specs=(
            pl.BlockSpec((1, 1, 1, n, LANE), lambda b, j: (j // per, j % per, b, 0, 0)),
            pl.BlockSpec((1, 1, ctx_len, LANE), lambda b, j: (j // per, b, 0, j % per)),
        ),
        compiler_params=_cparams(("parallel", "parallel")),
        name="hyena_short_conv",
    )(hy, w, b.reshape(1, HY_IN))


def _filter_kernel(z_ref, w1_ref, b1_ref, w2_ref, b2_ref, w3_ref, b3_ref, fr_ref, ad_ref, h_ref, sum_ref):
    i = pl.program_id(0)
    z = z_ref[...]
    fr = fr_ref[...]
    h = jnp.sin(fr * (jnp.dot(z, w1_ref[...], preferred_element_type=F32, precision=HIGHEST) + b1_ref[...]))
    h = jnp.sin(fr * (jnp.dot(h, w2_ref[...], preferred_element_type=F32, precision=HIGHEST) + b2_ref[...]))
    h = jnp.dot(h, w3_ref[...], preferred_element_type=F32, precision=HIGHEST) + b3_ref[...]
    h = h * jnp.exp(-z[:, 0:1] * ad_ref[...])
    for q in range(HY_FILTER_CH // HY_WIDTH):
        for hh in range(HY_WIDTH // LANE):
            c0 = q * HY_WIDTH + hh * LANE
            h_ref[hh, q] = h[:, c0:c0 + LANE]

    @pl.when(i == 0)
    def _():
        sum_ref[...] = jnp.zeros_like(sum_ref)

    sum_ref[...] += jnp.sum(jnp.abs(h), axis=0, keepdims=True)


def _hyena_filters(zfeat, w1, b1, w2, b2, w3, b3, freq, absd):
    n, kz = zfeat.shape
    tn = min(n, 1024)
    full = lambda a: pl.BlockSpec(a.shape, lambda i: (0, 0))
    return pl.pallas_call(
        _filter_kernel,
        out_shape=(
            jax.ShapeDtypeStruct((HY_WIDTH // LANE, HY_FILTER_CH // HY_WIDTH, n, LANE), F32),
            jax.ShapeDtypeStruct((1, HY_FILTER_CH), F32),
        ),
        grid=(n // tn,),
        in_specs=[pl.BlockSpec((tn, kz), lambda i: (i, 0)), full(w1), full(b1), full(w2), full(b2),
                  full(w3), full(b3), full(freq), full(absd)],
        out_specs=(
            pl.BlockSpec((HY_WIDTH // LANE, HY_FILTER_CH // HY_WIDTH, tn, LANE), lambda i: (0, 0, i, 0)),
            pl.BlockSpec((1, HY_FILTER_CH), lambda i: (0, 0)),
        ),
        compiler_params=_cparams(("arbitrary",)),
        name="hyena_filters",
    )(zfeat, w1, b1, w2, b2, w3, b3, freq, absd)


T2_BLOCK = 8


def _outer_fwd_kernel(mh_ref, ml_ref, x_ref, o_ref, *, n1, passes):
    rows = x_ref.shape[3]
    xs = [x_ref.at[0, h, 0].reshape(rows * T2_BLOCK, LANE) for h in range(2)]
    outs = [[o_ref.at[0, ri, h].reshape(n1 * T2_BLOCK, LANE) for h in range(2)] for ri in range(2)]
    for t in range(T2_BLOCK):
        sel = pl.ds(t, rows, stride=T2_BLOCK)
        y = _dot_p(mh_ref[...], ml_ref[...], jnp.concatenate([xs[0][sel, :], xs[1][sel, :]], axis=1), passes)
        for ri in range(2):
            for h in range(2):
                outs[ri][h][pl.ds(t, n1, stride=T2_BLOCK), :] = y[ri * n1:(ri + 1) * n1, h * LANE:(h + 1) * LANE]


def _outer_forward(m_hi, m_lo, x6, g, passes):
    _, _, p_n, rows, n2, _ = x6.shape
    n1 = m_hi.shape[0] // 2
    kern = functools.partial(_outer_fwd_kernel, n1=n1, passes=passes)
    return pl.pallas_call(
        kern,
        out_shape=jax.ShapeDtypeStruct((p_n, 2, 2, n1, n2, LANE), F32),
        grid=(p_n, n2 // T2_BLOCK),
        in_specs=[pl.BlockSpec(m_hi.shape, lambda p, j: (0, 0)), pl.BlockSpec(m_lo.shape, lambda p, j: (0, 0)),
                  pl.BlockSpec((1, 2, 1, rows, T2_BLOCK, LANE), lambda p, j: (g, 0, p, 0, j, 0))],
        out_specs=pl.BlockSpec((1, 2, 2, n1, T2_BLOCK, LANE), lambda p, j: (p, 0, 0, 0, j, 0)),
        compiler_params=_cparams(("parallel", "parallel")),
        name="dft_outer_fwd",
    )(m_hi, m_lo, x6)


def _outer_inv_kernel(mh_ref, c_ref, gate_ref, u_ref, bias_ref, o_ref):
    n1 = c_ref.shape[3]
    rows = o_ref.shape[2]
    cs = [[c_ref.at[0, ri, h].reshape(n1 * T2_BLOCK, LANE) for h in range(2)] for ri in range(2)]
    gates = [gate_ref.at[0, h, 0].reshape(rows * T2_BLOCK, LANE) for h in range(2)]
    us = [u_ref.at[0, h, 0].reshape(rows * T2_BLOCK, LANE) for h in range(2)]
    outs = [o_ref.at[h, 0].reshape(rows * T2_BLOCK, LANE) for h in range(2)]
    for t in range(T2_BLOCK):
        sel_f = pl.ds(t, n1, stride=T2_BLOCK)
        sel_t = pl.ds(t, rows, stride=T2_BLOCK)
        c2 = jnp.concatenate([jnp.concatenate([cs[ri][0][sel_f, :], cs[ri][1][sel_f, :]], axis=1)
                              for ri in range(2)], axis=0)
        y = _dot(mh_ref[...], c2.astype(BF16))
        for h in range(2):
            lanes = slice(h * LANE, (h + 1) * LANE)
            outs[h][sel_t, :] = gates[h][sel_t, :] * (y[:, lanes] + us[h][sel_t, :] * bias_ref[:, lanes])


def _outer_inverse(m_hi, c6, gate6, g_gate, u6, g_u, bias):
    p_n, _, _, n1, n2, _ = c6.shape
    rows = m_hi.shape[0]
    gspec = lambda g: pl.BlockSpec((1, 2, 1, rows, T2_BLOCK, LANE), lambda p, j: (g, 0, p, 0, j, 0))
    return pl.pallas_call(
        _outer_inv_kernel,
        out_shape=jax.ShapeDtypeStruct((2, p_n, rows, n2, LANE), F32),
        grid=(p_n, n2 // T2_BLOCK),
        in_specs=[pl.BlockSpec(m_hi.shape, lambda p, j: (0, 0)),
                  pl.BlockSpec((1, 2, 2, n1, T2_BLOCK, LANE), lambda p, j: (p, 0, 0, 0, j, 0)),
                  gspec(g_gate), gspec(g_u), pl.BlockSpec((1, 2 * LANE), lambda p, j: (0, 0))],
        out_specs=pl.BlockSpec((2, 1, rows, T2_BLOCK, LANE), lambda p, j: (0, p, 0, j, 0)),
        compiler_params=_cparams(("parallel", "parallel")),
        name="dft_outer_inv",
    )(m_hi, c6, gate6, u6, bias)


def _halves_to_rows(a_ref, j):
    return jnp.concatenate([jnp.concatenate([a_ref[0, ri, 0, j], a_ref[0, ri, 1, j]], axis=1) for ri in range(2)],
                           axis=0)


def _kf_kernel(gh_ref, gl_ref, af_ref, ab_ref, h0_ref, ss_ref, o_ref, *, fb, n2):
    scale = 1.0 / (ss_ref[0, 0:1, :] + ss_ref[0, 1:2, :] + EPS)
    b0 = jnp.concatenate([h0_ref[0, 0, 0:1, :], h0_ref[1, 0, 0:1, :]], axis=1)
    for j in range(fb):
        xf = _dot3(gh_ref[j], gl_ref[j], _halves_to_rows(af_ref, j))
        xb = _dot3(gh_ref[j], gl_ref[j], _halves_to_rows(ab_ref, j))
        o_ref[0, j, :n2, :] = (xf[:n2] + xb[:n2] - b0) * scale
        o_ref[0, j, n2:, :] = (xf[n2:] - xb[n2:]) * scale


def _filter_spectrum(g_hi, g_lo, a6, h6, ss, *, fb):
    n_q, _, _, n1, n2, _ = a6.shape
    n_ord = n_q // 2
    c = 2 * LANE
    kern = functools.partial(_kf_kernel, fb=fb, n2=n2)
    return pl.pallas_call(
        kern,
        out_shape=jax.ShapeDtypeStruct((n_ord, n1, 2 * n2, c), F32),
        grid=(n1 // fb, n_ord),
        in_specs=[
            pl.BlockSpec((fb, 2 * n2, 2 * n2), lambda f, o: (f, 0, 0)),
            pl.BlockSpec((fb, 2 * n2, 2 * n2), lambda f, o: (f, 0, 0)),
            pl.BlockSpec((1, 2, 2, fb, n2, LANE), lambda f, o: (o, 0, 0, f, 0, 0)),
            pl.BlockSpec((1, 2, 2, fb, n2, LANE), lambda f, o: (n_ord + o, 0, 0, f, 0, 0)),
            pl.BlockSpec((2, 1, 8, LANE), lambda f, o: (0, n_ord + o, 0, 0)),
            pl.BlockSpec((1, 2, c), lambda f, o: (o, 0, 0)),
        ],
        out_specs=pl.BlockSpec((1, fb, 2 * n2, c), lambda f, o: (o, f, 0, 0)),
        compiler_params=_cparams(("parallel", "parallel")),
        name="filter_spectrum",
    )(g_hi, g_lo, a6, a6, h6, ss)


def _spec_kernel(gh_ref, hh_ref, kf_ref, a_ref, o_ref, *, fb, n2):
    for j in range(fb):
        x = _dot(gh_ref[j], _halves_to_rows(a_ref, j).astype(BF16))
        xr, xi = x[:n2], x[n2:]
        kr, ki = kf_ref[0, j, :n2], kf_ref[0, j, n2:]
        y = jnp.concatenate([xr * kr - xi * ki, xr * ki + xi * kr], axis=0)
        c2 = _dot(hh_ref[j], y.astype(BF16))
        for ri in range(2):
            for h in range(2):
                o_ref[0, ri, h, j] = c2[ri * n2:(ri + 1) * n2, h * LANE:(h + 1) * LANE]


def _spectral_multiply(g_hi, h_hi, kf, a6, order, *, fb):
    p_n, _, _, n1, n2, _ = a6.shape
    c = 2 * LANE
    kern = functools.partial(_spec_kernel, fb=fb, n2=n2)
    gspec = pl.BlockSpec((fb, 2 * n2, 2 * n2), lambda f, p: (f, 0, 0))
    aspec = pl.BlockSpec((1, 2, 2, fb, n2, LANE), lambda f, p: (p, 0, 0, f, 0, 0))
    return pl.pallas_call(
        kern,
        out_shape=jax.ShapeDtypeStruct(a6.shape, F32),
        grid=(n1 // fb, p_n),
        in_specs=[gspec, gspec,
                  pl.BlockSpec((1, fb, 2 * n2, c), lambda f, p: (order, f, 0, 0)), aspec],
        out_specs=aspec,
        compiler_params=_cparams(("parallel", "parallel")),
        name="spectral_multiply",
    )(g_hi, h_hi, kf, a6)


def _ctx_kf_kernel(fh_ref, fl_ref, k_ref, ss_ref, o_ref):
    scale = 1.0 / (ss_ref[0, 0:1, :] + ss_ref[0, 1:2, :] + EPS)
    o_ref[0] = _dot3(fh_ref[...], fl_ref[...], k_ref[0]) * scale


def _ctx_filter_spectrum(f_hi, f_lo, k, ss):
    n_ord, nc, c = k.shape
    return pl.pallas_call(
        _ctx_kf_kernel,
        out_shape=jax.ShapeDtypeStruct((n_ord, 2 * nc, c), F32),
        grid=(n_ord,),
        in_specs=[
            pl.BlockSpec(f_hi.shape, lambda o: (0, 0)),
            pl.BlockSpec(f_lo.shape, lambda o: (0, 0)),
            pl.BlockSpec((1, nc, c), lambda o: (o, 0, 0)),
            pl.BlockSpec((1, 2, c), lambda o: (o, 0, 0)),
        ],
        out_specs=pl.BlockSpec((1, 2 * nc, c), lambda o: (o, 0, 0)),
        compiler_params=_cparams(("parallel",)),
        name="ctx_filter_spectrum",
    )(f_hi, f_lo, k, ss)


def _ctx_conv_kernel(fh_ref, fl_ref, eh_ref, el_ref, kf_ref, u_ref, bias_ref, o_ref, *, nc):
    v, x1, x2 = u_ref[0, 0], u_ref[1, 0], u_ref[2, 0]

    def conv(z, order):
        x = _dot3(fh_ref[...], fl_ref[...], z)
        xr, xi = x[:nc], x[nc:]
        kr, ki = kf_ref[order, :nc], kf_ref[order, nc:]
        y = jnp.concatenate([xr * kr - xi * ki, xr * ki + xi * kr], axis=0)
        return _dot3(eh_ref[...], el_ref[...], y) + z * bias_ref[order:order + 1, :]

    o_ref[0] = x2 * conv(x1 * conv(v, 0), 1)


def _ctx_hyena(f_hi, f_lo, e_hi, e_lo, kf, u3, bias):
    _, p_n, n2x, c = u3.shape
    nc = kf.shape[1] // 2
    kern = functools.partial(_ctx_conv_kernel, nc=nc)
    full = lambda a: pl.BlockSpec(a.shape, lambda p: (0,) * a.ndim)
    return pl.pallas_call(
        kern,
        out_shape=jax.ShapeDtypeStruct((p_n, n2x, c), F32),
        grid=(p_n,),
        in_specs=[full(f_hi), full(f_lo), full(e_hi), full(e_lo), full(kf),
                  pl.BlockSpec((3, 1, n2x, c), lambda p: (0, p, 0, 0)), full(bias)],
        out_specs=pl.BlockSpec((1, n2x, c), lambda p: (p, 0, 0)),
        compiler_params=_cparams(("parallel",)),
        name="ctx_hyena",
    )(f_hi, f_lo, e_hi, e_lo, kf, u3, bias)


def _merge_kernel(x_ref, ya_ref, yh_ref, yd_ref, gh_ref, w_ref, mod_ref, o_ref, *, tm, ctx_len):
    i = pl.program_id(1)
    row = i * tm + lax.broadcasted_iota(jnp.int32, (tm, 1), 0)
    yh = yh_ref[0]
    ms = jnp.mean(yh * yh, axis=-1, keepdims=True)
    yh = yh * lax.rsqrt(ms + EPS) * gh_ref[...]
    y = _dot(ya_ref[0].astype(BF16), w_ref[:A_WIDTH, :])
    y += _dot(yh.astype(BF16), w_ref[A_WIDTH:A_WIDTH + HY_WIDTH, :])
    y += _dot(yd_ref[0].astype(BF16), w_ref[A_WIDTH + HY_WIDTH:, :])
    gate = jnp.where(row < ctx_len, mod_ref[0, 2:3, :], mod_ref[0, 10:11, :])
    o_ref[0] = x_ref[0] + gate * y


def _merge(xs, ya, yh, yd, g_h, w_out, mod, *, ctx_len, tm):
    bsz, s_len, d = xs.shape
    kern = functools.partial(_merge_kernel, tm=tm, ctx_len=ctx_len)
    tok = lambda w: pl.BlockSpec((1, tm, w), lambda b, i: (b, i, 0))
    return pl.pallas_call(
        kern,
        out_shape=jax.ShapeDtypeStruct(xs.shape, F32),
        grid=(bsz, s_len // tm),
        in_specs=[tok(d), tok(A_WIDTH), tok(HY_WIDTH), tok(D_WIDTH),
                  pl.BlockSpec((1, HY_WIDTH), lambda b, i: (0, 0)),
                  pl.BlockSpec((MIX_WIDTH, d), lambda b, i: (0, 0)),
                  pl.BlockSpec((1, 16, d), lambda b, i: (b, 0, 0))],
        out_specs=tok(d),
        compiler_params=_cparams(("parallel", "parallel")),
        name="merge",
    )(xs, ya, yh, yd, g_h, w_out, mod)


HALO = 8
FF_TILE = 256
FF_CHUNKS = 2


def _ffn_kernel(x_ref, xp_ref, xn_ref, mod_ref, g_ref, wa_ref, wv_ref, cw_ref, cb_ref, wd_ref, o_ref,
                *, tm, chunks, ctx_len, s_len):
    i = pl.program_id(1)
    x = x_ref[0]
    xa = jnp.concatenate([xp_ref[0], x, xn_ref[0]], axis=0)
    rows = tm + 2 * HALO
    row = i * tm - HALO + lax.broadcasted_iota(jnp.int32, (rows, 1), 0)
    h = _norm_modulate(xa, g_ref[...], mod_ref, row < ctx_len, 1).astype(BF16)
    hm = h[HALO:HALO + tm]
    rowm = row[HALO:HALO + tm]
    has_prev = jnp.logical_and(rowm != 0, rowm != ctx_len)
    has_next = jnp.logical_and(rowm != ctx_len - 1, rowm != s_len - 1)
    def up(j):
        cs = slice(chunks[j], chunks[j + 1])
        return _dot(h, wa_ref[:, cs]), _dot(hm, wv_ref[:, cs])

    n_chunks = len(chunks) - 1
    acc = jnp.zeros((tm, x.shape[1]), F32)
    nxt = up(0)
    for j in range(n_chunks):
        cs = slice(chunks[j], chunks[j + 1])
        a, v = nxt
        if j + 1 < n_chunks:
            nxt = up(j + 1)
        a_prev = jnp.where(has_prev, pltpu.roll(a, 1, axis=0)[HALO:HALO + tm], 0.0)
        a_next = jnp.where(has_next, pltpu.roll(a, rows - 1, axis=0)[HALO:HALO + tm], 0.0)
        ac = a_prev * cw_ref[0:1, cs] + a[HALO:HALO + tm] * cw_ref[1:2, cs] + a_next * cw_ref[2:3, cs] + cb_ref[:, cs]
        gl = 0.5 * ac * (1.0 + jnp.tanh(math.sqrt(2.0 / math.pi) * (ac + 0.044715 * (ac * ac * ac))))
        acc += _dot((gl * v).astype(BF16), wd_ref[cs, :])
    gate = jnp.where(rowm < ctx_len, mod_ref[0, 5:6, :], mod_ref[0, 13:14, :])
    o_ref[0] = x + gate * acc


def _ffn(xs, mod, g, wa, wv, cw, cb, wd, *, ctx_len, tm, chunks):
    bsz, s_len, d = xs.shape
    d_ff = wa.shape[1]
    nh = tm // HALO
    last = s_len // HALO - 1
    kern = functools.partial(_ffn_kernel, tm=tm, chunks=chunks, ctx_len=ctx_len, s_len=s_len)
    const = lambda shape: pl.BlockSpec(shape, lambda b, i: (0, 0), pipeline_mode=pl.Buffered(1))
    return pl.pallas_call(
        kern,
        out_shape=jax.ShapeDtypeStruct(xs.shape, F32),
        grid=(bsz, s_len // tm),
        in_specs=[
            pl.BlockSpec((1, tm, d), lambda b, i: (b, i, 0)),
            pl.BlockSpec((1, HALO, d), lambda b, i: (b, jnp.maximum(i * nh - 1, 0), 0)),
            pl.BlockSpec((1, HALO, d), lambda b, i: (b, jnp.minimum((i + 1) * nh, last), 0)),
            pl.BlockSpec((1, 16, d), lambda b, i: (b, 0, 0)),
            pl.BlockSpec((1, d), lambda b, i: (0, 0)),
            const((d, d_ff)), const((d, d_ff)), const((3, d_ff)), const((1, d_ff)), const((d_ff, d)),
        ],
        out_specs=pl.BlockSpec((1, tm, d), lambda b, i: (b, i, 0)),
        compiler_params=_cparams(("parallel", "parallel")),
        name="conv_glu",
    )(xs, xs, xs, mod, g, wa, wv, cw, cb, wd)


def _cis(num, den):
    ang = num.astype(F32) * (2.0 * math.pi / den)
    return jnp.cos(ang), jnp.sin(ang)


def _blockmat(re, im):
    return jnp.concatenate([jnp.concatenate([re, -im], axis=-1), jnp.concatenate([im, re], axis=-1)], axis=-2)


def _dft_tables(n):
    n2 = DFT_N2
    t1n = n // n2
    n1 = 2 * t1n
    big = n1 * n2
    f1 = jnp.arange(n1, dtype=jnp.int32)
    t1 = jnp.arange(t1n, dtype=jnp.int32)
    j2 = jnp.arange(n2, dtype=jnp.int32)
    c, s = _cis((f1[:, None] * t1[None, :]) % n1, n1)
    fwd_sig = _blockmat(c, -s)
    c, s = _cis((f1[:, None] * f1[None, :]) % n1, n1)
    fwd_fil = jnp.concatenate([c, -s], axis=0)[:, :t1n]
    ph = (n1 * j2[None, :, None] * j2[None, None, :] + f1[:, None, None] * j2[None, None, :]) % big
    c, s = _cis(ph, big)
    g = _blockmat(c, -s)
    h = jnp.swapaxes(g, 1, 2)
    c, s = _cis((t1[:, None] * f1[None, :]) % n1, n1)
    inv = _blockmat(c, s) * (1.0 / big)
    return tuple(_split_bf16(m) for m in (fwd_sig, fwd_fil, g, h, inv))


def _dense_dft_tables(n):
    big = 2 * n
    f = jnp.arange(big, dtype=jnp.int32)
    t = jnp.arange(n, dtype=jnp.int32)
    c, s = _cis((f[:, None] * t[None, :]) % big, big)
    fwd_sig = _blockmat(c, -s)
    c, s = _cis((f[:, None] * f[None, :]) % big, big)
    fwd_fil = jnp.concatenate([c, -s], axis=0)
    c, s = _cis((t[:, None] * f[None, :]) % big, big)
    inv = _blockmat(c, s) * (1.0 / big)
    return tuple(_split_bf16(m) for m in (fwd_sig, fwd_fil, inv))


def _filter_features(n):
    pos = jnp.arange(n, dtype=F32)
    t = (pos / max(n - 1, 1))[:, None]
    w = 2.0 * math.pi * pos[:, None] / n
    bands = jnp.linspace(1e-4, HY_BANDS - 1, HY_BANDS, dtype=F32)
    z = jnp.concatenate([t, jnp.cos(bands * w), -jnp.sin(bands * w)], axis=-1)
    return jnp.pad(z, ((0, 0), (0, LANE - HY_EMB_DIM)))


def _circular_filters(h, big):
    n = h.shape[0]
    fwd = jnp.moveaxis(h[:, 0], 1, 0)
    bwd = jnp.moveaxis(h[:, 1], 1, 0)
    zeros = jnp.zeros((fwd.shape[0], big - 2 * n + 1, fwd.shape[2]), F32)
    return jnp.concatenate([fwd, zeros, bwd[:, :0:-1]], axis=1)


def _pick(total, candidates):
    for t in candidates:
        if total % t == 0:
            return t
    raise ValueError(f"no tile for {total}")


def kernel(x, c, ctx, c_ctx, norm1_g, norm2_g, w_mod, b_mod, w_in, qn_a, kn_a, qn_d, kn_d, lam_q1, lam_k1, lam_q2, lam_k2, hy_conv_w, hy_conv_b, hy_fw1, hy_fb1, hy_fw2, hy_fb2, hy_fw3, hy_fb3, hy_freq, hy_bias, g_out, w_out, w_up, ffn_conv_w, ffn_conv_b, w_down):
    bsz, n_tok, d = x.shape
    ctx_len = ctx.shape[1]
    depth = w_in.shape[0]
    s_len = ctx_len + n_tok
    d_ff = w_down.shape[1]
    assert bsz % 2 == 0 and n_tok % (DFT_N2 * 8) == 0 and ctx_len % LANE == 0

    tm = _pick(s_len, (768, 512, 256, 128))
    tq = _pick(ctx_len, (256, 128))
    tk = _pick(s_len, (2816, 768, 256))
    n_ff_tiles = -(-d_ff // FF_TILE)
    bounds = [min(d_ff, FF_TILE * ((n_ff_tiles * j + FF_CHUNKS - 1) // FF_CHUNKS)) for j in range(FF_CHUNKS + 1)]
    ff_chunks = tuple(sorted(set(bounds)))

    n_rows = n_tok // GRID_W
    rows = jnp.repeat(jnp.arange(n_rows, dtype=F32), GRID_W)
    cols = jnp.tile(jnp.arange(GRID_W, dtype=F32), n_rows)
    n_freq = HEAD_DIM // 4
    inv = ROPE_THETA ** (-jnp.arange(n_freq, dtype=F32) / n_freq)
    ang = jnp.concatenate([rows[:, None] * inv, cols[:, None] * inv], axis=-1)
    ang = jnp.concatenate([jnp.zeros((ctx_len, HALF), F32), ang], axis=0)
    cosT, sinT = jnp.cos(ang).T, jnp.sin(ang).T

    sig_f, fil_f, g_tab, h_tab, inv_f = _dft_tables(n_tok)
    csig_f, cfil_f, cinv_f = _dense_dft_tables(ctx_len)
    z_lat = _filter_features(n_tok)
    z_ctx = _filter_features(ctx_len)
    max_decay = math.log(HY_TARGET) / HY_FAST_DECAY
    min_decay = math.log(HY_TARGET) / HY_SLOW_DECAY
    absd = jnp.tile(jnp.abs(jnp.linspace(min_decay, max_decay, HY_WIDTH, dtype=F32)), 4)[None, :]
    t1n = n_tok // DFT_N2
    n1 = 2 * t1n
    big = n1 * DFT_N2
    fb = 8

    perm = np.concatenate([np.arange(0, HEAD_DIM, 2), np.arange(1, HEAD_DIM, 2)])
    qk_cols = np.concatenate(
        [OFF_QA + b * HEAD_DIM + perm for b in range(A_HEADS)]
        + [OFF_QD + b * HEAD_DIM + perm for b in range(2 * D_HEADS)]
        + [OFF_KA + b * HEAD_DIM + perm for b in range(A_KV_HEADS)]
        + [OFF_KD + b * HEAD_DIM + perm for b in range(2 * D_HEADS)])
    v_cols = np.concatenate([np.arange(OFF_VA, OFF_VA + A_KV_WIDTH), np.arange(OFF_VD, OFF_VD + D_WIDTH)])
    qkv_cols = np.concatenate([qk_cols, v_cols])
    wqT = jnp.swapaxes(w_in[:, :, qkv_cols], 1, 2).astype(BF16)
    why = w_in[:, :, OFF_HY:OFF_HY + HY_IN].astype(BF16)
    q_scale = HEAD_DIM ** -0.5 * math.log2(math.e)
    gains = jnp.concatenate(
        [jnp.tile(qn_a[:, perm], (1, A_HEADS)) * q_scale, jnp.tile(qn_d[:, perm], (1, 2 * D_HEADS)) * q_scale,
         jnp.tile(kn_a[:, perm], (1, A_KV_HEADS)), jnp.tile(kn_d[:, perm], (1, 2 * D_HEADS))],
        axis=1)[:, :, None].astype(F32)
    w_out_b = w_out.astype(BF16)
    w_up_a = w_up[:, :, :d_ff].astype(BF16)
    w_up_v = w_up[:, :, d_ff:].astype(BF16)
    w_down_b = w_down.astype(BF16)
    fw1p = jnp.pad(hy_fw1, ((0, 0), (0, LANE - HY_EMB_DIM), (0, 0)))
    lamv = jnp.stack([lam_q1, lam_k1, lam_q2, lam_k2], axis=1)
    lam_init = jnp.asarray([0.8 - 0.6 * math.exp(-0.3 * i) for i in range(depth)], F32)
    lami = jnp.broadcast_to(lam_init[:, None, None], (depth, 8, LANE))

    n_c = 8 * ((bsz + 1 + 7) // 8)
    cvec = jnp.concatenate([c, c_ctx[None, :], jnp.zeros((n_c - bsz - 1, d), F32)], axis=0)
    mods = _modulation(cvec, w_mod, b_mod).reshape(depth, n_c, 6, d)
    pad2 = jnp.zeros((depth, bsz, 2, d), F32)
    mod_all = jnp.concatenate(
        [jnp.broadcast_to(mods[:, bsz:bsz + 1], (depth, bsz, 6, d)), pad2, mods[:, :bsz], pad2], axis=2)

    xs = jnp.concatenate([ctx, x], axis=1)
    for i in range(depth):
        mod = mod_all[i]
        qT, k, vTa, vTd, hy = _projection(xs, mod, norm1_g[i][None, :], wqT[i], why[i], gains[i], cosT, sinT,
                                    ctx_len=ctx_len, tm=tm)
        ya = _attention_a(qT, k, vTa, g_out[i, :A_WIDTH, None], ctx_len=ctx_len, tq=tq, tk=tk)
        yd = _attention_d(qT, k, vTd, g_out[i, A_WIDTH + HY_WIDTH:].reshape(D_HEADS, D_VALUE_DIM, 1),
                          lamv[i], lami[i], ctx_len=ctx_len, tq=tq, tk=tk)

        u_lat, u_ctx = _hyena_short_conv(hy, hy_conv_w[i], hy_conv_b[i], ctx_len=ctx_len)
        fil_args = (fw1p[i], hy_fb1[i][None], hy_fw2[i], hy_fb2[i][None], hy_fw3[i], hy_fb3[i][None],
                    hy_freq[i][None], absd)
        h_lat, sum_lat = _hyena_filters(z_lat, *fil_args)
        h_ctx, sum_ctx = _hyena_filters(z_ctx, *fil_args)
        ss_lat = jnp.swapaxes(sum_lat.reshape(2, 2, HY_WIDTH), 0, 1)
        ss_ctx = jnp.swapaxes(sum_ctx.reshape(2, 2, HY_WIDTH), 0, 1)
        h_ctx = jnp.transpose(h_ctx, (2, 1, 0, 3)).reshape(ctx_len, 2, 2, HY_WIDTH)
        k_ctx = _circular_filters(h_ctx, 2 * ctx_len)

        ak = _outer_forward(fil_f[0], fil_f[1], h_lat.reshape(1, 2, 4, t1n, DFT_N2, LANE), 0, 3)
        kf = _filter_spectrum(g_tab[0], g_tab[1], ak, h_lat, ss_lat, fb=fb)
        p_n = bsz // 2
        u6 = u_lat.reshape(3, 2, p_n, 2 * t1n, DFT_N2, LANE)

        def long_conv(z6, g_z, g_gate, order):
            a = _outer_forward(sig_f[0], sig_f[1], z6, g_z, 1)
            cc = _spectral_multiply(g_tab[0], h_tab[0], kf, a, order, fb=fb)
            return _outer_inverse(inv_f[0], cc, u6, g_gate, z6, g_z, hy_bias[i, order][None, :])

        z1 = long_conv(u6, 0, 1, 0)
        yh2 = long_conv(z1[None], 0, 2, 1).reshape(2, bsz, n_tok, LANE)
        yh_lat = jnp.concatenate([yh2[0], yh2[1]], axis=-1)

        kf_c = _ctx_filter_spectrum(cfil_f[0], cfil_f[1], k_ctx, ss_ctx)
        yh_ctx = _ctx_hyena(csig_f[0], csig_f[1], cinv_f[0], cinv_f[1], kf_c,
                            u_ctx.reshape(3, p_n, 2 * ctx_len, HY_WIDTH), hy_bias[i])
        yh = jnp.concatenate([yh_ctx.reshape(bsz, ctx_len, HY_WIDTH), yh_lat], axis=1)

        xs = _merge(xs, ya, yh, yd, g_out[i, A_WIDTH:A_WIDTH + HY_WIDTH][None, :], w_out_b[i], mod,
                    ctx_len=ctx_len, tm=tm)
        xs = _ffn(xs, mod, norm2_g[i][None, :], w_up_a[i], w_up_v[i], ffn_conv_w[i], ffn_conv_b[i][None, :],
                  w_down_b[i], ctx_len=ctx_len, tm=tm, chunks=ff_chunks)
    return xs[:, ctx_len:]
```

```python
import functools
import math

import numpy as np
import jax
import jax.numpy as jnp
from jax import lax
from jax.experimental import pallas as pl
from jax.experimental.pallas import tpu as pltpu

F32 = jnp.float32
BF16 = jnp.bfloat16

HEAD_DIM = 64
HALF = HEAD_DIM // 2
A_HEADS = 4
A_KV_HEADS = 2
A_WIDTH = A_HEADS * HEAD_DIM
A_KV_WIDTH = A_KV_HEADS * HEAD_DIM
HY_WIDTH = 256
HY_IN = 3 * HY_WIDTH
HY_BANDS = 16
HY_EMB_DIM = 1 + 2 * HY_BANDS
HY_HIDDEN = 64
HY_FILTER_CH = 4 * HY_WIDTH
HY_FAST_DECAY = 0.3
HY_SLOW_DECAY = 1.5
HY_TARGET = 1e-2
D_HEADS = 4
D_VALUE_DIM = 2 * HEAD_DIM
D_WIDTH = D_HEADS * D_VALUE_DIM
D_QK_WIDTH = D_HEADS * 2 * HEAD_DIM
MIX_WIDTH = A_WIDTH + HY_WIDTH + D_WIDTH
OFF_QA = 0
OFF_KA = A_WIDTH
OFF_VA = OFF_KA + A_KV_WIDTH
OFF_HY = OFF_VA + A_KV_WIDTH
OFF_QD = OFF_HY + HY_IN
OFF_KD = OFF_QD + D_QK_WIDTH
OFF_VD = OFF_KD + D_QK_WIDTH
GRID_W = 64
ROPE_THETA = 10000.0
EPS = 1e-6

N_QBLK = A_HEADS + 2 * D_HEADS
N_KBLK = A_KV_HEADS + 2 * D_HEADS
Q_ROWS = N_QBLK * HEAD_DIM
K_ROWS = N_KBLK * HEAD_DIM
V_ROWS = A_KV_WIDTH + D_WIDTH
QK_ROWS = Q_ROWS + K_ROWS
QKV_ROWS = QK_ROWS + V_ROWS
SUM_ROWS = 16
VA_BLK = HEAD_DIM + SUM_ROWS

LANE = 128
DFT_N2 = 128
KEY_BLOCK = 256
MAX_UNROLLED_CHUNKS = 4
VMEM_LIMIT = 56 * 1024 * 1024
HIGHEST = lax.Precision.HIGHEST


def _cparams(sem):
    return pltpu.CompilerParams(dimension_semantics=sem, vmem_limit_bytes=VMEM_LIMIT)


def _split_bf16(x):
    hi = x.astype(BF16)
    lo = (x - hi.astype(F32)).astype(BF16)
    return hi, lo


def _dot(a, b):
    return jnp.dot(a, b, preferred_element_type=F32)


def _dot3(a_hi, a_lo, b):
    b_hi, b_lo = _split_bf16(b)
    return _dot(a_hi, b_hi) + (_dot(a_hi, b_lo) + _dot(a_lo, b_hi))


def _dot_p(a_hi, a_lo, b, passes):
    return _dot3(a_hi, a_lo, b) if passes == 3 else _dot(a_hi, b.astype(BF16))


def _mod_kernel(c_ref, w_ref, b_ref, o_ref):
    c = c_ref[...]
    s = c * (1.0 / (1.0 + jnp.exp(-c)))
    o_ref[0] = jnp.dot(s, w_ref[0], preferred_element_type=F32, precision=HIGHEST) + b_ref[0]


def _modulation(cvec, w_mod, b_mod):
    depth, d, n6 = w_mod.shape
    tn = 1024
    return pl.pallas_call(
        _mod_kernel,
        out_shape=jax.ShapeDtypeStruct((depth, cvec.shape[0], n6), F32),
        grid=(depth, n6 // tn),
        in_specs=[
            pl.BlockSpec(cvec.shape, lambda l, j: (0, 0)),
            pl.BlockSpec((1, d, tn), lambda l, j: (l, 0, j)),
            pl.BlockSpec((1, 1, tn), lambda l, j: (l, 0, j)),
        ],
        out_specs=pl.BlockSpec((1, cvec.shape[0], tn), lambda l, j: (l, 0, j)),
        compiler_params=_cparams(("arbitrary", "arbitrary")),
        name="modulation",
    )(cvec, w_mod, b_mod.reshape(depth, 1, n6))


def _norm_modulate(x, g, mod_ref, is_ctx, which):
    ms = jnp.mean(x * x, axis=-1, keepdims=True)
    xn = x * lax.rsqrt(ms + EPS) * g
    sh = jnp.where(is_ctx, mod_ref[0, 3 * which:3 * which + 1, :], mod_ref[0, 8 + 3 * which:9 + 3 * which, :])
    sc = jnp.where(is_ctx, mod_ref[0, 3 * which + 1:3 * which + 2, :],
                   mod_ref[0, 9 + 3 * which:10 + 3 * which, :])
    return xn * (1.0 + sc) + sh


def _proj_kernel(x_ref, mod_ref, g_ref, wq_ref, why_ref, gain_ref, cos_ref, sin_ref,
                 qT_ref, k_ref, vTa_ref, vTd_ref, hy_ref, *, tm, ctx_len):
    i = pl.program_id(1)
    row = i * tm + lax.broadcasted_iota(jnp.int32, (tm, 1), 0)
    h = _norm_modulate(x_ref[0], g_ref[...], mod_ref, row < ctx_len, 0).astype(BF16)
    pT = lax.dot_general(wq_ref[...], h, (((1,), (1,)), ((), ())), preferred_element_type=F32)
    hy_ref[0] = _dot(h, why_ref[...])
    c = cos_ref[...]
    s = sin_ref[...]
    blocks = []
    for b in range(N_QBLK + N_KBLK):
        blk = pT[b * HEAD_DIM:(b + 1) * HEAD_DIM]
        ssq = jnp.sum(blk * blk, axis=0, keepdims=True)
        blk = blk * lax.rsqrt(ssq * (1.0 / HEAD_DIM) + EPS) * gain_ref[b * HEAD_DIM:(b + 1) * HEAD_DIM, :]
        x1 = blk[:HALF]
        x2 = blk[HALF:]
        blocks.append(x1 * c - x2 * s)
        blocks.append(x1 * s + x2 * c)
    qT_ref[0] = jnp.concatenate(blocks[:2 * N_QBLK], axis=0).astype(BF16)
    kT = jnp.concatenate(blocks[2 * N_QBLK:], axis=0)
    k_ref[0] = kT.T.astype(BF16)
    ones = jnp.ones((SUM_ROWS, tm), BF16)
    va = [pT[QK_ROWS + g * HEAD_DIM:QK_ROWS + (g + 1) * HEAD_DIM].astype(BF16) for g in range(A_KV_HEADS)]
    vTa_ref[0] = jnp.concatenate([t for v in va for t in (v, ones)], axis=0)
    vTd_ref[0] = pT[QK_ROWS + A_KV_WIDTH:].astype(BF16)


def _projection(xs, mod, g, wqT, why, gains, cosT, sinT, *, ctx_len, tm):
    bsz, s_len, d = xs.shape
    kern = functools.partial(_proj_kernel, tm=tm, ctx_len=ctx_len)
    return pl.pallas_call(
        kern,
        out_shape=(
            jax.ShapeDtypeStruct((bsz, Q_ROWS, s_len), BF16),
            jax.ShapeDtypeStruct((bsz, s_len, K_ROWS), BF16),
            jax.ShapeDtypeStruct((bsz, A_KV_HEADS * VA_BLK, s_len), BF16),
            jax.ShapeDtypeStruct((bsz, D_WIDTH, s_len), BF16),
            jax.ShapeDtypeStruct((bsz, s_len, HY_IN), F32),
        ),
        grid=(bsz, s_len // tm),
        in_specs=[
            pl.BlockSpec((1, tm, d), lambda b, i: (b, i, 0)),
            pl.BlockSpec((1, 16, d), lambda b, i: (b, 0, 0)),
            pl.BlockSpec((1, d), lambda b, i: (0, 0)),
            pl.BlockSpec((QKV_ROWS, d), lambda b, i: (0, 0)),
            pl.BlockSpec((d, HY_IN), lambda b, i: (0, 0)),
            pl.BlockSpec((QK_ROWS, 1), lambda b, i: (0, 0)),
            pl.BlockSpec((HALF, tm), lambda b, i: (0, i)),
            pl.BlockSpec((HALF, tm), lambda b, i: (0, i)),
        ],
        out_specs=(
            pl.BlockSpec((1, Q_ROWS, tm), lambda b, i: (b, 0, i)),
            pl.BlockSpec((1, tm, K_ROWS), lambda b, i: (b, i, 0)),
            pl.BlockSpec((1, A_KV_HEADS * VA_BLK, tm), lambda b, i: (b, 0, i)),
            pl.BlockSpec((1, D_WIDTH, tm), lambda b, i: (b, 0, i)),
            pl.BlockSpec((1, tm, HY_IN), lambda b, i: (b, i, 0)),
        ),
        compiler_params=_cparams(("parallel", "parallel")),
        name="projection",
    )(xs, mod, g, wqT, why, gains, cosT, sinT)


def _attend(units, scr, *, ctx_only, ctx_len, tk, sum_rows):
    n_u = len(units)
    qz, m_run, l_run, acc, s_buf, cm_buf, qz0 = _split_scratch(scr, n_u)
    s_len = units[0][0].shape[1]
    for u in range(n_u):
        m_run[u][...] = jnp.full(m_run[u].shape, -1e30, F32)
        l_run[u][...] = jnp.zeros(l_run[u].shape, F32)
        acc[u][...] = jnp.zeros(acc[u].shape, F32)

    def fold8(s, op):
        return op(s.reshape(s.shape[0] // 8, 8, s.shape[1]), axis=0)

    def key_rows(start, j, rb):
        if isinstance(start, int):
            return pl.ds(start + j * rb, rb)
        return pl.ds(pl.multiple_of(start + j * rb, rb), rb)

    def advance(nxt, cur, size):
        rb = min(size, KEY_BLOCK)
        if cur is not None:
            u_c, start_c, slot_c = cur
            _, vT_ref, v0, dv = units[u_c]
            m = m_run[u_c][...]
            m_new = jnp.maximum(m, jnp.max(cm_buf[slot_c][...], axis=0, keepdims=True))
            m_run[u_c][...] = m_new
            alpha = jnp.exp2(m - m_new)
        cmax, psum, pv = None, None, None
        for j in range(size // rb):
            rows = slice(j * rb, (j + 1) * rb)
            if nxt is not None:
                u_n, start_n, slot_n = nxt[:3]
                q_n = qz[u_n][...] if len(nxt) == 3 else qz0[nxt[3]]
                s = _dot(units[u_n][0][0, key_rows(start_n, j, rb), :], q_n)
                s_buf[slot_n][rows, :] = s
                cmax = fold8(s, jnp.max) if cmax is None else jnp.maximum(cmax, fold8(s, jnp.max))
            if cur is not None:
                p = jnp.exp2(s_buf[slot_c][rows, :] - m_new)
                if not sum_rows:
                    psum = fold8(p, jnp.sum) if psum is None else psum + fold8(p, jnp.sum)
                d = _dot(vT_ref[0, v0:v0 + dv, key_rows(start_c, j, rb)], p.astype(BF16))
                pv = d if pv is None else pv + d
        if nxt is not None:
            cm_buf[slot_n][...] = cmax
        if cur is not None:
            if not sum_rows:
                l_run[u_c][...] = alpha * l_run[u_c][...] + jnp.sum(psum, axis=0, keepdims=True)
            acc[u_c][...] = alpha * acc[u_c][...] + pv

    if ctx_only:
        for u in range(n_u):
            advance((u, 0, u % 2), None, ctx_len)
            advance(None, (u, 0, u % 2), ctx_len)
        advance((0, 0, 0, 1), None, tk)
    else:
        n_chunks = s_len // tk

        def chunk_steps(c, start, nxt, wrap):
            for u in range(n_u):
                task_n = (u + 1, start, (u + 1) % 2) if u + 1 < n_u else (0, nxt, (u + 1) % 2, wrap)
                advance(task_n, (u, start, u % 2), tk)

        if n_chunks <= MAX_UNROLLED_CHUNKS:
            for c in range(n_chunks):
                last = c == n_chunks - 1
                chunk_steps(c, c * tk, 0 if last else (c + 1) * tk, int(last))
        else:
            def body(c, carry):
                wrap = jnp.where(c == n_chunks - 1, 1, 0).astype(jnp.int32)
                chunk_steps(c, pl.multiple_of(c * tk, tk), pl.multiple_of((1 - wrap) * (c + 1) * tk, tk), wrap)
                return carry
            lax.fori_loop(0, n_chunks, body, 0)
    outs = []
    for u in range(n_u):
        dv = units[u][3] - sum_rows
        l = acc[u][dv:dv + 1, :] if sum_rows else l_run[u][...]
        outs.append(acc[u][:dv, :] * (1.0 / l))
    return outs


def _attn_scratch(n_u, rows, tq, tk):
    per_unit = [((LANE, tq), BF16), ((1, tq), F32), ((1, tq), F32), ((rows, tq), F32)]
    per_slot = [((tk, tq), F32), ((8, tq), F32)]
    return ([pltpu.VMEM(s, d) for s, d in per_unit for _ in range(n_u)]
            + [pltpu.VMEM(s, d) for s, d in per_slot for _ in range(2)]
            + [pltpu.VMEM((2, LANE, tq), BF16)])


def _split_scratch(scr, n_u):
    groups = [scr[i * n_u:(i + 1) * n_u] for i in range(4)]
    rest = scr[4 * n_u:]
    return groups + [rest[0:2], rest[2:4], rest[4]]


def _run_ctx_or_latent(qi, n_ctx_tiles, run):
    @pl.when(qi < n_ctx_tiles)
    def _():
        run(True)

    @pl.when(qi >= n_ctx_tiles)
    def _():
        run(False)


def _attn_a_kernel(qT_ref, qn_ref, k_ref, vT_ref, g_ref, o_ref, *scr, ctx_len, tq, tk):
    zeros = jnp.zeros((HEAD_DIM, tq), BF16)
    group = A_HEADS // A_KV_HEADS
    for h in range(A_HEADS):
        q = qT_ref[0, h * HEAD_DIM:(h + 1) * HEAD_DIM, :]
        scr[h][...] = jnp.concatenate([q, zeros] if h // group == 0 else [zeros, q], axis=0)
    scr[-1][0] = jnp.concatenate([qT_ref[0, :HEAD_DIM, :], zeros], axis=0)
    scr[-1][1] = jnp.concatenate([qn_ref[0, :HEAD_DIM, :], zeros], axis=0)
    units = [(k_ref, vT_ref, (h // group) * VA_BLK, VA_BLK) for h in range(A_HEADS)]

    def run(ctx_only):
        outs = _attend(units, scr, ctx_only=ctx_only, ctx_len=ctx_len, tk=tk, sum_rows=SUM_ROWS)
        y = jnp.concatenate(outs, axis=0)
        ms = jnp.mean(y * y, axis=0, keepdims=True)
        y = y * lax.rsqrt(ms + EPS) * g_ref[...]
        o_ref[0] = y.T.astype(BF16)

    _run_ctx_or_latent(pl.program_id(1), ctx_len // tq, run)


def _attention_a(qT, k, vT, g_a, *, ctx_len, tq, tk):
    bsz, _, s_len = qT.shape
    kern = functools.partial(_attn_a_kernel, ctx_len=ctx_len, tq=tq, tk=tk)
    return pl.pallas_call(
        kern,
        out_shape=jax.ShapeDtypeStruct((bsz, s_len, A_WIDTH), BF16),
        grid=(bsz, s_len // tq),
        in_specs=[
            pl.BlockSpec((1, A_WIDTH, tq), lambda b, i: (b, 0, i)),
            pl.BlockSpec((1, A_WIDTH, tq), lambda b, i: (b, 0, jnp.minimum(i + 1, s_len // tq - 1))),
            pl.BlockSpec((1, s_len, LANE), lambda b, i: (b, 0, 0)),
            pl.BlockSpec((1, A_KV_HEADS * VA_BLK, s_len), lambda b, i: (b, 0, 0)),
            pl.BlockSpec((A_WIDTH, 1), lambda b, i: (0, 0)),
        ],
        out_specs=pl.BlockSpec((1, tq, A_WIDTH), lambda b, i: (b, i, 0)),
        scratch_shapes=_attn_scratch(A_HEADS, VA_BLK, tq, tk),
        compiler_params=_cparams(("parallel", "arbitrary")),
        name="attention_gqa",
    )(qT, qT, k, vT, g_a)


D_HEADS_PER_STEP = 2


def _attn_d_kernel(*refs, ctx_len, tq, tk):
    n_h = D_HEADS_PER_STEP
    q_refs, k_refs, v_refs = refs[:n_h], refs[n_h:2 * n_h], refs[2 * n_h:3 * n_h]
    qn_ref, g_ref, lamv_ref, lami_ref, o_ref = refs[3 * n_h:3 * n_h + 5]
    scr = refs[3 * n_h + 5:]
    zeros = jnp.zeros((HEAD_DIM, tq), BF16)
    lv = lamv_ref[...]
    lam_init = lami_ref[0:1, 0:1]
    lam = (jnp.exp(jnp.sum(lv[0:1] * lv[1:2], axis=-1, keepdims=True))
           - jnp.exp(jnp.sum(lv[2:3] * lv[3:4], axis=-1, keepdims=True)) + lam_init)
    units = []
    for h in range(n_h):
        scr[2 * h][...] = jnp.concatenate([q_refs[h][0, :HEAD_DIM, :], zeros], axis=0)
        scr[2 * h + 1][...] = jnp.concatenate([zeros, q_refs[h][0, HEAD_DIM:, :]], axis=0)
        units += [(k_refs[h], v_refs[h], 0, D_VALUE_DIM)] * 2
    scr[-1][0] = jnp.concatenate([q_refs[0][0, :HEAD_DIM, :], zeros], axis=0)
    scr[-1][1] = jnp.concatenate([qn_ref[0, :HEAD_DIM, :], zeros], axis=0)

    def run(ctx_only):
        outs = _attend(units, scr, ctx_only=ctx_only, ctx_len=ctx_len, tk=tk, sum_rows=0)
        ys = []
        for h in range(n_h):
            y = outs[2 * h] - lam * outs[2 * h + 1]
            ms = jnp.mean(y * y, axis=0, keepdims=True)
            ys.append(y * lax.rsqrt(ms + EPS) * g_ref[h] * (1.0 - lam_init))
        o_ref[0] = jnp.concatenate(ys, axis=0).T.astype(BF16)

    _run_ctx_or_latent(pl.program_id(2), ctx_len // tq, run)


def _attention_d(qT, k, vT, g_d, lamv, lami, *, ctx_len, tq, tk):
    bsz, _, s_len = qT.shape
    n_h = D_HEADS_PER_STEP
    kern = functools.partial(_attn_d_kernel, ctx_len=ctx_len, tq=tq, tk=tk)
    q_blk0 = A_WIDTH // LANE
    k_blk0 = A_KV_WIDTH // LANE
    q_specs = [pl.BlockSpec((1, LANE, tq), lambda b, hp, i, j=j: (b, q_blk0 + n_h * hp + j, i)) for j in range(n_h)]
    k_specs = [pl.BlockSpec((1, s_len, LANE), lambda b, hp, i, j=j: (b, 0, k_blk0 + n_h * hp + j)) for j in range(n_h)]
    v_specs = [pl.BlockSpec((1, D_VALUE_DIM, s_len), lambda b, hp, i, j=j: (b, n_h * hp + j, 0)) for j in range(n_h)]
    return pl.pallas_call(
        kern,
        out_shape=jax.ShapeDtypeStruct((bsz, s_len, D_WIDTH), BF16),
        grid=(bsz, D_HEADS // n_h, s_len // tq),
        in_specs=q_specs + k_specs + v_specs + [
            pl.BlockSpec((1, LANE, tq), lambda b, hp, i: (b, q_blk0 + n_h * hp, jnp.minimum(i + 1, s_len // tq - 1))),
            pl.BlockSpec((n_h, D_VALUE_DIM, 1), lambda b, hp, i: (hp, 0, 0)),
            pl.BlockSpec((4, HEAD_DIM), lambda b, hp, i: (0, 0)),
            pl.BlockSpec((8, LANE), lambda b, hp, i: (0, 0)),
        ],
        out_specs=pl.BlockSpec((1, tq, n_h * D_VALUE_DIM), lambda b, hp, i: (b, i, hp)),
        scratch_shapes=_attn_scratch(2 * n_h, D_VALUE_DIM, tq, tk),
        compiler_params=_cparams(("parallel", "parallel", "arbitrary")),
        name="attention_diff",
    )(*([qT] * n_h + [k] * n_h + [vT] * n_h + [qT, g_d, lamv, lami]))


def _dwconv3_seq(u, w_ref, b_ref):
    n = u.shape[0]
    row = lax.broadcasted_iota(jnp.int32, (n, 1), 0)
    prev = jnp.where(row == 0, 0.0, pltpu.roll(u, 1, axis=0))
    nxt = jnp.where(row == n - 1, 0.0, pltpu.roll(u, n - 1, axis=0))
    return prev * w_ref[0:1, :] + u * w_ref[1:2, :] + nxt * w_ref[2:3, :] + b_ref[...]


def _hy_conv_kernel(hy_ref, w_ref, b_ref, lat_ref, ctx_ref, *, ctx_len):
    ctx_ref[0, 0] = _dwconv3_seq(hy_ref[0, :ctx_len, :], w_ref, b_ref)
    lat_ref[0, 0, 0] = _dwconv3_seq(hy_ref[0, ctx_len:, :], w_ref, b_ref)


def _hyena_short_conv(hy, w, b, *, ctx_len):
    bsz, s_len, _ = hy.shape
    n = s_len - ctx_len
    per = HY_WIDTH // LANE
    kern = functools.partial(_hy_conv_kernel, ctx_len=ctx_len)
    return pl.pallas_call(
        kern,
        out_shape=(
            jax.ShapeDtypeStruct((3, per, bsz, n, LANE), F32),
            jax.ShapeDtypeStruct((3, bsz, ctx_len, HY_WIDTH), F32),
        ),
        grid=(bsz, HY_IN // LANE),
        in_specs=[
            pl.BlockSpec((1, s_len, LANE), lambda b, j: (b, 0, j)),
            pl.BlockSpec((3, LANE), lambda b, j: (0, j)),
            pl.BlockSpec((1, LANE), lambda b, j: (0, j)),
        ],
        out_specs=(
            pl.BlockSpec((1, 1, 1, n, LANE), lambda b, j: (j // per, j % per, b, 0, 0)),
            pl.BlockSpec((1, 1, ctx_len, LANE), lambda b, j: (j // per, b, 0, j % per)),
        ),
        compiler_params=_cparams(("parallel", "parallel")),
        name="hyena_short_conv",
    )(hy, w, b.reshape(1, HY_IN))


def _filter_kernel(z_ref, w1_ref, b1_ref, w2_ref, b2_ref, w3_ref, b3_ref, fr_ref, ad_ref, h_ref, sum_ref):
    i = pl.program_id(0)
    z = z_ref[...]
    fr = fr_ref[...]
    h = jnp.sin(fr * (jnp.dot(z, w1_ref[...], preferred_element_type=F32, precision=HIGHEST) + b1_ref[...]))
    h = jnp.sin(fr * (jnp.dot(h, w2_ref[...], preferred_element_type=F32, precision=HIGHEST) + b2_ref[...]))
    h = jnp.dot(h, w3_ref[...], preferred_element_type=F32, precision=HIGHEST) + b3_ref[...]
    h = h * jnp.exp(-z[:, 0:1] * ad_ref[...])
    for q in range(HY_FILTER_CH // HY_WIDTH):
        for hh in range(HY_WIDTH // LANE):
            c0 = q * HY_WIDTH + hh * LANE
            h_ref[hh, q] = h[:, c0:c0 + LANE]

    @pl.when(i == 0)
    def _():
        sum_ref[...] = jnp.zeros_like(sum_ref)

    sum_ref[...] += jnp.sum(jnp.abs(h), axis=0, keepdims=True)


def _hyena_filters(zfeat, w1, b1, w2, b2, w3, b3, freq, absd):
    n, kz = zfeat.shape
    tn = min(n, 1024)
    full = lambda a: pl.BlockSpec(a.shape, lambda i: (0, 0))
    return pl.pallas_call(
        _filter_kernel,
        out_shape=(
            jax.ShapeDtypeStruct((HY_WIDTH // LANE, HY_FILTER_CH // HY_WIDTH, n, LANE), F32),
            jax.ShapeDtypeStruct((1, HY_FILTER_CH), F32),
        ),
        grid=(n // tn,),
        in_specs=[pl.BlockSpec((tn, kz), lambda i: (i, 0)), full(w1), full(b1), full(w2), full(b2),
                  full(w3), full(b3), full(freq), full(absd)],
        out_specs=(
            pl.BlockSpec((HY_WIDTH // LANE, HY_FILTER_CH // HY_WIDTH, tn, LANE), lambda i: (0, 0, i, 0)),
            pl.BlockSpec((1, HY_FILTER_CH), lambda i: (0, 0)),
        ),
        compiler_params=_cparams(("arbitrary",)),
        name="hyena_filters",
    )(zfeat, w1, b1, w2, b2, w3, b3, freq, absd)


T2_BLOCK = 8


def _outer_fwd_kernel(mh_ref, ml_ref, x_ref, o_ref, *, n1, passes):
    rows = x_ref.shape[3]
    xs = [x_ref.at[0, h, 0].reshape(rows * T2_BLOCK, LANE) for h in range(2)]
    outs = [[o_ref.at[0, ri, h].reshape(n1 * T2_BLOCK, LANE) for h in range(2)] for ri in range(2)]
    for t in range(T2_BLOCK):
        sel = pl.ds(t, rows, stride=T2_BLOCK)
        y = _dot_p(mh_ref[...], ml_ref[...], jnp.concatenate([xs[0][sel, :], xs[1][sel, :]], axis=1), passes)
        for ri in range(2):
            for h in range(2):
                outs[ri][h][pl.ds(t, n1, stride=T2_BLOCK), :] = y[ri * n1:(ri + 1) * n1, h * LANE:(h + 1) * LANE]


def _outer_forward(m_hi, m_lo, x6, g, passes):
    _, _, p_n, rows, n2, _ = x6.shape
    n1 = m_hi.shape[0] // 2
    kern = functools.partial(_outer_fwd_kernel, n1=n1, passes=passes)
    return pl.pallas_call(
        kern,
        out_shape=jax.ShapeDtypeStruct((p_n, 2, 2, n1, n2, LANE), F32),
        grid=(p_n, n2 // T2_BLOCK),
        in_specs=[pl.BlockSpec(m_hi.shape, lambda p, j: (0, 0)), pl.BlockSpec(m_lo.shape, lambda p, j: (0, 0)),
                  pl.BlockSpec((1, 2, 1, rows, T2_BLOCK, LANE), lambda p, j: (g, 0, p, 0, j, 0))],
        out_specs=pl.BlockSpec((1, 2, 2, n1, T2_BLOCK, LANE), lambda p, j: (p, 0, 0, 0, j, 0)),
        compiler_params=_cparams(("parallel", "parallel")),
        name="dft_outer_fwd",
    )(m_hi, m_lo, x6)


def _outer_inv_kernel(mh_ref, c_ref, gate_ref, u_ref, bias_ref, o_ref):
    n1 = c_ref.shape[3]
    rows = o_ref.shape[2]
    cs = [[c_ref.at[0, ri, h].reshape(n1 * T2_BLOCK, LANE) for h in range(2)] for ri in range(2)]
    gates = [gate_ref.at[0, h, 0].reshape(rows * T2_BLOCK, LANE) for h in range(2)]
    us = [u_ref.at[0, h, 0].reshape(rows * T2_BLOCK, LANE) for h in range(2)]
    outs = [o_ref.at[h, 0].reshape(rows * T2_BLOCK, LANE) for h in range(2)]
    for t in range(T2_BLOCK):
        sel_f = pl.ds(t, n1, stride=T2_BLOCK)
        sel_t = pl.ds(t, rows, stride=T2_BLOCK)
        c2 = jnp.concatenate([jnp.concatenate([cs[ri][0][sel_f, :], cs[ri][1][sel_f, :]], axis=1)
                              for ri in range(2)], axis=0)
        y = _dot(mh_ref[...], c2.astype(BF16))
        for h in range(2):
            lanes = slice(h * LANE, (h + 1) * LANE)
            outs[h][sel_t, :] = gates[h][sel_t, :] * (y[:, lanes] + us[h][sel_t, :] * bias_ref[:, lanes])


def _outer_inverse(m_hi, c6, gate6, g_gate, u6, g_u, bias):
    p_n, _, _, n1, n2, _ = c6.shape
    rows = m_hi.shape[0]
    gspec = lambda g: pl.BlockSpec((1, 2, 1, rows, T2_BLOCK, LANE), lambda p, j: (g, 0, p, 0, j, 0))
    return pl.pallas_call(
        _outer_inv_kernel,
        out_shape=jax.ShapeDtypeStruct((2, p_n, rows, n2, LANE), F32),
        grid=(p_n, n2 // T2_BLOCK),
        in_specs=[pl.BlockSpec(m_hi.shape, lambda p, j: (0, 0)),
                  pl.BlockSpec((1, 2, 2, n1, T2_BLOCK, LANE), lambda p, j: (p, 0, 0, 0, j, 0)),
                  gspec(g_gate), gspec(g_u), pl.BlockSpec((1, 2 * LANE), lambda p, j: (0, 0))],
        out_specs=pl.BlockSpec((2, 1, rows, T2_BLOCK, LANE), lambda p, j: (0, p, 0, j, 0)),
        compiler_params=_cparams(("parallel", "parallel")),
        name="dft_outer_inv",
    )(m_hi, c6, gate6, u6, bias)


def _halves_to_rows(a_ref, j):
    return jnp.concatenate([jnp.concatenate([a_ref[0, ri, 0, j], a_ref[0, ri, 1, j]], axis=1) for ri in range(2)],
                           axis=0)


def _kf_kernel(gh_ref, gl_ref, af_ref, ab_ref, h0_ref, ss_ref, o_ref, *, fb, n2):
    scale = 1.0 / (ss_ref[0, 0:1, :] + ss_ref[0, 1:2, :] + EPS)
    b0 = jnp.concatenate([h0_ref[0, 0, 0:1, :], h0_ref[1, 0, 0:1, :]], axis=1)
    for j in range(fb):
        xf = _dot3(gh_ref[j], gl_ref[j], _halves_to_rows(af_ref, j))
        xb = _dot3(gh_ref[j], gl_ref[j], _halves_to_rows(ab_ref, j))
        o_ref[0, j, :n2, :] = (xf[:n2] + xb[:n2] - b0) * scale
        o_ref[0, j, n2:, :] = (xf[n2:] - xb[n2:]) * scale


def _filter_spectrum(g_hi, g_lo, a6, h6, ss, *, fb):
    n_q, _, _, n1, n2, _ = a6.shape
    n_ord = n_q // 2
    c = 2 * LANE
    kern = functools.partial(_kf_kernel, fb=fb, n2=n2)
    return pl.pallas_call(
        kern,
        out_shape=jax.ShapeDtypeStruct((n_ord, n1, 2 * n2, c), F32),
        grid=(n1 // fb, n_ord),
        in_specs=[
            pl.BlockSpec((fb, 2 * n2, 2 * n2), lambda f, o: (f, 0, 0)),
            pl.BlockSpec((fb, 2 * n2, 2 * n2), lambda f, o: (f, 0, 0)),
            pl.BlockSpec((1, 2, 2, fb, n2, LANE), lambda f, o: (o, 0, 0, f, 0, 0)),
            pl.BlockSpec((1, 2, 2, fb, n2, LANE), lambda f, o: (n_ord + o, 0, 0, f, 0, 0)),
            pl.BlockSpec((2, 1, 8, LANE), lambda f, o: (0, n_ord + o, 0, 0)),
            pl.BlockSpec((1, 2, c), lambda f, o: (o, 0, 0)),
        ],
        out_specs=pl.BlockSpec((1, fb, 2 * n2, c), lambda f, o: (o, f, 0, 0)),
        compiler_params=_cparams(("parallel", "parallel")),
        name="filter_spectrum",
    )(g_hi, g_lo, a6, a6, h6, ss)


def _spec_kernel(gh_ref, hh_ref, kf_ref, a_ref, o_ref, *, fb, n2):
    for j in range(fb):
        x = _dot(gh_ref[j], _halves_to_rows(a_ref, j).astype(BF16))
        xr, xi = x[:n2], x[n2:]
        kr, ki = kf_ref[0, j, :n2], kf_ref[0, j, n2:]
        y = jnp.concatenate([xr * kr - xi * ki, xr * ki + xi * kr], axis=0)
        c2 = _dot(hh_ref[j], y.astype(BF16))
        for ri in range(2):
            for h in range(2):
                o_ref[0, ri, h, j] = c2[ri * n2:(ri + 1) * n2, h * LANE:(h + 1) * LANE]


def _spectral_multiply(g_hi, h_hi, kf, a6, order, *, fb):
    p_n, _, _, n1, n2, _ = a6.shape
    c = 2 * LANE
    kern = functools.partial(_spec_kernel, fb=fb, n2=n2)
    gspec = pl.BlockSpec((fb, 2 * n2, 2 * n2), lambda f, p: (f, 0, 0))
    aspec = pl.BlockSpec((1, 2, 2, fb, n2, LANE), lambda f, p: (p, 0, 0, f, 0, 0))
    return pl.pallas_call(
        kern,
        out_shape=jax.ShapeDtypeStruct(a6.shape, F32),
        grid=(n1 // fb, p_n),
        in_specs=[gspec, gspec,
                  pl.BlockSpec((1, fb, 2 * n2, c), lambda f, p: (order, f, 0, 0)), aspec],
        out_specs=aspec,
        compiler_params=_cparams(("parallel", "parallel")),
        name="spectral_multiply",
    )(g_hi, h_hi, kf, a6)


def _ctx_kf_kernel(fh_ref, fl_ref, k_ref, ss_ref, o_ref):
    scale = 1.0 / (ss_ref[0, 0:1, :] + ss_ref[0, 1:2, :] + EPS)
    o_ref[0] = _dot3(fh_ref[...], fl_ref[...], k_ref[0]) * scale


def _ctx_filter_spectrum(f_hi, f_lo, k, ss):
    n_ord, nc, c = k.shape
    return pl.pallas_call(
        _ctx_kf_kernel,
        out_shape=jax.ShapeDtypeStruct((n_ord, 2 * nc, c), F32),
        grid=(n_ord,),
        in_specs=[
            pl.BlockSpec(f_hi.shape, lambda o: (0, 0)),
            pl.BlockSpec(f_lo.shape, lambda o: (0, 0)),
            pl.BlockSpec((1, nc, c), lambda o: (o, 0, 0)),
            pl.BlockSpec((1, 2, c), lambda o: (o, 0, 0)),
        ],
        out_specs=pl.BlockSpec((1, 2 * nc, c), lambda o: (o, 0, 0)),
        compiler_params=_cparams(("parallel",)),
        name="ctx_filter_spectrum",
    )(f_hi, f_lo, k, ss)


def _ctx_conv_kernel(fh_ref, fl_ref, eh_ref, el_ref, kf_ref, u_ref, bias_ref, o_ref, *, nc):
    v, x1, x2 = u_ref[0, 0], u_ref[1, 0], u_ref[2, 0]

    def conv(z, order):
        x = _dot3(fh_ref[...], fl_ref[...], z)
        xr, xi = x[:nc], x[nc:]
        kr, ki = kf_ref[order, :nc], kf_ref[order, nc:]
        y = jnp.concatenate([xr * kr - xi * ki, xr * ki + xi * kr], axis=0)
        return _dot3(eh_ref[...], el_ref[...], y) + z * bias_ref[order:order + 1, :]

    o_ref[0] = x2 * conv(x1 * conv(v, 0), 1)


def _ctx_hyena(f_hi, f_lo, e_hi, e_lo, kf, u3, bias):
    _, p_n, n2x, c = u3.shape
    nc = kf.shape[1] // 2
    kern = functools.partial(_ctx_conv_kernel, nc=nc)
    full = lambda a: pl.BlockSpec(a.shape, lambda p: (0,) * a.ndim)
    return pl.pallas_call(
        kern,
        out_shape=jax.ShapeDtypeStruct((p_n, n2x, c), F32),
        grid=(p_n,),
        in_specs=[full(f_hi), full(f_lo), full(e_hi), full(e_lo), full(kf),
                  pl.BlockSpec((3, 1, n2x, c), lambda p: (0, p, 0, 0)), full(bias)],
        out_specs=pl.BlockSpec((1, n2x, c), lambda p: (p, 0, 0)),
        compiler_params=_cparams(("parallel",)),
        name="ctx_hyena",
    )(f_hi, f_lo, e_hi, e_lo, kf, u3, bias)


def _merge_kernel(x_ref, ya_ref, yh_ref, yd_ref, gh_ref, w_ref, mod_ref, o_ref, *, tm, ctx_len):
    i = pl.program_id(1)
    row = i * tm + lax.broadcasted_iota(jnp.int32, (tm, 1), 0)
    yh = yh_ref[0]
    ms = jnp.mean(yh * yh, axis=-1, keepdims=True)
    yh = yh * lax.rsqrt(ms + EPS) * gh_ref[...]
    y = _dot(ya_ref[0].astype(BF16), w_ref[:A_WIDTH, :])
    y += _dot(yh.astype(BF16), w_ref[A_WIDTH:A_WIDTH + HY_WIDTH, :])
    y += _dot(yd_ref[0].astype(BF16), w_ref[A_WIDTH + HY_WIDTH:, :])
    gate = jnp.where(row < ctx_len, mod_ref[0, 2:3, :], mod_ref[0, 10:11, :])
    o_ref[0] = x_ref[0] + gate * y


def _merge(xs, ya, yh, yd, g_h, w_out, mod, *, ctx_len, tm):
    bsz, s_len, d = xs.shape
    kern = functools.partial(_merge_kernel, tm=tm, ctx_len=ctx_len)
    tok = lambda w: pl.BlockSpec((1, tm, w), lambda b, i: (b, i, 0))
    return pl.pallas_call(
        kern,
        out_shape=jax.ShapeDtypeStruct(xs.shape, F32),
        grid=(bsz, s_len // tm),
        in_specs=[tok(d), tok(A_WIDTH), tok(HY_WIDTH), tok(D_WIDTH),
                  pl.BlockSpec((1, HY_WIDTH), lambda b, i: (0, 0)),
                  pl.BlockSpec((MIX_WIDTH, d), lambda b, i: (0, 0)),
                  pl.BlockSpec((1, 16, d), lambda b, i: (b, 0, 0))],
        out_specs=tok(d),
        compiler_params=_cparams(("parallel", "parallel")),
        name="merge",
    )(xs, ya, yh, yd, g_h, w_out, mod)


HALO = 8
FF_TILE = 256
FF_CHUNKS = 2


def _ffn_kernel(x_ref, xp_ref, xn_ref, mod_ref, g_ref, wa_ref, wv_ref, cw_ref, cb_ref, wd_ref, o_ref,
                *, tm, chunks, ctx_len, s_len):
    i = pl.program_id(1)
    x = x_ref[0]
    xa = jnp.concatenate([xp_ref[0], x, xn_ref[0]], axis=0)
    rows = tm + 2 * HALO
    row = i * tm - HALO + lax.broadcasted_iota(jnp.int32, (rows, 1), 0)
    h = _norm_modulate(xa, g_ref[...], mod_ref, row < ctx_len, 1).astype(BF16)
    hm = h[HALO:HALO + tm]
    rowm = row[HALO:HALO + tm]
    has_prev = jnp.logical_and(rowm != 0, rowm != ctx_len)
    has_next = jnp.logical_and(rowm != ctx_len - 1, rowm != s_len - 1)
    def up(j):
        cs = slice(chunks[j], chunks[j + 1])
        return _dot(h, wa_ref[:, cs]), _dot(hm, wv_ref[:, cs])

    n_chunks = len(chunks) - 1
    acc = jnp.zeros((tm, x.shape[1]), F32)
    nxt = up(0)
    for j in range(n_chunks):
        cs = slice(chunks[j], chunks[j + 1])
        a, v = nxt
        if j + 1 < n_chunks:
            nxt = up(j + 1)
        a_prev = jnp.where(has_prev, pltpu.roll(a, 1, axis=0)[HALO:HALO + tm], 0.0)
        a_next = jnp.where(has_next, pltpu.roll(a, rows - 1, axis=0)[HALO:HALO + tm], 0.0)
        ac = a_prev * cw_ref[0:1, cs] + a[HALO:HALO + tm] * cw_ref[1:2, cs] + a_next * cw_ref[2:3, cs] + cb_ref[:, cs]
        gl = 0.5 * ac * (1.0 + jnp.tanh(math.sqrt(2.0 / math.pi) * (ac + 0.044715 * (ac * ac * ac))))
        acc += _dot((gl * v).astype(BF16), wd_ref[cs, :])
    gate = jnp.where(rowm < ctx_len, mod_ref[0, 5:6, :], mod_ref[0, 13:14, :])
    o_ref[0] = x + gate * acc


def _ffn(xs, mod, g, wa, wv, cw, cb, wd, *, ctx_len, tm, chunks):
    bsz, s_len, d = xs.shape
    d_ff = wa.shape[1]
    nh = tm // HALO
    last = s_len // HALO - 1
    kern = functools.partial(_ffn_kernel, tm=tm, chunks=chunks, ctx_len=ctx_len, s_len=s_len)
    const = lambda shape: pl.BlockSpec(shape, lambda b, i: (0, 0), pipeline_mode=pl.Buffered(1))
    return pl.pallas_call(
        kern,
        out_shape=jax.ShapeDtypeStruct(xs.shape, F32),
        grid=(bsz, s_len // tm),
        in_specs=[
            pl.BlockSpec((1, tm, d), lambda b, i: (b, i, 0)),
            pl.BlockSpec((1, HALO, d), lambda b, i: (b, jnp.maximum(i * nh - 1, 0), 0)),
            pl.BlockSpec((1, HALO, d), lambda b, i: (b, jnp.minimum((i + 1) * nh, last), 0)),
            pl.BlockSpec((1, 16, d), lambda b, i: (b, 0, 0)),
            pl.BlockSpec((1, d), lambda b, i: (0, 0)),
            const((d, d_ff)), const((d, d_ff)), const((3, d_ff)), const((1, d_ff)), const((d_ff, d)),
        ],
        out_specs=pl.BlockSpec((1, tm, d), lambda b, i: (b, i, 0)),
        compiler_params=_cparams(("parallel", "parallel")),
        name="conv_glu",
    )(xs, xs, xs, mod, g, wa, wv, cw, cb, wd)


def _cis(num, den):
    ang = num.astype(F32) * (2.0 * math.pi / den)
    return jnp.cos(ang), jnp.sin(ang)


def _blockmat(re, im):
    return jnp.concatenate([jnp.concatenate([re, -im], axis=-1), jnp.concatenate([im, re], axis=-1)], axis=-2)


def _dft_tables(n):
    n2 = DFT_N2
    t1n = n // n2
    n1 = 2 * t1n
    big = n1 * n2
    f1 = jnp.arange(n1, dtype=jnp.int32)
    t1 = jnp.arange(t1n, dtype=jnp.int32)
    j2 = jnp.arange(n2, dtype=jnp.int32)
    c, s = _cis((f1[:, None] * t1[None, :]) % n1, n1)
    fwd_sig = _blockmat(c, -s)
    c, s = _cis((f1[:, None] * f1[None, :]) % n1, n1)
    fwd_fil = jnp.concatenate([c, -s], axis=0)[:, :t1n]
    ph = (n1 * j2[None, :, None] * j2[None, None, :] + f1[:, None, None] * j2[None, None, :]) % big
    c, s = _cis(ph, big)
    g = _blockmat(c, -s)
    h = jnp.swapaxes(g, 1, 2)
    c, s = _cis((t1[:, None] * f1[None, :]) % n1, n1)
    inv = _blockmat(c, s) * (1.0 / big)
    return tuple(_split_bf16(m) for m in (fwd_sig, fwd_fil, g, h, inv))


def _dense_dft_tables(n):
    big = 2 * n
    f = jnp.arange(big, dtype=jnp.int32)
    t = jnp.arange(n, dtype=jnp.int32)
    c, s = _cis((f[:, None] * t[None, :]) % big, big)
    fwd_sig = _blockmat(c, -s)
    c, s = _cis((f[:, None] * f[None, :]) % big, big)
    fwd_fil = jnp.concatenate([c, -s], axis=0)
    c, s = _cis((t[:, None] * f[None, :]) % big, big)
    inv = _blockmat(c, s) * (1.0 / big)
    return tuple(_split_bf16(m) for m in (fwd_sig, fwd_fil, inv))


def _filter_features(n):
    pos = jnp.arange(n, dtype=F32)
    t = (pos / max(n - 1, 1))[:, None]
    w = 2.0 * math.pi * pos[:, None] / n
    bands = jnp.linspace(1e-4, HY_BANDS - 1, HY_BANDS, dtype=F32)
    z = jnp.concatenate([t, jnp.cos(bands * w), -jnp.sin(bands * w)], axis=-1)
    return jnp.pad(z, ((0, 0), (0, LANE - HY_EMB_DIM)))


def _circular_filters(h, big):
    n = h.shape[0]
    fwd = jnp.moveaxis(h[:, 0], 1, 0)
    bwd = jnp.moveaxis(h[:, 1], 1, 0)
    zeros = jnp.zeros((fwd.shape[0], big - 2 * n + 1, fwd.shape[2]), F32)
    return jnp.concatenate([fwd, zeros, bwd[:, :0:-1]], axis=1)


def _pick(total, candidates):
    for t in candidates:
        if total % t == 0:
            return t
    raise ValueError(f"no tile for {total}")


def kernel(x, c, ctx, c_ctx, norm1_g, norm2_g, w_mod, b_mod, w_in, qn_a, kn_a, qn_d, kn_d, lam_q1, lam_k1, lam_q2, lam_k2, hy_conv_w, hy_conv_b, hy_fw1, hy_fb1, hy_fw2, hy_fb2, hy_fw3, hy_fb3, hy_freq, hy_bias, g_out, w_out, w_up, ffn_conv_w, ffn_conv_b, w_down):
    bsz, n_tok, d = x.shape
    ctx_len = ctx.shape[1]
    depth = w_in.shape[0]
    s_len = ctx_len + n_tok
    d_ff = w_down.shape[1]
    assert bsz % 2 == 0 and n_tok % (DFT_N2 * 8) == 0 and ctx_len % LANE == 0

    tm = _pick(s_len, (768, 512, 256, 128))
    tq = _pick(ctx_len, (256, 128))
    tk = _pick(s_len, (2816, 768, 256))
    n_ff_tiles = -(-d_ff // FF_TILE)
    bounds = [min(d_ff, FF_TILE * ((n_ff_tiles * j + FF_CHUNKS - 1) // FF_CHUNKS)) for j in range(FF_CHUNKS + 1)]
    ff_chunks = tuple(sorted(set(bounds)))

    n_rows = n_tok // GRID_W
    rows = jnp.repeat(jnp.arange(n_rows, dtype=F32), GRID_W)
    cols = jnp.tile(jnp.arange(GRID_W, dtype=F32), n_rows)
    n_freq = HEAD_DIM // 4
    inv = ROPE_THETA ** (-jnp.arange(n_freq, dtype=F32) / n_freq)
    ang = jnp.concatenate([rows[:, None] * inv, cols[:, None] * inv], axis=-1)
    ang = jnp.concatenate([jnp.zeros((ctx_len, HALF), F32), ang], axis=0)
    cosT, sinT = jnp.cos(ang).T, jnp.sin(ang).T

    sig_f, fil_f, g_tab, h_tab, inv_f = _dft_tables(n_tok)
    csig_f, cfil_f, cinv_f = _dense_dft_tables(ctx_len)
    z_lat = _filter_features(n_tok)
    z_ctx = _filter_features(ctx_len)
    max_decay = math.log(HY_TARGET) / HY_FAST_DECAY
    min_decay = math.log(HY_TARGET) / HY_SLOW_DECAY
    absd = jnp.tile(jnp.abs(jnp.linspace(min_decay, max_decay, HY_WIDTH, dtype=F32)), 4)[None, :]
    t1n = n_tok // DFT_N2
    n1 = 2 * t1n
    big = n1 * DFT_N2
    fb = 8

    perm = np.concatenate([np.arange(0, HEAD_DIM, 2), np.arange(1, HEAD_DIM, 2)])
    qk_cols = np.concatenate(
        [OFF_QA + b * HEAD_DIM + perm for b in range(A_HEADS)]
        + [OFF_QD + b * HEAD_DIM + perm for b in range(2 * D_HEADS)]
        + [OFF_KA + b * HEAD_DIM + perm for b in range(A_KV_HEADS)]
        + [OFF_KD + b * HEAD_DIM + perm for b in range(2 * D_HEADS)])
    v_cols = np.concatenate([np.arange(OFF_VA, OFF_VA + A_KV_WIDTH), np.arange(OFF_VD, OFF_VD + D_WIDTH)])
    qkv_cols = np.concatenate([qk_cols, v_cols])
    wqT = jnp.swapaxes(w_in[:, :, qkv_cols], 1, 2).astype(BF16)
    why = w_in[:, :, OFF_HY:OFF_HY + HY_IN].astype(BF16)
    q_scale = HEAD_DIM ** -0.5 * math.log2(math.e)
    gains = jnp.concatenate(
        [jnp.tile(qn_a[:, perm], (1, A_HEADS)) * q_scale, jnp.tile(qn_d[:, perm], (1, 2 * D_HEADS)) * q_scale,
         jnp.tile(kn_a[:, perm], (1, A_KV_HEADS)), jnp.tile(kn_d[:, perm], (1, 2 * D_HEADS))],
        axis=1)[:, :, None].astype(F32)
    w_out_b = w_out.astype(BF16)
    w_up_a = w_up[:, :, :d_ff].astype(BF16)
    w_up_v = w_up[:, :, d_ff:].astype(BF16)
    w_down_b = w_down.astype(BF16)
    fw1p = jnp.pad(hy_fw1, ((0, 0), (0, LANE - HY_EMB_DIM), (0, 0)))
    lamv = jnp.stack([lam_q1, lam_k1, lam_q2, lam_k2], axis=1)
    lam_init = jnp.asarray([0.8 - 0.6 * math.exp(-0.3 * i) for i in range(depth)], F32)
    lami = jnp.broadcast_to(lam_init[:, None, None], (depth, 8, LANE))

    n_c = 8 * ((bsz + 1 + 7) // 8)
    cvec = jnp.concatenate([c, c_ctx[None, :], jnp.zeros((n_c - bsz - 1, d), F32)], axis=0)
    mods = _modulation(cvec, w_mod, b_mod).reshape(depth, n_c, 6, d)
    pad2 = jnp.zeros((depth, bsz, 2, d), F32)
    mod_all = jnp.concatenate(
        [jnp.broadcast_to(mods[:, bsz:bsz + 1], (depth, bsz, 6, d)), pad2, mods[:, :bsz], pad2], axis=2)

    xs = jnp.concatenate([ctx, x], axis=1)
    for i in range(depth):
        mod = mod_all[i]
        qT, k, vTa, vTd, hy = _projection(xs, mod, norm1_g[i][None, :], wqT[i], why[i], gains[i], cosT, sinT,
                                    ctx_len=ctx_len, tm=tm)
        ya = _attention_a(qT, k, vTa, g_out[i, :A_WIDTH, None], ctx_len=ctx_len, tq=tq, tk=tk)
        yd = _attention_d(qT, k, vTd, g_out[i, A_WIDTH + HY_WIDTH:].reshape(D_HEADS, D_VALUE_DIM, 1),
                          lamv[i], lami[i], ctx_len=ctx_len, tq=tq, tk=tk)

        u_lat, u_ctx = _hyena_short_conv(hy, hy_conv_w[i], hy_conv_b[i], ctx_len=ctx_len)
        fil_args = (fw1p[i], hy_fb1[i][None], hy_fw2[i], hy_fb2[i][None], hy_fw3[i], hy_fb3[i][None],
                    hy_freq[i][None], absd)
        h_lat, sum_lat = _hyena_filters(z_lat, *fil_args)
        h_ctx, sum_ctx = _hyena_filters(z_ctx, *fil_args)
        ss_lat = jnp.swapaxes(sum_lat.reshape(2, 2, HY_WIDTH), 0, 1)
        ss_ctx = jnp.swapaxes(sum_ctx.reshape(2, 2, HY_WIDTH), 0, 1)
        h_ctx = jnp.transpose(h_ctx, (2, 1, 0, 3)).reshape(ctx_len, 2, 2, HY_WIDTH)
        k_ctx = _circular_filters(h_ctx, 2 * ctx_len)

        ak = _outer_forward(fil_f[0], fil_f[1], h_lat.reshape(1, 2, 4, t1n, DFT_N2, LANE), 0, 3)
        kf = _filter_spectrum(g_tab[0], g_tab[1], ak, h_lat, ss_lat, fb=fb)
        p_n = bsz // 2
        u6 = u_lat.reshape(3, 2, p_n, 2 * t1n, DFT_N2, LANE)

        def long_conv(z6, g_z, g_gate, order):
            a = _outer_forward(sig_f[0], sig_f[1], z6, g_z, 1)
            cc = _spectral_multiply(g_tab[0], h_tab[0], kf, a, order, fb=fb)
            return _outer_inverse(inv_f[0], cc, u6, g_gate, z6, g_z, hy_bias[i, order][None, :])

        z1 = long_conv(u6, 0, 1, 0)
        yh2 = long_conv(z1[None], 0, 2, 1).reshape(2, bsz, n_tok, LANE)
        yh_lat = jnp.concatenate([yh2[0], yh2[1]], axis=-1)

        kf_c = _ctx_filter_spectrum(cfil_f[0], cfil_f[1], k_ctx, ss_ctx)
        yh_ctx = _ctx_hyena(csig_f[0], csig_f[1], cinv_f[0], cinv_f[1], kf_c,
                            u_ctx.reshape(3, p_n, 2 * ctx_len, HY_WIDTH), hy_bias[i])
        yh = jnp.concatenate([yh_ctx.reshape(bsz, ctx_len, HY_WIDTH), yh_lat], axis=1)

        xs = _merge(xs, ya, yh, yd, g_out[i, A_WIDTH:A_WIDTH + HY_WIDTH][None, :], w_out_b[i], mod,
                    ctx_len=ctx_len, tm=tm)
        xs = _ffn(xs, mod, norm2_g[i][None, :], w_up_a[i], w_up_v[i], ffn_conv_w[i], ffn_conv_b[i][None, :],
                  w_down_b[i], ctx_len=ctx_len, tm=tm, chunks=ff_chunks)
    return xs[:, ctx_len:]
```

```python
import functools
import math

import numpy as np
import jax
import jax.numpy as jnp
from jax import lax
from jax.experimental import pallas as pl
from jax.experimental.pallas import tpu as pltpu

F32 = jnp.float32
BF16 = jnp.bfloat16

HEAD_DIM = 64
HALF = HEAD_DIM // 2
A_HEADS = 4
A_KV_HEADS = 2
A_WIDTH = A_HEADS * HEAD_DIM
A_KV_WIDTH = A_KV_HEADS * HEAD_DIM
HY_WIDTH = 256
HY_IN = 3 * HY_WIDTH
HY_BANDS = 16
HY_EMB_DIM = 1 + 2 * HY_BANDS
HY_HIDDEN = 64
HY_FILTER_CH = 4 * HY_WIDTH
HY_FAST_DECAY = 0.3
HY_SLOW_DECAY = 1.5
HY_TARGET = 1e-2
D_HEADS = 4
D_VALUE_DIM = 2 * HEAD_DIM
D_WIDTH = D_HEADS * D_VALUE_DIM
D_QK_WIDTH = D_HEADS * 2 * HEAD_DIM
MIX_WIDTH = A_WIDTH + HY_WIDTH + D_WIDTH
OFF_QA = 0
OFF_KA = A_WIDTH
OFF_VA = OFF_KA + A_KV_WIDTH
OFF_HY = OFF_VA + A_KV_WIDTH
OFF_QD = OFF_HY + HY_IN
OFF_KD = OFF_QD + D_QK_WIDTH
OFF_VD = OFF_KD + D_QK_WIDTH
GRID_W = 64
ROPE_THETA = 10000.0
EPS = 1e-6

N_QBLK = A_HEADS + 2 * D_HEADS
N_KBLK = A_KV_HEADS + 2 * D_HEADS
Q_ROWS = N_QBLK * HEAD_DIM
K_ROWS = N_KBLK * HEAD_DIM
V_ROWS = A_KV_WIDTH + D_WIDTH
QK_ROWS = Q_ROWS + K_ROWS
QKV_ROWS = QK_ROWS + V_ROWS
SUM_ROWS = 16
VA_BLK = HEAD_DIM + SUM_ROWS

LANE = 128
DFT_N2 = 128
KEY_BLOCK = 256
MAX_UNROLLED_CHUNKS = 4
VMEM_LIMIT = 56 * 1024 * 1024
HIGHEST = lax.Precision.HIGHEST


def _cparams(sem):
    return pltpu.CompilerParams(dimension_semantics=sem, vmem_limit_bytes=VMEM_LIMIT)


def _split_bf16(x):
    hi = x.astype(BF16)
    lo = (x - hi.astype(F32)).astype(BF16)
    return hi, lo


def _dot(a, b):
    return jnp.dot(a, b, preferred_element_type=F32)


def _dot3(a_hi, a_lo, b):
    b_hi, b_lo = _split_bf16(b)
    return _dot(a_hi, b_hi) + (_dot(a_hi, b_lo) + _dot(a_lo, b_hi))


def _dot_p(a_hi, a_lo, b, passes):
    return _dot3(a_hi, a_lo, b) if passes == 3 else _dot(a_hi, b.astype(BF16))


def _mod_kernel(c_ref, w_ref, b_ref, o_ref):
    c = c_ref[...]
    s = c * (1.0 / (1.0 + jnp.exp(-c)))
    o_ref[0] = jnp.dot(s, w_ref[0], preferred_element_type=F32, precision=HIGHEST) + b_ref[0]


def _modulation(cvec, w_mod, b_mod):
    depth, d, n6 = w_mod.shape
    tn = 1024
    return pl.pallas_call(
        _mod_kernel,
        out_shape=jax.ShapeDtypeStruct((depth, cvec.shape[0], n6), F32),
        grid=(depth, n6 // tn),
        in_specs=[
            pl.BlockSpec(cvec.shape, lambda l, j: (0, 0)),
            pl.BlockSpec((1, d, tn), lambda l, j: (l, 0, j)),
            pl.BlockSpec((1, 1, tn), lambda l, j: (l, 0, j)),
        ],
        out_specs=pl.BlockSpec((1, cvec.shape[0], tn), lambda l, j: (l, 0, j)),
        compiler_params=_cparams(("arbitrary", "arbitrary")),
        name="modulation",
    )(cvec, w_mod, b_mod.reshape(depth, 1, n6))


def _norm_modulate(x, g, mod_ref, is_ctx, which):
    ms = jnp.mean(x * x, axis=-1, keepdims=True)
    xn = x * lax.rsqrt(ms + EPS) * g
    sh = jnp.where(is_ctx, mod_ref[0, 3 * which:3 * which + 1, :], mod_ref[0, 8 + 3 * which:9 + 3 * which, :])
    sc = jnp.where(is_ctx, mod_ref[0, 3 * which + 1:3 * which + 2, :],
                   mod_ref[0, 9 + 3 * which:10 + 3 * which, :])
    return xn * (1.0 + sc) + sh


def _proj_kernel(x_ref, mod_ref, g_ref, wq_ref, why_ref, gain_ref, cos_ref, sin_ref,
                 qT_ref, k_ref, vTa_ref, vTd_ref, hy_ref, *, tm, ctx_len):
    i = pl.program_id(1)
    row = i * tm + lax.broadcasted_iota(jnp.int32, (tm, 1), 0)
    h = _norm_modulate(x_ref[0], g_ref[...], mod_ref, row < ctx_len, 0).astype(BF16)
    pT = lax.dot_general(wq_ref[...], h, (((1,), (1,)), ((), ())), preferred_element_type=F32)
    hy_ref[0] = _dot(h, why_ref[...])
    c = cos_ref[...]
    s = sin_ref[...]
    blocks = []
    for b in range(N_QBLK + N_KBLK):
        blk = pT[b * HEAD_DIM:(b + 1) * HEAD_DIM]
        ssq = jnp.sum(blk * blk, axis=0, keepdims=True)
        blk = blk * lax.rsqrt(ssq * (1.0 / HEAD_DIM) + EPS) * gain_ref[b * HEAD_DIM:(b + 1) * HEAD_DIM, :]
        x1 = blk[:HALF]
        x2 = blk[HALF:]
        blocks.append(x1 * c - x2 * s)
        blocks.append(x1 * s + x2 * c)
    qT_ref[0] = jnp.concatenate(blocks[:2 * N_QBLK], axis=0).astype(BF16)
    kT = jnp.concatenate(blocks[2 * N_QBLK:], axis=0)
    k_ref[0] = kT.T.astype(BF16)
    ones = jnp.ones((SUM_ROWS, tm), BF16)
    va = [pT[QK_ROWS + g * HEAD_DIM:QK_ROWS + (g + 1) * HEAD_DIM].astype(BF16) for g in range(A_KV_HEADS)]
    vTa_ref[0] = jnp.concatenate([t for v in va for t in (v, ones)], axis=0)
    vTd_ref[0] = pT[QK_ROWS + A_KV_WIDTH:].astype(BF16)


def _projection(xs, mod, g, wqT, why, gains, cosT, sinT, *, ctx_len, tm):
    bsz, s_len, d = xs.shape
    kern = functools.partial(_proj_kernel, tm=tm, ctx_len=ctx_len)
    return pl.pallas_call(
        kern,
        out_shape=(
            jax.ShapeDtypeStruct((bsz, Q_ROWS, s_len), BF16),
            jax.ShapeDtypeStruct((bsz, s_len, K_ROWS), BF16),
            jax.ShapeDtypeStruct((bsz, A_KV_HEADS * VA_BLK, s_len), BF16),
            jax.ShapeDtypeStruct((bsz, D_WIDTH, s_len), BF16),
            jax.ShapeDtypeStruct((bsz, s_len, HY_IN), F32),
        ),
        grid=(bsz, s_len // tm),
        in_specs=[
            pl.BlockSpec((1, tm, d), lambda b, i: (b, i, 0)),
            pl.BlockSpec((1, 16, d), lambda b, i: (b, 0, 0)),
            pl.BlockSpec((1, d), lambda b, i: (0, 0)),
            pl.BlockSpec((QKV_ROWS, d), lambda b, i: (0, 0)),
            pl.BlockSpec((d, HY_IN), lambda b, i: (0, 0)),
            pl.BlockSpec((QK_ROWS, 1), lambda b, i: (0, 0)),
            pl.BlockSpec((HALF, tm), lambda b, i: (0, i)),
            pl.BlockSpec((HALF, tm), lambda b, i: (0, i)),
        ],
        out_specs=(
            pl.BlockSpec((1, Q_ROWS, tm), lambda b, i: (b, 0, i)),
            pl.BlockSpec((1, tm, K_ROWS), lambda b, i: (b, i, 0)),
            pl.BlockSpec((1, A_KV_HEADS * VA_BLK, tm), lambda b, i: (b, 0, i)),
            pl.BlockSpec((1, D_WIDTH, tm), lambda b, i: (b, 0, i)),
            pl.BlockSpec((1, tm, HY_IN), lambda b, i: (b, i, 0)),
        ),
        compiler_params=_cparams(("parallel", "parallel")),
        name="projection",
    )(xs, mod, g, wqT, why, gains, cosT, sinT)


def _attend(units, scr, *, ctx_only, ctx_len, tk, sum_rows):
    n_u = len(units)
    qz, m_run, l_run, acc, s_buf, cm_buf, qz0 = _split_scratch(scr, n_u)
    s_len = units[0][0].shape[1]
    for u in range(n_u):
        m_run[u][...] = jnp.full(m_run[u].shape, -1e30, F32)
        l_run[u][...] = jnp.zeros(l_run[u].shape, F32)
        acc[u][...] = jnp.zeros(acc[u].shape, F32)

    def fold8(s, op):
        return op(s.reshape(s.shape[0] // 8, 8, s.shape[1]), axis=0)

    def key_rows(start, j, rb):
        if isinstance(start, int):
            return pl.ds(start + j * rb, rb)
        return pl.ds(pl.multiple_of(start + j * rb, rb), rb)

    def advance(nxt, cur, size):
        rb = min(size, KEY_BLOCK)
        if cur is not None:
            u_c, start_c, slot_c = cur
            _, vT_ref, v0, dv = units[u_c]
            m = m_run[u_c][...]
            m_new = jnp.maximum(m, jnp.max(cm_buf[slot_c][...], axis=0, keepdims=True))
            m_run[u_c][...] = m_new
            alpha = jnp.exp2(m - m_new)
        cmax, psum, pv = None, None, None
        for j in range(size // rb):
            rows = slice(j * rb, (j + 1) * rb)
            if nxt is not None:
                u_n, start_n, slot_n = nxt[:3]
                q_n = qz[u_n][...] if len(nxt) == 3 else qz0[nxt[3]]
                s = _dot(units[u_n][0][0, key_rows(start_n, j, rb), :], q_n)
                s_buf[slot_n][rows, :] = s
                cmax = fold8(s, jnp.max) if cmax is None else jnp.maximum(cmax, fold8(s, jnp.max))
            if cur is not None:
                p = jnp.exp2(s_buf[slot_c][rows, :] - m_new)
                if not sum_rows:
                    psum = fold8(p, jnp.sum) if psum is None else psum + fold8(p, jnp.sum)
                d = _dot(vT_ref[0, v0:v0 + dv, key_rows(start_c, j, rb)], p.astype(BF16))
                pv = d if pv is None else pv + d
        if nxt is not None:
            cm_buf[slot_n][...] = cmax
        if cur is not None:
            if not sum_rows:
                l_run[u_c][...] = alpha * l_run[u_c][...] + jnp.sum(psum, axis=0, keepdims=True)
            acc[u_c][...] = alpha * acc[u_c][...] + pv

    if ctx_only:
        for u in range(n_u):
            advance((u, 0, u % 2), None, ctx_len)
            advance(None, (u, 0, u % 2), ctx_len)
        advance((0, 0, 0, 1), None, tk)
    else:
        n_chunks = s_len // tk

        def chunk_steps(c, start, nxt, wrap):
            for u in range(n_u):
                task_n = (u + 1, start, (u + 1) % 2) if u + 1 < n_u else (0, nxt, (u + 1) % 2, wrap)
                advance(task_n, (u, start, u % 2), tk)

        if n_chunks <= MAX_UNROLLED_CHUNKS:
            for c in range(n_chunks):
                last = c == n_chunks - 1
                chunk_steps(c, c * tk, 0 if last else (c + 1) * tk, int(last))
        else:
            def body(c, carry):
                wrap = jnp.where(c == n_chunks - 1, 1, 0).astype(jnp.int32)
                chunk_steps(c, pl.multiple_of(c * tk, tk), pl.multiple_of((1 - wrap) * (c + 1) * tk, tk), wrap)
                return carry
            lax.fori_loop(0, n_chunks, body, 0)
    outs = []
    for u in range(n_u):
        dv = units[u][3] - sum_rows
        l = acc[u][dv:dv + 1, :] if sum_rows else l_run[u][...]
        outs.append(acc[u][:dv, :] * (1.0 / l))
    return outs


def _attn_scratch(n_u, rows, tq, tk):
    per_unit = [((LANE, tq), BF16), ((1, tq), F32), ((1, tq), F32), ((rows, tq), F32)]
    per_slot = [((tk, tq), F32), ((8, tq), F32)]
    return ([pltpu.VMEM(s, d) for s, d in per_unit for _ in range(n_u)]
            + [pltpu.VMEM(s, d) for s, d in per_slot for _ in range(2)]
            + [pltpu.VMEM((2, LANE, tq), BF16)])


def _split_scratch(scr, n_u):
    groups = [scr[i * n_u:(i + 1) * n_u] for i in range(4)]
    rest = scr[4 * n_u:]
    return groups + [rest[0:2], rest[2:4], rest[4]]


def _run_ctx_or_latent(qi, n_ctx_tiles, run):
    @pl.when(qi < n_ctx_tiles)
    def _():
        run(True)

    @pl.when(qi >= n_ctx_tiles)
    def _():
        run(False)


def _attn_a_kernel(qT_ref, qn_ref, k_ref, vT_ref, g_ref, o_ref, *scr, ctx_len, tq, tk):
    zeros = jnp.zeros((HEAD_DIM, tq), BF16)
    group = A_HEADS // A_KV_HEADS
    for h in range(A_HEADS):
        q = qT_ref[0, h * HEAD_DIM:(h + 1) * HEAD_DIM, :]
        scr[h][...] = jnp.concatenate([q, zeros] if h // group == 0 else [zeros, q], axis=0)
    scr[-1][0] = jnp.concatenate([qT_ref[0, :HEAD_DIM, :], zeros], axis=0)
    scr[-1][1] = jnp.concatenate([qn_ref[0, :HEAD_DIM, :], zeros], axis=0)
    units = [(k_ref, vT_ref, (h // group) * VA_BLK, VA_BLK) for h in range(A_HEADS)]

    def run(ctx_only):
        outs = _attend(units, scr, ctx_only=ctx_only, ctx_len=ctx_len, tk=tk, sum_rows=SUM_ROWS)
        y = jnp.concatenate(outs, axis=0)
        ms = jnp.mean(y * y, axis=0, keepdims=True)
        y = y * lax.rsqrt(ms + EPS) * g_ref[...]
        o_ref[0] = y.T.astype(BF16)

    _run_ctx_or_latent(pl.program_id(1), ctx_len // tq, run)


def _attention_a(qT, k, vT, g_a, *, ctx_len, tq, tk):
    bsz, _, s_len = qT.shape
    kern = functools.partial(_attn_a_kernel, ctx_len=ctx_len, tq=tq, tk=tk)
    return pl.pallas_call(
        kern,
        out_shape=jax.ShapeDtypeStruct((bsz, s_len, A_WIDTH), BF16),
        grid=(bsz, s_len // tq),
        in_specs=[
            pl.BlockSpec((1, A_WIDTH, tq), lambda b, i: (b, 0, i)),
            pl.BlockSpec((1, A_WIDTH, tq), lambda b, i: (b, 0, jnp.minimum(i + 1, s_len // tq - 1))),
            pl.BlockSpec((1, s_len, LANE), lambda b, i: (b, 0, 0)),
            pl.BlockSpec((1, A_KV_HEADS * VA_BLK, s_len), lambda b, i: (b, 0, 0)),
            pl.BlockSpec((A_WIDTH, 1), lambda b, i: (0, 0)),
        ],
        out_specs=pl.BlockSpec((1, tq, A_WIDTH), lambda b, i: (b, i, 0)),
        scratch_shapes=_attn_scratch(A_HEADS, VA_BLK, tq, tk),
        compiler_params=_cparams(("parallel", "arbitrary")),
        name="attention_gqa",
    )(qT, qT, k, vT, g_a)


D_HEADS_PER_STEP = 2


def _attn_d_kernel(*refs, ctx_len, tq, tk):
    n_h = D_HEADS_PER_STEP
    q_refs, k_refs, v_refs = refs[:n_h], refs[n_h:2 * n_h], refs[2 * n_h:3 * n_h]
    qn_ref, g_ref, lamv_ref, lami_ref, o_ref = refs[3 * n_h:3 * n_h + 5]
    scr = refs[3 * n_h + 5:]
    zeros = jnp.zeros((HEAD_DIM, tq), BF16)
    lv = lamv_ref[...]
    lam_init = lami_ref[0:1, 0:1]
    lam = (jnp.exp(jnp.sum(lv[0:1] * lv[1:2], axis=-1, keepdims=True))
           - jnp.exp(jnp.sum(lv[2:3] * lv[3:4], axis=-1, keepdims=True)) + lam_init)
    units = []
    for h in range(n_h):
        scr[2 * h][...] = jnp.concatenate([q_refs[h][0, :HEAD_DIM, :], zeros], axis=0)
        scr[2 * h + 1][...] = jnp.concatenate([zeros, q_refs[h][0, HEAD_DIM:, :]], axis=0)
        units += [(k_refs[h], v_refs[h], 0, D_VALUE_DIM)] * 2
    scr[-1][0] = jnp.concatenate([q_refs[0][0, :HEAD_DIM, :], zeros], axis=0)
    scr[-1][1] = jnp.concatenate([qn_ref[0, :HEAD_DIM, :], zeros], axis=0)

    def run(ctx_only):
        outs = _attend(units, scr, ctx_only=ctx_only, ctx_len=ctx_len, tk=tk, sum_rows=0)
        ys = []
        for h in range(n_h):
            y = outs[2 * h] - lam * outs[2 * h + 1]
            ms = jnp.mean(y * y, axis=0, keepdims=True)
            ys.append(y * lax.rsqrt(ms + EPS) * g_ref[h] * (1.0 - lam_init))
        o_ref[0] = jnp.concatenate(ys, axis=0).T.astype(BF16)

    _run_ctx_or_latent(pl.program_id(2), ctx_len // tq, run)


def _attention_d(qT, k, vT, g_d, lamv, lami, *, ctx_len, tq, tk):
    bsz, _, s_len = qT.shape
    n_h = D_HEADS_PER_STEP
    kern = functools.partial(_attn_d_kernel, ctx_len=ctx_len, tq=tq, tk=tk)
    q_blk0 = A_WIDTH // LANE
    k_blk0 = A_KV_WIDTH // LANE
    q_specs = [pl.BlockSpec((1, LANE, tq), lambda b, hp, i, j=j: (b, q_blk0 + n_h * hp + j, i)) for j in range(n_h)]
    k_specs = [pl.BlockSpec((1, s_len, LANE), lambda b, hp, i, j=j: (b, 0, k_blk0 + n_h * hp + j)) for j in range(n_h)]
    v_specs = [pl.BlockSpec((1, D_VALUE_DIM, s_len), lambda b, hp, i, j=j: (b, n_h * hp + j, 0)) for j in range(n_h)]
    return pl.pallas_call(
        kern,
        out_shape=jax.ShapeDtypeStruct((bsz, s_len, D_WIDTH), BF16),
        grid=(bsz, D_HEADS // n_h, s_len // tq),
        in_specs=q_specs + k_specs + v_specs + [
            pl.BlockSpec((1, LANE, tq), lambda b, hp, i: (b, q_blk0 + n_h * hp, jnp.minimum(i + 1, s_len // tq - 1))),
            pl.BlockSpec((n_h, D_VALUE_DIM, 1), lambda b, hp, i: (hp, 0, 0)),
            pl.BlockSpec((4, HEAD_DIM), lambda b, hp, i: (0, 0)),
            pl.BlockSpec((8, LANE), lambda b, hp, i: (0, 0)),
        ],
        out_specs=pl.BlockSpec((1, tq, n_h * D_VALUE_DIM), lambda b, hp, i: (b, i, hp)),
        scratch_shapes=_attn_scratch(2 * n_h, D_VALUE_DIM, tq, tk),
        compiler_params=_cparams(("parallel", "parallel", "arbitrary")),
        name="attention_diff",
    )(*([qT] * n_h + [k] * n_h + [vT] * n_h + [qT, g_d, lamv, lami]))


def _dwconv3_seq(u, w_ref, b_ref):
    n = u.shape[0]
    row = lax.broadcasted_iota(jnp.int32, (n, 1), 0)
    prev = jnp.where(row == 0, 0.0, pltpu.roll(u, 1, axis=0))
    nxt = jnp.where(row == n - 1, 0.0, pltpu.roll(u, n - 1, axis=0))
    return prev * w_ref[0:1, :] + u * w_ref[1:2, :] + nxt * w_ref[2:3, :] + b_ref[...]


def _hy_conv_kernel(hy_ref, w_ref, b_ref, lat_ref, ctx_ref, *, ctx_len):
    ctx_ref[0, 0] = _dwconv3_seq(hy_ref[0, :ctx_len, :], w_ref, b_ref)
    lat_ref[0, 0, 0] = _dwconv3_seq(hy_ref[0, ctx_len:, :], w_ref, b_ref)


def _hyena_short_conv(hy, w, b, *, ctx_len):
    bsz, s_len, _ = hy.shape
    n = s_len - ctx_len
    per = HY_WIDTH // LANE
    kern = functools.partial(_hy_conv_kernel, ctx_len=ctx_len)
    return pl.pallas_call(
        kern,
        out_shape=(
            jax.ShapeDtypeStruct((3, per, bsz, n, LANE), F32),
            jax.ShapeDtypeStruct((3, bsz, ctx_len, HY_WIDTH), F32),
        ),
        grid=(bsz, HY_IN // LANE),
        in_specs=[
            pl.BlockSpec((1, s_len, LANE), lambda b, j: (b, 0, j)),
            pl.BlockSpec((3, LANE), lambda b, j: (0, j)),
            pl.BlockSpec((1, LANE), lambda b, j: (0, j)),
        ],
        out_specs=(
            pl.BlockSpec((1, 1, 1, n, LANE), lambda b, j: (j // per, j % per, b, 0, 0)),
            pl.BlockSpec((1, 1, ctx_len, LANE), lambda b, j: (j // per, b, 0, j % per)),
        ),
        compiler_params=_cparams(("parallel", "parallel")),
        name="hyena_short_conv",
    )(hy, w, b.reshape(1, HY_IN))


def _filter_kernel(z_ref, w1_ref, b1_ref, w2_ref, b2_ref, w3_ref, b3_ref, fr_ref, ad_ref, h_ref, sum_ref):
    i = pl.program_id(0)
    z = z_ref[...]
    fr = fr_ref[...]
    h = jnp.sin(fr * (jnp.dot(z, w1_ref[...], preferred_element_type=F32, precision=HIGHEST) + b1_ref[...]))
    h = jnp.sin(fr * (jnp.dot(h, w2_ref[...], preferred_element_type=F32, precision=HIGHEST) + b2_ref[...]))
    h = jnp.dot(h, w3_ref[...], preferred_element_type=F32, precision=HIGHEST) + b3_ref[...]
    h = h * jnp.exp(-z[:, 0:1] * ad_ref[...])
    for q in range(HY_FILTER_CH // HY_WIDTH):
        for hh in range(HY_WIDTH // LANE):
            c0 = q * HY_WIDTH + hh * LANE
            h_ref[hh, q] = h[:, c0:c0 + LANE]

    @pl.when(i == 0)
    def _():
        sum_ref[...] = jnp.zeros_like(sum_ref)

    sum_ref[...] += jnp.sum(jnp.abs(h), axis=0, keepdims=True)


def _hyena_filters(zfeat, w1, b1, w2, b2, w3, b3, freq, absd):
    n, kz = zfeat.shape
    tn = min(n, 1024)
    full = lambda a: pl.BlockSpec(a.shape, lambda i: (0, 0))
    return pl.pallas_call(
        _filter_kernel,
        out_shape=(
            jax.ShapeDtypeStruct((HY_WIDTH // LANE, HY_FILTER_CH // HY_WIDTH, n, LANE), F32),
            jax.ShapeDtypeStruct((1, HY_FILTER_CH), F32),
        ),
        grid=(n // tn,),
        in_specs=[pl.BlockSpec((tn, kz), lambda i: (i, 0)), full(w1), full(b1), full(w2), full(b2),
                  full(w3), full(b3), full(freq), full(absd)],
        out_specs=(
            pl.BlockSpec((HY_WIDTH // LANE, HY_FILTER_CH // HY_WIDTH, tn, LANE), lambda i: (0, 0, i, 0)),
            pl.BlockSpec((1, HY_FILTER_CH), lambda i: (0, 0)),
        ),
        compiler_params=_cparams(("arbitrary",)),
        name="hyena_filters",
    )(zfeat, w1, b1, w2, b2, w3, b3, freq, absd)


T2_BLOCK = 8


def _outer_fwd_kernel(mh_ref, ml_ref, x_ref, o_ref, *, n1, passes):
    rows = x_ref.shape[3]
    xs = [x_ref.at[0, h, 0].reshape(rows * T2_BLOCK, LANE) for h in range(2)]
    outs = [[o_ref.at[0, ri, h].reshape(n1 * T2_BLOCK, LANE) for h in range(2)] for ri in range(2)]
    for t in range(T2_BLOCK):
        sel = pl.ds(t, rows, stride=T2_BLOCK)
        y = _dot_p(mh_ref[...], ml_ref[...], jnp.concatenate([xs[0][sel, :], xs[1][sel, :]], axis=1), passes)
        for ri in range(2):
            for h in range(2):
                outs[ri][h][pl.ds(t, n1, stride=T2_BLOCK), :] = y[ri * n1:(ri + 1) * n1, h * LANE:(h + 1) * LANE]


def _outer_forward(m_hi, m_lo, x6, g, passes):
    _, _, p_n, rows, n2, _ = x6.shape
    n1 = m_hi.shape[0] // 2
    kern = functools.partial(_outer_fwd_kernel, n1=n1, passes=passes)
    return pl.pallas_call(
        kern,
        out_shape=jax.ShapeDtypeStruct((p_n, 2, 2, n1, n2, LANE), F32),
        grid=(p_n, n2 // T2_BLOCK),
        in_specs=[pl.BlockSpec(m_hi.shape, lambda p, j: (0, 0)), pl.BlockSpec(m_lo.shape, lambda p, j: (0, 0)),
                  pl.BlockSpec((1, 2, 1, rows, T2_BLOCK, LANE), lambda p, j: (g, 0, p, 0, j, 0))],
        out_specs=pl.BlockSpec((1, 2, 2, n1, T2_BLOCK, LANE), lambda p, j: (p, 0, 0, 0, j, 0)),
        compiler_params=_cparams(("parallel", "parallel")),
        name="dft_outer_fwd",
    )(m_hi, m_lo, x6)


def _outer_inv_kernel(mh_ref, c_ref, gate_ref, u_ref, bias_ref, o_ref):
    n1 = c_ref.shape[3]
    rows = o_ref.shape[2]
    cs = [[c_ref.at[0, ri, h].reshape(n1 * T2_BLOCK, LANE) for h in range(2)] for ri in range(2)]
    gates = [gate_ref.at[0, h, 0].reshape(rows * T2_BLOCK, LANE) for h in range(2)]
    us = [u_ref.at[0, h, 0].reshape(rows * T2_BLOCK, LANE) for h in range(2)]
    outs = [o_ref.at[h, 0].reshape(rows * T2_BLOCK, LANE) for h in range(2)]
    for t in range(T2_BLOCK):
        sel_f = pl.ds(t, n1, stride=T2_BLOCK)
        sel_t = pl.ds(t, rows, stride=T2_BLOCK)
        c2 = jnp.concatenate([jnp.concatenate([cs[ri][0][sel_f, :], cs[ri][1][sel_f, :]], axis=1)
                              for ri in range(2)], axis=0)
        y = _dot(mh_ref[...], c2.astype(BF16))
        for h in range(2):
            lanes = slice(h * LANE, (h + 1) * LANE)
            outs[h][sel_t, :] = gates[h][sel_t, :] * (y[:, lanes] + us[h][sel_t, :] * bias_ref[:, lanes])


def _outer_inverse(m_hi, c6, gate6, g_gate, u6, g_u, bias):
    p_n, _, _, n1, n2, _ = c6.shape
    rows = m_hi.shape[0]
    gspec = lambda g: pl.BlockSpec((1, 2, 1, rows, T2_BLOCK, LANE), lambda p, j: (g, 0, p, 0, j, 0))
    return pl.pallas_call(
        _outer_inv_kernel,
        out_shape=jax.ShapeDtypeStruct((2, p_n, rows, n2, LANE), F32),
        grid=(p_n, n2 // T2_BLOCK),
        in_specs=[pl.BlockSpec(m_hi.shape, lambda p, j: (0, 0)),
                  pl.BlockSpec((1, 2, 2, n1, T2_BLOCK, LANE), lambda p, j: (p, 0, 0, 0, j, 0)),
                  gspec(g_gate), gspec(g_u), pl.BlockSpec((1, 2 * LANE), lambda p, j: (0, 0))],
        out_specs=pl.BlockSpec((2, 1, rows, T2_BLOCK, LANE), lambda p, j: (0, p, 0, j, 0)),
        compiler_params=_cparams(("parallel", "parallel")),
        name="dft_outer_inv",
    )(m_hi, c6, gate6, u6, bias)


def _halves_to_rows(a_ref, j):
    return jnp.concatenate([jnp.concatenate([a_ref[0, ri, 0, j], a_ref[0, ri, 1, j]], axis=1) for ri in range(2)],
                           axis=0)


def _kf_kernel(gh_ref, gl_ref, af_ref, ab_ref, h0_ref, ss_ref, o_ref, *, fb, n2):
    scale = 1.0 / (ss_ref[0, 0:1, :] + ss_ref[0, 1:2, :] + EPS)
    b0 = jnp.concatenate([h0_ref[0, 0, 0:1, :], h0_ref[1, 0, 0:1, :]], axis=1)
    for j in range(fb):
        xf = _dot3(gh_ref[j], gl_ref[j], _halves_to_rows(af_ref, j))
        xb = _dot3(gh_ref[j], gl_ref[j], _halves_to_rows(ab_ref, j))
        o_ref[0, j, :n2, :] = (xf[:n2] + xb[:n2] - b0) * scale
        o_ref[0, j, n2:, :] = (xf[n2:] - xb[n2:]) * scale


def _filter_spectrum(g_hi, g_lo, a6, h6, ss, *, fb):
    n_q, _, _, n1, n2, _ = a6.shape
    n_ord = n_q // 2
    c = 2 * LANE
    kern = functools.partial(_kf_kernel, fb=fb, n2=n2)
    return pl.pallas_call(
        kern,
        out_shape=jax.ShapeDtypeStruct((n_ord, n1, 2 * n2, c), F32),
        grid=(n1 // fb, n_ord),
        in_specs=[
            pl.BlockSpec((fb, 2 * n2, 2 * n2), lambda f, o: (f, 0, 0)),
            pl.BlockSpec((fb, 2 * n2, 2 * n2), lambda f, o: (f, 0, 0)),
            pl.BlockSpec((1, 2, 2, fb, n2, LANE), lambda f, o: (o, 0, 0, f, 0, 0)),
            pl.BlockSpec((1, 2, 2, fb, n2, LANE), lambda f, o: (n_ord + o, 0, 0, f, 0, 0)),
            pl.BlockSpec((2, 1, 8, LANE), lambda f, o: (0, n_ord + o, 0, 0)),
            pl.BlockSpec((1, 2, c), lambda f, o: (o, 0, 0)),
        ],
        out_specs=pl.BlockSpec((1, fb, 2 * n2, c), lambda f, o: (o, f, 0, 0)),
        compiler_params=_cparams(("parallel", "parallel")),
        name="filter_spectrum",
    )(g_hi, g_lo, a6, a6, h6, ss)


def _spec_kernel(gh_ref, hh_ref, kf_ref, a_ref, o_ref, *, fb, n2):
    for j in range(fb):
        x = _dot(gh_ref[j], _halves_to_rows(a_ref, j).astype(BF16))
        xr, xi = x[:n2], x[n2:]
        kr, ki = kf_ref[0, j, :n2], kf_ref[0, j, n2:]
        y = jnp.concatenate([xr * kr - xi * ki, xr * ki + xi * kr], axis=0)
        c2 = _dot(hh_ref[j], y.astype(BF16))
        for ri in range(2):
            for h in range(2):
                o_ref[0, ri, h, j] = c2[ri * n2:(ri + 1) * n2, h * LANE:(h + 1) * LANE]


def _spectral_multiply(g_hi, h_hi, kf, a6, order, *, fb):
    p_n, _, _, n1, n2, _ = a6.shape
    c = 2 * LANE
    kern = functools.partial(_spec_kernel, fb=fb, n2=n2)
    gspec = pl.BlockSpec((fb, 2 * n2, 2 * n2), lambda f, p: (f, 0, 0))
    aspec = pl.BlockSpec((1, 2, 2, fb, n2, LANE), lambda f, p: (p, 0, 0, f, 0, 0))
    return pl.pallas_call(
        kern,
        out_shape=jax.ShapeDtypeStruct(a6.shape, F32),
        grid=(n1 // fb, p_n),
        in_specs=[gspec, gspec,
                  pl.BlockSpec((1, fb, 2 * n2, c), lambda f, p: (order, f, 0, 0)), aspec],
        out_specs=aspec,
        compiler_params=_cparams(("parallel", "parallel")),
        name="spectral_multiply",
    )(g_hi, h_hi, kf, a6)


def _ctx_kf_kernel(fh_ref, fl_ref, k_ref, ss_ref, o_ref):
    scale = 1.0 / (ss_ref[0, 0:1, :] + ss_ref[0, 1:2, :] + EPS)
    o_ref[0] = _dot3(fh_ref[...], fl_ref[...], k_ref[0]) * scale


def _ctx_filter_spectrum(f_hi, f_lo, k, ss):
    n_ord, nc, c = k.shape
    return pl.pallas_call(
        _ctx_kf_kernel,
        out_shape=jax.ShapeDtypeStruct((n_ord, 2 * nc, c), F32),
        grid=(n_ord,),
        in_specs=[
            pl.BlockSpec(f_hi.shape, lambda o: (0, 0)),
            pl.BlockSpec(f_lo.shape, lambda o: (0, 0)),
            pl.BlockSpec((1, nc, c), lambda o: (o, 0, 0)),
            pl.BlockSpec((1, 2, c), lambda o: (o, 0, 0)),
        ],
        out_specs=pl.BlockSpec((1, 2 * nc, c), lambda o: (o, 0, 0)),
        compiler_params=_cparams(("parallel",)),
        name="ctx_filter_spectrum",
    )(f_hi, f_lo, k, ss)


def _ctx_conv_kernel(fh_ref, fl_ref, eh_ref, el_ref, kf_ref, u_ref, bias_ref, o_ref, *, nc):
    v, x1, x2 = u_ref[0, 0], u_ref[1, 0], u_ref[2, 0]

    def conv(z, order):
        x = _dot3(fh_ref[...], fl_ref[...], z)
        xr, xi = x[:nc], x[nc:]
        kr, ki = kf_ref[order, :nc], kf_ref[order, nc:]
        y = jnp.concatenate([xr * kr - xi * ki, xr * ki + xi * kr], axis=0)
        return _dot3(eh_ref[...], el_ref[...], y) + z * bias_ref[order:order + 1, :]

    o_ref[0] = x2 * conv(x1 * conv(v, 0), 1)


def _ctx_hyena(f_hi, f_lo, e_hi, e_lo, kf, u3, bias):
    _, p_n, n2x, c = u3.shape
    nc = kf.shape[1] // 2
    kern = functools.partial(_ctx_conv_kernel, nc=nc)
    full = lambda a: pl.BlockSpec(a.shape, lambda p: (0,) * a.ndim)
    return pl.pallas_call(
        kern,
        out_shape=jax.ShapeDtypeStruct((p_n, n2x, c), F32),
        grid=(p_n,),
        in_specs=[full(f_hi), full(f_lo), full(e_hi), full(e_lo), full(kf),
                  pl.BlockSpec((3, 1, n2x, c), lambda p: (0, p, 0, 0)), full(bias)],
        out_specs=pl.BlockSpec((1, n2x, c), lambda p: (p, 0, 0)),
        compiler_params=_cparams(("parallel",)),
        name="ctx_hyena",
    )(f_hi, f_lo, e_hi, e_lo, kf, u3, bias)


def _merge_kernel(x_ref, ya_ref, yh_ref, yd_ref, gh_ref, w_ref, mod_ref, o_ref, *, tm, ctx_len):
    i = pl.program_id(1)
    row = i * tm + lax.broadcasted_iota(jnp.int32, (tm, 1), 0)
    yh = yh_ref[0]
    ms = jnp.mean(yh * yh, axis=-1, keepdims=True)
    yh = yh * lax.rsqrt(ms + EPS) * gh_ref[...]
    y = _dot(ya_ref[0].astype(BF16), w_ref[:A_WIDTH, :])
    y += _dot(yh.astype(BF16), w_ref[A_WIDTH:A_WIDTH + HY_WIDTH, :])
    y += _dot(yd_ref[0].astype(BF16), w_ref[A_WIDTH + HY_WIDTH:, :])
    gate = jnp.where(row < ctx_len, mod_ref[0, 2:3, :], mod_ref[0, 10:11, :])
    o_ref[0] = x_ref[0] + gate * y


def _merge(xs, ya, yh, yd, g_h, w_out, mod, *, ctx_len, tm):
    bsz, s_len, d = xs.shape
    kern = functools.partial(_merge_kernel, tm=tm, ctx_len=ctx_len)
    tok = lambda w: pl.BlockSpec((1, tm, w), lambda b, i: (b, i, 0))
    return pl.pallas_call(
        kern,
        out_shape=jax.ShapeDtypeStruct(xs.shape, F32),
        grid=(bsz, s_len // tm),
        in_specs=[tok(d), tok(A_WIDTH), tok(HY_WIDTH), tok(D_WIDTH),
                  pl.BlockSpec((1, HY_WIDTH), lambda b, i: (0, 0)),
                  pl.BlockSpec((MIX_WIDTH, d), lambda b, i: (0, 0)),
                  pl.BlockSpec((1, 16, d), lambda b, i: (b, 0, 0))],
        out_specs=tok(d),
        compiler_params=_cparams(("parallel", "parallel")),
        name="merge",
    )(xs, ya, yh, yd, g_h, w_out, mod)


HALO = 8
FF_TILE = 256
FF_CHUNKS = 2


def _ffn_kernel(x_ref, xp_ref, xn_ref, mod_ref, g_ref, wa_ref, wv_ref, cw_ref, cb_ref, wd_ref, o_ref,
                *, tm, chunks, ctx_len, s_len):
    i = pl.program_id(1)
    x = x_ref[0]
    xa = jnp.concatenate([xp_ref[0], x, xn_ref[0]], axis=0)
    rows = tm + 2 * HALO
    row = i * tm - HALO + lax.broadcasted_iota(jnp.int32, (rows, 1), 0)
    h = _norm_modulate(xa, g_ref[...], mod_ref, row < ctx_len, 1).astype(BF16)
    hm = h[HALO:HALO + tm]
    rowm = row[HALO:HALO + tm]
    has_prev = jnp.logical_and(rowm != 0, rowm != ctx_len)
    has_next = jnp.logical_and(rowm != ctx_len - 1, rowm != s_len - 1)
    def up(j):
        cs = slice(chunks[j], chunks[j + 1])
        return _dot(h, wa_ref[:, cs]), _dot(hm, wv_ref[:, cs])

    n_chunks = len(chunks) - 1
    acc = jnp.zeros((tm, x.shape[1]), F32)
    nxt = up(0)
    for j in range(n_chunks):
        cs = slice(chunks[j], chunks[j + 1])
        a, v = nxt
        if j + 1 < n_chunks:
            nxt = up(j + 1)
        a_prev = jnp.where(has_prev, pltpu.roll(a, 1, axis=0)[HALO:HALO + tm], 0.0)
        a_next = jnp.where(has_next, pltpu.roll(a, rows - 1, axis=0)[HALO:HALO + tm], 0.0)
        ac = a_prev * cw_ref[0:1, cs] + a[HALO:HALO + tm] * cw_ref[1:2, cs] + a_next * cw_ref[2:3, cs] + cb_ref[:, cs]
        gl = 0.5 * ac * (1.0 + jnp.tanh(math.sqrt(2.0 / math.pi) * (ac + 0.044715 * (ac * ac * ac))))
        acc += _dot((gl * v).astype(BF16), wd_ref[cs, :])
    gate = jnp.where(rowm < ctx_len, mod_ref[0, 5:6, :], mod_ref[0, 13:14, :])
    o_ref[0] = x + gate * acc


def _ffn(xs, mod, g, wa, wv, cw, cb, wd, *, ctx_len, tm, chunks):
    bsz, s_len, d = xs.shape
    d_ff = wa.shape[1]
    nh = tm // HALO
    last = s_len // HALO - 1
    kern = functools.partial(_ffn_kernel, tm=tm, chunks=chunks, ctx_len=ctx_len, s_len=s_len)
    const = lambda shape: pl.BlockSpec(shape, lambda b, i: (0, 0), pipeline_mode=pl.Buffered(1))
    return pl.pallas_call(
        kern,
        out_shape=jax.ShapeDtypeStruct(xs.shape, F32),
        grid=(bsz, s_len // tm),
        in_specs=[
            pl.BlockSpec((1, tm, d), lambda b, i: (b, i, 0)),
            pl.BlockSpec((1, HALO, d), lambda b, i: (b, jnp.maximum(i * nh - 1, 0), 0)),
            pl.BlockSpec((1, HALO, d), lambda b, i: (b, jnp.minimum((i + 1) * nh, last), 0)),
            pl.BlockSpec((1, 16, d), lambda b, i: (b, 0, 0)),
            pl.BlockSpec((1, d), lambda b, i: (0, 0)),
            const((d, d_ff)), const((d, d_ff)), const((3, d_ff)), const((1, d_ff)), const((d_ff, d)),
        ],
        out_specs=pl.BlockSpec((1, tm, d), lambda b, i: (b, i, 0)),
        compiler_params=_cparams(("parallel", "parallel")),
        name="conv_glu",
    )(xs, xs, xs, mod, g, wa, wv, cw, cb, wd)


def _cis(num, den):
    ang = num.astype(F32) * (2.0 * math.pi / den)
    return jnp.cos(ang), jnp.sin(ang)


def _blockmat(re, im):
    return jnp.concatenate([jnp.concatenate([re, -im], axis=-1), jnp.concatenate([im, re], axis=-1)], axis=-2)


def _dft_tables(n):
    n2 = DFT_N2
    t1n = n // n2
    n1 = 2 * t1n
    big = n1 * n2
    f1 = jnp.arange(n1, dtype=jnp.int32)
    t1 = jnp.arange(t1n, dtype=jnp.int32)
    j2 = jnp.arange(n2, dtype=jnp.int32)
    c, s = _cis((f1[:, None] * t1[None, :]) % n1, n1)
    fwd_sig = _blockmat(c, -s)
    c, s = _cis((f1[:, None] * f1[None, :]) % n1, n1)
    fwd_fil = jnp.concatenate([c, -s], axis=0)[:, :t1n]
    ph = (n1 * j2[None, :, None] * j2[None, None, :] + f1[:, None, None] * j2[None, None, :]) % big
    c, s = _cis(ph, big)
    g = _blockmat(c, -s)
    h = jnp.swapaxes(g, 1, 2)
    c, s = _cis((t1[:, None] * f1[None, :]) % n1, n1)
    inv = _blockmat(c, s) * (1.0 / big)
    return tuple(_split_bf16(m) for m in (fwd_sig, fwd_fil, g, h, inv))


def _dense_dft_tables(n):
    big = 2 * n
    f = jnp.arange(big, dtype=jnp.int32)
    t = jnp.arange(n, dtype=jnp.int32)
    c, s = _cis((f[:, None] * t[None, :]) % big, big)
    fwd_sig = _blockmat(c, -s)
    c, s = _cis((f[:, None] * f[None, :]) % big, big)
    fwd_fil = jnp.concatenate([c, -s], axis=0)
    c, s = _cis((t[:, None] * f[None, :]) % big, big)
    inv = _blockmat(c, s) * (1.0 / big)
    return tuple(_split_bf16(m) for m in (fwd_sig, fwd_fil, inv))


def _filter_features(n):
    pos = jnp.arange(n, dtype=F32)
    t = (pos / max(n - 1, 1))[:, None]
    w = 2.0 * math.pi * pos[:, None] / n
    bands = jnp.linspace(1e-4, HY_BANDS - 1, HY_BANDS, dtype=F32)
    z = jnp.concatenate([t, jnp.cos(bands * w), -jnp.sin(bands * w)], axis=-1)
    return jnp.pad(z, ((0, 0), (0, LANE - HY_EMB_DIM)))


def _circular_filters(h, big):
    n = h.shape[0]
    fwd = jnp.moveaxis(h[:, 0], 1, 0)
    bwd = jnp.moveaxis(h[:, 1], 1, 0)
    zeros = jnp.zeros((fwd.shape[0], big - 2 * n + 1, fwd.shape[2]), F32)
    return jnp.concatenate([fwd, zeros, bwd[:, :0:-1]], axis=1)


def _pick(total, candidates):
    for t in candidates:
        if total % t == 0:
            return t
    raise ValueError(f"no tile for {total}")


def kernel(x, c, ctx, c_ctx, norm1_g, norm2_g, w_mod, b_mod, w_in, qn_a, kn_a, qn_d, kn_d, lam_q1, lam_k1, lam_q2, lam_k2, hy_conv_w, hy_conv_b, hy_fw1, hy_fb1, hy_fw2, hy_fb2, hy_fw3, hy_fb3, hy_freq, hy_bias, g_out, w_out, w_up, ffn_conv_w, ffn_conv_b, w_down):
    bsz, n_tok, d = x.shape
    ctx_len = ctx.shape[1]
    depth = w_in.shape[0]
    s_len = ctx_len + n_tok
    d_ff = w_down.shape[1]
    assert bsz % 2 == 0 and n_tok % (DFT_N2 * 8) == 0 and ctx_len % LANE == 0

    tm = _pick(s_len, (768, 512, 256, 128))
    tq = _pick(ctx_len, (256, 128))
    tk = _pick(s_len, (8448, 2816, 768, 256))
    n_ff_tiles = -(-d_ff // FF_TILE)
    bounds = [min(d_ff, FF_TILE * ((n_ff_tiles * j + FF_CHUNKS - 1) // FF_CHUNKS)) for j in range(FF_CHUNKS + 1)]
    ff_chunks = tuple(sorted(set(bounds)))

    n_rows = n_tok // GRID_W
    rows = jnp.repeat(jnp.arange(n_rows, dtype=F32), GRID_W)
    cols = jnp.tile(jnp.arange(GRID_W, dtype=F32), n_rows)
    n_freq = HEAD_DIM // 4
    inv = ROPE_THETA ** (-jnp.arange(n_freq, dtype=F32) / n_freq)
    ang = jnp.concatenate([rows[:, None] * inv, cols[:, None] * inv], axis=-1)
    ang = jnp.concatenate([jnp.zeros((ctx_len, HALF), F32), ang], axis=0)
    cosT, sinT = jnp.cos(ang).T, jnp.sin(ang).T

    sig_f, fil_f, g_tab, h_tab, inv_f = _dft_tables(n_tok)
    csig_f, cfil_f, cinv_f = _dense_dft_tables(ctx_len)
    z_lat = _filter_features(n_tok)
    z_ctx = _filter_features(ctx_len)
    max_decay = math.log(HY_TARGET) / HY_FAST_DECAY
    min_decay = math.log(HY_TARGET) / HY_SLOW_DECAY
    absd = jnp.tile(jnp.abs(jnp.linspace(min_decay, max_decay, HY_WIDTH, dtype=F32)), 4)[None, :]
    t1n = n_tok // DFT_N2
    n1 = 2 * t1n
    big = n1 * DFT_N2
    fb = 8

    perm = np.concatenate([np.arange(0, HEAD_DIM, 2), np.arange(1, HEAD_DIM, 2)])
    qk_cols = np.concatenate(
        [OFF_QA + b * HEAD_DIM + perm for b in range(A_HEADS)]
        + [OFF_QD + b * HEAD_DIM + perm for b in range(2 * D_HEADS)]
        + [OFF_KA + b * HEAD_DIM + perm for b in range(A_KV_HEADS)]
        + [OFF_KD + b * HEAD_DIM + perm for b in range(2 * D_HEADS)])
    v_cols = np.concatenate([np.arange(OFF_VA, OFF_VA + A_KV_WIDTH), np.arange(OFF_VD, OFF_VD + D_WIDTH)])
    qkv_cols = np.concatenate([qk_cols, v_cols])
    wqT = jnp.swapaxes(w_in[:, :, qkv_cols], 1, 2).astype(BF16)
    why = w_in[:, :, OFF_HY:OFF_HY + HY_IN].astype(BF16)
    q_scale = HEAD_DIM ** -0.5 * math.log2(math.e)
    gains = jnp.concatenate(
        [jnp.tile(qn_a[:, perm], (1, A_HEADS)) * q_scale, jnp.tile(qn_d[:, perm], (1, 2 * D_HEADS)) * q_scale,
         jnp.tile(kn_a[:, perm], (1, A_KV_HEADS)), jnp.tile(kn_d[:, perm], (1, 2 * D_HEADS))],
        axis=1)[:, :, None].astype(F32)
    w_out_b = w_out.astype(BF16)
    w_up_a = w_up[:, :, :d_ff].astype(BF16)
    w_up_v = w_up[:, :, d_ff:].astype(BF16)
    w_down_b = w_down.astype(BF16)
    fw1p = jnp.pad(hy_fw1, ((0, 0), (0, LANE - HY_EMB_DIM), (0, 0)))
    lamv = jnp.stack([lam_q1, lam_k1, lam_q2, lam_k2], axis=1)
    lam_init = jnp.asarray([0.8 - 0.6 * math.exp(-0.3 * i) for i in range(depth)], F32)
    lami = jnp.broadcast_to(lam_init[:, None, None], (depth, 8, LANE))

    n_c = 8 * ((bsz + 1 + 7) // 8)
    cvec = jnp.concatenate([c, c_ctx[None, :], jnp.zeros((n_c - bsz - 1, d), F32)], axis=0)
    mods = _modulation(cvec, w_mod, b_mod).reshape(depth, n_c, 6, d)
    pad2 = jnp.zeros((depth, bsz, 2, d), F32)
    mod_all = jnp.concatenate(
        [jnp.broadcast_to(mods[:, bsz:bsz + 1], (depth, bsz, 6, d)), pad2, mods[:, :bsz], pad2], axis=2)

    xs = jnp.concatenate([ctx, x], axis=1)
    for i in range(depth):
        mod = mod_all[i]
        qT, k, vTa, vTd, hy = _projection(xs, mod, norm1_g[i][None, :], wqT[i], why[i], gains[i], cosT, sinT,
                                    ctx_len=ctx_len, tm=tm)
        ya = _attention_a(qT, k, vTa, g_out[i, :A_WIDTH, None], ctx_len=ctx_len, tq=tq, tk=tk)
        yd = _attention_d(qT, k, vTd, g_out[i, A_WIDTH + HY_WIDTH:].reshape(D_HEADS, D_VALUE_DIM, 1),
                          lamv[i], lami[i], ctx_len=ctx_len, tq=tq, tk=tk)

        u_lat, u_ctx = _hyena_short_conv(hy, hy_conv_w[i], hy_conv_b[i], ctx_len=ctx_len)
        fil_args = (fw1p[i], hy_fb1[i][None], hy_fw2[i], hy_fb2[i][None], hy_fw3[i], hy_fb3[i][None],
                    hy_freq[i][None], absd)
        h_lat, sum_lat = _hyena_filters(z_lat, *fil_args)
        h_ctx, sum_ctx = _hyena_filters(z_ctx, *fil_args)
        ss_lat = jnp.swapaxes(sum_lat.reshape(2, 2, HY_WIDTH), 0, 1)
        ss_ctx = jnp.swapaxes(sum_ctx.reshape(2, 2, HY_WIDTH), 0, 1)
        h_ctx = jnp.transpose(h_ctx, (2, 1, 0, 3)).reshape(ctx_len, 2, 2, HY_WIDTH)
        k_ctx = _circular_filters(h_ctx, 2 * ctx_len)

        ak = _outer_forward(fil_f[0], fil_f[1], h_lat.reshape(1, 2, 4, t1n, DFT_N2, LANE), 0, 3)
        kf = _filter_spectrum(g_tab[0], g_tab[1], ak, h_lat, ss_lat, fb=fb)
        p_n = bsz // 2
        u6 = u_lat.reshape(3, 2, p_n, 2 * t1n, DFT_N2, LANE)

        def long_conv(z6, g_z, g_gate, order):
            a = _outer_forward(sig_f[0], sig_f[1], z6, g_z, 1)
            cc = _spectral_multiply(g_tab[0], h_tab[0], kf, a, order, fb=fb)
            return _outer_inverse(inv_f[0], cc, u6, g_gate, z6, g_z, hy_bias[i, order][None, :])

        z1 = long_conv(u6, 0, 1, 0)
        yh2 = long_conv(z1[None], 0, 2, 1).reshape(2, bsz, n_tok, LANE)
        yh_lat = jnp.concatenate([yh2[0], yh2[1]], axis=-1)

        kf_c = _ctx_filter_spectrum(cfil_f[0], cfil_f[1], k_ctx, ss_ctx)
        yh_ctx = _ctx_hyena(csig_f[0], csig_f[1], cinv_f[0], cinv_f[1], kf_c,
                            u_ctx.reshape(3, p_n, 2 * ctx_len, HY_WIDTH), hy_bias[i])
        yh = jnp.concatenate([yh_ctx.reshape(bsz, ctx_len, HY_WIDTH), yh_lat], axis=1)

        xs = _merge(xs, ya, yh, yd, g_out[i, A_WIDTH:A_WIDTH + HY_WIDTH][None, :], w_out_b[i], mod,
                    ctx_len=ctx_len, tm=tm)
        xs = _ffn(xs, mod, norm2_g[i][None, :], w_up_a[i], w_up_v[i], ffn_conv_w[i], ffn_conv_b[i][None, :],
                  w_down_b[i], ctx_len=ctx_len, tm=tm, chunks=ff_chunks)
    return xs[:, ctx_len:]
```

```python
import functools
import math

import numpy as np
import jax
import jax.numpy as jnp
from jax import lax
from jax.experimental import pallas as pl
from jax.experimental.pallas import tpu as pltpu

F32 = jnp.float32
BF16 = jnp.bfloat16

HEAD_DIM = 64
HALF = HEAD_DIM // 2
A_HEADS = 4
A_KV_HEADS = 2
A_WIDTH = A_HEADS * HEAD_DIM
A_KV_WIDTH = A_KV_HEADS * HEAD_DIM
HY_WIDTH = 256
HY_IN = 3 * HY_WIDTH
HY_BANDS = 16
HY_EMB_DIM = 1 + 2 * HY_BANDS
HY_HIDDEN = 64
HY_FILTER_CH = 4 * HY_WIDTH
HY_FAST_DECAY = 0.3
HY_SLOW_DECAY = 1.5
HY_TARGET = 1e-2
D_HEADS = 4
D_VALUE_DIM = 2 * HEAD_DIM
D_WIDTH = D_HEADS * D_VALUE_DIM
D_QK_WIDTH = D_HEADS * 2 * HEAD_DIM
MIX_WIDTH = A_WIDTH + HY_WIDTH + D_WIDTH
OFF_QA = 0
OFF_KA = A_WIDTH
OFF_VA = OFF_KA + A_KV_WIDTH
OFF_HY = OFF_VA + A_KV_WIDTH
OFF_QD = OFF_HY + HY_IN
OFF_KD = OFF_QD + D_QK_WIDTH
OFF_VD = OFF_KD + D_QK_WIDTH
GRID_W = 64
ROPE_THETA = 10000.0
EPS = 1e-6

N_QBLK = A_HEADS + 2 * D_HEADS
N_KBLK = A_KV_HEADS + 2 * D_HEADS
Q_ROWS = N_QBLK * HEAD_DIM
K_ROWS = N_KBLK * HEAD_DIM
V_ROWS = A_KV_WIDTH + D_WIDTH
QK_ROWS = Q_ROWS + K_ROWS
QKV_ROWS = QK_ROWS + V_ROWS
SUM_ROWS = 16
VA_BLK = HEAD_DIM + SUM_ROWS

LANE = 128
DFT_N2 = 128
KEY_BLOCK = 768
MAX_UNROLLED_CHUNKS = 4
VMEM_LIMIT = 56 * 1024 * 1024
HIGHEST = lax.Precision.HIGHEST


def _cparams(sem):
    return pltpu.CompilerParams(dimension_semantics=sem, vmem_limit_bytes=VMEM_LIMIT)


def _split_bf16(x):
    hi = x.astype(BF16)
    lo = (x - hi.astype(F32)).astype(BF16)
    return hi, lo


def _dot(a, b):
    return jnp.dot(a, b, preferred_element_type=F32)


def _dot3(a_hi, a_lo, b):
    b_hi, b_lo = _split_bf16(b)
    return _dot(a_hi, b_hi) + (_dot(a_hi, b_lo) + _dot(a_lo, b_hi))


def _dot_p(a_hi, a_lo, b, passes):
    return _dot3(a_hi, a_lo, b) if passes == 3 else _dot(a_hi, b.astype(BF16))


def _mod_kernel(c_ref, w_ref, b_ref, o_ref):
    c = c_ref[...]
    s = c * (1.0 / (1.0 + jnp.exp(-c)))
    o_ref[0] = jnp.dot(s, w_ref[0], preferred_element_type=F32, precision=HIGHEST) + b_ref[0]


def _modulation(cvec, w_mod, b_mod):
    depth, d, n6 = w_mod.shape
    tn = 1024
    return pl.pallas_call(
        _mod_kernel,
        out_shape=jax.ShapeDtypeStruct((depth, cvec.shape[0], n6), F32),
        grid=(depth, n6 // tn),
        in_specs=[
            pl.BlockSpec(cvec.shape, lambda l, j: (0, 0)),
            pl.BlockSpec((1, d, tn), lambda l, j: (l, 0, j)),
            pl.BlockSpec((1, 1, tn), lambda l, j: (l, 0, j)),
        ],
        out_specs=pl.BlockSpec((1, cvec.shape[0], tn), lambda l, j: (l, 0, j)),
        compiler_params=_cparams(("arbitrary", "arbitrary")),
        name="modulation",
    )(cvec, w_mod, b_mod.reshape(depth, 1, n6))


def _norm_modulate(x, g, mod_ref, is_ctx, which):
    ms = jnp.mean(x * x, axis=-1, keepdims=True)
    xn = x * lax.rsqrt(ms + EPS) * g
    sh = jnp.where(is_ctx, mod_ref[0, 3 * which:3 * which + 1, :], mod_ref[0, 8 + 3 * which:9 + 3 * which, :])
    sc = jnp.where(is_ctx, mod_ref[0, 3 * which + 1:3 * which + 2, :],
                   mod_ref[0, 9 + 3 * which:10 + 3 * which, :])
    return xn * (1.0 + sc) + sh


def _proj_kernel(x_ref, mod_ref, g_ref, wq_ref, why_ref, gain_ref, cos_ref, sin_ref,
                 qT_ref, k_ref, vTa_ref, vTd_ref, hy_ref, *, tm, ctx_len):
    i = pl.program_id(1)
    row = i * tm + lax.broadcasted_iota(jnp.int32, (tm, 1), 0)
    h = _norm_modulate(x_ref[0], g_ref[...], mod_ref, row < ctx_len, 0).astype(BF16)
    pT = lax.dot_general(wq_ref[...], h, (((1,), (1,)), ((), ())), preferred_element_type=F32)
    hy_ref[0] = _dot(h, why_ref[...])
    c = cos_ref[...]
    s = sin_ref[...]
    blocks = []
    for b in range(N_QBLK + N_KBLK):
        blk = pT[b * HEAD_DIM:(b + 1) * HEAD_DIM]
        ssq = jnp.sum(blk * blk, axis=0, keepdims=True)
        blk = blk * lax.rsqrt(ssq * (1.0 / HEAD_DIM) + EPS) * gain_ref[b * HEAD_DIM:(b + 1) * HEAD_DIM, :]
        x1 = blk[:HALF]
        x2 = blk[HALF:]
        blocks.append(x1 * c - x2 * s)
        blocks.append(x1 * s + x2 * c)
    qT_ref[0] = jnp.concatenate(blocks[:2 * N_QBLK], axis=0).astype(BF16)
    kT = jnp.concatenate(blocks[2 * N_QBLK:], axis=0)
    k_ref[0] = kT.T.astype(BF16)
    ones = jnp.ones((SUM_ROWS, tm), BF16)
    va = [pT[QK_ROWS + g * HEAD_DIM:QK_ROWS + (g + 1) * HEAD_DIM].astype(BF16) for g in range(A_KV_HEADS)]
    vTa_ref[0] = jnp.concatenate([t for v in va for t in (v, ones)], axis=0)
    vTd_ref[0] = pT[QK_ROWS + A_KV_WIDTH:].astype(BF16)


def _projection(xs, mod, g, wqT, why, gains, cosT, sinT, *, ctx_len, tm):
    bsz, s_len, d = xs.shape
    kern = functools.partial(_proj_kernel, tm=tm, ctx_len=ctx_len)
    return pl.pallas_call(
        kern,
        out_shape=(
            jax.ShapeDtypeStruct((bsz, Q_ROWS, s_len), BF16),
            jax.ShapeDtypeStruct((bsz, s_len, K_ROWS), BF16),
            jax.ShapeDtypeStruct((bsz, A_KV_HEADS * VA_BLK, s_len), BF16),
            jax.ShapeDtypeStruct((bsz, D_WIDTH, s_len), BF16),
            jax.ShapeDtypeStruct((bsz, s_len, HY_IN), F32),
        ),
        grid=(bsz, s_len // tm),
        in_specs=[
            pl.BlockSpec((1, tm, d), lambda b, i: (b, i, 0)),
            pl.BlockSpec((1, 16, d), lambda b, i: (b, 0, 0)),
            pl.BlockSpec((1, d), lambda b, i: (0, 0)),
            pl.BlockSpec((QKV_ROWS, d), lambda b, i: (0, 0)),
            pl.BlockSpec((d, HY_IN), lambda b, i: (0, 0)),
            pl.BlockSpec((QK_ROWS, 1), lambda b, i: (0, 0)),
            pl.BlockSpec((HALF, tm), lambda b, i: (0, i)),
            pl.BlockSpec((HALF, tm), lambda b, i: (0, i)),
        ],
        out_specs=(
            pl.BlockSpec((1, Q_ROWS, tm), lambda b, i: (b, 0, i)),
            pl.BlockSpec((1, tm, K_ROWS), lambda b, i: (b, i, 0)),
            pl.BlockSpec((1, A_KV_HEADS * VA_BLK, tm), lambda b, i: (b, 0, i)),
            pl.BlockSpec((1, D_WIDTH, tm), lambda b, i: (b, 0, i)),
            pl.BlockSpec((1, tm, HY_IN), lambda b, i: (b, i, 0)),
        ),
        compiler_params=_cparams(("parallel", "parallel")),
        name="projection",
    )(xs, mod, g, wqT, why, gains, cosT, sinT)


def _attend(units, scr, *, ctx_only, ctx_len, tk, sum_rows):
    n_u = len(units)
    qz, m_run, l_run, acc, s_buf, cm_buf, qz0 = _split_scratch(scr, n_u)
    s_len = units[0][0].shape[1]
    for u in range(n_u):
        m_run[u][...] = jnp.full(m_run[u].shape, -1e30, F32)
        l_run[u][...] = jnp.zeros(l_run[u].shape, F32)
        acc[u][...] = jnp.zeros(acc[u].shape, F32)

    def fold8(s, op):
        return op(s.reshape(s.shape[0] // 8, 8, s.shape[1]), axis=0)

    def key_rows(start, j, rb):
        if isinstance(start, int):
            return pl.ds(start + j * rb, rb)
        return pl.ds(pl.multiple_of(start + j * rb, rb), rb)

    def advance(nxt, cur, size):
        rb = min(size, KEY_BLOCK)
        if cur is not None:
            u_c, start_c, slot_c = cur
            _, vT_ref, v0, dv = units[u_c]
            m = m_run[u_c][...]
            m_new = jnp.maximum(m, jnp.max(cm_buf[slot_c][...], axis=0, keepdims=True))
            m_run[u_c][...] = m_new
            alpha = jnp.exp2(m - m_new)
        cmax, psum, pv = None, None, None
        for j in range(size // rb):
            rows = slice(j * rb, (j + 1) * rb)
            if nxt is not None:
                u_n, start_n, slot_n = nxt[:3]
                q_n = qz[u_n][...] if len(nxt) == 3 else qz0[nxt[3]]
                s = _dot(units[u_n][0][0, key_rows(start_n, j, rb), :], q_n)
                s_buf[slot_n][rows, :] = s
                cmax = fold8(s, jnp.max) if cmax is None else jnp.maximum(cmax, fold8(s, jnp.max))
            if cur is not None:
                p = jnp.exp2(s_buf[slot_c][rows, :] - m_new)
                if not sum_rows:
                    psum = fold8(p, jnp.sum) if psum is None else psum + fold8(p, jnp.sum)
                d = _dot(vT_ref[0, v0:v0 + dv, key_rows(start_c, j, rb)], p.astype(BF16))
                pv = d if pv is None else pv + d
        if nxt is not None:
            cm_buf[slot_n][...] = cmax
        if cur is not None:
            if not sum_rows:
                l_run[u_c][...] = alpha * l_run[u_c][...] + jnp.sum(psum, axis=0, keepdims=True)
            acc[u_c][...] = alpha * acc[u_c][...] + pv

    if ctx_only:
        for u in range(n_u):
            advance((u, 0, u % 2), None, ctx_len)
            advance(None, (u, 0, u % 2), ctx_len)
        advance((0, 0, 0, 1), None, tk)
    else:
        n_chunks = s_len // tk

        def chunk_steps(c, start, nxt, wrap):
            for u in range(n_u):
                task_n = (u + 1, start, (u + 1) % 2) if u + 1 < n_u else (0, nxt, (u + 1) % 2, wrap)
                advance(task_n, (u, start, u % 2), tk)

        if n_chunks <= MAX_UNROLLED_CHUNKS:
            for c in range(n_chunks):
                last = c == n_chunks - 1
                chunk_steps(c, c * tk, 0 if last else (c + 1) * tk, int(last))
        else:
            def body(c, carry):
                wrap = jnp.where(c == n_chunks - 1, 1, 0).astype(jnp.int32)
                chunk_steps(c, pl.multiple_of(c * tk, tk), pl.multiple_of((1 - wrap) * (c + 1) * tk, tk), wrap)
                return carry
            lax.fori_loop(0, n_chunks, body, 0)
    outs = []
    for u in range(n_u):
        dv = units[u][3] - sum_rows
        l = acc[u][dv:dv + 1, :] if sum_rows else l_run[u][...]
        outs.append(acc[u][:dv, :] * (1.0 / l))
    return outs


def _attn_scratch(n_u, rows, tq, tk):
    per_unit = [((LANE, tq), BF16), ((1, tq), F32), ((1, tq), F32), ((rows, tq), F32)]
    per_slot = [((tk, tq), F32), ((8, tq), F32)]
    return ([pltpu.VMEM(s, d) for s, d in per_unit for _ in range(n_u)]
            + [pltpu.VMEM(s, d) for s, d in per_slot for _ in range(2)]
            + [pltpu.VMEM((2, LANE, tq), BF16)])


def _split_scratch(scr, n_u):
    groups = [scr[i * n_u:(i + 1) * n_u] for i in range(4)]
    rest = scr[4 * n_u:]
    return groups + [rest[0:2], rest[2:4], rest[4]]


def _run_ctx_or_latent(qi, n_ctx_tiles, run):
    @pl.when(qi < n_ctx_tiles)
    def _():
        run(True)

    @pl.when(qi >= n_ctx_tiles)
    def _():
        run(False)


def _attn_a_kernel(qT_ref, qn_ref, k_ref, vT_ref, g_ref, o_ref, *scr, ctx_len, tq, tk):
    zeros = jnp.zeros((HEAD_DIM, tq), BF16)
    group = A_HEADS // A_KV_HEADS
    for h in range(A_HEADS):
        q = qT_ref[0, h * HEAD_DIM:(h + 1) * HEAD_DIM, :]
        scr[h][...] = jnp.concatenate([q, zeros] if h // group == 0 else [zeros, q], axis=0)
    scr[-1][0] = jnp.concatenate([qT_ref[0, :HEAD_DIM, :], zeros], axis=0)
    scr[-1][1] = jnp.concatenate([qn_ref[0, :HEAD_DIM, :], zeros], axis=0)
    units = [(k_ref, vT_ref, (h // group) * VA_BLK, VA_BLK) for h in range(A_HEADS)]

    def run(ctx_only):
        outs = _attend(units, scr, ctx_only=ctx_only, ctx_len=ctx_len, tk=tk, sum_rows=SUM_ROWS)
        y = jnp.concatenate(outs, axis=0)
        ms = jnp.mean(y * y, axis=0, keepdims=True)
        y = y * lax.rsqrt(ms + EPS) * g_ref[...]
        o_ref[0] = y.T.astype(BF16)

    _run_ctx_or_latent(pl.program_id(1), ctx_len // tq, run)


def _attention_a(qT, k, vT, g_a, *, ctx_len, tq, tk):
    bsz, _, s_len = qT.shape
    kern = functools.partial(_attn_a_kernel, ctx_len=ctx_len, tq=tq, tk=tk)
    return pl.pallas_call(
        kern,
        out_shape=jax.ShapeDtypeStruct((bsz, s_len, A_WIDTH), BF16),
        grid=(bsz, s_len // tq),
        in_specs=[
            pl.BlockSpec((1, A_WIDTH, tq), lambda b, i: (b, 0, i)),
            pl.BlockSpec((1, A_WIDTH, tq), lambda b, i: (b, 0, jnp.minimum(i + 1, s_len // tq - 1))),
            pl.BlockSpec((1, s_len, LANE), lambda b, i: (b, 0, 0)),
            pl.BlockSpec((1, A_KV_HEADS * VA_BLK, s_len), lambda b, i: (b, 0, 0)),
            pl.BlockSpec((A_WIDTH, 1), lambda b, i: (0, 0)),
        ],
        out_specs=pl.BlockSpec((1, tq, A_WIDTH), lambda b, i: (b, i, 0)),
        scratch_shapes=_attn_scratch(A_HEADS, VA_BLK, tq, tk),
        compiler_params=_cparams(("parallel", "arbitrary")),
        name="attention_gqa",
    )(qT, qT, k, vT, g_a)


D_HEADS_PER_STEP = 2


def _attn_d_kernel(*refs, ctx_len, tq, tk):
    n_h = D_HEADS_PER_STEP
    q_refs, k_refs, v_refs = refs[:n_h], refs[n_h:2 * n_h], refs[2 * n_h:3 * n_h]
    qn_ref, g_ref, lamv_ref, lami_ref, o_ref = refs[3 * n_h:3 * n_h + 5]
    scr = refs[3 * n_h + 5:]
    zeros = jnp.zeros((HEAD_DIM, tq), BF16)
    lv = lamv_ref[...]
    lam_init = lami_ref[0:1, 0:1]
    lam = (jnp.exp(jnp.sum(lv[0:1] * lv[1:2], axis=-1, keepdims=True))
           - jnp.exp(jnp.sum(lv[2:3] * lv[3:4], axis=-1, keepdims=True)) + lam_init)
    units = []
    for h in range(n_h):
        scr[2 * h][...] = jnp.concatenate([q_refs[h][0, :HEAD_DIM, :], zeros], axis=0)
        scr[2 * h + 1][...] = jnp.concatenate([zeros, q_refs[h][0, HEAD_DIM:, :]], axis=0)
        units += [(k_refs[h], v_refs[h], 0, D_VALUE_DIM)] * 2
    scr[-1][0] = jnp.concatenate([q_refs[0][0, :HEAD_DIM, :], zeros], axis=0)
    scr[-1][1] = jnp.concatenate([qn_ref[0, :HEAD_DIM, :], zeros], axis=0)

    def run(ctx_only):
        outs = _attend(units, scr, ctx_only=ctx_only, ctx_len=ctx_len, tk=tk, sum_rows=0)
        ys = []
        for h in range(n_h):
            y = outs[2 * h] - lam * outs[2 * h + 1]
            ms = jnp.mean(y * y, axis=0, keepdims=True)
            ys.append(y * lax.rsqrt(ms + EPS) * g_ref[h] * (1.0 - lam_init))
        o_ref[0] = jnp.concatenate(ys, axis=0).T.astype(BF16)

    _run_ctx_or_latent(pl.program_id(2), ctx_len // tq, run)


def _attention_d(qT, k, vT, g_d, lamv, lami, *, ctx_len, tq, tk):
    bsz, _, s_len = qT.shape
    n_h = D_HEADS_PER_STEP
    kern = functools.partial(_attn_d_kernel, ctx_len=ctx_len, tq=tq, tk=tk)
    q_blk0 = A_WIDTH // LANE
    k_blk0 = A_KV_WIDTH // LANE
    q_specs = [pl.BlockSpec((1, LANE, tq), lambda b, hp, i, j=j: (b, q_blk0 + n_h * hp + j, i)) for j in range(n_h)]
    k_specs = [pl.BlockSpec((1, s_len, LANE), lambda b, hp, i, j=j: (b, 0, k_blk0 + n_h * hp + j)) for j in range(n_h)]
    v_specs = [pl.BlockSpec((1, D_VALUE_DIM, s_len), lambda b, hp, i, j=j: (b, n_h * hp + j, 0)) for j in range(n_h)]
    return pl.pallas_call(
        kern,
        out_shape=jax.ShapeDtypeStruct((bsz, s_len, D_WIDTH), BF16),
        grid=(bsz, D_HEADS // n_h, s_len // tq),
        in_specs=q_specs + k_specs + v_specs + [
            pl.BlockSpec((1, LANE, tq), lambda b, hp, i: (b, q_blk0 + n_h * hp, jnp.minimum(i + 1, s_len // tq - 1))),
            pl.BlockSpec((n_h, D_VALUE_DIM, 1), lambda b, hp, i: (hp, 0, 0)),
            pl.BlockSpec((4, HEAD_DIM), lambda b, hp, i: (0, 0)),
            pl.BlockSpec((8, LANE), lambda b, hp, i: (0, 0)),
        ],
        out_specs=pl.BlockSpec((1, tq, n_h * D_VALUE_DIM), lambda b, hp, i: (b, i, hp)),
        scratch_shapes=_attn_scratch(2 * n_h, D_VALUE_DIM, tq, tk),
        compiler_params=_cparams(("parallel", "parallel", "arbitrary")),
        name="attention_diff",
    )(*([qT] * n_h + [k] * n_h + [vT] * n_h + [qT, g_d, lamv, lami]))


def _dwconv3_seq(u, w_ref, b_ref):
    n = u.shape[0]
    row = lax.broadcasted_iota(jnp.int32, (n, 1), 0)
    prev = jnp.where(row == 0, 0.0, pltpu.roll(u, 1, axis=0))
    nxt = jnp.where(row == n - 1, 0.0, pltpu.roll(u, n - 1, axis=0))
    return prev * w_ref[0:1, :] + u * w_ref[1:2, :] + nxt * w_ref[2:3, :] + b_ref[...]


def _hy_conv_kernel(hy_ref, w_ref, b_ref, lat_ref, ctx_ref, *, ctx_len):
    ctx_ref[0, 0] = _dwconv3_seq(hy_ref[0, :ctx_len, :], w_ref, b_ref)
    lat_ref[0, 0, 0] = _dwconv3_seq(hy_ref[0, ctx_len:, :], w_ref, b_ref)


def _hyena_short_conv(hy, w, b, *, ctx_len):
    bsz, s_len, _ = hy.shape
    n = s_len - ctx_len
    per = HY_WIDTH // LANE
    kern = functools.partial(_hy_conv_kernel, ctx_len=ctx_len)
    return pl.pallas_call(
        kern,
        out_shape=(
            jax.ShapeDtypeStruct((3, per, bsz, n, LANE), F32),
            jax.ShapeDtypeStruct((3, bsz, ctx_len, HY_WIDTH), F32),
        ),
        grid=(bsz, HY_IN // LANE),
        in_specs=[
            pl.BlockSpec((1, s_len, LANE), lambda b, j: (b, 0, j)),
            pl.BlockSpec((3, LANE), lambda b, j: (0, j)),
            pl.BlockSpec((1, LANE), lambda b, j: (0, j)),
        ],
        out_specs=(
            pl.BlockSpec((1, 1, 1, n, LANE), lambda b, j: (j // per, j % per, b, 0, 0)),
            pl.BlockSpec((1, 1, ctx_len, LANE), lambda b, j: (j // per, b, 0, j % per)),
        ),
        compiler_params=_cparams(("parallel", "parallel")),
        name="hyena_short_conv",
    )(hy, w, b.reshape(1, HY_IN))


def _filter_kernel(z_ref, w1_ref, b1_ref, w2_ref, b2_ref, w3_ref, b3_ref, fr_ref, ad_ref, h_ref, sum_ref):
    i = pl.program_id(0)
    z = z_ref[...]
    fr = fr_ref[...]
    h = jnp.sin(fr * (jnp.dot(z, w1_ref[...], preferred_element_type=F32, precision=HIGHEST) + b1_ref[...]))
    h = jnp.sin(fr * (jnp.dot(h, w2_ref[...], preferred_element_type=F32, precision=HIGHEST) + b2_ref[...]))
    h = jnp.dot(h, w3_ref[...], preferred_element_type=F32, precision=HIGHEST) + b3_ref[...]
    h = h * jnp.exp(-z[:, 0:1] * ad_ref[...])
    for q in range(HY_FILTER_CH // HY_WIDTH):
        for hh in range(HY_WIDTH // LANE):
            c0 = q * HY_WIDTH + hh * LANE
            h_ref[hh, q] = h[:, c0:c0 + LANE]

    @pl.when(i == 0)
    def _():
        sum_ref[...] = jnp.zeros_like(sum_ref)

    sum_ref[...] += jnp.sum(jnp.abs(h), axis=0, keepdims=True)


def _hyena_filters(zfeat, w1, b1, w2, b2, w3, b3, freq, absd):
    n, kz = zfeat.shape
    tn = min(n, 1024)
    full = lambda a: pl.BlockSpec(a.shape, lambda i: (0, 0))
    return pl.pallas_call(
        _filter_kernel,
        out_shape=(
            jax.ShapeDtypeStruct((HY_WIDTH // LANE, HY_FILTER_CH // HY_WIDTH, n, LANE), F32),
            jax.ShapeDtypeStruct((1, HY_FILTER_CH), F32),
        ),
        grid=(n // tn,),
        in_specs=[pl.BlockSpec((tn, kz), lambda i: (i, 0)), full(w1), full(b1), full(w2), full(b2),
                  full(w3), full(b3), full(freq), full(absd)],
        out_specs=(
            pl.BlockSpec((HY_WIDTH // LANE, HY_FILTER_CH // HY_WIDTH, tn, LANE), lambda i: (0, 0, i, 0)),
            pl.BlockSpec((1, HY_FILTER_CH), lambda i: (0, 0)),
        ),
        compiler_params=_cparams(("arbitrary",)),
        name="hyena_filters",
    )(zfeat, w1, b1, w2, b2, w3, b3, freq, absd)


T2_BLOCK = 8


def _outer_fwd_kernel(mh_ref, ml_ref, x_ref, o_ref, *, n1, passes):
    rows = x_ref.shape[3]
    xs = [x_ref.at[0, h, 0].reshape(rows * T2_BLOCK, LANE) for h in range(2)]
    outs = [[o_ref.at[0, ri, h].reshape(n1 * T2_BLOCK, LANE) for h in range(2)] for ri in range(2)]
    for t in range(T2_BLOCK):
        sel = pl.ds(t, rows, stride=T2_BLOCK)
        y = _dot_p(mh_ref[...], ml_ref[...], jnp.concatenate([xs[0][sel, :], xs[1][sel, :]], axis=1), passes)
        for ri in range(2):
            for h in range(2):
                outs[ri][h][pl.ds(t, n1, stride=T2_BLOCK), :] = y[ri * n1:(ri + 1) * n1, h * LANE:(h + 1) * LANE]


def _outer_forward(m_hi, m_lo, x6, g, passes):
    _, _, p_n, rows, n2, _ = x6.shape
    n1 = m_hi.shape[0] // 2
    kern = functools.partial(_outer_fwd_kernel, n1=n1, passes=passes)
    return pl.pallas_call(
        kern,
        out_shape=jax.ShapeDtypeStruct((p_n, 2, 2, n1, n2, LANE), F32),
        grid=(p_n, n2 // T2_BLOCK),
        in_specs=[pl.BlockSpec(m_hi.shape, lambda p, j: (0, 0)), pl.BlockSpec(m_lo.shape, lambda p, j: (0, 0)),
                  pl.BlockSpec((1, 2, 1, rows, T2_BLOCK, LANE), lambda p, j: (g, 0, p, 0, j, 0))],
        out_specs=pl.BlockSpec((1, 2, 2, n1, T2_BLOCK, LANE), lambda p, j: (p, 0, 0, 0, j, 0)),
        compiler_params=_cparams(("parallel", "parallel")),
        name="dft_outer_fwd",
    )(m_hi, m_lo, x6)


def _outer_inv_kernel(mh_ref, c_ref, gate_ref, u_ref, bias_ref, o_ref):
    n1 = c_ref.shape[3]
    rows = o_ref.shape[2]
    cs = [[c_ref.at[0, ri, h].reshape(n1 * T2_BLOCK, LANE) for h in range(2)] for ri in range(2)]
    gates = [gate_ref.at[0, h, 0].reshape(rows * T2_BLOCK, LANE) for h in range(2)]
    us = [u_ref.at[0, h, 0].reshape(rows * T2_BLOCK, LANE) for h in range(2)]
    outs = [o_ref.at[h, 0].reshape(rows * T2_BLOCK, LANE) for h in range(2)]
    for t in range(T2_BLOCK):
        sel_f = pl.ds(t, n1, stride=T2_BLOCK)
        sel_t = pl.ds(t, rows, stride=T2_BLOCK)
        c2 = jnp.concatenate([jnp.concatenate([cs[ri][0][sel_f, :], cs[ri][1][sel_f, :]], axis=1)
                              for ri in range(2)], axis=0)
        y = _dot(mh_ref[...], c2.astype(BF16))
        for h in range(2):
            lanes = slice(h * LANE, (h + 1) * LANE)
            outs[h][sel_t, :] = gates[h][sel_t, :] * (y[:, lanes] + us[h][sel_t, :] * bias_ref[:, lanes])


def _outer_inverse(m_hi, c6, gate6, g_gate, u6, g_u, bias):
    p_n, _, _, n1, n2, _ = c6.shape
    rows = m_hi.shape[0]
    gspec = lambda g: pl.BlockSpec((1, 2, 1, rows, T2_BLOCK, LANE), lambda p, j: (g, 0, p, 0, j, 0))
    return pl.pallas_call(
        _outer_inv_kernel,
        out_shape=jax.ShapeDtypeStruct((2, p_n, rows, n2, LANE), F32),
        grid=(p_n, n2 // T2_BLOCK),
        in_specs=[pl.BlockSpec(m_hi.shape, lambda p, j: (0, 0)),
                  pl.BlockSpec((1, 2, 2, n1, T2_BLOCK, LANE), lambda p, j: (p, 0, 0, 0, j, 0)),
                  gspec(g_gate), gspec(g_u), pl.BlockSpec((1, 2 * LANE), lambda p, j: (0, 0))],
        out_specs=pl.BlockSpec((2, 1, rows, T2_BLOCK, LANE), lambda p, j: (0, p, 0, j, 0)),
        compiler_params=_cparams(("parallel", "parallel")),
        name="dft_outer_inv",
    )(m_hi, c6, gate6, u6, bias)


def _halves_to_rows(a_ref, j):
    return jnp.concatenate([jnp.concatenate([a_ref[0, ri, 0, j], a_ref[0, ri, 1, j]], axis=1) for ri in range(2)],
                           axis=0)


def _kf_kernel(gh_ref, gl_ref, af_ref, ab_ref, h0_ref, ss_ref, o_ref, *, fb, n2):
    scale = 1.0 / (ss_ref[0, 0:1, :] + ss_ref[0, 1:2, :] + EPS)
    b0 = jnp.concatenate([h0_ref[0, 0, 0:1, :], h0_ref[1, 0, 0:1, :]], axis=1)
    for j in range(fb):
        xf = _dot3(gh_ref[j], gl_ref[j], _halves_to_rows(af_ref, j))
        xb = _dot3(gh_ref[j], gl_ref[j], _halves_to_rows(ab_ref, j))
        o_ref[0, j, :n2, :] = (xf[:n2] + xb[:n2] - b0) * scale
        o_ref[0, j, n2:, :] = (xf[n2:] - xb[n2:]) * scale


def _filter_spectrum(g_hi, g_lo, a6, h6, ss, *, fb):
    n_q, _, _, n1, n2, _ = a6.shape
    n_ord = n_q // 2
    c = 2 * LANE
    kern = functools.partial(_kf_kernel, fb=fb, n2=n2)
    return pl.pallas_call(
        kern,
        out_shape=jax.ShapeDtypeStruct((n_ord, n1, 2 * n2, c), F32),
        grid=(n1 // fb, n_ord),
        in_specs=[
            pl.BlockSpec((fb, 2 * n2, 2 * n2), lambda f, o: (f, 0, 0)),
            pl.BlockSpec((fb, 2 * n2, 2 * n2), lambda f, o: (f, 0, 0)),
            pl.BlockSpec((1, 2, 2, fb, n2, LANE), lambda f, o: (o, 0, 0, f, 0, 0)),
            pl.BlockSpec((1, 2, 2, fb, n2, LANE), lambda f, o: (n_ord + o, 0, 0, f, 0, 0)),
            pl.BlockSpec((2, 1, 8, LANE), lambda f, o: (0, n_ord + o, 0, 0)),
            pl.BlockSpec((1, 2, c), lambda f, o: (o, 0, 0)),
        ],
        out_specs=pl.BlockSpec((1, fb, 2 * n2, c), lambda f, o: (o, f, 0, 0)),
        compiler_params=_cparams(("parallel", "parallel")),
        name="filter_spectrum",
    )(g_hi, g_lo, a6, a6, h6, ss)


def _spec_kernel(gh_ref, hh_ref, kf_ref, a_ref, o_ref, *, fb, n2):
    for j in range(fb):
        x = _dot(gh_ref[j], _halves_to_rows(a_ref, j).astype(BF16))
        xr, xi = x[:n2], x[n2:]
        kr, ki = kf_ref[0, j, :n2], kf_ref[0, j, n2:]
        y = jnp.concatenate([xr * kr - xi * ki, xr * ki + xi * kr], axis=0)
        c2 = _dot(hh_ref[j], y.astype(BF16))
        for ri in range(2):
            for h in range(2):
                o_ref[0, ri, h, j] = c2[ri * n2:(ri + 1) * n2, h * LANE:(h + 1) * LANE]


def _spectral_multiply(g_hi, h_hi, kf, a6, order, *, fb):
    p_n, _, _, n1, n2, _ = a6.shape
    c = 2 * LANE
    kern = functools.partial(_spec_kernel, fb=fb, n2=n2)
    gspec = pl.BlockSpec((fb, 2 * n2, 2 * n2), lambda f, p: (f, 0, 0))
    aspec = pl.BlockSpec((1, 2, 2, fb, n2, LANE), lambda f, p: (p, 0, 0, f, 0, 0))
    return pl.pallas_call(
        kern,
        out_shape=jax.ShapeDtypeStruct(a6.shape, F32),
        grid=(n1 // fb, p_n),
        in_specs=[gspec, gspec,
                  pl.BlockSpec((1, fb, 2 * n2, c), lambda f, p: (order, f, 0, 0)), aspec],
        out_specs=aspec,
        compiler_params=_cparams(("parallel", "parallel")),
        name="spectral_multiply",
    )(g_hi, h_hi, kf, a6)


def _ctx_kf_kernel(fh_ref, fl_ref, k_ref, ss_ref, o_ref):
    scale = 1.0 / (ss_ref[0, 0:1, :] + ss_ref[0, 1:2, :] + EPS)
    o_ref[0] = _dot3(fh_ref[...], fl_ref[...], k_ref[0]) * scale


def _ctx_filter_spectrum(f_hi, f_lo, k, ss):
    n_ord, nc, c = k.shape
    return pl.pallas_call(
        _ctx_kf_kernel,
        out_shape=jax.ShapeDtypeStruct((n_ord, 2 * nc, c), F32),
        grid=(n_ord,),
        in_specs=[
            pl.BlockSpec(f_hi.shape, lambda o: (0, 0)),
            pl.BlockSpec(f_lo.shape, lambda o: (0, 0)),
            pl.BlockSpec((1, nc, c), lambda o: (o, 0, 0)),
            pl.BlockSpec((1, 2, c), lambda o: (o, 0, 0)),
        ],
        out_specs=pl.BlockSpec((1, 2 * nc, c), lambda o: (o, 0, 0)),
        compiler_params=_cparams(("parallel",)),
        name="ctx_filter_spectrum",
    )(f_hi, f_lo, k, ss)


def _ctx_conv_kernel(fh_ref, fl_ref, eh_ref, el_ref, kf_ref, u_ref, bias_ref, o_ref, *, nc):
    v, x1, x2 = u_ref[0, 0], u_ref[1, 0], u_ref[2, 0]

    def conv(z, order):
        x = _dot3(fh_ref[...], fl_ref[...], z)
        xr, xi = x[:nc], x[nc:]
        kr, ki = kf_ref[order, :nc], kf_ref[order, nc:]
        y = jnp.concatenate([xr * kr - xi * ki, xr * ki + xi * kr], axis=0)
        return _dot3(eh_ref[...], el_ref[...], y) + z * bias_ref[order:order + 1, :]

    o_ref[0] = x2 * conv(x1 * conv(v, 0), 1)


def _ctx_hyena(f_hi, f_lo, e_hi, e_lo, kf, u3, bias):
    _, p_n, n2x, c = u3.shape
    nc = kf.shape[1] // 2
    kern = functools.partial(_ctx_conv_kernel, nc=nc)
    full = lambda a: pl.BlockSpec(a.shape, lambda p: (0,) * a.ndim)
    return pl.pallas_call(
        kern,
        out_shape=jax.ShapeDtypeStruct((p_n, n2x, c), F32),
        grid=(p_n,),
        in_specs=[full(f_hi), full(f_lo), full(e_hi), full(e_lo), full(kf),
                  pl.BlockSpec((3, 1, n2x, c), lambda p: (0, p, 0, 0)), full(bias)],
        out_specs=pl.BlockSpec((1, n2x, c), lambda p: (p, 0, 0)),
        compiler_params=_cparams(("parallel",)),
        name="ctx_hyena",
    )(f_hi, f_lo, e_hi, e_lo, kf, u3, bias)


def _merge_kernel(x_ref, ya_ref, yh_ref, yd_ref, gh_ref, w_ref, mod_ref, o_ref, *, tm, ctx_len):
    i = pl.program_id(1)
    row = i * tm + lax.broadcasted_iota(jnp.int32, (tm, 1), 0)
    yh = yh_ref[0]
    ms = jnp.mean(yh * yh, axis=-1, keepdims=True)
    yh = yh * lax.rsqrt(ms + EPS) * gh_ref[...]
    y = _dot(ya_ref[0].astype(BF16), w_ref[:A_WIDTH, :])
    y += _dot(yh.astype(BF16), w_ref[A_WIDTH:A_WIDTH + HY_WIDTH, :])
    y += _dot(yd_ref[0].astype(BF16), w_ref[A_WIDTH + HY_WIDTH:, :])
    gate = jnp.where(row < ctx_len, mod_ref[0, 2:3, :], mod_ref[0, 10:11, :])
    o_ref[0] = x_ref[0] + gate * y


def _merge(xs, ya, yh, yd, g_h, w_out, mod, *, ctx_len, tm):
    bsz, s_len, d = xs.shape
    kern = functools.partial(_merge_kernel, tm=tm, ctx_len=ctx_len)
    tok = lambda w: pl.BlockSpec((1, tm, w), lambda b, i: (b, i, 0))
    return pl.pallas_call(
        kern,
        out_shape=jax.ShapeDtypeStruct(xs.shape, F32),
        grid=(bsz, s_len // tm),
        in_specs=[tok(d), tok(A_WIDTH), tok(HY_WIDTH), tok(D_WIDTH),
                  pl.BlockSpec((1, HY_WIDTH), lambda b, i: (0, 0)),
                  pl.BlockSpec((MIX_WIDTH, d), lambda b, i: (0, 0)),
                  pl.BlockSpec((1, 16, d), lambda b, i: (b, 0, 0))],
        out_specs=tok(d),
        compiler_params=_cparams(("parallel", "parallel")),
        name="merge",
    )(xs, ya, yh, yd, g_h, w_out, mod)


HALO = 8
FF_TILE = 256
FF_CHUNKS = 2


def _ffn_kernel(x_ref, xp_ref, xn_ref, mod_ref, g_ref, wa_ref, wv_ref, cw_ref, cb_ref, wd_ref, o_ref,
                *, tm, chunks, ctx_len, s_len):
    i = pl.program_id(1)
    x = x_ref[0]
    xa = jnp.concatenate([xp_ref[0], x, xn_ref[0]], axis=0)
    rows = tm + 2 * HALO
    row = i * tm - HALO + lax.broadcasted_iota(jnp.int32, (rows, 1), 0)
    h = _norm_modulate(xa, g_ref[...], mod_ref, row < ctx_len, 1).astype(BF16)
    hm = h[HALO:HALO + tm]
    rowm = row[HALO:HALO + tm]
    has_prev = jnp.logical_and(rowm != 0, rowm != ctx_len)
    has_next = jnp.logical_and(rowm != ctx_len - 1, rowm != s_len - 1)
    def up(j):
        cs = slice(chunks[j], chunks[j + 1])
        return _dot(h, wa_ref[:, cs]), _dot(hm, wv_ref[:, cs])

    n_chunks = len(chunks) - 1
    acc = jnp.zeros((tm, x.shape[1]), F32)
    nxt = up(0)
    for j in range(n_chunks):
        cs = slice(chunks[j], chunks[j + 1])
        a, v = nxt
        if j + 1 < n_chunks:
            nxt = up(j + 1)
        a_prev = jnp.where(has_prev, pltpu.roll(a, 1, axis=0)[HALO:HALO + tm], 0.0)
        a_next = jnp.where(has_next, pltpu.roll(a, rows - 1, axis=0)[HALO:HALO + tm], 0.0)
        ac = a_prev * cw_ref[0:1, cs] + a[HALO:HALO + tm] * cw_ref[1:2, cs] + a_next * cw_ref[2:3, cs] + cb_ref[:, cs]
        gl = 0.5 * ac * (1.0 + jnp.tanh(math.sqrt(2.0 / math.pi) * (ac + 0.044715 * (ac * ac * ac))))
        acc += _dot((gl * v).astype(BF16), wd_ref[cs, :])
    gate = jnp.where(rowm < ctx_len, mod_ref[0, 5:6, :], mod_ref[0, 13:14, :])
    o_ref[0] = x + gate * acc


def _ffn(xs, mod, g, wa, wv, cw, cb, wd, *, ctx_len, tm, chunks):
    bsz, s_len, d = xs.shape
    d_ff = wa.shape[1]
    nh = tm // HALO
    last = s_len // HALO - 1
    kern = functools.partial(_ffn_kernel, tm=tm, chunks=chunks, ctx_len=ctx_len, s_len=s_len)
    const = lambda shape: pl.BlockSpec(shape, lambda b, i: (0, 0), pipeline_mode=pl.Buffered(1))
    return pl.pallas_call(
        kern,
        out_shape=jax.ShapeDtypeStruct(xs.shape, F32),
        grid=(bsz, s_len // tm),
        in_specs=[
            pl.BlockSpec((1, tm, d), lambda b, i: (b, i, 0)),
            pl.BlockSpec((1, HALO, d), lambda b, i: (b, jnp.maximum(i * nh - 1, 0), 0)),
            pl.BlockSpec((1, HALO, d), lambda b, i: (b, jnp.minimum((i + 1) * nh, last), 0)),
            pl.BlockSpec((1, 16, d), lambda b, i: (b, 0, 0)),
            pl.BlockSpec((1, d), lambda b, i: (0, 0)),
            const((d, d_ff)), const((d, d_ff)), const((3, d_ff)), const((1, d_ff)), const((d_ff, d)),
        ],
        out_specs=pl.BlockSpec((1, tm, d), lambda b, i: (b, i, 0)),
        compiler_params=_cparams(("parallel", "parallel")),
        name="conv_glu",
    )(xs, xs, xs, mod, g, wa, wv, cw, cb, wd)


def _cis(num, den):
    ang = num.astype(F32) * (2.0 * math.pi / den)
    return jnp.cos(ang), jnp.sin(ang)


def _blockmat(re, im):
    return jnp.concatenate([jnp.concatenate([re, -im], axis=-1), jnp.concatenate([im, re], axis=-1)], axis=-2)


def _dft_tables(n):
    n2 = DFT_N2
    t1n = n // n2
    n1 = 2 * t1n
    big = n1 * n2
    f1 = jnp.arange(n1, dtype=jnp.int32)
    t1 = jnp.arange(t1n, dtype=jnp.int32)
    j2 = jnp.arange(n2, dtype=jnp.int32)
    c, s = _cis((f1[:, None] * t1[None, :]) % n1, n1)
    fwd_sig = _blockmat(c, -s)
    c, s = _cis((f1[:, None] * f1[None, :]) % n1, n1)
    fwd_fil = jnp.concatenate([c, -s], axis=0)[:, :t1n]
    ph = (n1 * j2[None, :, None] * j2[None, None, :] + f1[:, None, None] * j2[None, None, :]) % big
    c, s = _cis(ph, big)
    g = _blockmat(c, -s)
    h = jnp.swapaxes(g, 1, 2)
    c, s = _cis((t1[:, None] * f1[None, :]) % n1, n1)
    inv = _blockmat(c, s) * (1.0 / big)
    return tuple(_split_bf16(m) for m in (fwd_sig, fwd_fil, g, h, inv))


def _dense_dft_tables(n):
    big = 2 * n
    f = jnp.arange(big, dtype=jnp.int32)
    t = jnp.arange(n, dtype=jnp.int32)
    c, s = _cis((f[:, None] * t[None, :]) % big, big)
    fwd_sig = _blockmat(c, -s)
    c, s = _cis((f[:, None] * f[None, :]) % big, big)
    fwd_fil = jnp.concatenate([c, -s], axis=0)
    c, s = _cis((t[:, None] * f[None, :]) % big, big)
    inv = _blockmat(c, s) * (1.0 / big)
    return tuple(_split_bf16(m) for m in (fwd_sig, fwd_fil, inv))


def _filter_features(n):
    pos = jnp.arange(n, dtype=F32)
    t = (pos / max(n - 1, 1))[:, None]
    w = 2.0 * math.pi * pos[:, None] / n
    bands = jnp.linspace(1e-4, HY_BANDS - 1, HY_BANDS, dtype=F32)
    z = jnp.concatenate([t, jnp.cos(bands * w), -jnp.sin(bands * w)], axis=-1)
    return jnp.pad(z, ((0, 0), (0, LANE - HY_EMB_DIM)))


def _circular_filters(h, big):
    n = h.shape[0]
    fwd = jnp.moveaxis(h[:, 0], 1, 0)
    bwd = jnp.moveaxis(h[:, 1], 1, 0)
    zeros = jnp.zeros((fwd.shape[0], big - 2 * n + 1, fwd.shape[2]), F32)
    return jnp.concatenate([fwd, zeros, bwd[:, :0:-1]], axis=1)


def _pick(total, candidates):
    for t in candidates:
        if total % t == 0:
            return t
    raise ValueError(f"no tile for {total}")


def kernel(x, c, ctx, c_ctx, norm1_g, norm2_g, w_mod, b_mod, w_in, qn_a, kn_a, qn_d, kn_d, lam_q1, lam_k1, lam_q2, lam_k2, hy_conv_w, hy_conv_b, hy_fw1, hy_fb1, hy_fw2, hy_fb2, hy_fw3, hy_fb3, hy_freq, hy_bias, g_out, w_out, w_up, ffn_conv_w, ffn_conv_b, w_down):
    bsz, n_tok, d = x.shape
    ctx_len = ctx.shape[1]
    depth = w_in.shape[0]
    s_len = ctx_len + n_tok
    d_ff = w_down.shape[1]
    assert bsz % 2 == 0 and n_tok % (DFT_N2 * 8) == 0 and ctx_len % LANE == 0

    tm = _pick(s_len, (768, 512, 256, 128))
    tq = _pick(ctx_len, (256, 128))
    tk = _pick(s_len, (8448, 2816, 768, 256))
    n_ff_tiles = -(-d_ff // FF_TILE)
    bounds = [min(d_ff, FF_TILE * ((n_ff_tiles * j + FF_CHUNKS - 1) // FF_CHUNKS)) for j in range(FF_CHUNKS + 1)]
    ff_chunks = tuple(sorted(set(bounds)))

    n_rows = n_tok // GRID_W
    rows = jnp.repeat(jnp.arange(n_rows, dtype=F32), GRID_W)
    cols = jnp.tile(jnp.arange(GRID_W, dtype=F32), n_rows)
    n_freq = HEAD_DIM // 4
    inv = ROPE_THETA ** (-jnp.arange(n_freq, dtype=F32) / n_freq)
    ang = jnp.concatenate([rows[:, None] * inv, cols[:, None] * inv], axis=-1)
    ang = jnp.concatenate([jnp.zeros((ctx_len, HALF), F32), ang], axis=0)
    cosT, sinT = jnp.cos(ang).T, jnp.sin(ang).T

    sig_f, fil_f, g_tab, h_tab, inv_f = _dft_tables(n_tok)
    csig_f, cfil_f, cinv_f = _dense_dft_tables(ctx_len)
    z_lat = _filter_features(n_tok)
    z_ctx = _filter_features(ctx_len)
    max_decay = math.log(HY_TARGET) / HY_FAST_DECAY
    min_decay = math.log(HY_TARGET) / HY_SLOW_DECAY
    absd = jnp.tile(jnp.abs(jnp.linspace(min_decay, max_decay, HY_WIDTH, dtype=F32)), 4)[None, :]
    t1n = n_tok // DFT_N2
    n1 = 2 * t1n
    big = n1 * DFT_N2
    fb = 8

    perm = np.concatenate([np.arange(0, HEAD_DIM, 2), np.arange(1, HEAD_DIM, 2)])
    qk_cols = np.concatenate(
        [OFF_QA + b * HEAD_DIM + perm for b in range(A_HEADS)]
        + [OFF_QD + b * HEAD_DIM + perm for b in range(2 * D_HEADS)]
        + [OFF_KA + b * HEAD_DIM + perm for b in range(A_KV_HEADS)]
        + [OFF_KD + b * HEAD_DIM + perm for b in range(2 * D_HEADS)])
    v_cols = np.concatenate([np.arange(OFF_VA, OFF_VA + A_KV_WIDTH), np.arange(OFF_VD, OFF_VD + D_WIDTH)])
    qkv_cols = np.concatenate([qk_cols, v_cols])
    wqT = jnp.swapaxes(w_in[:, :, qkv_cols], 1, 2).astype(BF16)
    why = w_in[:, :, OFF_HY:OFF_HY + HY_IN].astype(BF16)
    q_scale = HEAD_DIM ** -0.5 * math.log2(math.e)
    gains = jnp.concatenate(
        [jnp.tile(qn_a[:, perm], (1, A_HEADS)) * q_scale, jnp.tile(qn_d[:, perm], (1, 2 * D_HEADS)) * q_scale,
         jnp.tile(kn_a[:, perm], (1, A_KV_HEADS)), jnp.tile(kn_d[:, perm], (1, 2 * D_HEADS))],
        axis=1)[:, :, None].astype(F32)
    w_out_b = w_out.astype(BF16)
    w_up_a = w_up[:, :, :d_ff].astype(BF16)
    w_up_v = w_up[:, :, d_ff:].astype(BF16)
    w_down_b = w_down.astype(BF16)
    fw1p = jnp.pad(hy_fw1, ((0, 0), (0, LANE - HY_EMB_DIM), (0, 0)))
    lamv = jnp.stack([lam_q1, lam_k1, lam_q2, lam_k2], axis=1)
    lam_init = jnp.asarray([0.8 - 0.6 * math.exp(-0.3 * i) for i in range(depth)], F32)
    lami = jnp.broadcast_to(lam_init[:, None, None], (depth, 8, LANE))

    n_c = 8 * ((bsz + 1 + 7) // 8)
    cvec = jnp.concatenate([c, c_ctx[None, :], jnp.zeros((n_c - bsz - 1, d), F32)], axis=0)
    mods = _modulation(cvec, w_mod, b_mod).reshape(depth, n_c, 6, d)
    pad2 = jnp.zeros((depth, bsz, 2, d), F32)
    mod_all = jnp.concatenate(
        [jnp.broadcast_to(mods[:, bsz:bsz + 1], (depth, bsz, 6, d)), pad2, mods[:, :bsz], pad2], axis=2)

    xs = jnp.concatenate([ctx, x], axis=1)
    for i in range(depth):
        mod = mod_all[i]
        qT, k, vTa, vTd, hy = _projection(xs, mod, norm1_g[i][None, :], wqT[i], why[i], gains[i], cosT, sinT,
                                    ctx_len=ctx_len, tm=tm)
        ya = _attention_a(qT, k, vTa, g_out[i, :A_WIDTH, None], ctx_len=ctx_len, tq=tq, tk=tk)
        yd = _attention_d(qT, k, vTd, g_out[i, A_WIDTH + HY_WIDTH:].reshape(D_HEADS, D_VALUE_DIM, 1),
                          lamv[i], lami[i], ctx_len=ctx_len, tq=tq, tk=tk)

        u_lat, u_ctx = _hyena_short_conv(hy, hy_conv_w[i], hy_conv_b[i], ctx_len=ctx_len)
        fil_args = (fw1p[i], hy_fb1[i][None], hy_fw2[i], hy_fb2[i][None], hy_fw3[i], hy_fb3[i][None],
                    hy_freq[i][None], absd)
        h_lat, sum_lat = _hyena_filters(z_lat, *fil_args)
        h_ctx, sum_ctx = _hyena_filters(z_ctx, *fil_args)
        ss_lat = jnp.swapaxes(sum_lat.reshape(2, 2, HY_WIDTH), 0, 1)
        ss_ctx = jnp.swapaxes(sum_ctx.reshape(2, 2, HY_WIDTH), 0, 1)
        h_ctx = jnp.transpose(h_ctx, (2, 1, 0, 3)).reshape(ctx_len, 2, 2, HY_WIDTH)
        k_ctx = _circular_filters(h_ctx, 2 * ctx_len)

        ak = _outer_forward(fil_f[0], fil_f[1], h_lat.reshape(1, 2, 4, t1n, DFT_N2, LANE), 0, 3)
        kf = _filter_spectrum(g_tab[0], g_tab[1], ak, h_lat, ss_lat, fb=fb)
        p_n = bsz // 2
        u6 = u_lat.reshape(3, 2, p_n, 2 * t1n, DFT_N2, LANE)

        def long_conv(z6, g_z, g_gate, order):
            a = _outer_forward(sig_f[0], sig_f[1], z6, g_z, 1)
            cc = _spectral_multiply(g_tab[0], h_tab[0], kf, a, order, fb=fb)
            return _outer_inverse(inv_f[0], cc, u6, g_gate, z6, g_z, hy_bias[i, order][None, :])

        z1 = long_conv(u6, 0, 1, 0)
        yh2 = long_conv(z1[None], 0, 2, 1).reshape(2, bsz, n_tok, LANE)
        yh_lat = jnp.concatenate([yh2[0], yh2[1]], axis=-1)

        kf_c = _ctx_filter_spectrum(cfil_f[0], cfil_f[1], k_ctx, ss_ctx)
        yh_ctx = _ctx_hyena(csig_f[0], csig_f[1], cinv_f[0], cinv_f[1], kf_c,
                            u_ctx.reshape(3, p_n, 2 * ctx_len, HY_WIDTH), hy_bias[i])
        yh = jnp.concatenate([yh_ctx.reshape(bsz, ctx_len, HY_WIDTH), yh_lat], axis=1)

        xs = _merge(xs, ya, yh, yd, g_out[i, A_WIDTH:A_WIDTH + HY_WIDTH][None, :], w_out_b[i], mod,
                    ctx_len=ctx_len, tm=tm)
        xs = _ffn(xs, mod, norm2_g[i][None, :], w_up_a[i], w_up_v[i], ffn_conv_w[i], ffn_conv_b[i][None, :],
                  w_down_b[i], ctx_len=ctx_len, tm=tm, chunks=ff_chunks)
    return xs[:, ctx_len:]
```

```python
import functools
import math

import numpy as np
import jax
import jax.numpy as jnp
from jax import lax
from jax.experimental import pallas as pl
from jax.experimental.pallas import tpu as pltpu

F32 = jnp.float32
BF16 = jnp.bfloat16

HEAD_DIM = 64
HALF = HEAD_DIM // 2
A_HEADS = 4
A_KV_HEADS = 2
A_WIDTH = A_HEADS * HEAD_DIM
A_KV_WIDTH = A_KV_HEADS * HEAD_DIM
HY_WIDTH = 256
HY_IN = 3 * HY_WIDTH
HY_BANDS = 16
HY_EMB_DIM = 1 + 2 * HY_BANDS
HY_HIDDEN = 64
HY_FILTER_CH = 4 * HY_WIDTH
HY_FAST_DECAY = 0.3
HY_SLOW_DECAY = 1.5
HY_TARGET = 1e-2
D_HEADS = 4
D_VALUE_DIM = 2 * HEAD_DIM
D_WIDTH = D_HEADS * D_VALUE_DIM
D_QK_WIDTH = D_HEADS * 2 * HEAD_DIM
MIX_WIDTH = A_WIDTH + HY_WIDTH + D_WIDTH
OFF_QA = 0
OFF_KA = A_WIDTH
OFF_VA = OFF_KA + A_KV_WIDTH
OFF_HY = OFF_VA + A_KV_WIDTH
OFF_QD = OFF_HY + HY_IN
OFF_KD = OFF_QD + D_QK_WIDTH
OFF_VD = OFF_KD + D_QK_WIDTH
GRID_W = 64
ROPE_THETA = 10000.0
EPS = 1e-6

N_QBLK = A_HEADS + 2 * D_HEADS
N_KBLK = A_KV_HEADS + 2 * D_HEADS
Q_ROWS = N_QBLK * HEAD_DIM
K_ROWS = N_KBLK * HEAD_DIM
V_ROWS = A_KV_WIDTH + D_WIDTH
QK_ROWS = Q_ROWS + K_ROWS
QKV_ROWS = QK_ROWS + V_ROWS
SUM_ROWS = 16
VA_BLK = HEAD_DIM + SUM_ROWS

LANE = 128
DFT_N2 = 128
KEY_BLOCK = 768
MAX_UNROLLED_CHUNKS = 4
VMEM_LIMIT = 56 * 1024 * 1024
HIGHEST = lax.Precision.HIGHEST


def _cparams(sem):
    return pltpu.CompilerParams(dimension_semantics=sem, vmem_limit_bytes=VMEM_LIMIT)


def _split_bf16(x):
    hi = x.astype(BF16)
    lo = (x - hi.astype(F32)).astype(BF16)
    return hi, lo


def _dot(a, b):
    return jnp.dot(a, b, preferred_element_type=F32)


def _dot3(a_hi, a_lo, b):
    b_hi, b_lo = _split_bf16(b)
    return _dot(a_hi, b_hi) + (_dot(a_hi, b_lo) + _dot(a_lo, b_hi))


def _dot_p(a_hi, a_lo, b, passes):
    return _dot3(a_hi, a_lo, b) if passes == 3 else _dot(a_hi, b.astype(BF16))


def _mod_kernel(c_ref, w_ref, b_ref, o_ref):
    c = c_ref[...]
    s = c * (1.0 / (1.0 + jnp.exp(-c)))
    o_ref[0] = jnp.dot(s, w_ref[0], preferred_element_type=F32, precision=HIGHEST) + b_ref[0]


def _modulation(cvec, w_mod, b_mod):
    depth, d, n6 = w_mod.shape
    tn = 1024
    return pl.pallas_call(
        _mod_kernel,
        out_shape=jax.ShapeDtypeStruct((depth, cvec.shape[0], n6), F32),
        grid=(depth, n6 // tn),
        in_specs=[
            pl.BlockSpec(cvec.shape, lambda l, j: (0, 0)),
            pl.BlockSpec((1, d, tn), lambda l, j: (l, 0, j)),
            pl.BlockSpec((1, 1, tn), lambda l, j: (l, 0, j)),
        ],
        out_specs=pl.BlockSpec((1, cvec.shape[0], tn), lambda l, j: (l, 0, j)),
        compiler_params=_cparams(("arbitrary", "arbitrary")),
        name="modulation",
    )(cvec, w_mod, b_mod.reshape(depth, 1, n6))


def _norm_modulate(x, g, mod_ref, is_ctx, which):
    ms = jnp.mean(x * x, axis=-1, keepdims=True)
    xn = x * lax.rsqrt(ms + EPS) * g
    sh = jnp.where(is_ctx, mod_ref[0, 3 * which:3 * which + 1, :], mod_ref[0, 8 + 3 * which:9 + 3 * which, :])
    sc = jnp.where(is_ctx, mod_ref[0, 3 * which + 1:3 * which + 2, :],
                   mod_ref[0, 9 + 3 * which:10 + 3 * which, :])
    return xn * (1.0 + sc) + sh


def _proj_kernel(x_ref, mod_ref, g_ref, wq_ref, why_ref, gain_ref, cos_ref, sin_ref,
                 qT_ref, k_ref, vTa_ref, vTd_ref, hy_ref, *, tm, ctx_len):
    i = pl.program_id(1)
    row = i * tm + lax.broadcasted_iota(jnp.int32, (tm, 1), 0)
    h = _norm_modulate(x_ref[0], g_ref[...], mod_ref, row < ctx_len, 0).astype(BF16)
    pT = lax.dot_general(wq_ref[...], h, (((1,), (1,)), ((), ())), preferred_element_type=F32)
    hy_ref[0] = _dot(h, why_ref[...])
    c = cos_ref[...]
    s = sin_ref[...]
    blocks = []
    for b in range(N_QBLK + N_KBLK):
        blk = pT[b * HEAD_DIM:(b + 1) * HEAD_DIM]
        ssq = jnp.sum(blk * blk, axis=0, keepdims=True)
        blk = blk * lax.rsqrt(ssq * (1.0 / HEAD_DIM) + EPS) * gain_ref[b * HEAD_DIM:(b + 1) * HEAD_DIM, :]
        x1 = blk[:HALF]
        x2 = blk[HALF:]
        blocks.append(x1 * c - x2 * s)
        blocks.append(x1 * s + x2 * c)
    qT_ref[0] = jnp.concatenate(blocks[:2 * N_QBLK], axis=0).astype(BF16)
    kT = jnp.concatenate(blocks[2 * N_QBLK:], axis=0)
    k_ref[0] = kT.T.astype(BF16)
    ones = jnp.ones((SUM_ROWS, tm), BF16)
    va = [pT[QK_ROWS + g * HEAD_DIM:QK_ROWS + (g + 1) * HEAD_DIM].astype(BF16) for g in range(A_KV_HEADS)]
    vTa_ref[0] = jnp.concatenate([t for v in va for t in (v, ones)], axis=0)
    vTd_ref[0] = pT[QK_ROWS + A_KV_WIDTH:].astype(BF16)


def _projection(xs, mod, g, wqT, why, gains, cosT, sinT, *, ctx_len, tm):
    bsz, s_len, d = xs.shape
    kern = functools.partial(_proj_kernel, tm=tm, ctx_len=ctx_len)
    return pl.pallas_call(
        kern,
        out_shape=(
            jax.ShapeDtypeStruct((bsz, Q_ROWS, s_len), BF16),
            jax.ShapeDtypeStruct((bsz, s_len, K_ROWS), BF16),
            jax.ShapeDtypeStruct((bsz, A_KV_HEADS * VA_BLK, s_len), BF16),
            jax.ShapeDtypeStruct((bsz, D_WIDTH, s_len), BF16),
            jax.ShapeDtypeStruct((bsz, s_len, HY_IN), F32),
        ),
        grid=(bsz, s_len // tm),
        in_specs=[
            pl.BlockSpec((1, tm, d), lambda b, i: (b, i, 0)),
            pl.BlockSpec((1, 16, d), lambda b, i: (b, 0, 0)),
            pl.BlockSpec((1, d), lambda b, i: (0, 0)),
            pl.BlockSpec((QKV_ROWS, d), lambda b, i: (0, 0)),
            pl.BlockSpec((d, HY_IN), lambda b, i: (0, 0)),
            pl.BlockSpec((QK_ROWS, 1), lambda b, i: (0, 0)),
            pl.BlockSpec((HALF, tm), lambda b, i: (0, i)),
            pl.BlockSpec((HALF, tm), lambda b, i: (0, i)),
        ],
        out_specs=(
            pl.BlockSpec((1, Q_ROWS, tm), lambda b, i: (b, 0, i)),
            pl.BlockSpec((1, tm, K_ROWS), lambda b, i: (b, i, 0)),
            pl.BlockSpec((1, A_KV_HEADS * VA_BLK, tm), lambda b, i: (b, 0, i)),
            pl.BlockSpec((1, D_WIDTH, tm), lambda b, i: (b, 0, i)),
            pl.BlockSpec((1, tm, HY_IN), lambda b, i: (b, i, 0)),
        ),
        compiler_params=_cparams(("parallel", "parallel")),
        name="projection",
    )(xs, mod, g, wqT, why, gains, cosT, sinT)


def _attend(units, scr, *, ctx_only, ctx_len, tk, sum_rows):
    n_u = len(units)
    qz, m_run, l_run, acc, s_buf, cm_buf, qz0 = _split_scratch(scr, n_u)
    s_len = units[0][0].shape[1]
    for u in range(n_u):
        m_run[u][...] = jnp.full(m_run[u].shape, -1e30, F32)
        l_run[u][...] = jnp.zeros(l_run[u].shape, F32)
        acc[u][...] = jnp.zeros(acc[u].shape, F32)

    def fold8(s, op):
        return op(s.reshape(s.shape[0] // 8, 8, s.shape[1]), axis=0)

    def key_rows(start, j, rb):
        if isinstance(start, int):
            return pl.ds(start + j * rb, rb)
        return pl.ds(pl.multiple_of(start + j * rb, rb), rb)

    def advance(nxt, cur, size):
        rb = min(size, KEY_BLOCK)
        if cur is not None:
            u_c, start_c, slot_c = cur
            _, vT_ref, v0, dv = units[u_c]
            m = m_run[u_c][...]
            m_new = jnp.maximum(m, jnp.max(cm_buf[slot_c][...], axis=0, keepdims=True))
            m_run[u_c][...] = m_new
            alpha = jnp.exp2(m - m_new)
        cmax, psum, pv = None, None, None
        for j in range(size // rb):
            rows = slice(j * rb, (j + 1) * rb)
            if nxt is not None:
                u_n, start_n, slot_n = nxt[:3]
                q_n = qz[u_n][...] if len(nxt) == 3 else qz0[nxt[3]]
                s = _dot(units[u_n][0][0, key_rows(start_n, j, rb), :], q_n)
                s_buf[slot_n][rows, :] = s
                cmax = fold8(s, jnp.max) if cmax is None else jnp.maximum(cmax, fold8(s, jnp.max))
            if cur is not None:
                p = jnp.exp2(s_buf[slot_c][rows, :] - m_new)
                if not sum_rows:
                    psum = fold8(p, jnp.sum) if psum is None else psum + fold8(p, jnp.sum)
                d = _dot(vT_ref[0, v0:v0 + dv, key_rows(start_c, j, rb)], p.astype(BF16))
                pv = d if pv is None else pv + d
        if nxt is not None:
            cm_buf[slot_n][...] = cmax
        if cur is not None:
            if not sum_rows:
                l_run[u_c][...] = alpha * l_run[u_c][...] + jnp.sum(psum, axis=0, keepdims=True)
            acc[u_c][...] = alpha * acc[u_c][...] + pv

    if ctx_only:
        for u in range(n_u):
            advance((u, 0, u % 2), None, ctx_len)
            advance(None, (u, 0, u % 2), ctx_len)
        advance((0, 0, 0, 1), None, tk)
    else:
        n_chunks = s_len // tk

        def chunk_steps(c, start, nxt, wrap):
            for u in range(n_u):
                task_n = (u + 1, start, (u + 1) % 2) if u + 1 < n_u else (0, nxt, (u + 1) % 2, wrap)
                advance(task_n, (u, start, u % 2), tk)

        if n_chunks <= MAX_UNROLLED_CHUNKS:
            for c in range(n_chunks):
                last = c == n_chunks - 1
                chunk_steps(c, c * tk, 0 if last else (c + 1) * tk, int(last))
        else:
            def body(c, carry):
                wrap = jnp.where(c == n_chunks - 1, 1, 0).astype(jnp.int32)
                chunk_steps(c, pl.multiple_of(c * tk, tk), pl.multiple_of((1 - wrap) * (c + 1) * tk, tk), wrap)
                return carry
            lax.fori_loop(0, n_chunks, body, 0)
    outs = []
    for u in range(n_u):
        dv = units[u][3] - sum_rows
        l = acc[u][dv:dv + 1, :] if sum_rows else l_run[u][...]
        outs.append(acc[u][:dv, :] * (1.0 / l))
    return outs


def _attn_scratch(n_u, rows, tq, tk):
    per_unit = [((LANE, tq), BF16), ((1, tq), F32), ((1, tq), F32), ((rows, tq), F32)]
    per_slot = [((tk, tq), F32), ((8, tq), F32)]
    return ([pltpu.VMEM(s, d) for s, d in per_unit for _ in range(n_u)]
            + [pltpu.VMEM(s, d) for s, d in per_slot for _ in range(2)]
            + [pltpu.VMEM((2, LANE, tq), BF16)])


def _split_scratch(scr, n_u):
    groups = [scr[i * n_u:(i + 1) * n_u] for i in range(4)]
    rest = scr[4 * n_u:]
    return groups + [rest[0:2], rest[2:4], rest[4]]


def _run_ctx_or_latent(qi, n_ctx_tiles, run):
    @pl.when(qi < n_ctx_tiles)
    def _():
        run(True)

    @pl.when(qi >= n_ctx_tiles)
    def _():
        run(False)


def _attn_a_kernel(qT_ref, qn_ref, k_ref, vT_ref, g_ref, o_ref, *scr, ctx_len, tq, tk):
    zeros = jnp.zeros((HEAD_DIM, tq), BF16)
    group = A_HEADS // A_KV_HEADS
    for h in range(A_HEADS):
        q = qT_ref[0, h * HEAD_DIM:(h + 1) * HEAD_DIM, :]
        scr[h][...] = jnp.concatenate([q, zeros] if h // group == 0 else [zeros, q], axis=0)
    scr[-1][0] = jnp.concatenate([qT_ref[0, :HEAD_DIM, :], zeros], axis=0)
    scr[-1][1] = jnp.concatenate([qn_ref[0, :HEAD_DIM, :], zeros], axis=0)
    units = [(k_ref, vT_ref, (h // group) * VA_BLK, VA_BLK) for h in range(A_HEADS)]

    def run(ctx_only):
        outs = _attend(units, scr, ctx_only=ctx_only, ctx_len=ctx_len, tk=tk, sum_rows=SUM_ROWS)
        y = jnp.concatenate(outs, axis=0)
        ms = jnp.mean(y * y, axis=0, keepdims=True)
        y = y * lax.rsqrt(ms + EPS) * g_ref[...]
        o_ref[0] = y.T.astype(BF16)

    _run_ctx_or_latent(pl.program_id(1), ctx_len // tq, run)


def _attention_a(qT, k, vT, g_a, *, ctx_len, tq, tk):
    bsz, _, s_len = qT.shape
    kern = functools.partial(_attn_a_kernel, ctx_len=ctx_len, tq=tq, tk=tk)
    return pl.pallas_call(
        kern,
        out_shape=jax.ShapeDtypeStruct((bsz, s_len, A_WIDTH), BF16),
        grid=(bsz, s_len // tq),
        in_specs=[
            pl.BlockSpec((1, A_WIDTH, tq), lambda b, i: (b, 0, i)),
            pl.BlockSpec((1, A_WIDTH, tq), lambda b, i: (b, 0, jnp.minimum(i + 1, s_len // tq - 1))),
            pl.BlockSpec((1, s_len, LANE), lambda b, i: (b, 0, 0)),
            pl.BlockSpec((1, A_KV_HEADS * VA_BLK, s_len), lambda b, i: (b, 0, 0)),
            pl.BlockSpec((A_WIDTH, 1), lambda b, i: (0, 0)),
        ],
        out_specs=pl.BlockSpec((1, tq, A_WIDTH), lambda b, i: (b, i, 0)),
        scratch_shapes=_attn_scratch(A_HEADS, VA_BLK, tq, tk),
        compiler_params=_cparams(("parallel", "arbitrary")),
        name="attention_gqa",
    )(qT, qT, k, vT, g_a)


D_HEADS_PER_STEP = 2


def _attn_d_kernel(*refs, ctx_len, tq, tk):
    n_h = D_HEADS_PER_STEP
    q_refs, k_refs, v_refs = refs[:n_h], refs[n_h:2 * n_h], refs[2 * n_h:3 * n_h]
    qn_ref, g_ref, lamv_ref, lami_ref, o_ref = refs[3 * n_h:3 * n_h + 5]
    scr = refs[3 * n_h + 5:]
    zeros = jnp.zeros((HEAD_DIM, tq), BF16)
    lv = lamv_ref[...]
    lam_init = lami_ref[0:1, 0:1]
    lam = (jnp.exp(jnp.sum(lv[0:1] * lv[1:2], axis=-1, keepdims=True))
           - jnp.exp(jnp.sum(lv[2:3] * lv[3:4], axis=-1, keepdims=True)) + lam_init)
    units = []
    for h in range(n_h):
        scr[2 * h][...] = jnp.concatenate([q_refs[h][0, :HEAD_DIM, :], zeros], axis=0)
        scr[2 * h + 1][...] = jnp.concatenate([zeros, q_refs[h][0, HEAD_DIM:, :]], axis=0)
        units += [(k_refs[h], v_refs[h], 0, D_VALUE_DIM)] * 2
    scr[-1][0] = jnp.concatenate([q_refs[0][0, :HEAD_DIM, :], zeros], axis=0)
    scr[-1][1] = jnp.concatenate([qn_ref[0, :HEAD_DIM, :], zeros], axis=0)

    def run(ctx_only):
        outs = _attend(units, scr, ctx_only=ctx_only, ctx_len=ctx_len, tk=tk, sum_rows=0)
        ys = []
        for h in range(n_h):
            y = outs[2 * h] - lam * outs[2 * h + 1]
            ms = jnp.mean(y * y, axis=0, keepdims=True)
            ys.append(y * lax.rsqrt(ms + EPS) * g_ref[h] * (1.0 - lam_init))
        o_ref[0] = jnp.concatenate(ys, axis=0).T.astype(BF16)

    _run_ctx_or_latent(pl.program_id(2), ctx_len // tq, run)


def _attention_d(qT, k, vT, g_d, lamv, lami, *, ctx_len, tq, tk):
    bsz, _, s_len = qT.shape
    n_h = D_HEADS_PER_STEP
    kern = functools.partial(_attn_d_kernel, ctx_len=ctx_len, tq=tq, tk=tk)
    q_blk0 = A_WIDTH // LANE
    k_blk0 = A_KV_WIDTH // LANE
    q_specs = [pl.BlockSpec((1, LANE, tq), lambda b, hp, i, j=j: (b, q_blk0 + n_h * hp + j, i)) for j in range(n_h)]
    k_specs = [pl.BlockSpec((1, s_len, LANE), lambda b, hp, i, j=j: (b, 0, k_blk0 + n_h * hp + j)) for j in range(n_h)]
    v_specs = [pl.BlockSpec((1, D_VALUE_DIM, s_len), lambda b, hp, i, j=j: (b, n_h * hp + j, 0)) for j in range(n_h)]
    return pl.pallas_call(
        kern,
        out_shape=jax.ShapeDtypeStruct((bsz, s_len, D_WIDTH), BF16),
        grid=(bsz, D_HEADS // n_h, s_len // tq),
        in_specs=q_specs + k_specs + v_specs + [
            pl.BlockSpec((1, LANE, tq), lambda b, hp, i: (b, q_blk0 + n_h * hp, jnp.minimum(i + 1, s_len // tq - 1))),
            pl.BlockSpec((n_h, D_VALUE_DIM, 1), lambda b, hp, i: (hp, 0, 0)),
            pl.BlockSpec((4, HEAD_DIM), lambda b, hp, i: (0, 0)),
            pl.BlockSpec((8, LANE), lambda b, hp, i: (0, 0)),
        ],
        out_specs=pl.BlockSpec((1, tq, n_h * D_VALUE_DIM), lambda b, hp, i: (b, i, hp)),
        scratch_shapes=_attn_scratch(2 * n_h, D_VALUE_DIM, tq, tk),
        compiler_params=_cparams(("parallel", "parallel", "arbitrary")),
        name="attention_diff",
    )(*([qT] * n_h + [k] * n_h + [vT] * n_h + [qT, g_d, lamv, lami]))


def _dwconv3_seq(u, w_ref, b_ref):
    n = u.shape[0]
    row = lax.broadcasted_iota(jnp.int32, (n, 1), 0)
    prev = jnp.where(row == 0, 0.0, pltpu.roll(u, 1, axis=0))
    nxt = jnp.where(row == n - 1, 0.0, pltpu.roll(u, n - 1, axis=0))
    return prev * w_ref[0:1, :] + u * w_ref[1:2, :] + nxt * w_ref[2:3, :] + b_ref[...]


def _hy_conv_kernel(hy_ref, w_ref, b_ref, lat_ref, ctx_ref, *, ctx_len):
    ctx_ref[0, 0] = _dwconv3_seq(hy_ref[0, :ctx_len, :], w_ref, b_ref)
    lat_ref[0, 0, 0] = _dwconv3_seq(hy_ref[0, ctx_len:, :], w_ref, b_ref)


def _hyena_short_conv(hy, w, b, *, ctx_len):
    bsz, s_len, _ = hy.shape
    n = s_len - ctx_len
    per = HY_WIDTH // LANE
    kern = functools.partial(_hy_conv_kernel, ctx_len=ctx_len)
    return pl.pallas_call(
        kern,
        out_shape=(
            jax.ShapeDtypeStruct((3, per, bsz, n, LANE), F32),
            jax.ShapeDtypeStruct((3, bsz, ctx_len, HY_WIDTH), F32),
        ),
        grid=(bsz, HY_IN // LANE),
        in_specs=[
            pl.BlockSpec((1, s_len, LANE), lambda b, j: (b, 0, j)),
            pl.BlockSpec((3, LANE), lambda b, j: (0, j)),
            pl.BlockSpec((1, LANE), lambda b, j: (0, j)),
        ],
        out_specs=(
            pl.BlockSpec((1, 1, 1, n, LANE), lambda b, j: (j // per, j % per, b, 0, 0)),
            pl.BlockSpec((1, 1, ctx_len, LANE), lambda b, j: (j // per, b, 0, j % per)),
        ),
        compiler_params=_cparams(("parallel", "parallel")),
        name="hyena_short_conv",
    )(hy, w, b.reshape(1, HY_IN))


def _filter_kernel(z_ref, w1_ref, b1_ref, w2_ref, b2_ref, w3_ref, b3_ref, fr_ref, ad_ref, h_ref, sum_ref):
    i = pl.program_id(0)
    z = z_ref[...]
    fr = fr_ref[...]
    h = jnp.sin(fr * (jnp.dot(z, w1_ref[...], preferred_element_type=F32, precision=HIGHEST) + b1_ref[...]))
    h = jnp.sin(fr * (jnp.dot(h, w2_ref[...], preferred_element_type=F32, precision=HIGHEST) + b2_ref[...]))
    h = jnp.dot(h, w3_ref[...], preferred_element_type=F32, precision=HIGHEST) + b3_ref[...]
    h = h * jnp.exp(-z[:, 0:1] * ad_ref[...])
    for q in range(HY_FILTER_CH // HY_WIDTH):
        for hh in range(HY_WIDTH // LANE):
            c0 = q * HY_WIDTH + hh * LANE
            h_ref[hh, q] = h[:, c0:c0 + LANE]

    @pl.when(i == 0)
    def _():
        sum_ref[...] = jnp.zeros_like(sum_ref)

    sum_ref[...] += jnp.sum(jnp.abs(h), axis=0, keepdims=True)


def _hyena_filters(zfeat, w1, b1, w2, b2, w3, b3, freq, absd):
    n, kz = zfeat.shape
    tn = min(n, 1024)
    full = lambda a: pl.BlockSpec(a.shape, lambda i: (0, 0))
    return pl.pallas_call(
        _filter_kernel,
        out_shape=(
            jax.ShapeDtypeStruct((HY_WIDTH // LANE, HY_FILTER_CH // HY_WIDTH, n, LANE), F32),
            jax.ShapeDtypeStruct((1, HY_FILTER_CH), F32),
        ),
        grid=(n // tn,),
        in_specs=[pl.BlockSpec((tn, kz), lambda i: (i, 0)), full(w1), full(b1), full(w2), full(b2),
                  full(w3), full(b3), full(freq), full(absd)],
        out_specs=(
            pl.BlockSpec((HY_WIDTH // LANE, HY_FILTER_CH // HY_WIDTH, tn, LANE), lambda i: (0, 0, i, 0)),
            pl.BlockSpec((1, HY_FILTER_CH), lambda i: (0, 0)),
        ),
        compiler_params=_cparams(("arbitrary",)),
        name="hyena_filters",
    )(zfeat, w1, b1, w2, b2, w3, b3, freq, absd)


T2_BLOCK = 8


def _outer_fwd_kernel(mh_ref, ml_ref, x_ref, o_ref, *, n1, passes):
    rows = x_ref.shape[3]
    xs = [x_ref.at[0, h, 0].reshape(rows * T2_BLOCK, LANE) for h in range(2)]
    outs = [[o_ref.at[0, ri, h].reshape(n1 * T2_BLOCK, LANE) for h in range(2)] for ri in range(2)]
    for t in range(T2_BLOCK):
        sel = pl.ds(t, rows, stride=T2_BLOCK)
        y = _dot_p(mh_ref[...], ml_ref[...], jnp.concatenate([xs[0][sel, :], xs[1][sel, :]], axis=1), passes)
        for ri in range(2):
            for h in range(2):
                outs[ri][h][pl.ds(t, n1, stride=T2_BLOCK), :] = y[ri * n1:(ri + 1) * n1, h * LANE:(h + 1) * LANE]


def _outer_forward(m_hi, m_lo, x6, g, passes):
    _, _, p_n, rows, n2, _ = x6.shape
    n1 = m_hi.shape[0] // 2
    kern = functools.partial(_outer_fwd_kernel, n1=n1, passes=passes)
    return pl.pallas_call(
        kern,
        out_shape=jax.ShapeDtypeStruct((p_n, 2, 2, n1, n2, LANE), F32),
        grid=(p_n, n2 // T2_BLOCK),
        in_specs=[pl.BlockSpec(m_hi.shape, lambda p, j: (0, 0)), pl.BlockSpec(m_lo.shape, lambda p, j: (0, 0)),
                  pl.BlockSpec((1, 2, 1, rows, T2_BLOCK, LANE), lambda p, j: (g, 0, p, 0, j, 0))],
        out_specs=pl.BlockSpec((1, 2, 2, n1, T2_BLOCK, LANE), lambda p, j: (p, 0, 0, 0, j, 0)),
        compiler_params=_cparams(("parallel", "parallel")),
        name="dft_outer_fwd",
    )(m_hi, m_lo, x6)


def _outer_inv_kernel(mh_ref, c_ref, gate_ref, u_ref, bias_ref, o_ref):
    n1 = c_ref.shape[3]
    rows = o_ref.shape[2]
    cs = [[c_ref.at[0, ri, h].reshape(n1 * T2_BLOCK, LANE) for h in range(2)] for ri in range(2)]
    gates = [gate_ref.at[0, h, 0].reshape(rows * T2_BLOCK, LANE) for h in range(2)]
    us = [u_ref.at[0, h, 0].reshape(rows * T2_BLOCK, LANE) for h in range(2)]
    outs = [o_ref.at[h, 0].reshape(rows * T2_BLOCK, LANE) for h in range(2)]
    for t in range(T2_BLOCK):
        sel_f = pl.ds(t, n1, stride=T2_BLOCK)
        sel_t = pl.ds(t, rows, stride=T2_BLOCK)
        c2 = jnp.concatenate([jnp.concatenate([cs[ri][0][sel_f, :], cs[ri][1][sel_f, :]], axis=1)
                              for ri in range(2)], axis=0)
        y = _dot(mh_ref[...], c2.astype(BF16))
        for h in range(2):
            lanes = slice(h * LANE, (h + 1) * LANE)
            outs[h][sel_t, :] = gates[h][sel_t, :] * (y[:, lanes] + us[h][sel_t, :] * bias_ref[:, lanes])


def _outer_inverse(m_hi, c6, gate6, g_gate, u6, g_u, bias):
    p_n, _, _, n1, n2, _ = c6.shape
    rows = m_hi.shape[0]
    gspec = lambda g: pl.BlockSpec((1, 2, 1, rows, T2_BLOCK, LANE), lambda p, j: (g, 0, p, 0, j, 0))
    return pl.pallas_call(
        _outer_inv_kernel,
        out_shape=jax.ShapeDtypeStruct((2, p_n, rows, n2, LANE), F32),
        grid=(p_n, n2 // T2_BLOCK),
        in_specs=[pl.BlockSpec(m_hi.shape, lambda p, j: (0, 0)),
                  pl.BlockSpec((1, 2, 2, n1, T2_BLOCK, LANE), lambda p, j: (p, 0, 0, 0, j, 0)),
                  gspec(g_gate), gspec(g_u), pl.BlockSpec((1, 2 * LANE), lambda p, j: (0, 0))],
        out_specs=pl.BlockSpec((2, 1, rows, T2_BLOCK, LANE), lambda p, j: (0, p, 0, j, 0)),
        compiler_params=_cparams(("parallel", "parallel")),
        name="dft_outer_inv",
    )(m_hi, c6, gate6, u6, bias)


def _halves_to_rows(a_ref, j):
    return jnp.concatenate([jnp.concatenate([a_ref[0, ri, 0, j], a_ref[0, ri, 1, j]], axis=1) for ri in range(2)],
                           axis=0)


def _kf_kernel(gh_ref, gl_ref, af_ref, ab_ref, h0_ref, ss_ref, o_ref, *, fb, n2):
    scale = 1.0 / (ss_ref[0, 0:1, :] + ss_ref[0, 1:2, :] + EPS)
    b0 = jnp.concatenate([h0_ref[0, 0, 0:1, :], h0_ref[1, 0, 0:1, :]], axis=1)
    for j in range(fb):
        xf = _dot3(gh_ref[j], gl_ref[j], _halves_to_rows(af_ref, j))
        xb = _dot3(gh_ref[j], gl_ref[j], _halves_to_rows(ab_ref, j))
        o_ref[0, j, :n2, :] = (xf[:n2] + xb[:n2] - b0) * scale
        o_ref[0, j, n2:, :] = (xf[n2:] - xb[n2:]) * scale


def _filter_spectrum(g_hi, g_lo, a6, h6, ss, *, fb):
    n_q, _, _, n1, n2, _ = a6.shape
    n_ord = n_q // 2
    c = 2 * LANE
    kern = functools.partial(_kf_kernel, fb=fb, n2=n2)
    return pl.pallas_call(
        kern,
        out_shape=jax.ShapeDtypeStruct((n_ord, n1, 2 * n2, c), F32),
        grid=(n1 // fb, n_ord),
        in_specs=[
            pl.BlockSpec((fb, 2 * n2, 2 * n2), lambda f, o: (f, 0, 0)),
            pl.BlockSpec((fb, 2 * n2, 2 * n2), lambda f, o: (f, 0, 0)),
            pl.BlockSpec((1, 2, 2, fb, n2, LANE), lambda f, o: (o, 0, 0, f, 0, 0)),
            pl.BlockSpec((1, 2, 2, fb, n2, LANE), lambda f, o: (n_ord + o, 0, 0, f, 0, 0)),
            pl.BlockSpec((2, 1, 8, LANE), lambda f, o: (0, n_ord + o, 0, 0)),
            pl.BlockSpec((1, 2, c), lambda f, o: (o, 0, 0)),
        ],
        out_specs=pl.BlockSpec((1, fb, 2 * n2, c), lambda f, o: (o, f, 0, 0)),
        compiler_params=_cparams(("parallel", "parallel")),
        name="filter_spectrum",
    )(g_hi, g_lo, a6, a6, h6, ss)


def _spec_kernel(gh_ref, hh_ref, kf_ref, a_ref, o_ref, *, fb, n2):
    for j in range(fb):
        x = _dot(gh_ref[j], _halves_to_rows(a_ref, j).astype(BF16))
        xr, xi = x[:n2], x[n2:]
        kr, ki = kf_ref[0, j, :n2], kf_ref[0, j, n2:]
        y = jnp.concatenate([xr * kr - xi * ki, xr * ki + xi * kr], axis=0)
        c2 = _dot(hh_ref[j], y.astype(BF16))
        for ri in range(2):
            for h in range(2):
                o_ref[0, ri, h, j] = c2[ri * n2:(ri + 1) * n2, h * LANE:(h + 1) * LANE]


def _spectral_multiply(g_hi, h_hi, kf, a6, order, *, fb):
    p_n, _, _, n1, n2, _ = a6.shape
    c = 2 * LANE
    kern = functools.partial(_spec_kernel, fb=fb, n2=n2)
    gspec = pl.BlockSpec((fb, 2 * n2, 2 * n2), lambda f, p: (f, 0, 0))
    aspec = pl.BlockSpec((1, 2, 2, fb, n2, LANE), lambda f, p: (p, 0, 0, f, 0, 0))
    return pl.pallas_call(
        kern,
        out_shape=jax.ShapeDtypeStruct(a6.shape, F32),
        grid=(n1 // fb, p_n),
        in_specs=[gspec, gspec,
                  pl.BlockSpec((1, fb, 2 * n2, c), lambda f, p: (order, f, 0, 0)), aspec],
        out_specs=aspec,
        compiler_params=_cparams(("parallel", "parallel")),
        name="spectral_multiply",
    )(g_hi, h_hi, kf, a6)


def _ctx_kf_kernel(fh_ref, fl_ref, k_ref, ss_ref, o_ref):
    scale = 1.0 / (ss_ref[0, 0:1, :] + ss_ref[0, 1:2, :] + EPS)
    o_ref[0] = _dot3(fh_ref[...], fl_ref[...], k_ref[0]) * scale


def _ctx_filter_spectrum(f_hi, f_lo, k, ss):
    n_ord, nc, c = k.shape
    return pl.pallas_call(
        _ctx_kf_kernel,
        out_shape=jax.ShapeDtypeStruct((n_ord, 2 * nc, c), F32),
        grid=(n_ord,),
        in_specs=[
            pl.BlockSpec(f_hi.shape, lambda o: (0, 0)),
            pl.BlockSpec(f_lo.shape, lambda o: (0, 0)),
            pl.BlockSpec((1, nc, c), lambda o: (o, 0, 0)),
            pl.BlockSpec((1, 2, c), lambda o: (o, 0, 0)),
        ],
        out_specs=pl.BlockSpec((1, 2 * nc, c), lambda o: (o, 0, 0)),
        compiler_params=_cparams(("parallel",)),
        name="ctx_filter_spectrum",
    )(f_hi, f_lo, k, ss)


def _ctx_conv_kernel(fh_ref, fl_ref, eh_ref, el_ref, kf_ref, u_ref, bias_ref, o_ref, *, nc):
    v, x1, x2 = u_ref[0, 0], u_ref[1, 0], u_ref[2, 0]

    def conv(z, order):
        x = _dot3(fh_ref[...], fl_ref[...], z)
        xr, xi = x[:nc], x[nc:]
        kr, ki = kf_ref[order, :nc], kf_ref[order, nc:]
        y = jnp.concatenate([xr * kr - xi * ki, xr * ki + xi * kr], axis=0)
        return _dot3(eh_ref[...], el_ref[...], y) + z * bias_ref[order:order + 1, :]

    o_ref[0] = x2 * conv(x1 * conv(v, 0), 1)


def _ctx_hyena(f_hi, f_lo, e_hi, e_lo, kf, u3, bias):
    _, p_n, n2x, c = u3.shape
    nc = kf.shape[1] // 2
    kern = functools.partial(_ctx_conv_kernel, nc=nc)
    full = lambda a: pl.BlockSpec(a.shape, lambda p: (0,) * a.ndim)
    return pl.pallas_call(
        kern,
        out_shape=jax.ShapeDtypeStruct((p_n, n2x, c), F32),
        grid=(p_n,),
        in_specs=[full(f_hi), full(f_lo), full(e_hi), full(e_lo), full(kf),
                  pl.BlockSpec((3, 1, n2x, c), lambda p: (0, p, 0, 0)), full(bias)],
        out_specs=pl.BlockSpec((1, n2x, c), lambda p: (p, 0, 0)),
        compiler_params=_cparams(("parallel",)),
        name="ctx_hyena",
    )(f_hi, f_lo, e_hi, e_lo, kf, u3, bias)


def _merge_kernel(x_ref, ya_ref, yh_ref, yd_ref, gh_ref, w_ref, mod_ref, o_ref, *, tm, ctx_len):
    i = pl.program_id(1)
    row = i * tm + lax.broadcasted_iota(jnp.int32, (tm, 1), 0)
    yh = yh_ref[0]
    ms = jnp.mean(yh * yh, axis=-1, keepdims=True)
    yh = yh * lax.rsqrt(ms + EPS) * gh_ref[...]
    y = _dot(ya_ref[0].astype(BF16), w_ref[:A_WIDTH, :])
    y += _dot(yh.astype(BF16), w_ref[A_WIDTH:A_WIDTH + HY_WIDTH, :])
    y += _dot(yd_ref[0].astype(BF16), w_ref[A_WIDTH + HY_WIDTH:, :])
    gate = jnp.where(row < ctx_len, mod_ref[0, 2:3, :], mod_ref[0, 10:11, :])
    o_ref[0] = x_ref[0] + gate * y


def _merge(xs, ya, yh, yd, g_h, w_out, mod, *, ctx_len, tm):
    bsz, s_len, d = xs.shape
    kern = functools.partial(_merge_kernel, tm=tm, ctx_len=ctx_len)
    tok = lambda w: pl.BlockSpec((1, tm, w), lambda b, i: (b, i, 0))
    return pl.pallas_call(
        kern,
        out_shape=jax.ShapeDtypeStruct(xs.shape, F32),
        grid=(bsz, s_len // tm),
        in_specs=[tok(d), tok(A_WIDTH), tok(HY_WIDTH), tok(D_WIDTH),
                  pl.BlockSpec((1, HY_WIDTH), lambda b, i: (0, 0)),
                  pl.BlockSpec((MIX_WIDTH, d), lambda b, i: (0, 0)),
                  pl.BlockSpec((1, 16, d), lambda b, i: (b, 0, 0))],
        out_specs=tok(d),
        compiler_params=_cparams(("parallel", "parallel")),
        name="merge",
    )(xs, ya, yh, yd, g_h, w_out, mod)


HALO = 8
FF_TILE = 256
FF_CHUNKS = 2


def _ffn_kernel(x_ref, xp_ref, xn_ref, mod_ref, g_ref, wa_ref, wv_ref, cw_ref, cb_ref, wd_ref, o_ref,
                *, tm, chunks, ctx_len, s_len):
    i = pl.program_id(1)
    x = x_ref[0]
    xa = jnp.concatenate([xp_ref[0], x, xn_ref[0]], axis=0)
    rows = tm + 2 * HALO
    row = i * tm - HALO + lax.broadcasted_iota(jnp.int32, (rows, 1), 0)
    h = _norm_modulate(xa, g_ref[...], mod_ref, row < ctx_len, 1).astype(BF16)
    hm = h[HALO:HALO + tm]
    rowm = row[HALO:HALO + tm]
    has_prev = jnp.logical_and(rowm != 0, rowm != ctx_len)
    has_next = jnp.logical_and(rowm != ctx_len - 1, rowm != s_len - 1)
    def up(j):
        cs = slice(chunks[j], chunks[j + 1])
        return _dot(h, wa_ref[:, cs]), _dot(hm, wv_ref[:, cs])

    n_chunks = len(chunks) - 1
    acc = jnp.zeros((tm, x.shape[1]), F32)
    nxt = up(0)
    for j in range(n_chunks):
        cs = slice(chunks[j], chunks[j + 1])
        a, v = nxt
        if j + 1 < n_chunks:
            nxt = up(j + 1)
        a_prev = jnp.where(has_prev, pltpu.roll(a, 1, axis=0)[HALO:HALO + tm], 0.0)
        a_next = jnp.where(has_next, pltpu.roll(a, rows - 1, axis=0)[HALO:HALO + tm], 0.0)
        ac = a_prev * cw_ref[0:1, cs] + a[HALO:HALO + tm] * cw_ref[1:2, cs] + a_next * cw_ref[2:3, cs] + cb_ref[:, cs]
        gl = 0.5 * ac * (1.0 + jnp.tanh(math.sqrt(2.0 / math.pi) * (ac + 0.044715 * (ac * ac * ac))))
        acc += _dot((gl * v).astype(BF16), wd_ref[cs, :])
    gate = jnp.where(rowm < ctx_len, mod_ref[0, 5:6, :], mod_ref[0, 13:14, :])
    o_ref[0] = x + gate * acc


def _ffn(xs, mod, g, wa, wv, cw, cb, wd, *, ctx_len, tm, chunks):
    bsz, s_len, d = xs.shape
    d_ff = wa.shape[1]
    nh = tm // HALO
    last = s_len // HALO - 1
    kern = functools.partial(_ffn_kernel, tm=tm, chunks=chunks, ctx_len=ctx_len, s_len=s_len)
    const = lambda shape: pl.BlockSpec(shape, lambda b, i: (0, 0), pipeline_mode=pl.Buffered(1))
    return pl.pallas_call(
        kern,
        out_shape=jax.ShapeDtypeStruct(xs.shape, F32),
        grid=(bsz, s_len // tm),
        in_specs=[
            pl.BlockSpec((1, tm, d), lambda b, i: (b, i, 0)),
            pl.BlockSpec((1, HALO, d), lambda b, i: (b, jnp.maximum(i * nh - 1, 0), 0)),
            pl.BlockSpec((1, HALO, d), lambda b, i: (b, jnp.minimum((i + 1) * nh, last), 0)),
            pl.BlockSpec((1, 16, d), lambda b, i: (b, 0, 0)),
            pl.BlockSpec((1, d), lambda b, i: (0, 0)),
            const((d, d_ff)), const((d, d_ff)), const((3, d_ff)), const((1, d_ff)), const((d_ff, d)),
        ],
        out_specs=pl.BlockSpec((1, tm, d), lambda b, i: (b, i, 0)),
        compiler_params=_cparams(("parallel", "parallel")),
        name="conv_glu",
    )(xs, xs, xs, mod, g, wa, wv, cw, cb, wd)


def _cis(num, den):
    ang = num.astype(F32) * (2.0 * math.pi / den)
    return jnp.cos(ang), jnp.sin(ang)


def _blockmat(re, im):
    return jnp.concatenate([jnp.concatenate([re, -im], axis=-1), jnp.concatenate([im, re], axis=-1)], axis=-2)


def _dft_tables(n):
    n2 = DFT_N2
    t1n = n // n2
    n1 = 2 * t1n
    big = n1 * n2
    f1 = jnp.arange(n1, dtype=jnp.int32)
    t1 = jnp.arange(t1n, dtype=jnp.int32)
    j2 = jnp.arange(n2, dtype=jnp.int32)
    c, s = _cis((f1[:, None] * t1[None, :]) % n1, n1)
    fwd_sig = _blockmat(c, -s)
    c, s = _cis((f1[:, None] * f1[None, :]) % n1, n1)
    fwd_fil = jnp.concatenate([c, -s], axis=0)[:, :t1n]
    ph = (n1 * j2[None, :, None] * j2[None, None, :] + f1[:, None, None] * j2[None, None, :]) % big
    c, s = _cis(ph, big)
    g = _blockmat(c, -s)
    h = jnp.swapaxes(g, 1, 2)
    c, s = _cis((t1[:, None] * f1[None, :]) % n1, n1)
    inv = _blockmat(c, s) * (1.0 / big)
    return tuple(_split_bf16(m) for m in (fwd_sig, fwd_fil, g, h, inv))


def _dense_dft_tables(n):
    big = 2 * n
    f = jnp.arange(big, dtype=jnp.int32)
    t = jnp.arange(n, dtype=jnp.int32)
    c, s = _cis((f[:, None] * t[None, :]) % big, big)
    fwd_sig = _blockmat(c, -s)
    c, s = _cis((f[:, None] * f[None, :]) % big, big)
    fwd_fil = jnp.concatenate([c, -s], axis=0)
    c, s = _cis((t[:, None] * f[None, :]) % big, big)
    inv = _blockmat(c, s) * (1.0 / big)
    return tuple(_split_bf16(m) for m in (fwd_sig, fwd_fil, inv))


def _filter_features(n):
    pos = jnp.arange(n, dtype=F32)
    t = (pos / max(n - 1, 1))[:, None]
    w = 2.0 * math.pi * pos[:, None] / n
    bands = jnp.linspace(1e-4, HY_BANDS - 1, HY_BANDS, dtype=F32)
    z = jnp.concatenate([t, jnp.cos(bands * w), -jnp.sin(bands * w)], axis=-1)
    return jnp.pad(z, ((0, 0), (0, LANE - HY_EMB_DIM)))


def _circular_filters(h, big):
    n = h.shape[0]
    fwd = jnp.moveaxis(h[:, 0], 1, 0)
    bwd = jnp.moveaxis(h[:, 1], 1, 0)
    zeros = jnp.zeros((fwd.shape[0], big - 2 * n + 1, fwd.shape[2]), F32)
    return jnp.concatenate([fwd, zeros, bwd[:, :0:-1]], axis=1)


def _pick(total, candidates):
    for t in candidates:
        if total % t == 0:
            return t
    raise ValueError(f"no tile for {total}")


def kernel(x, c, ctx, c_ctx, norm1_g, norm2_g, w_mod, b_mod, w_in, qn_a, kn_a, qn_d, kn_d, lam_q1, lam_k1, lam_q2, lam_k2, hy_conv_w, hy_conv_b, hy_fw1, hy_fb1, hy_fw2, hy_fb2, hy_fw3, hy_fb3, hy_freq, hy_bias, g_out, w_out, w_up, ffn_conv_w, ffn_conv_b, w_down):
    bsz, n_tok, d = x.shape
    ctx_len = ctx.shape[1]
    depth = w_in.shape[0]
    s_len = ctx_len + n_tok
    d_ff = w_down.shape[1]
    assert bsz % 2 == 0 and n_tok % (DFT_N2 * 8) == 0 and ctx_len % LANE == 0

    tm = _pick(s_len, (768, 512, 256, 128))
    tq = _pick(ctx_len, (256, 128))
    tk = _pick(s_len, (8448, 2816, 768, 256))
    n_ff_tiles = -(-d_ff // FF_TILE)
    bounds = [min(d_ff, FF_TILE * ((n_ff_tiles * j + FF_CHUNKS - 1) // FF_CHUNKS)) for j in range(FF_CHUNKS + 1)]
    ff_chunks = tuple(sorted(set(bounds)))

    n_rows = n_tok // GRID_W
    rows = jnp.repeat(jnp.arange(n_rows, dtype=F32), GRID_W)
    cols = jnp.tile(jnp.arange(GRID_W, dtype=F32), n_rows)
    n_freq = HEAD_DIM // 4
    inv = ROPE_THETA ** (-jnp.arange(n_freq, dtype=F32) / n_freq)
    ang = jnp.concatenate([rows[:, None] * inv, cols[:, None] * inv], axis=-1)
    ang = jnp.concatenate([jnp.zeros((ctx_len, HALF), F32), ang], axis=0)
    cosT, sinT = jnp.cos(ang).T, jnp.sin(ang).T

    sig_f, fil_f, g_tab, h_tab, inv_f = _dft_tables(n_tok)
    csig_f, cfil_f, cinv_f = _dense_dft_tables(ctx_len)
    z_lat = _filter_features(n_tok)
    z_ctx = _filter_features(ctx_len)
    max_decay = math.log(HY_TARGET) / HY_FAST_DECAY
    min_decay = math.log(HY_TARGET) / HY_SLOW_DECAY
    absd = jnp.tile(jnp.abs(jnp.linspace(min_decay, max_decay, HY_WIDTH, dtype=F32)), 4)[None, :]
    t1n = n_tok // DFT_N2
    n1 = 2 * t1n
    big = n1 * DFT_N2
    fb = 8

    perm = np.concatenate([np.arange(0, HEAD_DIM, 2), np.arange(1, HEAD_DIM, 2)])
    qk_cols = np.concatenate(
        [OFF_QA + b * HEAD_DIM + perm for b in range(A_HEADS)]
        + [OFF_QD + b * HEAD_DIM + perm for b in range(2 * D_HEADS)]
        + [OFF_KA + b * HEAD_DIM + perm for b in range(A_KV_HEADS)]
        + [OFF_KD + b * HEAD_DIM + perm for b in range(2 * D_HEADS)])
    v_cols = np.concatenate([np.arange(OFF_VA, OFF_VA + A_KV_WIDTH), np.arange(OFF_VD, OFF_VD + D_WIDTH)])
    qkv_cols = np.concatenate([qk_cols, v_cols])
    wqT = jnp.swapaxes(w_in[:, :, qkv_cols], 1, 2).astype(BF16)
    why = w_in[:, :, OFF_HY:OFF_HY + HY_IN].astype(BF16)
    q_scale = HEAD_DIM ** -0.5 * math.log2(math.e)
    gains = jnp.concatenate(
        [jnp.tile(qn_a[:, perm], (1, A_HEADS)) * q_scale, jnp.tile(qn_d[:, perm], (1, 2 * D_HEADS)) * q_scale,
         jnp.tile(kn_a[:, perm], (1, A_KV_HEADS)), jnp.tile(kn_d[:, perm], (1, 2 * D_HEADS))],
        axis=1)[:, :, None].astype(F32)
    w_out_b = w_out.astype(BF16)
    w_up_a = w_up[:, :, :d_ff].astype(BF16)
    w_up_v = w_up[:, :, d_ff:].astype(BF16)
    w_down_b = w_down.astype(BF16)
    fw1p = jnp.pad(hy_fw1, ((0, 0), (0, LANE - HY_EMB_DIM), (0, 0)))
    lamv = jnp.stack([lam_q1, lam_k1, lam_q2, lam_k2], axis=1)
    lam_init = jnp.asarray([0.8 - 0.6 * math.exp(-0.3 * i) for i in range(depth)], F32)
    lami = jnp.broadcast_to(lam_init[:, None, None], (depth, 8, LANE))

    n_c = 8 * ((bsz + 1 + 7) // 8)
    cvec = jnp.concatenate([c, c_ctx[None, :], jnp.zeros((n_c - bsz - 1, d), F32)], axis=0)
    mods = _modulation(cvec, w_mod, b_mod).reshape(depth, n_c, 6, d)
    pad2 = jnp.zeros((depth, bsz, 2, d), F32)
    mod_all = jnp.concatenate(
        [jnp.broadcast_to(mods[:, bsz:bsz + 1], (depth, bsz, 6, d)), pad2, mods[:, :bsz], pad2], axis=2)

    xs = jnp.concatenate([ctx, x], axis=1)
    for i in range(depth):
        mod = mod_all[i]
        qT, k, vTa, vTd, hy = _projection(xs, mod, norm1_g[i][None, :], wqT[i], why[i], gains[i], cosT, sinT,
                                    ctx_len=ctx_len, tm=tm)
        ya = _attention_a(qT, k, vTa, g_out[i, :A_WIDTH, None], ctx_len=ctx_len, tq=tq, tk=tk)
        yd = _attention_d(qT, k, vTd, g_out[i, A_WIDTH + HY_WIDTH:].reshape(D_HEADS, D_VALUE_DIM, 1),
                          lamv[i], lami[i], ctx_len=ctx_len, tq=tq, tk=tk)

        u_lat, u_ctx = _hyena_short_conv(hy, hy_conv_w[i], hy_conv_b[i], ctx_len=ctx_len)
        fil_args = (fw1p[i], hy_fb1[i][None], hy_fw2[i], hy_fb2[i][None], hy_fw3[i], hy_fb3[i][None],
                    hy_freq[i][None], absd)
        h_lat, sum_lat = _hyena_filters(z_lat, *fil_args)
        h_ctx, sum_ctx = _hyena_filters(z_ctx, *fil_args)
        ss_lat = jnp.swapaxes(sum_lat.reshape(2, 2, HY_WIDTH), 0, 1)
        ss_ctx = jnp.swapaxes(sum_ctx.reshape(2, 2, HY_WIDTH), 0, 1)
        h_ctx = jnp.transpose(h_ctx, (2, 1, 0, 3)).reshape(ctx_len, 2, 2, HY_WIDTH)
        k_ctx = _circular_filters(h_ctx, 2 * ctx_len)

        ak = _outer_forward(fil_f[0], fil_f[1], h_lat.reshape(1, 2, 4, t1n, DFT_N2, LANE), 0, 1)
        kf = _filter_spectrum(g_tab[0], g_tab[1], ak, h_lat, ss_lat, fb=fb)
        p_n = bsz // 2
        u6 = u_lat.reshape(3, 2, p_n, 2 * t1n, DFT_N2, LANE)

        def long_conv(z6, g_z, g_gate, order):
            a = _outer_forward(sig_f[0], sig_f[1], z6, g_z, 1)
            cc = _spectral_multiply(g_tab[0], h_tab[0], kf, a, order, fb=fb)
            return _outer_inverse(inv_f[0], cc, u6, g_gate, z6, g_z, hy_bias[i, order][None, :])

        z1 = long_conv(u6, 0, 1, 0)
        yh2 = long_conv(z1[None], 0, 2, 1).reshape(2, bsz, n_tok, LANE)
        yh_lat = jnp.concatenate([yh2[0], yh2[1]], axis=-1)

        kf_c = _ctx_filter_spectrum(cfil_f[0], cfil_f[1], k_ctx, ss_ctx)
        yh_ctx = _ctx_hyena(csig_f[0], csig_f[1], cinv_f[0], cinv_f[1], kf_c,
                            u_ctx.reshape(3, p_n, 2 * ctx_len, HY_WIDTH), hy_bias[i])
        yh = jnp.concatenate([yh_ctx.reshape(bsz, ctx_len, HY_WIDTH), yh_lat], axis=1)

        xs = _merge(xs, ya, yh, yd, g_out[i, A_WIDTH:A_WIDTH + HY_WIDTH][None, :], w_out_b[i], mod,
                    ctx_len=ctx_len, tm=tm)
        xs = _ffn(xs, mod, norm2_g[i][None, :], w_up_a[i], w_up_v[i], ffn_conv_w[i], ffn_conv_b[i][None, :],
                  w_down_b[i], ctx_len=ctx_len, tm=tm, chunks=ff_chunks)
    return xs[:, ctx_len:]
```

```python
import functools
import math

import numpy as np
import jax
import jax.numpy as jnp
from jax import lax
from jax.experimental import pallas as pl
from jax.experimental.pallas import tpu as pltpu

F32 = jnp.float32
BF16 = jnp.bfloat16

HEAD_DIM = 64
HALF = HEAD_DIM // 2
A_HEADS = 4
A_KV_HEADS = 2
A_WIDTH = A_HEADS * HEAD_DIM
A_KV_WIDTH = A_KV_HEADS * HEAD_DIM
HY_WIDTH = 256
HY_IN = 3 * HY_WIDTH
HY_BANDS = 16
HY_EMB_DIM = 1 + 2 * HY_BANDS
HY_HIDDEN = 64
HY_FILTER_CH = 4 * HY_WIDTH
HY_FAST_DECAY = 0.3
HY_SLOW_DECAY = 1.5
HY_TARGET = 1e-2
D_HEADS = 4
D_VALUE_DIM = 2 * HEAD_DIM
D_WIDTH = D_HEADS * D_VALUE_DIM
D_QK_WIDTH = D_HEADS * 2 * HEAD_DIM
MIX_WIDTH = A_WIDTH + HY_WIDTH + D_WIDTH
OFF_QA = 0
OFF_KA = A_WIDTH
OFF_VA = OFF_KA + A_KV_WIDTH
OFF_HY = OFF_VA + A_KV_WIDTH
OFF_QD = OFF_HY + HY_IN
OFF_KD = OFF_QD + D_QK_WIDTH
OFF_VD = OFF_KD + D_QK_WIDTH
GRID_W = 64
ROPE_THETA = 10000.0
EPS = 1e-6

N_QBLK = A_HEADS + 2 * D_HEADS
N_KBLK = A_KV_HEADS + 2 * D_HEADS
Q_ROWS = N_QBLK * HEAD_DIM
K_ROWS = N_KBLK * HEAD_DIM
V_ROWS = A_KV_WIDTH + D_WIDTH
QK_ROWS = Q_ROWS + K_ROWS
QKV_ROWS = QK_ROWS + V_ROWS
SUM_ROWS = 16
VA_BLK = HEAD_DIM + SUM_ROWS

LANE = 128
DFT_N2 = 128
KEY_BLOCK = 768
MAX_UNROLLED_CHUNKS = 4
VMEM_LIMIT = 56 * 1024 * 1024
HIGHEST = lax.Precision.HIGHEST


def _cparams(sem):
    return pltpu.CompilerParams(dimension_semantics=sem, vmem_limit_bytes=VMEM_LIMIT)


def _split_bf16(x):
    hi = x.astype(BF16)
    lo = (x - hi.astype(F32)).astype(BF16)
    return hi, lo


def _dot(a, b):
    return jnp.dot(a, b, preferred_element_type=F32)


def _dot3(a_hi, a_lo, b):
    b_hi, b_lo = _split_bf16(b)
    return _dot(a_hi, b_hi) + (_dot(a_hi, b_lo) + _dot(a_lo, b_hi))


def _dot_p(a_hi, a_lo, b, passes):
    return _dot3(a_hi, a_lo, b) if passes == 3 else _dot(a_hi, b.astype(BF16))


def _mod_kernel(c_ref, w_ref, b_ref, o_ref):
    c = c_ref[...]
    s = c * (1.0 / (1.0 + jnp.exp(-c)))
    o_ref[0] = jnp.dot(s, w_ref[0], preferred_element_type=F32, precision=HIGHEST) + b_ref[0]


def _modulation(cvec, w_mod, b_mod):
    depth, d, n6 = w_mod.shape
    tn = 1024
    return pl.pallas_call(
        _mod_kernel,
        out_shape=jax.ShapeDtypeStruct((depth, cvec.shape[0], n6), F32),
        grid=(depth, n6 // tn),
        in_specs=[
            pl.BlockSpec(cvec.shape, lambda l, j: (0, 0)),
            pl.BlockSpec((1, d, tn), lambda l, j: (l, 0, j)),
            pl.BlockSpec((1, 1, tn), lambda l, j: (l, 0, j)),
        ],
        out_specs=pl.BlockSpec((1, cvec.shape[0], tn), lambda l, j: (l, 0, j)),
        compiler_params=_cparams(("arbitrary", "arbitrary")),
        name="modulation",
    )(cvec, w_mod, b_mod.reshape(depth, 1, n6))


def _norm_modulate(x, g, mod_ref, is_ctx, which):
    ms = jnp.mean(x * x, axis=-1, keepdims=True)
    xn = x * lax.rsqrt(ms + EPS) * g
    sh = jnp.where(is_ctx, mod_ref[0, 3 * which:3 * which + 1, :], mod_ref[0, 8 + 3 * which:9 + 3 * which, :])
    sc = jnp.where(is_ctx, mod_ref[0, 3 * which + 1:3 * which + 2, :],
                   mod_ref[0, 9 + 3 * which:10 + 3 * which, :])
    return xn * (1.0 + sc) + sh


def _proj_kernel(x_ref, mod_ref, g_ref, wq_ref, why_ref, gain_ref, cos_ref, sin_ref,
                 qT_ref, k_ref, vTa_ref, vTd_ref, hy_ref, *, tm, ctx_len):
    i = pl.program_id(1)
    row = i * tm + lax.broadcasted_iota(jnp.int32, (tm, 1), 0)
    h = _norm_modulate(x_ref[0], g_ref[...], mod_ref, row < ctx_len, 0).astype(BF16)
    pT = lax.dot_general(wq_ref[...], h, (((1,), (1,)), ((), ())), preferred_element_type=F32)
    hy_ref[0] = _dot(h, why_ref[...])
    c = cos_ref[...]
    s = sin_ref[...]
    blocks = []
    for b in range(N_QBLK + N_KBLK):
        blk = pT[b * HEAD_DIM:(b + 1) * HEAD_DIM]
        ssq = jnp.sum(blk * blk, axis=0, keepdims=True)
        blk = blk * lax.rsqrt(ssq * (1.0 / HEAD_DIM) + EPS) * gain_ref[b * HEAD_DIM:(b + 1) * HEAD_DIM, :]
        x1 = blk[:HALF]
        x2 = blk[HALF:]
        blocks.append(x1 * c - x2 * s)
        blocks.append(x1 * s + x2 * c)
    qT_ref[0] = jnp.concatenate(blocks[:2 * N_QBLK], axis=0).astype(BF16)
    kT = jnp.concatenate(blocks[2 * N_QBLK:], axis=0)
    k_ref[0] = kT.T.astype(BF16)
    ones = jnp.ones((SUM_ROWS, tm), BF16)
    va = [pT[QK_ROWS + g * HEAD_DIM:QK_ROWS + (g + 1) * HEAD_DIM].astype(BF16) for g in range(A_KV_HEADS)]
    vTa_ref[0] = jnp.concatenate([t for v in va for t in (v, ones)], axis=0)
    vTd_ref[0] = pT[QK_ROWS + A_KV_WIDTH:].astype(BF16)


def _projection(xs, mod, g, wqT, why, gains, cosT, sinT, *, ctx_len, tm):
    bsz, s_len, d = xs.shape
    kern = functools.partial(_proj_kernel, tm=tm, ctx_len=ctx_len)
    return pl.pallas_call(
        kern,
        out_shape=(
            jax.ShapeDtypeStruct((bsz, Q_ROWS, s_len), BF16),
            jax.ShapeDtypeStruct((bsz, s_len, K_ROWS), BF16),
            jax.ShapeDtypeStruct((bsz, A_KV_HEADS * VA_BLK, s_len), BF16),
            jax.ShapeDtypeStruct((bsz, D_WIDTH, s_len), BF16),
            jax.ShapeDtypeStruct((bsz, s_len, HY_IN), F32),
        ),
        grid=(bsz, s_len // tm),
        in_specs=[
            pl.BlockSpec((1, tm, d), lambda b, i: (b, i, 0)),
            pl.BlockSpec((1, 16, d), lambda b, i: (b, 0, 0)),
            pl.BlockSpec((1, d), lambda b, i: (0, 0)),
            pl.BlockSpec((QKV_ROWS, d), lambda b, i: (0, 0)),
            pl.BlockSpec((d, HY_IN), lambda b, i: (0, 0)),
            pl.BlockSpec((QK_ROWS, 1), lambda b, i: (0, 0)),
            pl.BlockSpec((HALF, tm), lambda b, i: (0, i)),
            pl.BlockSpec((HALF, tm), lambda b, i: (0, i)),
        ],
        out_specs=(
            pl.BlockSpec((1, Q_ROWS, tm), lambda b, i: (b, 0, i)),
            pl.BlockSpec((1, tm, K_ROWS), lambda b, i: (b, i, 0)),
            pl.BlockSpec((1, A_KV_HEADS * VA_BLK, tm), lambda b, i: (b, 0, i)),
            pl.BlockSpec((1, D_WIDTH, tm), lambda b, i: (b, 0, i)),
            pl.BlockSpec((1, tm, HY_IN), lambda b, i: (b, i, 0)),
        ),
        compiler_params=_cparams(("parallel", "parallel")),
        name="projection",
    )(xs, mod, g, wqT, why, gains, cosT, sinT)


def _attend(units, scr, *, ctx_only, ctx_len, tk, sum_rows):
    n_u = len(units)
    qz, m_run, l_run, acc, s_buf, cm_buf, qz0 = _split_scratch(scr, n_u)
    s_len = units[0][0].shape[1]
    for u in range(n_u):
        m_run[u][...] = jnp.full(m_run[u].shape, -1e30, F32)
        l_run[u][...] = jnp.zeros(l_run[u].shape, F32)
        acc[u][...] = jnp.zeros(acc[u].shape, F32)

    def fold8(s, op):
        return op(s.reshape(s.shape[0] // 8, 8, s.shape[1]), axis=0)

    def key_rows(start, j, rb):
        if isinstance(start, int):
            return pl.ds(start + j * rb, rb)
        return pl.ds(pl.multiple_of(start + j * rb, rb), rb)

    def advance(nxt, cur, size):
        rb = min(size, KEY_BLOCK)
        if cur is not None:
            u_c, start_c, slot_c = cur
            _, vT_ref, v0, dv = units[u_c]
            m = m_run[u_c][...]
            m_new = jnp.maximum(m, jnp.max(cm_buf[slot_c][...], axis=0, keepdims=True))
            m_run[u_c][...] = m_new
            alpha = jnp.exp2(m - m_new)
        cmax, psum, pv = None, None, None
        for j in range(size // rb):
            rows = slice(j * rb, (j + 1) * rb)
            if nxt is not None:
                u_n, start_n, slot_n = nxt[:3]
                q_n = qz[u_n][...] if len(nxt) == 3 else qz0[nxt[3]]
                s = _dot(units[u_n][0][0, key_rows(start_n, j, rb), :], q_n)
                s_buf[slot_n][rows, :] = s
                cmax = fold8(s, jnp.max) if cmax is None else jnp.maximum(cmax, fold8(s, jnp.max))
            if cur is not None:
                p = jnp.exp2(s_buf[slot_c][rows, :] - m_new)
                if not sum_rows:
                    psum = fold8(p, jnp.sum) if psum is None else psum + fold8(p, jnp.sum)
                d = _dot(vT_ref[0, v0:v0 + dv, key_rows(start_c, j, rb)], p.astype(BF16))
                pv = d if pv is None else pv + d
        if nxt is not None:
            cm_buf[slot_n][...] = cmax
        if cur is not None:
            if not sum_rows:
                l_run[u_c][...] = alpha * l_run[u_c][...] + jnp.sum(psum, axis=0, keepdims=True)
            acc[u_c][...] = alpha * acc[u_c][...] + pv

    if ctx_only:
        for u in range(n_u):
            advance((u, 0, u % 2), None, ctx_len)
            advance(None, (u, 0, u % 2), ctx_len)
        advance((0, 0, 0, 1), None, tk)
    else:
        n_chunks = s_len // tk

        def chunk_steps(c, start, nxt, wrap):
            for u in range(n_u):
                task_n = (u + 1, start, (u + 1) % 2) if u + 1 < n_u else (0, nxt, (u + 1) % 2, wrap)
                advance(task_n, (u, start, u % 2), tk)

        if n_chunks <= MAX_UNROLLED_CHUNKS:
            for c in range(n_chunks):
                last = c == n_chunks - 1
                chunk_steps(c, c * tk, 0 if last else (c + 1) * tk, int(last))
        else:
            def body(c, carry):
                wrap = jnp.where(c == n_chunks - 1, 1, 0).astype(jnp.int32)
                chunk_steps(c, pl.multiple_of(c * tk, tk), pl.multiple_of((1 - wrap) * (c + 1) * tk, tk), wrap)
                return carry
            lax.fori_loop(0, n_chunks, body, 0)
    outs = []
    for u in range(n_u):
        dv = units[u][3] - sum_rows
        l = acc[u][dv:dv + 1, :] if sum_rows else l_run[u][...]
        outs.append(acc[u][:dv, :] * (1.0 / l))
    return outs


def _attn_scratch(n_u, rows, tq, tk):
    per_unit = [((LANE, tq), BF16), ((1, tq), F32), ((1, tq), F32), ((rows, tq), F32)]
    per_slot = [((tk, tq), F32), ((8, tq), F32)]
    return ([pltpu.VMEM(s, d) for s, d in per_unit for _ in range(n_u)]
            + [pltpu.VMEM(s, d) for s, d in per_slot for _ in range(2)]
            + [pltpu.VMEM((2, LANE, tq), BF16)])


def _split_scratch(scr, n_u):
    groups = [scr[i * n_u:(i + 1) * n_u] for i in range(4)]
    rest = scr[4 * n_u:]
    return groups + [rest[0:2], rest[2:4], rest[4]]


def _run_ctx_or_latent(qi, n_ctx_tiles, run):
    @pl.when(qi < n_ctx_tiles)
    def _():
        run(True)

    @pl.when(qi >= n_ctx_tiles)
    def _():
        run(False)


def _attn_a_kernel(qT_ref, qn_ref, k_ref, vT_ref, g_ref, o_ref, *scr, ctx_len, tq, tk):
    zeros = jnp.zeros((HEAD_DIM, tq), BF16)
    group = A_HEADS // A_KV_HEADS
    for h in range(A_HEADS):
        q = qT_ref[0, h * HEAD_DIM:(h + 1) * HEAD_DIM, :]
        scr[h][...] = jnp.concatenate([q, zeros] if h // group == 0 else [zeros, q], axis=0)
    scr[-1][0] = jnp.concatenate([qT_ref[0, :HEAD_DIM, :], zeros], axis=0)
    scr[-1][1] = jnp.concatenate([qn_ref[0, :HEAD_DIM, :], zeros], axis=0)
    units = [(k_ref, vT_ref, (h // group) * VA_BLK, VA_BLK) for h in range(A_HEADS)]

    def run(ctx_only):
        outs = _attend(units, scr, ctx_only=ctx_only, ctx_len=ctx_len, tk=tk, sum_rows=SUM_ROWS)
        y = jnp.concatenate(outs, axis=0)
        ms = jnp.mean(y * y, axis=0, keepdims=True)
        y = y * lax.rsqrt(ms + EPS) * g_ref[...]
        o_ref[0] = y.T.astype(BF16)

    _run_ctx_or_latent(pl.program_id(1), ctx_len // tq, run)


def _attention_a(qT, k, vT, g_a, *, ctx_len, tq, tk):
    bsz, _, s_len = qT.shape
    kern = functools.partial(_attn_a_kernel, ctx_len=ctx_len, tq=tq, tk=tk)
    return pl.pallas_call(
        kern,
        out_shape=jax.ShapeDtypeStruct((bsz, s_len, A_WIDTH), BF16),
        grid=(bsz, s_len // tq),
        in_specs=[
            pl.BlockSpec((1, A_WIDTH, tq), lambda b, i: (b, 0, i)),
            pl.BlockSpec((1, A_WIDTH, tq), lambda b, i: (b, 0, jnp.minimum(i + 1, s_len // tq - 1))),
            pl.BlockSpec((1, s_len, LANE), lambda b, i: (b, 0, 0)),
            pl.BlockSpec((1, A_KV_HEADS * VA_BLK, s_len), lambda b, i: (b, 0, 0)),
            pl.BlockSpec((A_WIDTH, 1), lambda b, i: (0, 0)),
        ],
        out_specs=pl.BlockSpec((1, tq, A_WIDTH), lambda b, i: (b, i, 0)),
        scratch_shapes=_attn_scratch(A_HEADS, VA_BLK, tq, tk),
        compiler_params=_cparams(("parallel", "arbitrary")),
        name="attention_gqa",
    )(qT, qT, k, vT, g_a)


D_HEADS_PER_STEP = 4


def _attn_d_kernel(*refs, ctx_len, tq, tk):
    n_h = D_HEADS_PER_STEP
    q_refs, k_refs, v_refs = refs[:n_h], refs[n_h:2 * n_h], refs[2 * n_h:3 * n_h]
    qn_ref, g_ref, lamv_ref, lami_ref, o_ref = refs[3 * n_h:3 * n_h + 5]
    scr = refs[3 * n_h + 5:]
    zeros = jnp.zeros((HEAD_DIM, tq), BF16)
    lv = lamv_ref[...]
    lam_init = lami_ref[0:1, 0:1]
    lam = (jnp.exp(jnp.sum(lv[0:1] * lv[1:2], axis=-1, keepdims=True))
           - jnp.exp(jnp.sum(lv[2:3] * lv[3:4], axis=-1, keepdims=True)) + lam_init)
    units = []
    for h in range(n_h):
        scr[2 * h][...] = jnp.concatenate([q_refs[h][0, :HEAD_DIM, :], zeros], axis=0)
        scr[2 * h + 1][...] = jnp.concatenate([zeros, q_refs[h][0, HEAD_DIM:, :]], axis=0)
        units += [(k_refs[h], v_refs[h], 0, D_VALUE_DIM)] * 2
    scr[-1][0] = jnp.concatenate([q_refs[0][0, :HEAD_DIM, :], zeros], axis=0)
    scr[-1][1] = jnp.concatenate([qn_ref[0, :HEAD_DIM, :], zeros], axis=0)

    def run(ctx_only):
        outs = _attend(units, scr, ctx_only=ctx_only, ctx_len=ctx_len, tk=tk, sum_rows=0)
        ys = []
        for h in range(n_h):
            y = outs[2 * h] - lam * outs[2 * h + 1]
            ms = jnp.mean(y * y, axis=0, keepdims=True)
            ys.append(y * lax.rsqrt(ms + EPS) * g_ref[h] * (1.0 - lam_init))
        o_ref[0] = jnp.concatenate(ys, axis=0).T.astype(BF16)

    _run_ctx_or_latent(pl.program_id(2), ctx_len // tq, run)


def _attention_d(qT, k, vT, g_d, lamv, lami, *, ctx_len, tq, tk):
    bsz, _, s_len = qT.shape
    n_h = D_HEADS_PER_STEP
    kern = functools.partial(_attn_d_kernel, ctx_len=ctx_len, tq=tq, tk=tk)
    q_blk0 = A_WIDTH // LANE
    k_blk0 = A_KV_WIDTH // LANE
    q_specs = [pl.BlockSpec((1, LANE, tq), lambda b, hp, i, j=j: (b, q_blk0 + n_h * hp + j, i)) for j in range(n_h)]
    once = pl.Buffered(1)
    k_specs = [pl.BlockSpec((1, s_len, LANE), lambda b, hp, i, j=j: (b, 0, k_blk0 + n_h * hp + j), pipeline_mode=once)
               for j in range(n_h)]
    v_specs = [pl.BlockSpec((1, D_VALUE_DIM, s_len), lambda b, hp, i, j=j: (b, n_h * hp + j, 0), pipeline_mode=once)
               for j in range(n_h)]
    return pl.pallas_call(
        kern,
        out_shape=jax.ShapeDtypeStruct((bsz, s_len, D_WIDTH), BF16),
        grid=(bsz, D_HEADS // n_h, s_len // tq),
        in_specs=q_specs + k_specs + v_specs + [
            pl.BlockSpec((1, LANE, tq), lambda b, hp, i: (b, q_blk0 + n_h * hp, jnp.minimum(i + 1, s_len // tq - 1))),
            pl.BlockSpec((n_h, D_VALUE_DIM, 1), lambda b, hp, i: (hp, 0, 0)),
            pl.BlockSpec((4, HEAD_DIM), lambda b, hp, i: (0, 0)),
            pl.BlockSpec((8, LANE), lambda b, hp, i: (0, 0)),
        ],
        out_specs=pl.BlockSpec((1, tq, n_h * D_VALUE_DIM), lambda b, hp, i: (b, i, hp)),
        scratch_shapes=_attn_scratch(2 * n_h, D_VALUE_DIM, tq, tk),
        compiler_params=_cparams(("parallel", "parallel", "arbitrary")),
        name="attention_diff",
    )(*([qT] * n_h + [k] * n_h + [vT] * n_h + [qT, g_d, lamv, lami]))


def _dwconv3_seq(u, w_ref, b_ref):
    n = u.shape[0]
    row = lax.broadcasted_iota(jnp.int32, (n, 1), 0)
    prev = jnp.where(row == 0, 0.0, pltpu.roll(u, 1, axis=0))
    nxt = jnp.where(row == n - 1, 0.0, pltpu.roll(u, n - 1, axis=0))
    return prev * w_ref[0:1, :] + u * w_ref[1:2, :] + nxt * w_ref[2:3, :] + b_ref[...]


def _hy_conv_kernel(hy_ref, w_ref, b_ref, lat_ref, ctx_ref, *, ctx_len):
    ctx_ref[0, 0] = _dwconv3_seq(hy_ref[0, :ctx_len, :], w_ref, b_ref)
    lat_ref[0, 0, 0] = _dwconv3_seq(hy_ref[0, ctx_len:, :], w_ref, b_ref)


def _hyena_short_conv(hy, w, b, *, ctx_len):
    bsz, s_len, _ = hy.shape
    n = s_len - ctx_len
    per = HY_WIDTH // LANE
    kern = functools.partial(_hy_conv_kernel, ctx_len=ctx_len)
    return pl.pallas_call(
        kern,
        out_shape=(
            jax.ShapeDtypeStruct((3, per, bsz, n, LANE), F32),
            jax.ShapeDtypeStruct((3, bsz, ctx_len, HY_WIDTH), F32),
        ),
        grid=(bsz, HY_IN // LANE),
        in_specs=[
            pl.BlockSpec((1, s_len, LANE), lambda b, j: (b, 0, j)),
            pl.BlockSpec((3, LANE), lambda b, j: (0, j)),
            pl.BlockSpec((1, LANE), lambda b, j: (0, j)),
        ],
        out_specs=(
            pl.BlockSpec((1, 1, 1, n, LANE), lambda b, j: (j // per, j % per, b, 0, 0)),
            pl.BlockSpec((1, 1, ctx_len, LANE), lambda b, j: (j // per, b, 0, j % per)),
        ),
        compiler_params=_cparams(("parallel", "parallel")),
        name="hyena_short_conv",
    )(hy, w, b.reshape(1, HY_IN))


def _filter_kernel(z_ref, w1_ref, b1_ref, w2_ref, b2_ref, w3_ref, b3_ref, fr_ref, ad_ref, h_ref, sum_ref):
    i = pl.program_id(0)
    z = z_ref[...]
    fr = fr_ref[...]
    h = jnp.sin(fr * (jnp.dot(z, w1_ref[...], preferred_element_type=F32, precision=HIGHEST) + b1_ref[...]))
    h = jnp.sin(fr * (jnp.dot(h, w2_ref[...], preferred_element_type=F32, precision=HIGHEST) + b2_ref[...]))
    h = jnp.dot(h, w3_ref[...], preferred_element_type=F32, precision=HIGHEST) + b3_ref[...]
    h = h * jnp.exp(-z[:, 0:1] * ad_ref[...])
    for q in range(HY_FILTER_CH // HY_WIDTH):
        for hh in range(HY_WIDTH // LANE):
            c0 = q * HY_WIDTH + hh * LANE
            h_ref[hh, q] = h[:, c0:c0 + LANE]

    @pl.when(i == 0)
    def _():
        sum_ref[...] = jnp.zeros_like(sum_ref)

    sum_ref[...] += jnp.sum(jnp.abs(h), axis=0, keepdims=True)


def _hyena_filters(zfeat, w1, b1, w2, b2, w3, b3, freq, absd):
    n, kz = zfeat.shape
    tn = min(n, 1024)
    full = lambda a: pl.BlockSpec(a.shape, lambda i: (0, 0))
    return pl.pallas_call(
        _filter_kernel,
        out_shape=(
            jax.ShapeDtypeStruct((HY_WIDTH // LANE, HY_FILTER_CH // HY_WIDTH, n, LANE), F32),
            jax.ShapeDtypeStruct((1, HY_FILTER_CH), F32),
        ),
        grid=(n // tn,),
        in_specs=[pl.BlockSpec((tn, kz), lambda i: (i, 0)), full(w1), full(b1), full(w2), full(b2),
                  full(w3), full(b3), full(freq), full(absd)],
        out_specs=(
            pl.BlockSpec((HY_WIDTH // LANE, HY_FILTER_CH // HY_WIDTH, tn, LANE), lambda i: (0, 0, i, 0)),
            pl.BlockSpec((1, HY_FILTER_CH), lambda i: (0, 0)),
        ),
        compiler_params=_cparams(("arbitrary",)),
        name="hyena_filters",
    )(zfeat, w1, b1, w2, b2, w3, b3, freq, absd)


T2_BLOCK = 8


def _outer_fwd_kernel(mh_ref, ml_ref, x_ref, o_ref, *, n1, passes):
    rows = x_ref.shape[3]
    xs = [x_ref.at[0, h, 0].reshape(rows * T2_BLOCK, LANE) for h in range(2)]
    outs = [[o_ref.at[0, ri, h].reshape(n1 * T2_BLOCK, LANE) for h in range(2)] for ri in range(2)]
    for t in range(T2_BLOCK):
        sel = pl.ds(t, rows, stride=T2_BLOCK)
        y = _dot_p(mh_ref[...], ml_ref[...], jnp.concatenate([xs[0][sel, :], xs[1][sel, :]], axis=1), passes)
        for ri in range(2):
            for h in range(2):
                outs[ri][h][pl.ds(t, n1, stride=T2_BLOCK), :] = y[ri * n1:(ri + 1) * n1, h * LANE:(h + 1) * LANE]


def _outer_forward(m_hi, m_lo, x6, g, passes):
    _, _, p_n, rows, n2, _ = x6.shape
    n1 = m_hi.shape[0] // 2
    kern = functools.partial(_outer_fwd_kernel, n1=n1, passes=passes)
    return pl.pallas_call(
        kern,
        out_shape=jax.ShapeDtypeStruct((p_n, 2, 2, n1, n2, LANE), F32),
        grid=(p_n, n2 // T2_BLOCK),
        in_specs=[pl.BlockSpec(m_hi.shape, lambda p, j: (0, 0)), pl.BlockSpec(m_lo.shape, lambda p, j: (0, 0)),
                  pl.BlockSpec((1, 2, 1, rows, T2_BLOCK, LANE), lambda p, j: (g, 0, p, 0, j, 0))],
        out_specs=pl.BlockSpec((1, 2, 2, n1, T2_BLOCK, LANE), lambda p, j: (p, 0, 0, 0, j, 0)),
        compiler_params=_cparams(("parallel", "parallel")),
        name="dft_outer_fwd",
    )(m_hi, m_lo, x6)


def _outer_inv_kernel(mh_ref, c_ref, gate_ref, u_ref, bias_ref, o_ref):
    n1 = c_ref.shape[3]
    rows = o_ref.shape[2]
    cs = [[c_ref.at[0, ri, h].reshape(n1 * T2_BLOCK, LANE) for h in range(2)] for ri in range(2)]
    gates = [gate_ref.at[0, h, 0].reshape(rows * T2_BLOCK, LANE) for h in range(2)]
    us = [u_ref.at[0, h, 0].reshape(rows * T2_BLOCK, LANE) for h in range(2)]
    outs = [o_ref.at[h, 0].reshape(rows * T2_BLOCK, LANE) for h in range(2)]
    for t in range(T2_BLOCK):
        sel_f = pl.ds(t, n1, stride=T2_BLOCK)
        sel_t = pl.ds(t, rows, stride=T2_BLOCK)
        c2 = jnp.concatenate([jnp.concatenate([cs[ri][0][sel_f, :], cs[ri][1][sel_f, :]], axis=1)
                              for ri in range(2)], axis=0)
        y = _dot(mh_ref[...], c2.astype(BF16))
        for h in range(2):
            lanes = slice(h * LANE, (h + 1) * LANE)
            outs[h][sel_t, :] = gates[h][sel_t, :] * (y[:, lanes] + us[h][sel_t, :] * bias_ref[:, lanes])


def _outer_inverse(m_hi, c6, gate6, g_gate, u6, g_u, bias):
    p_n, _, _, n1, n2, _ = c6.shape
    rows = m_hi.shape[0]
    gspec = lambda g: pl.BlockSpec((1, 2, 1, rows, T2_BLOCK, LANE), lambda p, j: (g, 0, p, 0, j, 0))
    return pl.pallas_call(
        _outer_inv_kernel,
        out_shape=jax.ShapeDtypeStruct((2, p_n, rows, n2, LANE), F32),
        grid=(p_n, n2 // T2_BLOCK),
        in_specs=[pl.BlockSpec(m_hi.shape, lambda p, j: (0, 0)),
                  pl.BlockSpec((1, 2, 2, n1, T2_BLOCK, LANE), lambda p, j: (p, 0, 0, 0, j, 0)),
                  gspec(g_gate), gspec(g_u), pl.BlockSpec((1, 2 * LANE), lambda p, j: (0, 0))],
        out_specs=pl.BlockSpec((2, 1, rows, T2_BLOCK, LANE), lambda p, j: (0, p, 0, j, 0)),
        compiler_params=_cparams(("parallel", "parallel")),
        name="dft_outer_inv",
    )(m_hi, c6, gate6, u6, bias)


def _halves_to_rows(a_ref, j):
    return jnp.concatenate([jnp.concatenate([a_ref[0, ri, 0, j], a_ref[0, ri, 1, j]], axis=1) for ri in range(2)],
                           axis=0)


def _kf_kernel(gh_ref, gl_ref, af_ref, ab_ref, h0_ref, ss_ref, o_ref, *, fb, n2):
    scale = 1.0 / (ss_ref[0, 0:1, :] + ss_ref[0, 1:2, :] + EPS)
    b0 = jnp.concatenate([h0_ref[0, 0, 0:1, :], h0_ref[1, 0, 0:1, :]], axis=1)
    for j in range(fb):
        xf = _dot3(gh_ref[j], gl_ref[j], _halves_to_rows(af_ref, j))
        xb = _dot3(gh_ref[j], gl_ref[j], _halves_to_rows(ab_ref, j))
        o_ref[0, j, :n2, :] = (xf[:n2] + xb[:n2] - b0) * scale
        o_ref[0, j, n2:, :] = (xf[n2:] - xb[n2:]) * scale


def _filter_spectrum(g_hi, g_lo, a6, h6, ss, *, fb):
    n_q, _, _, n1, n2, _ = a6.shape
    n_ord = n_q // 2
    c = 2 * LANE
    kern = functools.partial(_kf_kernel, fb=fb, n2=n2)
    return pl.pallas_call(
        kern,
        out_shape=jax.ShapeDtypeStruct((n_ord, n1, 2 * n2, c), F32),
        grid=(n1 // fb, n_ord),
        in_specs=[
            pl.BlockSpec((fb, 2 * n2, 2 * n2), lambda f, o: (f, 0, 0)),
            pl.BlockSpec((fb, 2 * n2, 2 * n2), lambda f, o: (f, 0, 0)),
            pl.BlockSpec((1, 2, 2, fb, n2, LANE), lambda f, o: (o, 0, 0, f, 0, 0)),
            pl.BlockSpec((1, 2, 2, fb, n2, LANE), lambda f, o: (n_ord + o, 0, 0, f, 0, 0)),
            pl.BlockSpec((2, 1, 8, LANE), lambda f, o: (0, n_ord + o, 0, 0)),
            pl.BlockSpec((1, 2, c), lambda f, o: (o, 0, 0)),
        ],
        out_specs=pl.BlockSpec((1, fb, 2 * n2, c), lambda f, o: (o, f, 0, 0)),
        compiler_params=_cparams(("parallel", "parallel")),
        name="filter_spectrum",
    )(g_hi, g_lo, a6, a6, h6, ss)


def _spec_kernel(gh_ref, hh_ref, kf_ref, a_ref, o_ref, *, fb, n2):
    for j in range(fb):
        x = _dot(gh_ref[j], _halves_to_rows(a_ref, j).astype(BF16))
        xr, xi = x[:n2], x[n2:]
        kr, ki = kf_ref[0, j, :n2], kf_ref[0, j, n2:]
        y = jnp.concatenate([xr * kr - xi * ki, xr * ki + xi * kr], axis=0)
        c2 = _dot(hh_ref[j], y.astype(BF16))
        for ri in range(2):
            for h in range(2):
                o_ref[0, ri, h, j] = c2[ri * n2:(ri + 1) * n2, h * LANE:(h + 1) * LANE]


def _spectral_multiply(g_hi, h_hi, kf, a6, order, *, fb):
    p_n, _, _, n1, n2, _ = a6.shape
    c = 2 * LANE
    kern = functools.partial(_spec_kernel, fb=fb, n2=n2)
    gspec = pl.BlockSpec((fb, 2 * n2, 2 * n2), lambda f, p: (f, 0, 0))
    aspec = pl.BlockSpec((1, 2, 2, fb, n2, LANE), lambda f, p: (p, 0, 0, f, 0, 0))
    return pl.pallas_call(
        kern,
        out_shape=jax.ShapeDtypeStruct(a6.shape, F32),
        grid=(n1 // fb, p_n),
        in_specs=[gspec, gspec,
                  pl.BlockSpec((1, fb, 2 * n2, c), lambda f, p: (order, f, 0, 0)), aspec],
        out_specs=aspec,
        compiler_params=_cparams(("parallel", "parallel")),
        name="spectral_multiply",
    )(g_hi, h_hi, kf, a6)


def _ctx_kf_kernel(fh_ref, fl_ref, k_ref, ss_ref, o_ref):
    scale = 1.0 / (ss_ref[0, 0:1, :] + ss_ref[0, 1:2, :] + EPS)
    o_ref[0] = _dot3(fh_ref[...], fl_ref[...], k_ref[0]) * scale


def _ctx_filter_spectrum(f_hi, f_lo, k, ss):
    n_ord, nc, c = k.shape
    return pl.pallas_call(
        _ctx_kf_kernel,
        out_shape=jax.ShapeDtypeStruct((n_ord, 2 * nc, c), F32),
        grid=(n_ord,),
        in_specs=[
            pl.BlockSpec(f_hi.shape, lambda o: (0, 0)),
            pl.BlockSpec(f_lo.shape, lambda o: (0, 0)),
            pl.BlockSpec((1, nc, c), lambda o: (o, 0, 0)),
            pl.BlockSpec((1, 2, c), lambda o: (o, 0, 0)),
        ],
        out_specs=pl.BlockSpec((1, 2 * nc, c), lambda o: (o, 0, 0)),
        compiler_params=_cparams(("parallel",)),
        name="ctx_filter_spectrum",
    )(f_hi, f_lo, k, ss)


def _ctx_conv_kernel(fh_ref, fl_ref, eh_ref, el_ref, kf_ref, u_ref, bias_ref, o_ref, *, nc):
    v, x1, x2 = u_ref[0, 0], u_ref[1, 0], u_ref[2, 0]

    def conv(z, order):
        x = _dot3(fh_ref[...], fl_ref[...], z)
        xr, xi = x[:nc], x[nc:]
        kr, ki = kf_ref[order, :nc], kf_ref[order, nc:]
        y = jnp.concatenate([xr * kr - xi * ki, xr * ki + xi * kr], axis=0)
        return _dot3(eh_ref[...], el_ref[...], y) + z * bias_ref[order:order + 1, :]

    o_ref[0] = x2 * conv(x1 * conv(v, 0), 1)


def _ctx_hyena(f_hi, f_lo, e_hi, e_lo, kf, u3, bias):
    _, p_n, n2x, c = u3.shape
    nc = kf.shape[1] // 2
    kern = functools.partial(_ctx_conv_kernel, nc=nc)
    full = lambda a: pl.BlockSpec(a.shape, lambda p: (0,) * a.ndim)
    return pl.pallas_call(
        kern,
        out_shape=jax.ShapeDtypeStruct((p_n, n2x, c), F32),
        grid=(p_n,),
        in_specs=[full(f_hi), full(f_lo), full(e_hi), full(e_lo), full(kf),
                  pl.BlockSpec((3, 1, n2x, c), lambda p: (0, p, 0, 0)), full(bias)],
        out_specs=pl.BlockSpec((1, n2x, c), lambda p: (p, 0, 0)),
        compiler_params=_cparams(("parallel",)),
        name="ctx_hyena",
    )(f_hi, f_lo, e_hi, e_lo, kf, u3, bias)


def _merge_kernel(x_ref, ya_ref, yh_ref, yd_ref, gh_ref, w_ref, mod_ref, o_ref, *, tm, ctx_len):
    i = pl.program_id(1)
    row = i * tm + lax.broadcasted_iota(jnp.int32, (tm, 1), 0)
    yh = yh_ref[0]
    ms = jnp.mean(yh * yh, axis=-1, keepdims=True)
    yh = yh * lax.rsqrt(ms + EPS) * gh_ref[...]
    y = _dot(ya_ref[0].astype(BF16), w_ref[:A_WIDTH, :])
    y += _dot(yh.astype(BF16), w_ref[A_WIDTH:A_WIDTH + HY_WIDTH, :])
    y += _dot(yd_ref[0].astype(BF16), w_ref[A_WIDTH + HY_WIDTH:, :])
    gate = jnp.where(row < ctx_len, mod_ref[0, 2:3, :], mod_ref[0, 10:11, :])
    o_ref[0] = x_ref[0] + gate * y


def _merge(xs, ya, yh, yd, g_h, w_out, mod, *, ctx_len, tm):
    bsz, s_len, d = xs.shape
    kern = functools.partial(_merge_kernel, tm=tm, ctx_len=ctx_len)
    tok = lambda w: pl.BlockSpec((1, tm, w), lambda b, i: (b, i, 0))
    return pl.pallas_call(
        kern,
        out_shape=jax.ShapeDtypeStruct(xs.shape, F32),
        grid=(bsz, s_len // tm),
        in_specs=[tok(d), tok(A_WIDTH), tok(HY_WIDTH), tok(D_WIDTH),
                  pl.BlockSpec((1, HY_WIDTH), lambda b, i: (0, 0)),
                  pl.BlockSpec((MIX_WIDTH, d), lambda b, i: (0, 0)),
                  pl.BlockSpec((1, 16, d), lambda b, i: (b, 0, 0))],
        out_specs=tok(d),
        compiler_params=_cparams(("parallel", "parallel")),
        name="merge",
    )(xs, ya, yh, yd, g_h, w_out, mod)


HALO = 8
FF_TILE = 256
FF_CHUNKS = 2


def _ffn_kernel(x_ref, xp_ref, xn_ref, mod_ref, g_ref, wa_ref, wv_ref, cw_ref, cb_ref, wd_ref, o_ref,
                *, tm, chunks, ctx_len, s_len):
    i = pl.program_id(1)
    x = x_ref[0]
    xa = jnp.concatenate([xp_ref[0], x, xn_ref[0]], axis=0)
    rows = tm + 2 * HALO
    row = i * tm - HALO + lax.broadcasted_iota(jnp.int32, (rows, 1), 0)
    h = _norm_modulate(xa, g_ref[...], mod_ref, row < ctx_len, 1).astype(BF16)
    hm = h[HALO:HALO + tm]
    rowm = row[HALO:HALO + tm]
    has_prev = jnp.logical_and(rowm != 0, rowm != ctx_len)
    has_next = jnp.logical_and(rowm != ctx_len - 1, rowm != s_len - 1)
    def up(j):
        cs = slice(chunks[j], chunks[j + 1])
        return _dot(h, wa_ref[:, cs]), _dot(hm, wv_ref[:, cs])

    n_chunks = len(chunks) - 1
    acc = jnp.zeros((tm, x.shape[1]), F32)
    nxt = up(0)
    for j in range(n_chunks):
        cs = slice(chunks[j], chunks[j + 1])
        a, v = nxt
        if j + 1 < n_chunks:
            nxt = up(j + 1)
        a_prev = jnp.where(has_prev, pltpu.roll(a, 1, axis=0)[HALO:HALO + tm], 0.0)
        a_next = jnp.where(has_next, pltpu.roll(a, rows - 1, axis=0)[HALO:HALO + tm], 0.0)
        ac = a_prev * cw_ref[0:1, cs] + a[HALO:HALO + tm] * cw_ref[1:2, cs] + a_next * cw_ref[2:3, cs] + cb_ref[:, cs]
        gl = 0.5 * ac * (1.0 + jnp.tanh(math.sqrt(2.0 / math.pi) * (ac + 0.044715 * (ac * ac * ac))))
        acc += _dot((gl * v).astype(BF16), wd_ref[cs, :])
    gate = jnp.where(rowm < ctx_len, mod_ref[0, 5:6, :], mod_ref[0, 13:14, :])
    o_ref[0] = x + gate * acc


def _ffn(xs, mod, g, wa, wv, cw, cb, wd, *, ctx_len, tm, chunks):
    bsz, s_len, d = xs.shape
    d_ff = wa.shape[1]
    nh = tm // HALO
    last = s_len // HALO - 1
    kern = functools.partial(_ffn_kernel, tm=tm, chunks=chunks, ctx_len=ctx_len, s_len=s_len)
    const = lambda shape: pl.BlockSpec(shape, lambda b, i: (0, 0), pipeline_mode=pl.Buffered(1))
    return pl.pallas_call(
        kern,
        out_shape=jax.ShapeDtypeStruct(xs.shape, F32),
        grid=(bsz, s_len // tm),
        in_specs=[
            pl.BlockSpec((1, tm, d), lambda b, i: (b, i, 0)),
            pl.BlockSpec((1, HALO, d), lambda b, i: (b, jnp.maximum(i * nh - 1, 0), 0)),
            pl.BlockSpec((1, HALO, d), lambda b, i: (b, jnp.minimum((i + 1) * nh, last), 0)),
            pl.BlockSpec((1, 16, d), lambda b, i: (b, 0, 0)),
            pl.BlockSpec((1, d), lambda b, i: (0, 0)),
            const((d, d_ff)), const((d, d_ff)), const((3, d_ff)), const((1, d_ff)), const((d_ff, d)),
        ],
        out_specs=pl.BlockSpec((1, tm, d), lambda b, i: (b, i, 0)),
        compiler_params=_cparams(("parallel", "parallel")),
        name="conv_glu",
    )(xs, xs, xs, mod, g, wa, wv, cw, cb, wd)


def _cis(num, den):
    ang = num.astype(F32) * (2.0 * math.pi / den)
    return jnp.cos(ang), jnp.sin(ang)


def _blockmat(re, im):
    return jnp.concatenate([jnp.concatenate([re, -im], axis=-1), jnp.concatenate([im, re], axis=-1)], axis=-2)


def _dft_tables(n):
    n2 = DFT_N2
    t1n = n // n2
    n1 = 2 * t1n
    big = n1 * n2
    f1 = jnp.arange(n1, dtype=jnp.int32)
    t1 = jnp.arange(t1n, dtype=jnp.int32)
    j2 = jnp.arange(n2, dtype=jnp.int32)
    c, s = _cis((f1[:, None] * t1[None, :]) % n1, n1)
    fwd_sig = _blockmat(c, -s)
    c, s = _cis((f1[:, None] * f1[None, :]) % n1, n1)
    fwd_fil = jnp.concatenate([c, -s], axis=0)[:, :t1n]
    ph = (n1 * j2[None, :, None] * j2[None, None, :] + f1[:, None, None] * j2[None, None, :]) % big
    c, s = _cis(ph, big)
    g = _blockmat(c, -s)
    h = jnp.swapaxes(g, 1, 2)
    c, s = _cis((t1[:, None] * f1[None, :]) % n1, n1)
    inv = _blockmat(c, s) * (1.0 / big)
    return tuple(_split_bf16(m) for m in (fwd_sig, fwd_fil, g, h, inv))


def _dense_dft_tables(n):
    big = 2 * n
    f = jnp.arange(big, dtype=jnp.int32)
    t = jnp.arange(n, dtype=jnp.int32)
    c, s = _cis((f[:, None] * t[None, :]) % big, big)
    fwd_sig = _blockmat(c, -s)
    c, s = _cis((f[:, None] * f[None, :]) % big, big)
    fwd_fil = jnp.concatenate([c, -s], axis=0)
    c, s = _cis((t[:, None] * f[None, :]) % big, big)
    inv = _blockmat(c, s) * (1.0 / big)
    return tuple(_split_bf16(m) for m in (fwd_sig, fwd_fil, inv))


def _filter_features(n):
    pos = jnp.arange(n, dtype=F32)
    t = (pos / max(n - 1, 1))[:, None]
    w = 2.0 * math.pi * pos[:, None] / n
    bands = jnp.linspace(1e-4, HY_BANDS - 1, HY_BANDS, dtype=F32)
    z = jnp.concatenate([t, jnp.cos(bands * w), -jnp.sin(bands * w)], axis=-1)
    return jnp.pad(z, ((0, 0), (0, LANE - HY_EMB_DIM)))


def _circular_filters(h, big):
    n = h.shape[0]
    fwd = jnp.moveaxis(h[:, 0], 1, 0)
    bwd = jnp.moveaxis(h[:, 1], 1, 0)
    zeros = jnp.zeros((fwd.shape[0], big - 2 * n + 1, fwd.shape[2]), F32)
    return jnp.concatenate([fwd, zeros, bwd[:, :0:-1]], axis=1)


def _pick(total, candidates):
    for t in candidates:
        if total % t == 0:
            return t
    raise ValueError(f"no tile for {total}")


def kernel(x, c, ctx, c_ctx, norm1_g, norm2_g, w_mod, b_mod, w_in, qn_a, kn_a, qn_d, kn_d, lam_q1, lam_k1, lam_q2, lam_k2, hy_conv_w, hy_conv_b, hy_fw1, hy_fb1, hy_fw2, hy_fb2, hy_fw3, hy_fb3, hy_freq, hy_bias, g_out, w_out, w_up, ffn_conv_w, ffn_conv_b, w_down):
    bsz, n_tok, d = x.shape
    ctx_len = ctx.shape[1]
    depth = w_in.shape[0]
    s_len = ctx_len + n_tok
    d_ff = w_down.shape[1]
    assert bsz % 2 == 0 and n_tok % (DFT_N2 * 8) == 0 and ctx_len % LANE == 0

    tm = _pick(s_len, (768, 512, 256, 128))
    tq = _pick(ctx_len, (256, 128))
    tk = _pick(s_len, (8448, 2816, 768, 256))
    n_ff_tiles = -(-d_ff // FF_TILE)
    bounds = [min(d_ff, FF_TILE * ((n_ff_tiles * j + FF_CHUNKS - 1) // FF_CHUNKS)) for j in range(FF_CHUNKS + 1)]
    ff_chunks = tuple(sorted(set(bounds)))

    n_rows = n_tok // GRID_W
    rows = jnp.repeat(jnp.arange(n_rows, dtype=F32), GRID_W)
    cols = jnp.tile(jnp.arange(GRID_W, dtype=F32), n_rows)
    n_freq = HEAD_DIM // 4
    inv = ROPE_THETA ** (-jnp.arange(n_freq, dtype=F32) / n_freq)
    ang = jnp.concatenate([rows[:, None] * inv, cols[:, None] * inv], axis=-1)
    ang = jnp.concatenate([jnp.zeros((ctx_len, HALF), F32), ang], axis=0)
    cosT, sinT = jnp.cos(ang).T, jnp.sin(ang).T

    sig_f, fil_f, g_tab, h_tab, inv_f = _dft_tables(n_tok)
    csig_f, cfil_f, cinv_f = _dense_dft_tables(ctx_len)
    z_lat = _filter_features(n_tok)
    z_ctx = _filter_features(ctx_len)
    max_decay = math.log(HY_TARGET) / HY_FAST_DECAY
    min_decay = math.log(HY_TARGET) / HY_SLOW_DECAY
    absd = jnp.tile(jnp.abs(jnp.linspace(min_decay, max_decay, HY_WIDTH, dtype=F32)), 4)[None, :]
    t1n = n_tok // DFT_N2
    n1 = 2 * t1n
    big = n1 * DFT_N2
    fb = 8

    perm = np.concatenate([np.arange(0, HEAD_DIM, 2), np.arange(1, HEAD_DIM, 2)])
    qk_cols = np.concatenate(
        [OFF_QA + b * HEAD_DIM + perm for b in range(A_HEADS)]
        + [OFF_QD + b * HEAD_DIM + perm for b in range(2 * D_HEADS)]
        + [OFF_KA + b * HEAD_DIM + perm for b in range(A_KV_HEADS)]
        + [OFF_KD + b * HEAD_DIM + perm for b in range(2 * D_HEADS)])
    v_cols = np.concatenate([np.arange(OFF_VA, OFF_VA + A_KV_WIDTH), np.arange(OFF_VD, OFF_VD + D_WIDTH)])
    qkv_cols = np.concatenate([qk_cols, v_cols])
    wqT = jnp.swapaxes(w_in[:, :, qkv_cols], 1, 2).astype(BF16)
    why = w_in[:, :, OFF_HY:OFF_HY + HY_IN].astype(BF16)
    q_scale = HEAD_DIM ** -0.5 * math.log2(math.e)
    gains = jnp.concatenate(
        [jnp.tile(qn_a[:, perm], (1, A_HEADS)) * q_scale, jnp.tile(qn_d[:, perm], (1, 2 * D_HEADS)) * q_scale,
         jnp.tile(kn_a[:, perm], (1, A_KV_HEADS)), jnp.tile(kn_d[:, perm], (1, 2 * D_HEADS))],
        axis=1)[:, :, None].astype(F32)
    w_out_b = w_out.astype(BF16)
    w_up_a = w_up[:, :, :d_ff].astype(BF16)
    w_up_v = w_up[:, :, d_ff:].astype(BF16)
    w_down_b = w_down.astype(BF16)
    fw1p = jnp.pad(hy_fw1, ((0, 0), (0, LANE - HY_EMB_DIM), (0, 0)))
    lamv = jnp.stack([lam_q1, lam_k1, lam_q2, lam_k2], axis=1)
    lam_init = jnp.asarray([0.8 - 0.6 * math.exp(-0.3 * i) for i in range(depth)], F32)
    lami = jnp.broadcast_to(lam_init[:, None, None], (depth, 8, LANE))

    n_c = 8 * ((bsz + 1 + 7) // 8)
    cvec = jnp.concatenate([c, c_ctx[None, :], jnp.zeros((n_c - bsz - 1, d), F32)], axis=0)
    mods = _modulation(cvec, w_mod, b_mod).reshape(depth, n_c, 6, d)
    pad2 = jnp.zeros((depth, bsz, 2, d), F32)
    mod_all = jnp.concatenate(
        [jnp.broadcast_to(mods[:, bsz:bsz + 1], (depth, bsz, 6, d)), pad2, mods[:, :bsz], pad2], axis=2)

    xs = jnp.concatenate([ctx, x], axis=1)
    for i in range(depth):
        mod = mod_all[i]
        qT, k, vTa, vTd, hy = _projection(xs, mod, norm1_g[i][None, :], wqT[i], why[i], gains[i], cosT, sinT,
                                    ctx_len=ctx_len, tm=tm)
        ya = _attention_a(qT, k, vTa, g_out[i, :A_WIDTH, None], ctx_len=ctx_len, tq=tq, tk=tk)
        yd = _attention_d(qT, k, vTd, g_out[i, A_WIDTH + HY_WIDTH:].reshape(D_HEADS, D_VALUE_DIM, 1),
                          lamv[i], lami[i], ctx_len=ctx_len, tq=tq, tk=tk)

        u_lat, u_ctx = _hyena_short_conv(hy, hy_conv_w[i], hy_conv_b[i], ctx_len=ctx_len)
        fil_args = (fw1p[i], hy_fb1[i][None], hy_fw2[i], hy_fb2[i][None], hy_fw3[i], hy_fb3[i][None],
                    hy_freq[i][None], absd)
        h_lat, sum_lat = _hyena_filters(z_lat, *fil_args)
        h_ctx, sum_ctx = _hyena_filters(z_ctx, *fil_args)
        ss_lat = jnp.swapaxes(sum_lat.reshape(2, 2, HY_WIDTH), 0, 1)
        ss_ctx = jnp.swapaxes(sum_ctx.reshape(2, 2, HY_WIDTH), 0, 1)
        h_ctx = jnp.transpose(h_ctx, (2, 1, 0, 3)).reshape(ctx_len, 2, 2, HY_WIDTH)
        k_ctx = _circular_filters(h_ctx, 2 * ctx_len)

        ak = _outer_forward(fil_f[0], fil_f[1], h_lat.reshape(1, 2, 4, t1n, DFT_N2, LANE), 0, 3)
        kf = _filter_spectrum(g_tab[0], g_tab[1], ak, h_lat, ss_lat, fb=fb)
        p_n = bsz // 2
        u6 = u_lat.reshape(3, 2, p_n, 2 * t1n, DFT_N2, LANE)

        def long_conv(z6, g_z, g_gate, order):
            a = _outer_forward(sig_f[0], sig_f[1], z6, g_z, 1)
            cc = _spectral_multiply(g_tab[0], h_tab[0], kf, a, order, fb=fb)
            return _outer_inverse(inv_f[0], cc, u6, g_gate, z6, g_z, hy_bias[i, order][None, :])

        z1 = long_conv(u6, 0, 1, 0)
        yh2 = long_conv(z1[None], 0, 2, 1).reshape(2, bsz, n_tok, LANE)
        yh_lat = jnp.concatenate([yh2[0], yh2[1]], axis=-1)

        kf_c = _ctx_filter_spectrum(cfil_f[0], cfil_f[1], k_ctx, ss_ctx)
        yh_ctx = _ctx_hyena(csig_f[0], csig_f[1], cinv_f[0], cinv_f[1], kf_c,
                            u_ctx.reshape(3, p_n, 2 * ctx_len, HY_WIDTH), hy_bias[i])
        yh = jnp.concatenate([yh_ctx.reshape(bsz, ctx_len, HY_WIDTH), yh_lat], axis=1)

        xs = _merge(xs, ya, yh, yd, g_out[i, A_WIDTH:A_WIDTH + HY_WIDTH][None, :], w_out_b[i], mod,
                    ctx_len=ctx_len, tm=tm)
        xs = _ffn(xs, mod, norm2_g[i][None, :], w_up_a[i], w_up_v[i], ffn_conv_w[i], ffn_conv_b[i][None, :],
                  w_down_b[i], ctx_len=ctx_len, tm=tm, chunks=ff_chunks)
    return xs[:, ctx_len:]
```
